```python
import math
import jax, jax.numpy as jnp
from jax import lax
import numpy as np

D_MODEL = 2048
BATCH = 8
SEQ = 4096
DEPTH = 2

MIX_WIDTH = D_MODEL
ATTN_WIDTH = MIX_WIDTH // 2
SSM_WIDTH = MIX_WIDTH - ATTN_WIDTH
HEAD_DIM = 64
N_Q_HEADS = ATTN_WIDTH // HEAD_DIM
KV_RATIO = 8
N_KV_HEADS = N_Q_HEADS // KV_RATIO
KV_DIM = N_KV_HEADS * HEAD_DIM
WINDOW = 128
SSM_GROUP = 16
N_SSM_GROUPS = SSM_WIDTH // SSM_GROUP
STATE = 64
IN_COLS = ATTN_WIDTH + 2 * KV_DIM + SSM_WIDTH
D_FF = ((8 * D_MODEL // 3 + 255) // 256) * 256
CONV_WIDTH = 3
EPS = 1e-6
NEG = -1e30

kernel_name = "hybrid_swa_s5_convffn_sandwich_adaln"


def rmsnorm(x, g):
    x32 = x.astype(jnp.float32)
    y = x32 * lax.rsqrt(jnp.mean(x32 * x32, axis=-1, keepdims=True) + EPS)
    return y.astype(x.dtype) * g


def sliding_window_attention(q, k, v, sinks):
    bsz, seq = q.shape[0], q.shape[1]
    nb = seq // WINDOW
    grp = N_Q_HEADS // N_KV_HEADS
    qb = q.reshape(bsz, nb, WINDOW, N_KV_HEADS, grp, HEAD_DIM).astype(jnp.float32)

    def band(t):
        tb = t.reshape(bsz, nb, WINDOW, N_KV_HEADS, HEAD_DIM)
        prev = jnp.pad(tb, ((0, 0), (1, 0), (0, 0), (0, 0), (0, 0)))[:, :-1]
        return jnp.concatenate([prev, tb], axis=2)

    kb = band(k).astype(jnp.float32)
    vb = band(v)
    s = jnp.einsum('bnqhgd,bnkhd->bnhgqk', qb, kb) * (HEAD_DIM ** -0.5)
    qi = jnp.arange(WINDOW)[:, None]
    kj = jnp.arange(2 * WINDOW)[None, :]
    in_band = (kj > qi) & (kj <= qi + WINDOW)
    blk = jnp.arange(nb)[:, None, None]
    valid = in_band[None] & ((blk * WINDOW + kj[None] - WINDOW) >= 0)
    s = jnp.where(valid[None, :, None, None], s, NEG)
    sink = sinks.astype(jnp.float32).reshape(1, 1, N_KV_HEADS, grp, 1, 1)
    m = jnp.maximum(jnp.max(s, axis=-1, keepdims=True), sink)
    e = jnp.exp(s - m)
    p = e / (jnp.sum(e, axis=-1, keepdims=True) + jnp.exp(sink - m))
    out = jnp.einsum('bnhgqk,bnkhd->bnqhgd', p.astype(v.dtype), vb)
    return out.reshape(bsz, seq, N_Q_HEADS * HEAD_DIM)


def s5_ssm(u, lam_re, lam_im, log_step, b_re, b_im, c_re, c_im, d_skip):
    bsz, seq = u.shape[0], u.shape[1]
    dtype = u.dtype
    u4 = u.reshape(bsz, seq, N_SSM_GROUPS, SSM_GROUP).astype(jnp.float32)
    lr = lam_re.astype(jnp.float32)
    li = lam_im.astype(jnp.float32)
    dt = jnp.exp(log_step.astype(jnp.float32))[:, None]
    mag = jnp.exp(lr * dt)
    ang = li * dt
    ab_re = mag * jnp.cos(ang)
    ab_im = mag * jnp.sin(ang)
    den = lr * lr + li * li
    f_re = ((ab_re - 1.0) * lr + ab_im * li) / den
    f_im = (ab_im * lr - (ab_re - 1.0) * li) / den
    br = b_re.astype(jnp.float32)
    bi = b_im.astype(jnp.float32)
    bb_re = f_re[..., None] * br - f_im[..., None] * bi
    bb_im = f_re[..., None] * bi + f_im[..., None] * br
    bu_re = jnp.einsum('bsgh,gph->bsgp', u4, bb_re)
    bu_im = jnp.einsum('bsgh,gph->bsgp', u4, bb_im)
    a_re = jnp.broadcast_to(ab_re[None, None], (1, seq, N_SSM_GROUPS, STATE))
    a_im = jnp.broadcast_to(ab_im[None, None], (1, seq, N_SSM_GROUPS, STATE))

    def combine(e1, e2):
        a1r, a1i, b1r, b1i = e1
        a2r, a2i, b2r, b2i = e2
        return (a2r * a1r - a2i * a1i,
                a2r * a1i + a2i * a1r,
                a2r * b1r - a2i * b1i + b2r,
                a2r * b1i + a2i * b1r + b2i)

    _, _, xr, xi = lax.associative_scan(combine, (a_re, a_im, bu_re, bu_im), axis=1)
    y = (jnp.einsum('bsgp,ghp->bsgh', xr, c_re.astype(jnp.float32))
         - jnp.einsum('bsgp,ghp->bsgh', xi, c_im.astype(jnp.float32))
         + d_skip.astype(jnp.float32)[None, None] * u4)
    return y.reshape(bsz, seq, SSM_WIDTH).astype(dtype)


def causal_depthwise_conv(h, w, b):
    seq = h.shape[1]
    hp = jnp.pad(h, ((0, 0), (CONV_WIDTH - 1, 0), (0, 0)))
    out = b
    for k in range(CONV_WIDTH):
        out = out + hp[:, k:k + seq] * w[k]
    return out


def _fwd_setup_inputs(seed: int = 0) -> dict:
    key = jax.random.key(seed)
    ks = jax.random.split(key, 32)
    nrm = jax.random.normal
    G, P, H = N_SSM_GROUPS, STATE, SSM_GROUP
    return {
        "x": nrm(ks[0], (BATCH, SEQ, D_MODEL), jnp.float32),
        "c": nrm(ks[1], (BATCH, D_MODEL), jnp.float32),
        "w_ada": nrm(ks[2], (DEPTH, D_MODEL, 6 * D_MODEL), jnp.float32) * (0.5 * D_MODEL ** -0.5),
        "b_ada": nrm(ks[3], (DEPTH, 6 * D_MODEL), jnp.float32) * 0.02,
        "g_pre_mix": 1.0 + 0.1 * nrm(ks[4], (DEPTH, D_MODEL), jnp.float32),
        "g_post_mix": 1.0 + 0.1 * nrm(ks[5], (DEPTH, D_MODEL), jnp.float32),
        "w_in": nrm(ks[6], (DEPTH, D_MODEL, IN_COLS), jnp.float32) * D_MODEL ** -0.5,
        "attn_sinks": nrm(ks[7], (DEPTH, N_Q_HEADS), jnp.float32),
        "lam_re": -0.5 + 0.01 * nrm(ks[8], (DEPTH, G, P), jnp.float32),
        "lam_im": jnp.pi * jnp.arange(P, dtype=jnp.float32)[None, None, :]
                  + 0.01 * nrm(ks[9], (DEPTH, G, P), jnp.float32),
        "log_step": jax.random.uniform(ks[10], (DEPTH, G), jnp.float32,
                                       minval=math.log(1e-3), maxval=math.log(1e-1)),
        "ssm_b_re": nrm(ks[11], (DEPTH, G, P, H), jnp.float32) * (2 * H) ** -0.5,
        "ssm_b_im": nrm(ks[12], (DEPTH, G, P, H), jnp.float32) * (2 * H) ** -0.5,
        "ssm_c_re": nrm(ks[13], (DEPTH, G, H, P), jnp.float32) * 0.5,
        "ssm_c_im": nrm(ks[14], (DEPTH, G, H, P), jnp.float32) * 0.5,
        "ssm_d": nrm(ks[15], (DEPTH, G, H), jnp.float32),
        "w_glu": nrm(ks[16], (DEPTH, SSM_WIDTH, SSM_WIDTH), jnp.float32) * SSM_WIDTH ** -0.5,
        "g_attn_out": 1.0 + 0.1 * nrm(ks[17], (DEPTH, ATTN_WIDTH), jnp.float32),
        "g_ssm_out": 1.0 + 0.1 * nrm(ks[18], (DEPTH, SSM_WIDTH), jnp.float32),
        "w_out": nrm(ks[19], (DEPTH, MIX_WIDTH, D_MODEL), jnp.float32) * MIX_WIDTH ** -0.5,
        "g_pre_ffn": 1.0 + 0.1 * nrm(ks[20], (DEPTH, D_MODEL), jnp.float32),
        "g_post_ffn": 1.0 + 0.1 * nrm(ks[21], (DEPTH, D_MODEL), jnp.float32),
        "w_up": nrm(ks[22], (DEPTH, D_MODEL, 2 * D_FF), jnp.float32) * D_MODEL ** -0.5,
        "conv_w": nrm(ks[23], (DEPTH, CONV_WIDTH, 2 * D_FF), jnp.float32) * CONV_WIDTH ** -0.5,
        "conv_b": nrm(ks[24], (DEPTH, 2 * D_FF), jnp.float32) * 0.01,
        "w_down": nrm(ks[25], (DEPTH, D_FF, D_MODEL), jnp.float32) * D_FF ** -0.5,
    }


def _fwd_reference(x, c, w_ada, b_ada, g_pre_mix, g_post_mix, w_in, attn_sinks, lam_re, lam_im,
              log_step, ssm_b_re, ssm_b_im, ssm_c_re, ssm_c_im, ssm_d, w_glu, g_attn_out,
              g_ssm_out, w_out, g_pre_ffn, g_post_ffn, w_up, conv_w, conv_b, w_down):
    bsz, seq = x.shape[0], x.shape[1]
    c_act = jax.nn.silu(c)
    for l in range(DEPTH):
        ada = c_act @ w_ada[l] + b_ada[l]
        sh_m, sc_m, gt_m, sh_f, sc_f, gt_f = [t[:, None, :] for t in jnp.split(ada, 6, axis=-1)]

        h = rmsnorm(x, g_pre_mix[l]) * (1.0 + sc_m) + sh_m
        proj = h @ w_in[l]
        q = proj[..., :ATTN_WIDTH].reshape(bsz, seq, N_Q_HEADS, HEAD_DIM)
        k = proj[..., ATTN_WIDTH:ATTN_WIDTH + KV_DIM].reshape(bsz, seq, N_KV_HEADS, HEAD_DIM)
        v = proj[..., ATTN_WIDTH + KV_DIM:ATTN_WIDTH + 2 * KV_DIM].reshape(bsz, seq, N_KV_HEADS, HEAD_DIM)
        u = proj[..., ATTN_WIDTH + 2 * KV_DIM:]

        attn = sliding_window_attention(q, k, v, attn_sinks[l])
        y = s5_ssm(u, lam_re[l], lam_im[l], log_step[l], ssm_b_re[l], ssm_b_im[l],
                   ssm_c_re[l], ssm_c_im[l], ssm_d[l])
        z = jax.nn.gelu(y, approximate=True)
        ssm = z * jax.nn.sigmoid(z @ w_glu[l])

        merged = jnp.concatenate([rmsnorm(attn, g_attn_out[l]), rmsnorm(ssm, g_ssm_out[l])], axis=-1)
        mix = merged @ w_out[l]
        x = x + (1.0 + gt_m) * rmsnorm(mix, g_post_mix[l])

        h = rmsnorm(x, g_pre_ffn[l]) * (1.0 + sc_f) + sh_f
        up = causal_depthwise_conv(h @ w_up[l], conv_w[l], conv_b[l])
        val, gate = up[..., :D_FF], up[..., D_FF:]
        ff = (jax.nn.gelu(gate, approximate=True) * val) @ w_down[l]
        x = x + (1.0 + gt_f) * rmsnorm(ff, g_post_ffn[l])
    return x


import jax as _jax
import jax.numpy as _jnp

TWIN_FORMAT = 'train_step'
FWD_PARAMS = ['x', 'c', 'w_ada', 'b_ada', 'g_pre_mix', 'g_post_mix', 'w_in', 'attn_sinks', 'lam_re', 'lam_im', 'log_step', 'ssm_b_re', 'ssm_b_im', 'ssm_c_re', 'ssm_c_im', 'ssm_d', 'w_glu', 'g_attn_out', 'g_ssm_out', 'w_out', 'g_pre_ffn', 'g_post_ffn', 'w_up', 'conv_w', 'conv_b', 'w_down']
TWIN_WEIGHTS = ['w_ada', 'b_ada', 'g_pre_mix', 'g_post_mix', 'w_in', 'attn_sinks', 'lam_re', 'lam_im', 'log_step', 'ssm_b_re', 'ssm_b_im', 'ssm_c_re', 'ssm_c_im', 'ssm_d', 'w_glu', 'g_attn_out', 'g_ssm_out', 'w_out', 'g_pre_ffn', 'g_post_ffn', 'w_up', 'conv_w', 'conv_b', 'w_down']
TWIN_DIFF_INPUT = 'x'
TWIN_INPUTS = ['x', 'c', 'w_ada', 'b_ada', 'g_pre_mix', 'g_post_mix', 'w_in', 'attn_sinks', 'lam_re', 'lam_im', 'log_step', 'ssm_b_re', 'ssm_b_im', 'ssm_c_re', 'ssm_c_im', 'ssm_d', 'w_glu', 'g_attn_out', 'g_ssm_out', 'w_out', 'g_pre_ffn', 'g_post_ffn', 'w_up', 'conv_w', 'conv_b', 'w_down', 'loss_target', 'm_w_ada', 'm_b_ada', 'm_g_pre_mix', 'm_g_post_mix', 'm_w_in', 'm_attn_sinks', 'm_lam_re', 'm_lam_im', 'm_log_step', 'm_ssm_b_re', 'm_ssm_b_im', 'm_ssm_c_re', 'm_ssm_c_im', 'm_ssm_d', 'm_w_glu', 'm_g_attn_out', 'm_g_ssm_out', 'm_w_out', 'm_g_pre_ffn', 'm_g_post_ffn', 'm_w_up', 'm_conv_w', 'm_conv_b', 'm_w_down', 'v_w_ada', 'v_b_ada', 'v_g_pre_mix', 'v_g_post_mix', 'v_w_in', 'v_attn_sinks', 'v_lam_re', 'v_lam_im', 'v_log_step', 'v_ssm_b_re', 'v_ssm_b_im', 'v_ssm_c_re', 'v_ssm_c_im', 'v_ssm_d', 'v_w_glu', 'v_g_attn_out', 'v_g_ssm_out', 'v_w_out', 'v_g_pre_ffn', 'v_g_post_ffn', 'v_w_up', 'v_conv_w', 'v_conv_b', 'v_w_down']
TWIN_OUTPUTS = ['loss', 'grad_x', 'grad_w_ada', 'grad_b_ada', 'grad_g_pre_mix', 'grad_g_post_mix', 'grad_w_in', 'grad_attn_sinks', 'grad_lam_re', 'grad_lam_im', 'grad_log_step', 'grad_ssm_b_re', 'grad_ssm_b_im', 'grad_ssm_c_re', 'grad_ssm_c_im', 'grad_ssm_d', 'grad_w_glu', 'grad_g_attn_out', 'grad_g_ssm_out', 'grad_w_out', 'grad_g_pre_ffn', 'grad_g_post_ffn', 'grad_w_up', 'grad_conv_w', 'grad_conv_b', 'grad_w_down', 'delta_w_ada', 'delta_b_ada', 'delta_g_pre_mix', 'delta_g_post_mix', 'delta_w_in', 'delta_attn_sinks', 'delta_lam_re', 'delta_lam_im', 'delta_log_step', 'delta_ssm_b_re', 'delta_ssm_b_im', 'delta_ssm_c_re', 'delta_ssm_c_im', 'delta_ssm_d', 'delta_w_glu', 'delta_g_attn_out', 'delta_g_ssm_out', 'delta_w_out', 'delta_g_pre_ffn', 'delta_g_post_ffn', 'delta_w_up', 'delta_conv_w', 'delta_conv_b', 'delta_w_down', 'new_m_w_ada', 'new_m_b_ada', 'new_m_g_pre_mix', 'new_m_g_post_mix', 'new_m_w_in', 'new_m_attn_sinks', 'new_m_lam_re', 'new_m_lam_im', 'new_m_log_step', 'new_m_ssm_b_re', 'new_m_ssm_b_im', 'new_m_ssm_c_re', 'new_m_ssm_c_im', 'new_m_ssm_d', 'new_m_w_glu', 'new_m_g_attn_out', 'new_m_g_ssm_out', 'new_m_w_out', 'new_m_g_pre_ffn', 'new_m_g_post_ffn', 'new_m_w_up', 'new_m_conv_w', 'new_m_conv_b', 'new_m_w_down', 'new_v_w_ada', 'new_v_b_ada', 'new_v_g_pre_mix', 'new_v_g_post_mix', 'new_v_w_in', 'new_v_attn_sinks', 'new_v_lam_re', 'new_v_lam_im', 'new_v_log_step', 'new_v_ssm_b_re', 'new_v_ssm_b_im', 'new_v_ssm_c_re', 'new_v_ssm_c_im', 'new_v_ssm_d', 'new_v_w_glu', 'new_v_g_attn_out', 'new_v_g_ssm_out', 'new_v_w_out', 'new_v_g_pre_ffn', 'new_v_g_post_ffn', 'new_v_w_up', 'new_v_conv_w', 'new_v_conv_b', 'new_v_w_down']
TWIN_LEAF_KINDS = {'loss': 'loss', 'grad_x': 'grad_x', 'grad_w_ada': 'grad_w', 'grad_b_ada': 'grad_w', 'grad_g_pre_mix': 'grad_w', 'grad_g_post_mix': 'grad_w', 'grad_w_in': 'grad_w', 'grad_attn_sinks': 'grad_w', 'grad_lam_re': 'grad_w', 'grad_lam_im': 'grad_w', 'grad_log_step': 'grad_w', 'grad_ssm_b_re': 'grad_w', 'grad_ssm_b_im': 'grad_w', 'grad_ssm_c_re': 'grad_w', 'grad_ssm_c_im': 'grad_w', 'grad_ssm_d': 'grad_w', 'grad_w_glu': 'grad_w', 'grad_g_attn_out': 'grad_w', 'grad_g_ssm_out': 'grad_w', 'grad_w_out': 'grad_w', 'grad_g_pre_ffn': 'grad_w', 'grad_g_post_ffn': 'grad_w', 'grad_w_up': 'grad_w', 'grad_conv_w': 'grad_w', 'grad_conv_b': 'grad_w', 'grad_w_down': 'grad_w', 'delta_w_ada': 'delta_w', 'delta_b_ada': 'delta_w', 'delta_g_pre_mix': 'delta_w', 'delta_g_post_mix': 'delta_w', 'delta_w_in': 'delta_w', 'delta_attn_sinks': 'delta_w', 'delta_lam_re': 'delta_w', 'delta_lam_im': 'delta_w', 'delta_log_step': 'delta_w', 'delta_ssm_b_re': 'delta_w', 'delta_ssm_b_im': 'delta_w', 'delta_ssm_c_re': 'delta_w', 'delta_ssm_c_im': 'delta_w', 'delta_ssm_d': 'delta_w', 'delta_w_glu': 'delta_w', 'delta_g_attn_out': 'delta_w', 'delta_g_ssm_out': 'delta_w', 'delta_w_out': 'delta_w', 'delta_g_pre_ffn': 'delta_w', 'delta_g_post_ffn': 'delta_w', 'delta_w_up': 'delta_w', 'delta_conv_w': 'delta_w', 'delta_conv_b': 'delta_w', 'delta_w_down': 'delta_w', 'new_m_w_ada': 'new_m', 'new_m_b_ada': 'new_m', 'new_m_g_pre_mix': 'new_m', 'new_m_g_post_mix': 'new_m', 'new_m_w_in': 'new_m', 'new_m_attn_sinks': 'new_m', 'new_m_lam_re': 'new_m', 'new_m_lam_im': 'new_m', 'new_m_log_step': 'new_m', 'new_m_ssm_b_re': 'new_m', 'new_m_ssm_b_im': 'new_m', 'new_m_ssm_c_re': 'new_m', 'new_m_ssm_c_im': 'new_m', 'new_m_ssm_d': 'new_m', 'new_m_w_glu': 'new_m', 'new_m_g_attn_out': 'new_m', 'new_m_g_ssm_out': 'new_m', 'new_m_w_out': 'new_m', 'new_m_g_pre_ffn': 'new_m', 'new_m_g_post_ffn': 'new_m', 'new_m_w_up': 'new_m', 'new_m_conv_w': 'new_m', 'new_m_conv_b': 'new_m', 'new_m_w_down': 'new_m', 'new_v_w_ada': 'new_v', 'new_v_b_ada': 'new_v', 'new_v_g_pre_mix': 'new_v', 'new_v_g_post_mix': 'new_v', 'new_v_w_in': 'new_v', 'new_v_attn_sinks': 'new_v', 'new_v_lam_re': 'new_v', 'new_v_lam_im': 'new_v', 'new_v_log_step': 'new_v', 'new_v_ssm_b_re': 'new_v', 'new_v_ssm_b_im': 'new_v', 'new_v_ssm_c_re': 'new_v', 'new_v_ssm_c_im': 'new_v', 'new_v_ssm_d': 'new_v', 'new_v_w_glu': 'new_v', 'new_v_g_attn_out': 'new_v', 'new_v_g_ssm_out': 'new_v', 'new_v_w_out': 'new_v', 'new_v_g_pre_ffn': 'new_v', 'new_v_g_post_ffn': 'new_v', 'new_v_w_up': 'new_v', 'new_v_conv_w': 'new_v', 'new_v_conv_b': 'new_v', 'new_v_w_down': 'new_v'}


def _forward(args):
    return _fwd_reference(*[args[k] for k in FWD_PARAMS])


def _output_shape():
    out = _jax.eval_shape(lambda: _forward(_fwd_setup_inputs(0)))
    return out.shape, out.dtype

N_MICROBATCH = 1
ADAM_LR = 0.001
ADAM_B1 = 0.9
ADAM_B2 = 0.999
ADAM_EPS = 1e-08
ADAM_WD = 0.01
ADAM_STEP = 10
PER_EXAMPLE_BATCH_AXIS = {'x': 0, 'c': 0, 'loss_target': 0}
SHARED_INPUTS = []
_WEIGHT_DTYPES = {'w_ada': _jnp.float32, 'b_ada': _jnp.float32, 'g_pre_mix': _jnp.float32, 'g_post_mix': _jnp.float32, 'w_in': _jnp.float32, 'attn_sinks': _jnp.float32, 'lam_re': _jnp.float32, 'lam_im': _jnp.float32, 'log_step': _jnp.float32, 'ssm_b_re': _jnp.float32, 'ssm_b_im': _jnp.float32, 'ssm_c_re': _jnp.float32, 'ssm_c_im': _jnp.float32, 'ssm_d': _jnp.float32, 'w_glu': _jnp.float32, 'g_attn_out': _jnp.float32, 'g_ssm_out': _jnp.float32, 'w_out': _jnp.float32, 'g_pre_ffn': _jnp.float32, 'g_post_ffn': _jnp.float32, 'w_up': _jnp.float32, 'conv_w': _jnp.float32, 'conv_b': _jnp.float32, 'w_down': _jnp.float32}
MOMENT_SCALE = {'w_ada': 7.966622e+00, 'b_ada': 1.726026e+01, 'g_pre_mix': 4.223028e+00, 'g_post_mix': 2.347743e+01, 'w_in': 8.373456e+00, 'attn_sinks': 3.007964e-01, 'lam_re': 1.817264e+00, 'lam_im': 1.608317e+00, 'log_step': 5.284565e+01, 'ssm_b_re': 1.247649e+00, 'ssm_b_im': 1.129018e+00, 'ssm_c_re': 4.284029e-01, 'ssm_c_im': 4.164781e-01, 'ssm_d': 7.773227e+00, 'w_glu': 1.781361e+00, 'g_attn_out': 1.129521e+01, 'g_ssm_out': 8.596675e+00, 'w_out': 1.013329e+01, 'g_pre_ffn': 5.573542e+00, 'g_post_ffn': 1.922597e+01, 'w_up': 2.620930e+00, 'conv_w': 2.787860e+00, 'conv_b': 4.326834e+00, 'w_down': 5.142433e+00}


def _to_microbatches(a, axis):
    t = _jnp.moveaxis(a, axis, 0)
    t = t.reshape((N_MICROBATCH, t.shape[0] // N_MICROBATCH) + t.shape[1:])
    return _jnp.moveaxis(t, 1, axis + 1)


def setup_inputs(seed: int = 0) -> dict:
    inp = _fwd_setup_inputs(seed)
    key = _jax.random.fold_in(_jax.random.key(seed), 7919)
    shape, _ = _output_shape()
    out = dict(inp)
    out["loss_target"] = _jax.random.normal(_jax.random.fold_in(key, 0), shape, _jnp.float32)
    for i, name in enumerate(TWIN_WEIGHTS):
        w = inp[name].astype(_jnp.float32)
        if MOMENT_SCALE is None:
            s = _jnp.sqrt(_jnp.mean(_jnp.square(w)) + 1e-30)
        else:
            s = MOMENT_SCALE[name]
        km, kv = _jax.random.split(_jax.random.fold_in(key, i + 1))
        out[name] = w
        out["m_" + name] = s * _jax.random.normal(km, w.shape, _jnp.float32)
        out["v_" + name] = (s * s) * _jax.random.uniform(kv, w.shape, _jnp.float32, 0.5, 1.5)
    if N_MICROBATCH > 1:
        for name, axis in PER_EXAMPLE_BATCH_AXIS.items():
            out[name] = _to_microbatches(out[name], axis)
    return {'x': out['x'], 'c': out['c'], 'w_ada': out['w_ada'], 'b_ada': out['b_ada'], 'g_pre_mix': out['g_pre_mix'], 'g_post_mix': out['g_post_mix'], 'w_in': out['w_in'], 'attn_sinks': out['attn_sinks'], 'lam_re': out['lam_re'], 'lam_im': out['lam_im'], 'log_step': out['log_step'], 'ssm_b_re': out['ssm_b_re'], 'ssm_b_im': out['ssm_b_im'], 'ssm_c_re': out['ssm_c_re'], 'ssm_c_im': out['ssm_c_im'], 'ssm_d': out['ssm_d'], 'w_glu': out['w_glu'], 'g_attn_out': out['g_attn_out'], 'g_ssm_out': out['g_ssm_out'], 'w_out': out['w_out'], 'g_pre_ffn': out['g_pre_ffn'], 'g_post_ffn': out['g_post_ffn'], 'w_up': out['w_up'], 'conv_w': out['conv_w'], 'conv_b': out['conv_b'], 'w_down': out['w_down'], 'loss_target': out['loss_target'], 'm_w_ada': out['m_w_ada'], 'm_b_ada': out['m_b_ada'], 'm_g_pre_mix': out['m_g_pre_mix'], 'm_g_post_mix': out['m_g_post_mix'], 'm_w_in': out['m_w_in'], 'm_attn_sinks': out['m_attn_sinks'], 'm_lam_re': out['m_lam_re'], 'm_lam_im': out['m_lam_im'], 'm_log_step': out['m_log_step'], 'm_ssm_b_re': out['m_ssm_b_re'], 'm_ssm_b_im': out['m_ssm_b_im'], 'm_ssm_c_re': out['m_ssm_c_re'], 'm_ssm_c_im': out['m_ssm_c_im'], 'm_ssm_d': out['m_ssm_d'], 'm_w_glu': out['m_w_glu'], 'm_g_attn_out': out['m_g_attn_out'], 'm_g_ssm_out': out['m_g_ssm_out'], 'm_w_out': out['m_w_out'], 'm_g_pre_ffn': out['m_g_pre_ffn'], 'm_g_post_ffn': out['m_g_post_ffn'], 'm_w_up': out['m_w_up'], 'm_conv_w': out['m_conv_w'], 'm_conv_b': out['m_conv_b'], 'm_w_down': out['m_w_down'], 'v_w_ada': out['v_w_ada'], 'v_b_ada': out['v_b_ada'], 'v_g_pre_mix': out['v_g_pre_mix'], 'v_g_post_mix': out['v_g_post_mix'], 'v_w_in': out['v_w_in'], 'v_attn_sinks': out['v_attn_sinks'], 'v_lam_re': out['v_lam_re'], 'v_lam_im': out['v_lam_im'], 'v_log_step': out['v_log_step'], 'v_ssm_b_re': out['v_ssm_b_re'], 'v_ssm_b_im': out['v_ssm_b_im'], 'v_ssm_c_re': out['v_ssm_c_re'], 'v_ssm_c_im': out['v_ssm_c_im'], 'v_ssm_d': out['v_ssm_d'], 'v_w_glu': out['v_w_glu'], 'v_g_attn_out': out['v_g_attn_out'], 'v_g_ssm_out': out['v_g_ssm_out'], 'v_w_out': out['v_w_out'], 'v_g_pre_ffn': out['v_g_pre_ffn'], 'v_g_post_ffn': out['v_g_post_ffn'], 'v_w_up': out['v_w_up'], 'v_conv_w': out['v_conv_w'], 'v_conv_b': out['v_conv_b'], 'v_w_down': out['v_w_down']}


def _loss(weights, diff, rest, loss_target):
    with _jax.named_scope("forward"):
        args = {**rest, TWIN_DIFF_INPUT: diff, **{k: w.astype(_WEIGHT_DTYPES[k]) for k, w in weights.items()}}
        y = _forward(args)
    with _jax.named_scope("loss_head"):
        err = _jnp.square(y.astype(_jnp.float32) - loss_target)
        return 0.5 * _jnp.sum(_jnp.mean(err, axis=-1)) if err.ndim else 0.5 * err


def _adamw(w, g, m, v):
    m = ADAM_B1 * m + (1.0 - ADAM_B1) * g
    v = ADAM_B2 * v + (1.0 - ADAM_B2) * _jnp.square(g)
    m_hat = m / (1.0 - ADAM_B1 ** ADAM_STEP)
    v_hat = v / (1.0 - ADAM_B2 ** ADAM_STEP)
    delta = -ADAM_LR * (m_hat / (_jnp.sqrt(v_hat) + ADAM_EPS) + ADAM_WD * w)
    return delta, m, v


def reference(x, c, w_ada, b_ada, g_pre_mix, g_post_mix, w_in, attn_sinks, lam_re, lam_im, log_step, ssm_b_re, ssm_b_im, ssm_c_re, ssm_c_im, ssm_d, w_glu, g_attn_out, g_ssm_out, w_out, g_pre_ffn, g_post_ffn, w_up, conv_w, conv_b, w_down, loss_target, m_w_ada, m_b_ada, m_g_pre_mix, m_g_post_mix, m_w_in, m_attn_sinks, m_lam_re, m_lam_im, m_log_step, m_ssm_b_re, m_ssm_b_im, m_ssm_c_re, m_ssm_c_im, m_ssm_d, m_w_glu, m_g_attn_out, m_g_ssm_out, m_w_out, m_g_pre_ffn, m_g_post_ffn, m_w_up, m_conv_w, m_conv_b, m_w_down, v_w_ada, v_b_ada, v_g_pre_mix, v_g_post_mix, v_w_in, v_attn_sinks, v_lam_re, v_lam_im, v_log_step, v_ssm_b_re, v_ssm_b_im, v_ssm_c_re, v_ssm_c_im, v_ssm_d, v_w_glu, v_g_attn_out, v_g_ssm_out, v_w_out, v_g_pre_ffn, v_g_post_ffn, v_w_up, v_conv_w, v_conv_b, v_w_down):
    given = dict(x=x, c=c, w_ada=w_ada, b_ada=b_ada, g_pre_mix=g_pre_mix, g_post_mix=g_post_mix, w_in=w_in, attn_sinks=attn_sinks, lam_re=lam_re, lam_im=lam_im, log_step=log_step, ssm_b_re=ssm_b_re, ssm_b_im=ssm_b_im, ssm_c_re=ssm_c_re, ssm_c_im=ssm_c_im, ssm_d=ssm_d, w_glu=w_glu, g_attn_out=g_attn_out, g_ssm_out=g_ssm_out, w_out=w_out, g_pre_ffn=g_pre_ffn, g_post_ffn=g_post_ffn, w_up=w_up, conv_w=conv_w, conv_b=conv_b, w_down=w_down, loss_target=loss_target, m_w_ada=m_w_ada, m_b_ada=m_b_ada, m_g_pre_mix=m_g_pre_mix, m_g_post_mix=m_g_post_mix, m_w_in=m_w_in, m_attn_sinks=m_attn_sinks, m_lam_re=m_lam_re, m_lam_im=m_lam_im, m_log_step=m_log_step, m_ssm_b_re=m_ssm_b_re, m_ssm_b_im=m_ssm_b_im, m_ssm_c_re=m_ssm_c_re, m_ssm_c_im=m_ssm_c_im, m_ssm_d=m_ssm_d, m_w_glu=m_w_glu, m_g_attn_out=m_g_attn_out, m_g_ssm_out=m_g_ssm_out, m_w_out=m_w_out, m_g_pre_ffn=m_g_pre_ffn, m_g_post_ffn=m_g_post_ffn, m_w_up=m_w_up, m_conv_w=m_conv_w, m_conv_b=m_conv_b, m_w_down=m_w_down, v_w_ada=v_w_ada, v_b_ada=v_b_ada, v_g_pre_mix=v_g_pre_mix, v_g_post_mix=v_g_post_mix, v_w_in=v_w_in, v_attn_sinks=v_attn_sinks, v_lam_re=v_lam_re, v_lam_im=v_lam_im, v_log_step=v_log_step, v_ssm_b_re=v_ssm_b_re, v_ssm_b_im=v_ssm_b_im, v_ssm_c_re=v_ssm_c_re, v_ssm_c_im=v_ssm_c_im, v_ssm_d=v_ssm_d, v_w_glu=v_w_glu, v_g_attn_out=v_g_attn_out, v_g_ssm_out=v_g_ssm_out, v_w_out=v_w_out, v_g_pre_ffn=v_g_pre_ffn, v_g_post_ffn=v_g_post_ffn, v_w_up=v_w_up, v_conv_w=v_conv_w, v_conv_b=v_conv_b, v_w_down=v_w_down)
    weights = {n: given[n] for n in TWIN_WEIGHTS}
    shared = {n: given[n] for n in SHARED_INPUTS}
    per_example = {n: given[n] for n in ['x', 'c']}
    grad_fn = _jax.value_and_grad(_loss, argnums=(0, 1))

    def one_microbatch(ex, loss_target):
        ex = dict(ex)
        diff = ex.pop(TWIN_DIFF_INPUT)
        return grad_fn(weights, diff, {**shared, **ex}, loss_target)

    if N_MICROBATCH == 1:
        loss, (grad_w, grad_x) = one_microbatch(per_example, given["loss_target"])
    else:
        def body(carry, xs):
            loss_sum, grad_sum = carry
            l_k, (gw_k, gx_k) = one_microbatch(xs[0], xs[1])
            with _jax.named_scope("update"):
                return (loss_sum + l_k, _jax.tree.map(_jnp.add, grad_sum, gw_k)), gx_k

        init = (_jnp.zeros((), _jnp.float32), _jax.tree.map(_jnp.zeros_like, weights))
        (loss, grad_w), grad_x = _jax.lax.scan(body, init, (per_example, given["loss_target"]))
    with _jax.named_scope("update"):
        delta_w, new_m, new_v = {}, {}, {}
        for n in TWIN_WEIGHTS:
            delta_w[n], new_m[n], new_v[n] = _adamw(weights[n], grad_w[n], given["m_" + n], given["v_" + n])
    return (loss, grad_x, *[grad_w[n] for n in TWIN_WEIGHTS], *[delta_w[n] for n in TWIN_WEIGHTS],
            *[new_m[n] for n in TWIN_WEIGHTS], *[new_v[n] for n in TWIN_WEIGHTS])
```

```python
import functools
import math

import jax
import jax.numpy as jnp
from jax import lax
from jax.experimental import pallas as pl
from jax.experimental.pallas import tpu as pltpu

F32 = jnp.float32
BF16 = jnp.bfloat16
EPS = 1e-6
NEG = -1e30
WINDOW = 128
HEAD_DIM = 64
KV_RATIO = 8
SSM_GROUP = 16
STATE = 64
PIECE = 128
PSTATES = 512
DEPTH = 2
ADAM_LR, ADAM_B1, ADAM_B2, ADAM_EPS, ADAM_WD, ADAM_STEP = 0.001, 0.9, 0.999, 1e-08, 0.01, 10
MIB = 1024 * 1024
MESH = pl.DeviceIdType.MESH


def _cparams(sem=None, vmem_mib=48):
    return pltpu.CompilerParams(dimension_semantics=sem, vmem_limit_bytes=vmem_mib * MIB)


def _gelu(x):
    c = math.sqrt(2.0 / math.pi)
    return 0.5 * x * (1.0 + jnp.tanh(c * (x + 0.044715 * (x * x * x))))


def _gelu_grad(x):
    c = math.sqrt(2.0 / math.pi)
    t = jnp.tanh(c * (x + 0.044715 * (x * x * x)))
    return 0.5 * (1.0 + t) + 0.5 * x * (1.0 - t * t) * c * (1.0 + 3.0 * 0.044715 * x * x)


def _sigmoid(x):
    return 1.0 / (1.0 + jnp.exp(-x))


def _matmul(a, b, *, m, n, k, tm, tn, tk, out_dtype, name, ta=False, tb=False, a_stack=0, b_stack=0, o_stack=0):
    assert m % tm == 0 and n % tn == 0 and k % tk == 0, (name, m, n, k, tm, tn, tk)
    nk = k // tk

    if a_stack:
        assert not ta and (k // a_stack) % tk == 0
        per = (k // a_stack) // tk
        a_spec = pl.BlockSpec((None, tm, tk), lambda i, j, kk: (kk // per, i, kk % per))
    elif ta:
        a_spec = pl.BlockSpec((tk, tm), lambda i, j, kk: (kk, i))
    else:
        a_spec = pl.BlockSpec((tm, tk), lambda i, j, kk: (i, kk))
    if b_stack and tb:
        perb = (k // b_stack) // tk
        b_spec = pl.BlockSpec((None, tn, tk), lambda i, j, kk: (kk // perb, j, kk % perb))
    elif b_stack:
        perb = (n // b_stack) // tn
        b_spec = pl.BlockSpec((None, tk, tn), lambda i, j, kk: (j // perb, kk, j % perb))
    elif tb:
        b_spec = pl.BlockSpec((tn, tk), lambda i, j, kk: (j, kk))
    else:
        b_spec = pl.BlockSpec((tk, tn), lambda i, j, kk: (kk, j))
    if o_stack:
        pero = (n // o_stack) // tn
        o_spec = pl.BlockSpec((None, tm, tn), lambda i, j, kk: (j // pero, i, j % pero))
        o_shape = jax.ShapeDtypeStruct((o_stack, m, n // o_stack), out_dtype)
    else:
        o_spec = pl.BlockSpec((tm, tn), lambda i, j, kk: (i, j))
        o_shape = jax.ShapeDtypeStruct((m, n), out_dtype)
    dims = (((0 if ta else 1,), (1 if tb else 0,)), ((), ()))

    def body(a_ref, b_ref, o_ref, *acc):
        p = lax.dot_general(a_ref[...].astype(BF16), b_ref[...].astype(BF16), dims, preferred_element_type=F32)
        if nk == 1:
            o_ref[...] = p.astype(o_ref.dtype)
        else:
            acc_ref = acc[0]
            kk = pl.program_id(2)

            @pl.when(kk == 0)
            def _():
                acc_ref[...] = p

            @pl.when(kk > 0)
            def _():
                acc_ref[...] += p

            @pl.when(kk == nk - 1)
            def _():
                o_ref[...] = acc_ref[...].astype(o_ref.dtype)

    return pl.pallas_call(
        body, name=name, grid=(m // tm, n // tn, nk), in_specs=[a_spec, b_spec], out_specs=o_spec, out_shape=o_shape,
        scratch_shapes=[] if nk == 1 else [pltpu.VMEM((tm, tn), F32)],
        compiler_params=_cparams(("parallel", "parallel", "arbitrary"), 56),
    )(a, b)


def _row(d):
    return pl.BlockSpec((1, d), lambda i: (0, 0))


def _tok(tm, d):
    return pl.BlockSpec((tm, d), lambda i: (i, 0))


def _pre_fwd(x, g, sc, sh, name):
    s, d = x.shape
    tm = min(256, s)

    def body(x_ref, g_ref, sc_ref, sh_ref, h_ref):
        xv = x_ref[...]
        r = lax.rsqrt(jnp.mean(xv * xv, axis=-1, keepdims=True) + EPS)
        h_ref[...] = (((xv * r) * g_ref[...]) * (1.0 + sc_ref[...]) + sh_ref[...]).astype(BF16)

    return pl.pallas_call(
        body, name=name, grid=(s // tm,), in_specs=[_tok(tm, d), _row(d), _row(d), _row(d)], out_specs=_tok(tm, d),
        out_shape=jax.ShapeDtypeStruct((s, d), BF16), compiler_params=_cparams(("parallel",)),
    )(x, g, sc, sh)


def _post_fwd(x, o, g, gt, name):
    s, d = x.shape
    tm = min(256, s)

    def body(x_ref, o_ref, g_ref, gt_ref, y_ref):
        ov = o_ref[...]
        r = lax.rsqrt(jnp.mean(ov * ov, axis=-1, keepdims=True) + EPS)
        y_ref[...] = x_ref[...] + (1.0 + gt_ref[...]) * ((ov * r) * g_ref[...])

    return pl.pallas_call(
        body, name=name, grid=(s // tm,), in_specs=[_tok(tm, d), _tok(tm, d), _row(d), _row(d)], out_specs=_tok(tm, d),
        out_shape=jax.ShapeDtypeStruct((s, d), F32), compiler_params=_cparams(("parallel",)),
    )(x, o, g, gt)


def _post_bwd(dxo, o, g, gt, name):
    s, d = o.shape
    tm = min(256, s)

    def body(dx_ref, o_ref, g_ref, gt_ref, do_ref, dgt_ref, dg_ref):
        i = pl.program_id(0)
        dx = dx_ref[...]
        ov = o_ref[...]
        gv = g_ref[...]
        r = lax.rsqrt(jnp.mean(ov * ov, axis=-1, keepdims=True) + EPS)
        oh = ov * r
        dn = dx * (1.0 + gt_ref[...])
        e = dn * gv
        do_ref[...] = (r * (e - oh * jnp.mean(e * oh, axis=-1, keepdims=True))).astype(BF16)
        p_gt = jnp.sum(dx * (oh * gv), axis=0, keepdims=True)
        p_g = jnp.sum(dn * oh, axis=0, keepdims=True)

        @pl.when(i == 0)
        def _():
            dgt_ref[...] = p_gt
            dg_ref[...] = p_g

        @pl.when(i > 0)
        def _():
            dgt_ref[...] += p_gt
            dg_ref[...] += p_g

    row = jax.ShapeDtypeStruct((1, d), F32)
    return pl.pallas_call(
        body, name=name, grid=(s // tm,), in_specs=[_tok(tm, d), _tok(tm, d), _row(d), _row(d)],
        out_specs=(_tok(tm, d), _row(d), _row(d)), out_shape=(jax.ShapeDtypeStruct((s, d), BF16), row, row),
        compiler_params=_cparams(("arbitrary",)),
    )(dxo, o, g, gt)


def _pre_bwd(dres, dh, x, g, sc, name):
    s, d = x.shape
    tm = min(256, s)

    def body(dres_ref, dh_ref, x_ref, g_ref, sc_ref, dx_ref, dsh_ref, dsc_ref, dg_ref):
        i = pl.program_id(0)
        dh_v = dh_ref[...]
        xv = x_ref[...]
        gv = g_ref[...]
        one_sc = 1.0 + sc_ref[...]
        r = lax.rsqrt(jnp.mean(xv * xv, axis=-1, keepdims=True) + EPS)
        xh = xv * r
        e = dh_v * one_sc * gv
        dx_ref[...] = dres_ref[...] + r * (e - xh * jnp.mean(e * xh, axis=-1, keepdims=True))
        p_sh = jnp.sum(dh_v, axis=0, keepdims=True)
        p_sc = jnp.sum(dh_v * (xh * gv), axis=0, keepdims=True)
        p_g = jnp.sum(dh_v * one_sc * xh, axis=0, keepdims=True)

        @pl.when(i == 0)
        def _():
            dsh_ref[...] = p_sh
            dsc_ref[...] = p_sc
            dg_ref[...] = p_g

        @pl.when(i > 0)
        def _():
            dsh_ref[...] += p_sh
            dsc_ref[...] += p_sc
            dg_ref[...] += p_g

    row = jax.ShapeDtypeStruct((1, d), F32)
    return pl.pallas_call(
        body, name=name, grid=(s // tm,), in_specs=[_tok(tm, d), _tok(tm, d), _tok(tm, d), _row(d), _row(d)],
        out_specs=(_tok(tm, d), _row(d), _row(d), _row(d)), out_shape=(jax.ShapeDtypeStruct((s, d), F32), row, row, row),
        compiler_params=_cparams(("arbitrary",)),
    )(dres, dh, x, g, sc)


def _loss_head(y, tgt, name):
    s, d = y.shape
    tm = min(256, s)

    def body(y_ref, t_ref, dy_ref, l_ref):
        i = pl.program_id(0)
        err = y_ref[...] - t_ref[...]
        dy_ref[...] = err * (1.0 / d)
        part = jnp.zeros((1, 128), F32) + jnp.sum(err * err)

        @pl.when(i == 0)
        def _():
            l_ref[...] = part

        @pl.when(i > 0)
        def _():
            l_ref[...] += part

    return pl.pallas_call(
        body, name=name, grid=(s // tm,), in_specs=[_tok(tm, d), _tok(tm, d)],
        out_specs=(_tok(tm, d), pl.BlockSpec((1, 128), lambda i: (0, 0))),
        out_shape=(jax.ShapeDtypeStruct((s, d), F32), jax.ShapeDtypeStruct((1, 128), F32)),
        compiler_params=_cparams(("arbitrary",)),
    )(y, tgt)


def _gelu_fwd(y, name):
    s, u = y.shape
    tm = min(512, s)

    def body(y_ref, z_ref):
        z_ref[...] = _gelu(y_ref[...]).astype(BF16)

    return pl.pallas_call(
        body, name=name, grid=(s // tm,), in_specs=[_tok(tm, u)], out_specs=_tok(tm, u),
        out_shape=jax.ShapeDtypeStruct((s, u), BF16), compiler_params=_cparams(("parallel",)),
    )(y)


def _merge_fwd(attn, y, gl, ga, gs, name):
    s, aw = attn.shape
    uw = y.shape[1]
    tm = min(256, s)

    def body(a_ref, y_ref, gl_ref, ga_ref, gs_ref, m_ref):
        av = a_ref[...]
        ra = lax.rsqrt(jnp.mean(av * av, axis=-1, keepdims=True) + EPS)
        m_ref[:, :aw] = ((av * ra) * ga_ref[...]).astype(BF16)
        ssm = _gelu(y_ref[...]) * _sigmoid(gl_ref[...])
        rs = lax.rsqrt(jnp.mean(ssm * ssm, axis=-1, keepdims=True) + EPS)
        m_ref[:, aw:] = ((ssm * rs) * gs_ref[...]).astype(BF16)

    return pl.pallas_call(
        body, name=name, grid=(s // tm,), in_specs=[_tok(tm, aw), _tok(tm, uw), _tok(tm, uw), _row(aw), _row(uw)],
        out_specs=_tok(tm, aw + uw), out_shape=jax.ShapeDtypeStruct((s, aw + uw), BF16),
        compiler_params=_cparams(("parallel",)),
    )(attn, y, gl, ga, gs)


def _merge_bwd(dm, attn, y, gl, ga, gs, name):
    s, aw = attn.shape
    uw = y.shape[1]
    tm = min(256, s)

    def body(dm_ref, a_ref, y_ref, gl_ref, ga_ref, gs_ref, da_ref, dgl_ref, dz_ref, dga_ref, dgs_ref):
        i = pl.program_id(0)
        av = a_ref[...]
        dma = dm_ref[:, :aw]
        ra = lax.rsqrt(jnp.mean(av * av, axis=-1, keepdims=True) + EPS)
        ah = av * ra
        e = dma * ga_ref[...]
        da_ref[...] = (ra * (e - ah * jnp.mean(e * ah, axis=-1, keepdims=True))).astype(BF16)
        p_ga = jnp.sum(dma * ah, axis=0, keepdims=True)

        z = _gelu(y_ref[...])
        sig = _sigmoid(gl_ref[...])
        ssm = z * sig
        dms = dm_ref[:, aw:]
        rs = lax.rsqrt(jnp.mean(ssm * ssm, axis=-1, keepdims=True) + EPS)
        sh = ssm * rs
        e2 = dms * gs_ref[...]
        dssm = rs * (e2 - sh * jnp.mean(e2 * sh, axis=-1, keepdims=True))
        dz_ref[...] = dssm * sig
        dgl_ref[...] = (dssm * z * sig * (1.0 - sig)).astype(BF16)
        p_gs = jnp.sum(dms * sh, axis=0, keepdims=True)

        @pl.when(i == 0)
        def _():
            dga_ref[...] = p_ga
            dgs_ref[...] = p_gs

        @pl.when(i > 0)
        def _():
            dga_ref[...] += p_ga
            dgs_ref[...] += p_gs

    return pl.pallas_call(
        body, name=name, grid=(s // tm,),
        in_specs=[_tok(tm, aw + uw), _tok(tm, aw), _tok(tm, uw), _tok(tm, uw), _row(aw), _row(uw)],
        out_specs=(_tok(tm, aw), _tok(tm, uw), _tok(tm, uw), _row(aw), _row(uw)),
        out_shape=(jax.ShapeDtypeStruct((s, aw), BF16), jax.ShapeDtypeStruct((s, uw), BF16), jax.ShapeDtypeStruct((s, uw), F32),
                   jax.ShapeDtypeStruct((1, aw), F32), jax.ShapeDtypeStruct((1, uw), F32)),
        compiler_params=_cparams(("arbitrary",)),
    )(dm, attn, y, gl, ga, gs)


def _gelu_bwd(dzd, dz2, y, name):
    s, u = y.shape
    tm = min(512, s)

    def body(a_ref, b_ref, y_ref, o_ref):
        o_ref[...] = (a_ref[...] + b_ref[...]) * _gelu_grad(y_ref[...])

    return pl.pallas_call(
        body, name=name, grid=(s // tm,), in_specs=[_tok(tm, u), _tok(tm, u), _tok(tm, u)], out_specs=_tok(tm, u),
        out_shape=jax.ShapeDtypeStruct((s, u), F32), compiler_params=_cparams(("parallel",)),
    )(dzd, dz2, y)


def _attn_scores(qh, kb, sink, valid):
    s = lax.dot_general(qh, kb, (((1,), (1,)), ((), ())), preferred_element_type=F32) * (HEAD_DIM ** -0.5)
    s = jnp.where(valid, s, NEG)
    m = jnp.maximum(jnp.max(s, axis=-1, keepdims=True), sink)
    e = jnp.exp(s - m)
    esink = jnp.exp(sink - m)
    den = jnp.sum(e, axis=-1, keepdims=True) + esink
    return e / den, esink / den


def _attn_valid(i):
    qi = lax.broadcasted_iota(jnp.int32, (WINDOW, 2 * WINDOW), 0)
    kj = lax.broadcasted_iota(jnp.int32, (WINDOW, 2 * WINDOW), 1)
    return (kj > qi) & (kj <= qi + WINDOW) & ((kj >= WINDOW) | (i > 0))


def _attn_specs(aw, uw, kvw):
    qblk = uw // aw
    kvblk = (uw + aw) // (2 * kvw)
    assert uw % aw == 0 and (uw + aw) % (2 * kvw) == 0
    return [
        pl.BlockSpec(memory_space=pltpu.SMEM),
        pl.BlockSpec((WINDOW, aw), lambda i: (i, qblk)),
        pl.BlockSpec((WINDOW, 2 * kvw), lambda i: (i, kvblk)),
        pl.BlockSpec((WINDOW, 2 * kvw), lambda i: (jnp.maximum(i - 1, 0), kvblk)),
    ]


def _attn_fwd(proj, sinks, aw, uw, name):
    s = proj.shape[0]
    nq = aw // HEAD_DIM
    nkv = nq // KV_RATIO
    kvw = nkv * HEAD_DIM

    def body(sink_ref, q_ref, kvc_ref, kvp_ref, o_ref):
        valid = _attn_valid(pl.program_id(0))
        q = q_ref[...]
        kvc = kvc_ref[...]
        kvp = kvp_ref[...]
        for hk in range(nkv):
            kb = jnp.concatenate([kvp[:, hk * HEAD_DIM:(hk + 1) * HEAD_DIM], kvc[:, hk * HEAD_DIM:(hk + 1) * HEAD_DIM]], axis=0)
            vb = jnp.concatenate([kvp[:, kvw + hk * HEAD_DIM:kvw + (hk + 1) * HEAD_DIM],
                                  kvc[:, kvw + hk * HEAD_DIM:kvw + (hk + 1) * HEAD_DIM]], axis=0)
            for g in range(KV_RATIO):
                hq = hk * KV_RATIO + g
                p, _ = _attn_scores(q[:, hq * HEAD_DIM:(hq + 1) * HEAD_DIM], kb, sink_ref[hq], valid)
                o_ref[:, hq * HEAD_DIM:(hq + 1) * HEAD_DIM] = jnp.dot(p.astype(BF16), vb, preferred_element_type=F32)

    return pl.pallas_call(
        body, name=name, grid=(s // WINDOW,), in_specs=_attn_specs(aw, uw, kvw),
        out_specs=pl.BlockSpec((WINDOW, aw), lambda i: (i, 0)), out_shape=jax.ShapeDtypeStruct((s, aw), F32),
        compiler_params=_cparams(("parallel",)),
    )(sinks, proj, proj, proj)


def _attn_bwd(proj, sinks, attn, dattn, aw, uw, name):
    s = proj.shape[0]
    nq = aw // HEAD_DIM
    nkv = nq // KV_RATIO
    kvw = nkv * HEAD_DIM
    hd = HEAD_DIM

    def body(sink_ref, q_ref, kvc_ref, kvp_ref, o_ref, do_ref, dq_ref, dc_ref, dp_ref, ds_ref):
        i = pl.program_id(0)
        valid = _attn_valid(i)
        q = q_ref[...]
        kvc = kvc_ref[...]
        kvp = kvp_ref[...]
        lane = lax.broadcasted_iota(jnp.int32, (1, nq), 1)
        dsink = jnp.zeros((1, nq), F32)
        for hk in range(nkv):
            kb = jnp.concatenate([kvp[:, hk * hd:(hk + 1) * hd], kvc[:, hk * hd:(hk + 1) * hd]], axis=0)
            vb = jnp.concatenate([kvp[:, kvw + hk * hd:kvw + (hk + 1) * hd], kvc[:, kvw + hk * hd:kvw + (hk + 1) * hd]], axis=0)
            dkb = jnp.zeros((2 * WINDOW, hd), F32)
            dvb = jnp.zeros((2 * WINDOW, hd), F32)
            for g in range(KV_RATIO):
                hq = hk * KV_RATIO + g
                qh = q[:, hq * hd:(hq + 1) * hd]
                p, psink = _attn_scores(qh, kb, sink_ref[hq], valid)
                do_h = do_ref[:, hq * hd:(hq + 1) * hd]
                delta = jnp.sum(do_h.astype(F32) * o_ref[:, hq * hd:(hq + 1) * hd], axis=-1, keepdims=True)
                dpv = lax.dot_general(do_h, vb, (((1,), (1,)), ((), ())), preferred_element_type=F32)
                dsb = (p * (dpv - delta) * (hd ** -0.5)).astype(BF16)
                dq_ref[:, hq * hd:(hq + 1) * hd] = jnp.dot(dsb, kb, preferred_element_type=F32).astype(BF16)
                dkb = dkb + lax.dot_general(dsb, qh, (((0,), (0,)), ((), ())), preferred_element_type=F32)
                dvb = dvb + lax.dot_general(p.astype(BF16), do_h, (((0,), (0,)), ((), ())), preferred_element_type=F32)
                dsink = dsink + jnp.where(lane == hq, -jnp.sum(psink * delta), 0.0)
            dp_ref[:, hk * hd:(hk + 1) * hd] = dkb[:WINDOW]
            dc_ref[:, hk * hd:(hk + 1) * hd] = dkb[WINDOW:]
            dp_ref[:, kvw + hk * hd:kvw + (hk + 1) * hd] = dvb[:WINDOW]
            dc_ref[:, kvw + hk * hd:kvw + (hk + 1) * hd] = dvb[WINDOW:]

        @pl.when(i == 0)
        def _():
            ds_ref[...] = dsink

        @pl.when(i > 0)
        def _():
            ds_ref[...] += dsink

    blk_a = pl.BlockSpec((WINDOW, aw), lambda i: (i, 0))
    blk_kv = pl.BlockSpec((WINDOW, 2 * kvw), lambda i: (i, 0))
    return pl.pallas_call(
        body, name=name, grid=(s // WINDOW,), in_specs=_attn_specs(aw, uw, kvw) + [blk_a, blk_a],
        out_specs=(blk_a, blk_kv, blk_kv, pl.BlockSpec((1, nq), lambda i: (0, 0))),
        out_shape=(jax.ShapeDtypeStruct((s, aw), BF16), jax.ShapeDtypeStruct((s, 2 * kvw), F32),
                   jax.ShapeDtypeStruct((s, 2 * kvw), F32), jax.ShapeDtypeStruct((1, nq), F32)),
        compiler_params=_cparams(("arbitrary",)),
    )(sinks, proj, proj, proj, attn, dattn)


def _assemble_dproj(du, dq, dkv_cur, dkv_prev, name):
    s, uw = du.shape
    aw = dq.shape[1]
    kv2 = dkv_cur.shape[1]
    nb = s // WINDOW

    def body(du_ref, dq_ref, dc_ref, dp_ref, o_ref):
        i = pl.program_id(0)
        o_ref[:, :uw] = du_ref[...].astype(BF16)
        o_ref[:, uw:uw + aw] = dq_ref[...]
        nxt = jnp.where(i < nb - 1, 1.0, 0.0)
        o_ref[:, uw + aw:] = (dc_ref[...] + nxt * dp_ref[...]).astype(BF16)

    return pl.pallas_call(
        body, name=name, grid=(nb,),
        in_specs=[_tok(WINDOW, uw), _tok(WINDOW, aw), _tok(WINDOW, kv2),
                  pl.BlockSpec((WINDOW, kv2), lambda i: (jnp.minimum(i + 1, nb - 1), 0))],
        out_specs=_tok(WINDOW, uw + aw + kv2), out_shape=jax.ShapeDtypeStruct((s, uw + aw + kv2), BF16),
        compiler_params=_cparams(("parallel",)),
    )(du, dq, dkv_cur, dkv_prev)


def _zoh(lr, li, ls, btr, bti):
    dt = jnp.exp(ls)
    mag = jnp.exp(lr * dt)
    ang = li * dt
    ar = mag * jnp.cos(ang)
    ai = mag * jnp.sin(ang)
    den = lr * lr + li * li
    fr = ((ar - 1.0) * lr + ai * li) / den
    fi = (ai * lr - (ar - 1.0) * li) / den
    return ar, ai, fr[None] * btr - fi[None] * bti, fr[None] * bti + fi[None] * btr


def _ssm_prep(lr, li, ls, btr, bti, name):
    def body(lr_ref, li_ref, ls_ref, btr_ref, bti_ref, ar_ref, ai_ref, bbr_ref, bbi_ref):
        ar, ai, bbr, bbi = _zoh(lr_ref[...], li_ref[...], ls_ref[...], btr_ref[...], bti_ref[...])
        ar_ref[...] = ar
        ai_ref[...] = ai
        bbr_ref[...] = bbr
        bbi_ref[...] = bbi

    s2 = jax.ShapeDtypeStruct(lr.shape, F32)
    s3 = jax.ShapeDtypeStruct(btr.shape, F32)
    return pl.pallas_call(body, name=name, out_shape=(s2, s2, s3, s3))(lr, li, ls, btr, bti)


def _ssm_prep_bwd(lr, li, ls, btr, bti, dar, dai, dbbr, dbbi, name):
    def body(lr_ref, li_ref, ls_ref, btr_ref, bti_ref, dar_ref, dai_ref, dbbr_ref, dbbi_ref, o1, o2, o3, o4, o5):
        _, vjp = jax.vjp(_zoh, lr_ref[...], li_ref[...], ls_ref[...], btr_ref[...], bti_ref[...])
        g = vjp((dar_ref[...], dai_ref[...], dbbr_ref[...], dbbi_ref[...]))
        for o, v in zip((o1, o2, o3, o4, o5), g):
            o[...] = v

    s2 = jax.ShapeDtypeStruct(lr.shape, F32)
    s3 = jax.ShapeDtypeStruct(btr.shape, F32)
    return pl.pallas_call(body, name=name, out_shape=(s2, s2, s2, s3, s3))(lr, li, ls, btr, bti, dar, dai, dbbr, dbbi)


def _state_tiles(ref):
    return [ref[:, cb * 128:(cb + 1) * 128] for cb in range(4)]


def _gather_rows(ref_re, ref_im, r, t):
    return jnp.concatenate([ref_re.at[cb][pl.ds(r, t, stride=8), :] for cb in range(4)]
                           + [ref_im.at[cb][pl.ds(r, t, stride=8), :] for cb in range(4)], axis=1)


def _scatter_rows(ref_re, ref_im, r, t, val):
    for cb in range(4):
        ref_re.at[cb][pl.ds(r, t, stride=8), :] = val[:, cb * 128:(cb + 1) * 128]
        ref_im.at[cb][pl.ds(r, t, stride=8), :] = val[:, PSTATES + cb * 128:PSTATES + (cb + 1) * 128]


def _ssm_fwd(proj, bp, cp, a_re, a_im, dvec, uw, name, t=128):
    s = proj.shape[0]
    npc = uw // PIECE
    assert npc == 8 and s % t == 0

    def body(u_ref, bp_ref, cp_ref, ar_ref, ai_ref, d_ref, y_ref, xr_ref, xi_ref, cr_ref, ci_ref):
        i = pl.program_id(0)

        @pl.when(i == 0)
        def _():
            cr_ref[...] = jnp.zeros_like(cr_ref)
            ci_ref[...] = jnp.zeros_like(ci_ref)

        for r in range(npc):
            bu = jnp.dot(u_ref[:, r * PIECE:(r + 1) * PIECE], bp_ref[r], preferred_element_type=F32)
            _scatter_rows(xr_ref, xi_ref, r, t, bu)
        ar = _state_tiles(ar_ref)
        ai = _state_tiles(ai_ref)

        def step(tt, carry):
            xr, xi = carry
            off = pl.multiple_of(tt * 8, 8)
            nr, ni = [], []
            for cb in range(4):
                vr = ar[cb] * xr[cb] - ai[cb] * xi[cb] + xr_ref[cb, pl.ds(off, 8), :]
                vi = ar[cb] * xi[cb] + ai[cb] * xr[cb] + xi_ref[cb, pl.ds(off, 8), :]
                xr_ref[cb, pl.ds(off, 8), :] = vr
                xi_ref[cb, pl.ds(off, 8), :] = vi
                nr.append(vr)
                ni.append(vi)
            return tuple(nr), tuple(ni)

        xr, xi = lax.fori_loop(0, t, step, (tuple(_state_tiles(cr_ref)), tuple(_state_tiles(ci_ref))), unroll=4)
        for cb in range(4):
            cr_ref[:, cb * 128:(cb + 1) * 128] = xr[cb]
            ci_ref[:, cb * 128:(cb + 1) * 128] = xi[cb]
        for r in range(npc):
            xs = _gather_rows(xr_ref, xi_ref, r, t).astype(BF16)
            y_ref[:, r * PIECE:(r + 1) * PIECE] = (
                jnp.dot(xs, cp_ref[r], preferred_element_type=F32)
                + d_ref[:, r * PIECE:(r + 1) * PIECE] * u_ref[:, r * PIECE:(r + 1) * PIECE].astype(F32))

    full3 = lambda shp: pl.BlockSpec(shp, lambda i: (0, 0, 0))
    full2 = lambda shp: pl.BlockSpec(shp, lambda i: (0, 0))
    xs_spec = pl.BlockSpec((4, t * 8, 128), lambda i: (0, i, 0))
    xs_shape = jax.ShapeDtypeStruct((4, s * 8, 128), F32)
    return pl.pallas_call(
        body, name=name, grid=(s // t,),
        in_specs=[pl.BlockSpec((t, uw), lambda i: (i, 0)), full3(bp.shape), full3(cp.shape), full2(a_re.shape), full2(a_im.shape),
                  full2(dvec.shape)],
        out_specs=(pl.BlockSpec((t, uw), lambda i: (i, 0)), xs_spec, xs_spec),
        out_shape=(jax.ShapeDtypeStruct((s, uw), F32), xs_shape, xs_shape),
        scratch_shapes=[pltpu.VMEM((8, PSTATES), F32), pltpu.VMEM((8, PSTATES), F32)],
        compiler_params=_cparams(("arbitrary",), 56),
    )(proj, bp, cp, a_re, a_im, dvec)


def _ssm_bwd(dy, proj, xs_re, xs_im, cpt, bpt, a_re, a_im, dvec, uw, name, t=128):
    s = proj.shape[0]
    npc = uw // PIECE
    nt = s // t
    assert npc == 8 and s % t == 0

    def body(dy_ref, u_ref, xr_ref, xi_ref, hr_ref, hi_ref, cpt_ref, bpt_ref, ar_ref, ai_ref, d_ref,
             du_ref, dbp_ref, dcp_ref, dar_ref, dai_ref, dd_ref, gr_ref, gi_ref, lr_ref, li_ref):
        i = pl.program_id(0)

        @pl.when(i == 0)
        def _():
            lr_ref[...] = jnp.zeros_like(lr_ref)
            li_ref[...] = jnp.zeros_like(li_ref)
            dbp_ref[...] = jnp.zeros_like(dbp_ref)
            dcp_ref[...] = jnp.zeros_like(dcp_ref)
            dar_ref[...] = jnp.zeros_like(dar_ref)
            dai_ref[...] = jnp.zeros_like(dai_ref)
            dd_ref[...] = jnp.zeros_like(dd_ref)

        dyb = dy_ref[...].astype(BF16)
        for r in range(npc):
            gx = jnp.dot(dyb[:, r * PIECE:(r + 1) * PIECE], cpt_ref[r], preferred_element_type=F32)
            _scatter_rows(gr_ref, gi_ref, r, t, gx)
        ar = _state_tiles(ar_ref)
        ai = _state_tiles(ai_ref)

        def adjoint(off, lam_r, lam_i, xpr, xpi, acc_r, acc_i):
            nr, ni, qr, qi = [], [], [], []
            for cb in range(4):
                vr = gr_ref[cb, pl.ds(off, 8), :] + ar[cb] * lam_r[cb] + ai[cb] * lam_i[cb]
                vi = gi_ref[cb, pl.ds(off, 8), :] + ar[cb] * lam_i[cb] - ai[cb] * lam_r[cb]
                gr_ref[cb, pl.ds(off, 8), :] = vr
                gi_ref[cb, pl.ds(off, 8), :] = vi
                nr.append(vr)
                ni.append(vi)
                qr.append(acc_r[cb] + vr * xpr[cb] + vi * xpi[cb])
                qi.append(acc_i[cb] + vi * xpr[cb] - vr * xpi[cb])
            return tuple(nr), tuple(ni), tuple(qr), tuple(qi)

        def step(j, carry):
            lam_r, lam_i, acc_r, acc_i = carry
            tt = t - 1 - j
            off = pl.multiple_of(tt * 8, 8)
            offp = pl.multiple_of(tt * 8 - 8, 8)
            xpr = [xr_ref[cb, pl.ds(offp, 8), :] for cb in range(4)]
            xpi = [xi_ref[cb, pl.ds(offp, 8), :] for cb in range(4)]
            return adjoint(off, lam_r, lam_i, xpr, xpi, acc_r, acc_i)

        zero4 = tuple(jnp.zeros((8, 128), F32) for _ in range(4))
        carry = lax.fori_loop(0, t - 1, step, (tuple(_state_tiles(lr_ref)), tuple(_state_tiles(li_ref)), zero4, zero4), unroll=4)
        has_prev = jnp.where(i < nt - 1, 1.0, 0.0)
        xpr = [hr_ref[cb] * has_prev for cb in range(4)]
        xpi = [hi_ref[cb] * has_prev for cb in range(4)]
        lam_r, lam_i, acc_r, acc_i = adjoint(0, carry[0], carry[1], xpr, xpi, carry[2], carry[3])
        for cb in range(4):
            lr_ref[:, cb * 128:(cb + 1) * 128] = lam_r[cb]
            li_ref[:, cb * 128:(cb + 1) * 128] = lam_i[cb]
            dar_ref[:, cb * 128:(cb + 1) * 128] += acc_r[cb]
            dai_ref[:, cb * 128:(cb + 1) * 128] += acc_i[cb]

        dyv = dy_ref[...]
        uv = u_ref[...]
        dd_ref[...] += jnp.sum(dyv * uv.astype(F32), axis=0, keepdims=True)
        for r in range(npc):
            lam = _gather_rows(gr_ref, gi_ref, r, t).astype(BF16)
            sl = slice(r * PIECE, (r + 1) * PIECE)
            du_ref[:, sl] = jnp.dot(lam, bpt_ref[r], preferred_element_type=F32) + d_ref[:, sl] * dyv[:, sl]
            dbp_ref[r] += lax.dot_general(uv[:, sl], lam, (((0,), (0,)), ((), ())), preferred_element_type=F32)
            xs = _gather_rows(xr_ref, xi_ref, r, t).astype(BF16)
            dcp_ref[r] += lax.dot_general(xs, dyb[:, sl], (((0,), (0,)), ((), ())), preferred_element_type=F32)

    rev = lambda i: (nt - 1 - i, 0)
    full3 = lambda shp: pl.BlockSpec(shp, lambda i: (0, 0, 0))
    full2 = lambda shp: pl.BlockSpec(shp, lambda i: (0, 0))
    xs_spec = pl.BlockSpec((4, t * 8, 128), lambda i: (0, nt - 1 - i, 0))
    halo_spec = pl.BlockSpec((4, 8, 128), lambda i: (0, jnp.maximum((nt - 1 - i) * t - 1, 0), 0))
    st = jax.ShapeDtypeStruct((8, PSTATES), F32)
    return pl.pallas_call(
        body, name=name, grid=(nt,),
        in_specs=[pl.BlockSpec((t, uw), rev), pl.BlockSpec((t, uw), rev), xs_spec, xs_spec, halo_spec, halo_spec,
                  full3(cpt.shape), full3(bpt.shape), full2(a_re.shape), full2(a_im.shape), full2(dvec.shape)],
        out_specs=(pl.BlockSpec((t, uw), rev), full3((npc, PIECE, 2 * PSTATES)), full3((npc, 2 * PSTATES, PIECE)),
                   full2((8, PSTATES)), full2((8, PSTATES)), full2((1, uw))),
        out_shape=(jax.ShapeDtypeStruct((s, uw), F32), jax.ShapeDtypeStruct((npc, PIECE, 2 * PSTATES), F32),
                   jax.ShapeDtypeStruct((npc, 2 * PSTATES, PIECE), F32), st, st, jax.ShapeDtypeStruct((1, uw), F32)),
        scratch_shapes=[pltpu.VMEM((4, t * 8, 128), F32), pltpu.VMEM((4, t * 8, 128), F32),
                        pltpu.VMEM((8, PSTATES), F32), pltpu.VMEM((8, PSTATES), F32)],
        compiler_params=_cparams(("arbitrary",), 56),
    )(dy, proj, xs_re, xs_im, xs_re, xs_im, cpt, bpt, a_re, a_im, dvec)


def _conv3(x, h6, h7, w, b):
    row = lax.broadcasted_iota(jnp.int32, x.shape, 0)
    x1 = jnp.where(row == 0, h7, pltpu.roll(x, 1, 0))
    x2 = jnp.where(row == 0, h6, jnp.where(row == 1, h7, pltpu.roll(x, 2, 0)))
    return ((b + x2 * w[0:1]) + x1 * w[1:2]) + x * w[2:3], x1, x2


def _ffn_tiles(s, f):
    tm = min(256, s)
    tn = f // 4 if (f // 4) % 128 == 0 else f
    return tm, tn


def _conv_glu_fwd(up0, cw, cb, name):
    _, s, f = up0.shape
    tm, tn = _ffn_tiles(s, f)
    hb = tm // 8

    def body(x_ref, h_ref, w_ref, b_ref, a_ref):
        first = jnp.where(pl.program_id(0) > 0, 1.0, 0.0)
        ups = []
        for p in range(2):
            h = h_ref[p].astype(F32) * first
            ups.append(_conv3(x_ref[p].astype(F32), h[6:7], h[7:8], w_ref[p], b_ref[p])[0])
        a_ref[...] = (_gelu(ups[1]) * ups[0]).astype(BF16)

    return pl.pallas_call(
        body, name=name, grid=(s // tm, f // tn),
        in_specs=[pl.BlockSpec((2, tm, tn), lambda i, j: (0, i, j)),
                  pl.BlockSpec((2, 8, tn), lambda i, j: (0, jnp.maximum(i * hb - 1, 0), j)),
                  pl.BlockSpec((2, 3, tn), lambda i, j: (0, 0, j)), pl.BlockSpec((2, 1, tn), lambda i, j: (0, 0, j))],
        out_specs=pl.BlockSpec((tm, tn), lambda i, j: (i, j)), out_shape=jax.ShapeDtypeStruct((s, f), BF16),
        compiler_params=_cparams(("parallel", "parallel")),
    )(up0, up0, cw, cb)


def _ffn_bwd_gate(da, up0, cw, cb, name):
    _, s, f = up0.shape
    tm, tn = _ffn_tiles(s, f)
    hb = tm // 8

    def body(da_ref, x_ref, h_ref, w_ref, b_ref, d_ref, dw_ref, db_ref):
        i = pl.program_id(1)
        first = jnp.where(i > 0, 1.0, 0.0)
        ups, taps = [], []
        for p in range(2):
            h = h_ref[p].astype(F32) * first
            xv = x_ref[p].astype(F32)
            up, x1, x2 = _conv3(xv, h[6:7], h[7:8], w_ref[p], b_ref[p])
            ups.append(up)
            taps.append((x2, x1, xv))
        dav = da_ref[...]
        douts = (dav * _gelu(ups[1]), dav * ups[0] * _gelu_grad(ups[1]))

        @pl.when(i == 0)
        def _():
            dw_ref[...] = jnp.zeros_like(dw_ref)
            db_ref[...] = jnp.zeros_like(db_ref)

        for p in range(2):
            d_ref[p] = douts[p].astype(BF16)
            db_ref[p] += jnp.sum(douts[p], axis=0, keepdims=True)
            for kk in range(3):
                dw_ref[p, kk:kk + 1, :] += jnp.sum(douts[p] * taps[p][kk], axis=0, keepdims=True)

    return pl.pallas_call(
        body, name=name, grid=(f // tn, s // tm),
        in_specs=[pl.BlockSpec((tm, tn), lambda j, i: (i, j)), pl.BlockSpec((2, tm, tn), lambda j, i: (0, i, j)),
                  pl.BlockSpec((2, 8, tn), lambda j, i: (0, jnp.maximum(i * hb - 1, 0), j)),
                  pl.BlockSpec((2, 3, tn), lambda j, i: (0, 0, j)), pl.BlockSpec((2, 1, tn), lambda j, i: (0, 0, j))],
        out_specs=(pl.BlockSpec((2, tm, tn), lambda j, i: (0, i, j)), pl.BlockSpec((2, 3, tn), lambda j, i: (0, 0, j)),
                   pl.BlockSpec((2, 1, tn), lambda j, i: (0, 0, j))),
        out_shape=(jax.ShapeDtypeStruct((2, s, f), BF16), jax.ShapeDtypeStruct((2, 3, f), F32), jax.ShapeDtypeStruct((2, 1, f), F32)),
        compiler_params=_cparams(("parallel", "arbitrary")),
    )(da, up0, up0, cw, cb)


def _conv_bwd(dup, cw, name):
    _, s, f = dup.shape
    tm, tn = _ffn_tiles(s, f)
    hb = tm // 8
    nb = s // tm

    def body(d_ref, h_ref, w_ref, o_ref):
        last = jnp.where(pl.program_id(0) < nb - 1, 1.0, 0.0)
        row = lax.broadcasted_iota(jnp.int32, (tm, tn), 0)
        for p in range(2):
            d = d_ref[p].astype(F32)
            h = h_ref[p].astype(F32) * last
            d1 = jnp.where(row == tm - 1, h[0:1], pltpu.roll(d, tm - 1, 0))
            d2 = jnp.where(row == tm - 1, h[1:2], jnp.where(row == tm - 2, h[0:1], pltpu.roll(d, tm - 2, 0)))
            w = w_ref[p]
            o_ref[p] = (d * w[2:3] + d1 * w[1:2] + d2 * w[0:1]).astype(BF16)

    return pl.pallas_call(
        body, name=name, grid=(nb, f // tn),
        in_specs=[pl.BlockSpec((2, tm, tn), lambda i, j: (0, i, j)),
                  pl.BlockSpec((2, 8, tn), lambda i, j: (0, jnp.minimum((i + 1) * hb, s // 8 - 1), j)),
                  pl.BlockSpec((2, 3, tn), lambda i, j: (0, 0, j))],
        out_specs=pl.BlockSpec((2, tm, tn), lambda i, j: (0, i, j)), out_shape=jax.ShapeDtypeStruct((2, s, f), BF16),
        compiler_params=_cparams(("parallel", "parallel")),
    )(dup, dup, cw)


def _ada_part(c_all, w_ada, name):
    nb, d = c_all.shape
    depth, _, cols = w_ada.shape
    tn = 1024 if cols % 1024 == 0 else cols

    def body(c_ref, w_ref, o_ref, ca_ref):
        cv = c_ref[...]
        ca = cv * _sigmoid(cv)
        ca_ref[...] = ca
        o_ref[...] = jnp.dot(ca.astype(BF16), w_ref[...].astype(BF16), preferred_element_type=F32)

    return pl.pallas_call(
        body, name=name, grid=(depth, cols // tn),
        in_specs=[pl.BlockSpec((nb, d), lambda l, j: (0, 0)), pl.BlockSpec((None, d, tn), lambda l, j: (l, 0, j))],
        out_specs=(pl.BlockSpec((None, nb, tn), lambda l, j: (l, 0, j)), pl.BlockSpec((nb, d), lambda l, j: (0, 0))),
        out_shape=(jax.ShapeDtypeStruct((depth, nb, cols), F32), jax.ShapeDtypeStruct((nb, d), F32)),
        compiler_params=_cparams(("arbitrary", "arbitrary")),
    )(c_all, w_ada)


def _ada_select(gath, b_ada, name):
    depth, n6 = b_ada.shape
    cols = gath.shape[1]

    def body(g_ref, b_ref, o_ref):
        me = 4 * lax.axis_index("x") + 2 * lax.axis_index("y") + lax.axis_index("c")
        for l in range(depth):
            for j in range(n6 // cols):
                row = (2 * j) * (8 * depth) + l * 8 + me
                o_ref[l:l + 1, j * cols:(j + 1) * cols] = g_ref[pl.ds(row, 1), :] + b_ref[l:l + 1, j * cols:(j + 1) * cols]

    return pl.pallas_call(body, name=name, out_shape=jax.ShapeDtypeStruct((depth, n6), F32))(gath, b_ada)


def _wada_grad(ca_t, d_sel, name):
    d, nb = ca_t.shape
    cols = d_sel.shape[1]
    tm = min(256, d)

    def body(a_ref, g_ref, o_ref):
        acc = a_ref[:, 0:1] * g_ref[0:1, :]
        for b in range(1, nb):
            acc = acc + a_ref[:, b:b + 1] * g_ref[b:b + 1, :]
        o_ref[...] = acc

    return pl.pallas_call(
        body, name=name, grid=(d // tm,), in_specs=[pl.BlockSpec((tm, nb), lambda i: (i, 0)), pl.BlockSpec((nb, cols), lambda i: (0, 0))],
        out_specs=pl.BlockSpec((tm, cols), lambda i: (i, 0)), out_shape=jax.ShapeDtypeStruct((d, cols), F32),
        compiler_params=_cparams(("parallel",)),
    )(ca_t, d_sel)


def _block_rows(r, c):
    tr = r
    for cand in (2048, 1024, 512, 256, 128, 64, 32, 16, 8):
        if r % cand == 0 and cand * c * 4 <= MIB:
            tr = cand
            break
    else:
        for cand in (8, 16, 32):
            if r % cand == 0:
                tr = cand
                break
    return tr


def _adamw(w, g, m, v, name):
    r, c = w.shape
    tr = _block_rows(r, c)
    c1 = 1.0 - ADAM_B1 ** ADAM_STEP
    c2 = 1.0 - ADAM_B2 ** ADAM_STEP

    def body(w_ref, g_ref, m_ref, v_ref, d_ref, nm_ref, nv_ref):
        gv = g_ref[...]
        nm = ADAM_B1 * m_ref[...] + (1.0 - ADAM_B1) * gv
        nv = ADAM_B2 * v_ref[...] + (1.0 - ADAM_B2) * (gv * gv)
        d_ref[...] = -ADAM_LR * ((nm / c1) / (jnp.sqrt(nv / c2) + ADAM_EPS) + ADAM_WD * w_ref[...])
        nm_ref[...] = nm
        nv_ref[...] = nv

    spec = pl.BlockSpec((tr, c), lambda i: (i, 0))
    shp = jax.ShapeDtypeStruct((r, c), F32)
    return pl.pallas_call(
        body, name=name, grid=(r // tr,), in_specs=[spec] * 4, out_specs=(spec,) * 3, out_shape=(shp,) * 3,
        compiler_params=_cparams(("parallel",)),
    )(w, g, m, v)


def _sum_slots(x, nslots, name, out_dtype=F32):
    r = x.shape[0] // nslots
    c = x.shape[1]
    tr = _block_rows(r, c)
    nbk = r // tr

    def body(*refs):
        acc = refs[0][...].astype(F32)
        for k in range(1, nslots):
            acc = acc + refs[k][...].astype(F32)
        refs[nslots][...] = acc.astype(out_dtype)

    specs = [pl.BlockSpec((tr, c), functools.partial(lambda k, i: (k * nbk + i, 0), k)) for k in range(nslots)]
    return pl.pallas_call(
        body, name=name, grid=(nbk,), in_specs=specs, out_specs=pl.BlockSpec((tr, c), lambda i: (i, 0)),
        out_shape=jax.ShapeDtypeStruct((r, c), out_dtype), compiler_params=_cparams(("parallel",)),
    )(*([x] * nslots))


def _pick(dim, prefs):
    for p in prefs:
        if dim % p == 0:
            return p
    return dim


def _mm(a, b, m, n, k, name, out_dtype, **kw):
    tm = _pick(m, (512, 256, 128)) if not kw.get("ta") else _pick(m, (1408, 1024, 512, 256, 128))
    tn = _pick(n, (1408, 1152, 1024, 512, 256, 128))
    tk = _pick(k, (1408, 1152, 1024, 512, 256, 128))
    return _matmul(a, b, m=m, n=n, k=k, tm=tm, tn=tn, tk=tk, out_dtype=out_dtype, name=name, **kw)


def _ssm_layout(p):
    g, st = p["lam_re"].shape
    npc = g * st // PSTATES
    lr = p["lam_re"].reshape(npc, PSTATES)
    li = p["lam_im"].reshape(npc, PSTATES)
    ls = jnp.broadcast_to(p["log_step"][:, None], (g, st)).reshape(npc, PSTATES)
    btr = jnp.transpose(p["ssm_b_re"], (2, 0, 1)).reshape(SSM_GROUP, npc, PSTATES)
    bti = jnp.transpose(p["ssm_b_im"], (2, 0, 1)).reshape(SSM_GROUP, npc, PSTATES)
    return lr, li, ls, btr, bti


def _ssm_pieces(bbr, bbi, c_re, c_im):
    npc = bbr.shape[1]
    gl = PSTATES // STATE
    eye = jnp.eye(gl, dtype=bool)

    def b_piece(bb):
        t = jnp.transpose(bb.reshape(SSM_GROUP, npc, gl, STATE), (1, 2, 0, 3))
        full = jnp.where(eye[None, :, None, :, None], t[:, :, :, None, :], 0.0)
        return full.reshape(npc, gl * SSM_GROUP, PSTATES)

    def c_piece(cc):
        t = jnp.transpose(cc.reshape(npc, gl, SSM_GROUP, STATE), (0, 1, 3, 2))
        full = jnp.where(eye[None, :, None, :, None], t[:, :, :, None, :], 0.0)
        return full.reshape(npc, PSTATES, gl * SSM_GROUP)

    bp = jnp.concatenate([b_piece(bbr), b_piece(bbi)], axis=2).astype(BF16)
    cp = jnp.concatenate([c_piece(c_re), c_piece(-c_im)], axis=1).astype(BF16)
    return bp, cp, jnp.swapaxes(bp, 1, 2), jnp.swapaxes(cp, 1, 2)


def _ssm_unpieces(dbp, dcp):
    npc = dbp.shape[0]
    gl = PSTATES // STATE
    idx = jnp.arange(gl)

    def b_diag(x):
        d = x.reshape(npc, gl, SSM_GROUP, gl, STATE)[:, idx, :, idx, :]
        return jnp.transpose(d, (2, 1, 0, 3)).reshape(SSM_GROUP, npc, PSTATES)

    def c_diag(x):
        d = x.reshape(npc, gl, STATE, gl, SSM_GROUP)[:, idx, :, idx, :]
        return jnp.transpose(d, (1, 0, 3, 2)).reshape(npc * gl, SSM_GROUP, STATE)

    return b_diag(dbp[:, :, :PSTATES]), b_diag(dbp[:, :, PSTATES:]), c_diag(dcp[:, :PSTATES, :]), -c_diag(dcp[:, PSTATES:, :])


def _layer_fwd(l, x, ada6, w, p):
    s, d = x.shape
    sh_m, sc_m, gt_m, sh_f, sc_f, gt_f = ada6
    uw = w["w_glu"].shape[0]
    aw = w["w_out"].shape[0] - uw
    ncol = w["w_in"].shape[1]
    f = w["w_down"].shape[0]
    row = lambda v: v.reshape(1, -1)
    n = lambda t: f"l{l}_{t}"

    h = _pre_fwd(x, row(p["g_pre_mix"]), sc_m, sh_m, n("pre_mix"))
    proj = _mm(h, w["w_in"], s, ncol, d, n("proj"), BF16)
    attn = _attn_fwd(proj, p["attn_sinks"], aw, uw, n("attn_fwd"))
    zin = _ssm_layout(p)
    a_re, a_im, bbr, bbi = _ssm_prep(*zin, n("ssm_prep"))
    bp, cp, bpt, cpt = _ssm_pieces(bbr, bbi, p["ssm_c_re"], p["ssm_c_im"])
    dvec = p["ssm_d"].reshape(1, uw)
    y, xs_re, xs_im = _ssm_fwd(proj, bp, cp, a_re, a_im, dvec, uw, n("ssm_fwd"))
    z = _gelu_fwd(y, n("gelu_fwd"))
    gl = _mm(z, w["w_glu"], s, uw, uw, n("glu"), F32)
    merged = _merge_fwd(attn, y, gl, row(p["g_attn_out"]), row(p["g_ssm_out"]), n("merge_fwd"))
    mix = _mm(merged, w["w_out"], s, d, aw + uw, n("out_proj"), F32)
    x1 = _post_fwd(x, mix, row(p["g_post_mix"]), gt_m, n("post_mix"))

    h2 = _pre_fwd(x1, row(p["g_pre_ffn"]), sc_f, sh_f, n("pre_ffn"))
    up0 = _mm(h2, w["w_up"], s, 2 * f, d, n("up_proj"), BF16, b_stack=w["w_up"].shape[0], o_stack=2)
    cw2 = jnp.transpose(p["conv_w"].reshape(3, 2, f), (1, 0, 2))
    cb2 = p["conv_b"].reshape(2, 1, f)
    act = _conv_glu_fwd(up0, cw2, cb2, n("conv_glu"))
    ff = _mm(act, w["w_down"], s, d, f, n("down_proj"), F32)
    x2 = _post_fwd(x1, ff, row(p["g_post_ffn"]), gt_f, n("post_ffn"))
    saved = dict(x=x, h=h, proj=proj, attn=attn, zin=zin, a_re=a_re, a_im=a_im, bpt=bpt, cpt=cpt, dvec=dvec, y=y, xs_re=xs_re,
                 xs_im=xs_im, z=z, gl=gl, merged=merged, mix=mix, x1=x1, h2=h2, up0=up0, cw2=cw2, cb2=cb2, act=act, ff=ff)
    return x2, saved


def _layer_bwd(l, dx2, ada6, w, p, sv):
    s, d = dx2.shape
    sh_m, sc_m, gt_m, sh_f, sc_f, gt_f = ada6
    uw = w["w_glu"].shape[0]
    aw = w["w_out"].shape[0] - uw
    ncol = w["w_in"].shape[1]
    f = w["w_down"].shape[0]
    nst = w["w_up"].shape[0]
    row = lambda v: v.reshape(1, -1)
    n = lambda t: f"l{l}_{t}"
    gw, gs = {}, {}

    dff, dgt_f, gs["g_post_ffn"] = _post_bwd(dx2, sv["ff"], row(p["g_post_ffn"]), gt_f, n("post_ffn_bwd"))
    gw["w_down"] = _mm(sv["act"], dff, f, d, s, n("down_dw"), BF16, ta=True)
    dact = _mm(dff, w["w_down"], s, f, d, n("down_dx"), F32, tb=True)
    dup, dcw2, dcb2 = _ffn_bwd_gate(dact, sv["up0"], sv["cw2"], sv["cb2"], n("ffn_gate_bwd"))
    gs["conv_w"] = jnp.transpose(dcw2, (1, 0, 2)).reshape(3, 2 * f)
    gs["conv_b"] = dcb2.reshape(2 * f)
    dup0 = _conv_bwd(dup, sv["cw2"], n("conv_bwd"))
    gw["w_up"] = _mm(sv["h2"], dup0, d, 2 * f, s, n("up_dw"), BF16, ta=True, b_stack=2, o_stack=nst)
    dh2 = _mm(dup0, w["w_up"], s, d, 2 * f, n("up_dx"), F32, tb=True, a_stack=2, b_stack=nst)
    dx1, dsh_f, dsc_f, gs["g_pre_ffn"] = _pre_bwd(dx2, dh2, sv["x1"], row(p["g_pre_ffn"]), sc_f, n("pre_ffn_bwd"))

    dmix, dgt_m, gs["g_post_mix"] = _post_bwd(dx1, sv["mix"], row(p["g_post_mix"]), gt_m, n("post_mix_bwd"))
    gw["w_out"] = _mm(sv["merged"], dmix, aw + uw, d, s, n("out_dw"), BF16, ta=True)
    dmerged = _mm(dmix, w["w_out"], s, aw + uw, d, n("out_dx"), F32, tb=True)
    dattn, dgl, dzd, gs["g_attn_out"], gs["g_ssm_out"] = _merge_bwd(
        dmerged, sv["attn"], sv["y"], sv["gl"], row(p["g_attn_out"]), row(p["g_ssm_out"]), n("merge_bwd"))
    gw["w_glu"] = _mm(sv["z"], dgl, uw, uw, s, n("glu_dw"), BF16, ta=True)
    dz2 = _mm(dgl, w["w_glu"], s, uw, uw, n("glu_dx"), F32, tb=True)
    dy = _gelu_bwd(dzd, dz2, sv["y"], n("gelu_bwd"))
    du, dbp, dcp, dar, dai, dd = _ssm_bwd(dy, sv["proj"], sv["xs_re"], sv["xs_im"], sv["cpt"], sv["bpt"], sv["a_re"], sv["a_im"],
                                          sv["dvec"], uw, n("ssm_bwd"))
    dq, dkv_c, dkv_p, dsinks = _attn_bwd(sv["proj"], p["attn_sinks"], sv["attn"], dattn, aw, uw, n("attn_bwd"))
    dproj = _assemble_dproj(du, dq, dkv_c, dkv_p, n("dproj"))
    gw["w_in"] = _mm(sv["h"], dproj, d, ncol, s, n("in_dw"), BF16, ta=True)
    dh = _mm(dproj, w["w_in"], s, d, ncol, n("in_dx"), F32, tb=True)
    dx0, dsh_m, dsc_m, gs["g_pre_mix"] = _pre_bwd(dx1, dh, sv["x"], row(p["g_pre_mix"]), sc_m, n("pre_mix_bwd"))

    dbbr, dbbi, dc_re, dc_im = _ssm_unpieces(dbp, dcp)
    dlr, dli, dls, dbtr, dbti = _ssm_prep_bwd(*sv["zin"], dar, dai, dbbr, dbbi, n("ssm_prep_bwd"))
    g, st = p["lam_re"].shape
    gs["lam_re"] = dlr.reshape(g, st)
    gs["lam_im"] = dli.reshape(g, st)
    gs["log_step"] = jnp.sum(dls.reshape(g, st), axis=1)
    gs["ssm_b_re"] = jnp.transpose(dbtr.reshape(SSM_GROUP, g, st), (1, 2, 0))
    gs["ssm_b_im"] = jnp.transpose(dbti.reshape(SSM_GROUP, g, st), (1, 2, 0))
    gs["ssm_c_re"] = dc_re
    gs["ssm_c_im"] = dc_im
    gs["ssm_d"] = dd.reshape(p["ssm_d"].shape)
    gs["attn_sinks"] = dsinks.reshape(-1)
    gs["b_ada"] = jnp.concatenate([dsh_m, dsc_m, dgt_m, dsh_f, dsc_f, dgt_f], axis=1).reshape(-1)
    for key in ("g_post_ffn", "g_pre_ffn", "g_post_mix", "g_attn_out", "g_ssm_out", "g_pre_mix"):
        gs[key] = gs[key].reshape(-1)
    return dx0, gw, gs


def _local_step(x, tgt, ada, wl, pl_small):
    d = x.shape[1]
    depth = len(wl)
    ada6 = [[ada[l:l + 1, k * d:(k + 1) * d] for k in range(6)] for l in range(depth)]
    saved = []
    h = x
    for l in range(depth):
        h, sv = _layer_fwd(l, h, ada6[l], wl[l], pl_small[l])
        saved.append(sv)
    dy, lsum = _loss_head(h, tgt, "loss_head")
    loss = 0.5 * lsum[0, 0] / d
    gws, gss = [None] * depth, [None] * depth
    dx = dy
    for l in reversed(range(depth)):
        dx, gws[l], gss[l] = _layer_bwd(l, dx, ada6[l], wl[l], pl_small[l], saved[l])
    return loss, dx, gws, gss


_ANY = pl.BlockSpec(memory_space=pl.ANY)


def _mesh_pos():
    return lax.axis_index("x"), lax.axis_index("y"), lax.axis_index("c")


def _other_chips(x, y):
    return [(1 - x, y), (x, 1 - y), (1 - x, 1 - y)]


def _remote(src, dst, send_sems, recv_sems, k, to):
    return pltpu.make_async_remote_copy(src_ref=src, dst_ref=dst, send_sem=send_sems.at[k], recv_sem=recv_sems.at[k],
                                        device_id=to, device_id_type=MESH)


def _allgather8(xs, name):
    m, n = xs.shape

    def body(x_ref, out_ref, send_sems, recv_sems, local_sem):
        x, y, c = _mesh_pos()
        me, sibling = (x, y, c), (x, y, 1 - c)
        chips = _other_chips(x, y)

        def rows(px, py, pc):
            return out_ref.at[pl.ds((4 * px + 2 * py + pc) * m, m), :]

        def copy(k, block, to, src=None):
            return _remote(rows(*block) if src is None else src, rows(*block), send_sems, recv_sems, k, to)

        mine = pltpu.make_async_copy(x_ref, rows(*me), local_sem)
        mine.start()
        first = [copy(0, me, sibling, src=x_ref)]
        first += [copy(1 + j, me, (*chip, c), src=x_ref) for j, chip in enumerate(chips)]
        for cp in first:
            cp.start()
        passed = [copy(4 + j, (*chip, c), sibling) for j, chip in enumerate(chips)]
        for j, chip in enumerate(chips):
            copy(1 + j, (*chip, c), me).wait_recv()
            passed[j].start()
        copy(0, sibling, me).wait_recv()
        for j, chip in enumerate(chips):
            copy(4 + j, (*chip, 1 - c), me).wait_recv()
        for cp in first + passed:
            cp.wait_send()
        mine.wait()

    return pl.pallas_call(
        body, name=name, out_shape=jax.ShapeDtypeStruct((8 * m, n), xs.dtype), in_specs=[_ANY], out_specs=_ANY,
        scratch_shapes=[pltpu.SemaphoreType.DMA((7,)), pltpu.SemaphoreType.DMA((7,)), pltpu.SemaphoreType.DMA],
    )(xs)


def _half_rows(ref_rows, half, align):
    h = ref_rows // 2
    return pl.ds(pl.multiple_of(half * h, align), h)


def _allgather_chips(shards, name):
    nw = len(shards)

    def body(*refs):
        ins, outs = refs[:nw], refs[nw:2 * nw]
        send_sems, recv_sems, local_sems = refs[2 * nw:]
        x, y, c = _mesh_pos()
        mine = 2 * x + y
        sibling = (x, y, 1 - c)
        chips = _other_chips(x, y)

        def blk(w, chip_idx, half):
            return outs[w].at[chip_idx, _half_rows(shards[w].shape[0], half, 16), :]

        local = [pltpu.make_async_copy(ins[w], outs[w].at[mine], local_sems.at[w]) for w in range(nw)]
        for cp in local:
            cp.start()
        sends = []
        for w in range(nw):
            for q, chip in enumerate(chips):
                src = ins[w].at[_half_rows(shards[w].shape[0], c, 16), :]
                sends.append(_remote(src, blk(w, mine, c), send_sems, recv_sems, 6 * w + q, (*chip, c)))
                sends[-1].start()
        for w in range(nw):
            for q, chip in enumerate(chips):
                landed = blk(w, 2 * chip[0] + chip[1], c)
                _remote(landed, landed, send_sems, recv_sems, 6 * w + q, (x, y, c)).wait_recv()
                sends.append(_remote(landed, landed, send_sems, recv_sems, 6 * w + 3 + q, sibling))
                sends[-1].start()
        for w in range(nw):
            for q, chip in enumerate(chips):
                other = blk(w, 2 * chip[0] + chip[1], 1 - c)
                _remote(other, other, send_sems, recv_sems, 6 * w + 3 + q, (x, y, c)).wait_recv()
        for cp in sends:
            cp.wait_send()
        for cp in local:
            cp.wait()

    return pl.pallas_call(
        body, name=name, out_shape=[jax.ShapeDtypeStruct((4,) + s.shape, s.dtype) for s in shards],
        in_specs=[_ANY] * nw, out_specs=[_ANY] * nw,
        scratch_shapes=[pltpu.SemaphoreType.DMA((6 * nw,)), pltpu.SemaphoreType.DMA((6 * nw,)), pltpu.SemaphoreType.DMA((nw,))],
    )(*shards)


def _exchange_halves(gs, name):
    nw = len(gs)

    def body(*refs):
        ins, outs = refs[:nw], refs[nw:2 * nw]
        send_sems, recv_sems = refs[2 * nw:]
        x, y, c = _mesh_pos()
        cps = []
        for w in range(nw):
            src = ins[w].at[:, _half_rows(gs[w].shape[1], 1 - c, 16), :]
            cps.append(_remote(src, outs[w], send_sems, recv_sems, w, (x, y, 1 - c)))
            cps[-1].start()
        for cp in cps:
            cp.wait_recv()
        for cp in cps:
            cp.wait_send()

    return pl.pallas_call(
        body, name=name, out_shape=[jax.ShapeDtypeStruct((4, g.shape[1] // 2, g.shape[2]), g.dtype) for g in gs],
        in_specs=[_ANY] * nw, out_specs=[_ANY] * nw,
        scratch_shapes=[pltpu.SemaphoreType.DMA((nw,)), pltpu.SemaphoreType.DMA((nw,))],
    )(*gs)


def _add_half(g, recv, cidx, name):
    _, r, c = g.shape
    h = r // 2
    tr = _block_rows(h, c)
    nbh = h // tr
    assert tr % 16 == 0

    def body(c_ref, g_ref, r_ref, o_ref):
        o_ref[...] = (g_ref[...].astype(F32) + r_ref[...].astype(F32)).astype(BF16)

    grid_spec = pltpu.PrefetchScalarGridSpec(
        num_scalar_prefetch=1, grid=(4, nbh),
        in_specs=[pl.BlockSpec((None, tr, c), lambda s, i, cr: (s, cr[0] * nbh + i, 0)),
                  pl.BlockSpec((None, tr, c), lambda s, i, cr: (s, i, 0))],
        out_specs=pl.BlockSpec((None, tr, c), lambda s, i, cr: (s, i, 0)))
    return pl.pallas_call(
        body, name=name, grid_spec=grid_spec, out_shape=jax.ShapeDtypeStruct((4, h, c), BF16),
        compiler_params=_cparams(("parallel", "parallel")),
    )(cidx, g, recv)


def _scatter_chips(ps, name):
    nw = len(ps)

    def body(*refs):
        ins, outs = refs[:nw], refs[nw:2 * nw]
        send_sems, recv_sems, local_sems = refs[2 * nw:]
        x, y, c = _mesh_pos()
        mine = 2 * x + y
        chips = _other_chips(x, y)
        local = [pltpu.make_async_copy(ins[w].at[mine], outs[w].at[mine], local_sems.at[w]) for w in range(nw)]
        for cp in local:
            cp.start()
        sends = []
        for w in range(nw):
            for q, chip in enumerate(chips):
                sends.append(_remote(ins[w].at[2 * chip[0] + chip[1]], outs[w].at[mine], send_sems, recv_sems, 3 * w + q, (*chip, c)))
                sends[-1].start()
        for w in range(nw):
            for q, chip in enumerate(chips):
                landed = outs[w].at[2 * chip[0] + chip[1]]
                _remote(landed, landed, send_sems, recv_sems, 3 * w + q, (x, y, c)).wait_recv()
        for cp in sends:
            cp.wait_send()
        for cp in local:
            cp.wait()

    return pl.pallas_call(
        body, name=name, out_shape=[jax.ShapeDtypeStruct(p.shape, p.dtype) for p in ps],
        in_specs=[_ANY] * nw, out_specs=[_ANY] * nw,
        scratch_shapes=[pltpu.SemaphoreType.DMA((3 * nw,)), pltpu.SemaphoreType.DMA((3 * nw,)), pltpu.SemaphoreType.DMA((nw,))],
    )(*ps)


def _share_halves(reds, depth, name):
    nw = len(reds)
    flat = [reds[w][l] for w in range(nw) for l in range(depth)]
    nf = len(flat)

    def body(*refs):
        ins, outs = refs[:nf], refs[nf:nf + nw]
        send_sems, recv_sems, local_sems = refs[nf + nw:]
        x, y, c = _mesh_pos()
        local, sends = [], []
        for w in range(nw):
            for l in range(depth):
                k = w * depth + l
                r = 2 * flat[k].shape[0]
                dst = outs[w].at[l, _half_rows(r, c, 8), :]
                local.append(pltpu.make_async_copy(ins[k], dst, local_sems.at[k]))
                local[-1].start()
                sends.append(_remote(ins[k], dst, send_sems, recv_sems, k, (x, y, 1 - c)))
                sends[-1].start()
        for w in range(nw):
            for l in range(depth):
                k = w * depth + l
                other = outs[w].at[l, _half_rows(2 * flat[k].shape[0], 1 - c, 8), :]
                _remote(other, other, send_sems, recv_sems, k, (x, y, c)).wait_recv()
        for cp in sends:
            cp.wait_send()
        for cp in local:
            cp.wait()

    return pl.pallas_call(
        body, name=name,
        out_shape=[jax.ShapeDtypeStruct((depth, 2 * reds[w][0].shape[0], reds[w][0].shape[1]), F32) for w in range(nw)],
        in_specs=[_ANY] * nf, out_specs=[_ANY] * nw,
        scratch_shapes=[pltpu.SemaphoreType.DMA((nf,)), pltpu.SemaphoreType.DMA((nf,)), pltpu.SemaphoreType.DMA((nf,))],
    )(*flat)


_BIG = ("w_in", "w_glu", "w_out", "w_up", "w_down")
_SMALL = ("b_ada", "g_pre_mix", "g_post_mix", "attn_sinks", "lam_re", "lam_im", "log_step", "ssm_b_re", "ssm_b_im", "ssm_c_re",
          "ssm_c_im", "ssm_d", "g_attn_out", "g_ssm_out", "g_pre_ffn", "g_post_ffn", "conv_b")
_WEIGHTS = ("w_ada", "b_ada", "g_pre_mix", "g_post_mix", "w_in", "attn_sinks", "lam_re", "lam_im", "log_step", "ssm_b_re", "ssm_b_im",
            "ssm_c_re", "ssm_c_im", "ssm_d", "w_glu", "g_attn_out", "g_ssm_out", "w_out", "g_pre_ffn", "g_post_ffn", "w_up", "conv_w",
            "conv_b", "w_down")
_LANES = 1024


def _pack(parts, rows_to):
    flat = jnp.concatenate([p.reshape(-1) for p in parts])
    per = _LANES * rows_to
    total = -(-flat.shape[0] // per) * per
    return jnp.pad(flat, (0, total - flat.shape[0])).reshape(total // _LANES, _LANES)


def _unpack(packed, shapes):
    flat = packed.reshape(-1)
    out, off = [], 0
    for shp in shapes:
        size = math.prod(shp)
        out.append(flat[off:off + size].reshape(shp))
        off += size
    return out


def kernel(x, c, w_ada, b_ada, g_pre_mix, g_post_mix, w_in, attn_sinks, lam_re, lam_im, log_step, ssm_b_re, ssm_b_im, ssm_c_re, ssm_c_im, ssm_d, w_glu, g_attn_out, g_ssm_out, w_out, g_pre_ffn, g_post_ffn, w_up, conv_w, conv_b, w_down, loss_target, m_w_ada, m_b_ada, m_g_pre_mix, m_g_post_mix, m_w_in, m_attn_sinks, m_lam_re, m_lam_im, m_log_step, m_ssm_b_re, m_ssm_b_im, m_ssm_c_re, m_ssm_c_im, m_ssm_d, m_w_glu, m_g_attn_out, m_g_ssm_out, m_w_out, m_g_pre_ffn, m_g_post_ffn, m_w_up, m_conv_w, m_conv_b, m_w_down, v_w_ada, v_b_ada, v_g_pre_mix, v_g_post_mix, v_w_in, v_attn_sinks, v_lam_re, v_lam_im, v_log_step, v_ssm_b_re, v_ssm_b_im, v_ssm_c_re, v_ssm_c_im, v_ssm_d, v_w_glu, v_g_attn_out, v_g_ssm_out, v_w_out, v_g_pre_ffn, v_g_post_ffn, v_w_up, v_conv_w, v_conv_b, v_w_down):
    given = dict(locals())
    wts = {n: given[n] for n in _WEIGHTS}
    mom = {n: given["m_" + n] for n in _WEIGHTS}
    var = {n: given["v_" + n] for n in _WEIGHTS}
    depth, d, ada_cols = w_ada.shape
    nchips = 4
    xi, yi, ci = lax.axis_index("x"), lax.axis_index("y"), lax.axis_index("c")
    chip = 2 * xi + yi
    cidx = jnp.reshape(ci, (1,)).astype(jnp.int32)

    cw_cols = conv_w.shape[2]
    vec = _pack([c, conv_w], 8)
    g1 = _allgather8(vec, "ag_cond").reshape(8, -1)
    c_all = g1[:, :d]
    cw_sh = g1[0::2, d:d + depth * 3 * cw_cols].reshape(nchips, depth, 3, cw_cols)
    conv_w_full = jnp.transpose(cw_sh, (1, 2, 0, 3)).reshape(depth, 3, nchips * cw_cols)

    ada_part, c_act = _ada_part(c_all, w_ada, "ada_part")
    g2 = _allgather8(ada_part.reshape(depth * 8, ada_cols), "ag_ada")
    ada = _ada_select(g2, b_ada, "ada_select")

    shards = [wts[n][l].astype(BF16) for l in range(depth) for n in _BIG]
    gathered = _allgather_chips(shards, "ag_weights")
    wl, ps = [], []
    for l in range(depth):
        gw = dict(zip(_BIG, gathered[l * len(_BIG):(l + 1) * len(_BIG)]))
        w_in_full = jnp.transpose(gw["w_in"], (1, 0, 2)).reshape(d, -1)
        uw = nchips * gw["w_glu"].shape[1]
        split = w_in_full.shape[1] - uw
        wl.append(dict(
            w_in=jnp.concatenate([w_in_full[:, split:], w_in_full[:, :split]], axis=1),
            w_glu=gw["w_glu"].reshape(-1, gw["w_glu"].shape[2]), w_out=gw["w_out"].reshape(-1, gw["w_out"].shape[2]),
            w_up=gw["w_up"], w_down=gw["w_down"].reshape(-1, gw["w_down"].shape[2])))
        small = {n: wts[n][l] for n in _SMALL if n != "b_ada"}
        small["conv_w"] = conv_w_full[l]
        ps.append(small)

    loss_sum, grad_x, gws, gss = _local_step(x[0], loss_target[0], ada, wl, ps)
    loss = lax.psum(loss_sum, ("x", "y", "c"))

    stacks = []
    for l in range(depth):
        for n in _BIG:
            g = gws[l][n]
            if n == "w_in":
                uw = wl[l]["w_glu"].shape[0]
                g = jnp.concatenate([g[:, uw:], g[:, :uw]], axis=1)
                g = jnp.transpose(g.reshape(d, nchips, -1), (1, 0, 2))
            elif n != "w_up":
                g = g.reshape(nchips, g.shape[0] // nchips, g.shape[1])
            stacks.append(g)
    from_sibling = _exchange_halves(stacks, "rs_sibling")
    partials = [_add_half(stacks[k], from_sibling[k], cidx, f"rs_add_sibling_{k}") for k in range(len(stacks))]
    from_chips = _scatter_chips(partials, "rs_chips")
    reduced = [_sum_slots(t.reshape(-1, t.shape[2]), nchips, f"rs_sum_chips_{k}") for k, t in enumerate(from_chips)]
    nb = len(_BIG)
    big_grads = dict(zip(_BIG, _share_halves([[reduced[l * nb + w] for l in range(depth)] for w in range(nb)], depth, "rs_share")))

    small_parts = [jnp.stack([gss[l][n] for l in range(depth)]) for n in _SMALL]
    pack_small = _pack(small_parts, 8)
    pack_cw = _pack([jnp.stack([gss[l]["conv_w"] for l in range(depth)])], 8)
    rows_small = pack_small.shape[0]
    mine = jnp.concatenate([pack_small, pack_cw], axis=0)
    g3 = _allgather8(mine, "ag_small")
    total = _sum_slots(g3, 8, "sum_small")
    grads = dict(big_grads)
    for n, v in zip(_SMALL, _unpack(total[:rows_small], [wts[n].shape for n in _SMALL])):
        grads[n] = v
    conv_w_grad = _unpack(total[rows_small:], [(depth, 3, nchips * cw_cols)])[0]
    grads["conv_w"] = lax.dynamic_slice_in_dim(conv_w_grad, chip * cw_cols, cw_cols, axis=2)

    d_ada_all = g3.reshape(8, -1)[:, :depth * 6 * d].reshape(8, depth, 6 * d)
    ca_t = jnp.transpose(c_act)
    grads["w_ada"] = jnp.stack([
        _wada_grad(ca_t, lax.dynamic_slice_in_dim(d_ada_all[:, l], chip * ada_cols, ada_cols, axis=1), f"w_ada_grad_{l}")
        for l in range(depth)])

    delta, new_m, new_v = {}, {}, {}
    for n in ("w_ada",) + _BIG + ("conv_w",):
        shp = wts[n].shape
        two_d = lambda t: t.reshape(-1, shp[-1])
        dl, nm, nv = _adamw(two_d(wts[n]), two_d(grads[n]), two_d(mom[n]), two_d(var[n]), f"adamw_{n}")
        delta[n], new_m[n], new_v[n] = dl.reshape(shp), nm.reshape(shp), nv.reshape(shp)
    packs = [_pack([t[n] for n in _SMALL], 8) for t in (wts, mom, var)]
    outs = _adamw(packs[0], total[:rows_small], packs[1], packs[2], "adamw_small")
    shapes = [wts[n].shape for n in _SMALL]
    for dst, packed in zip((delta, new_m, new_v), outs):
        for n, v in zip(_SMALL, _unpack(packed, shapes)):
            dst[n] = v

    return (loss, grad_x[None], *[grads[n] for n in _WEIGHTS], *[delta[n] for n in _WEIGHTS],
            *[new_m[n] for n in _WEIGHTS], *[new_v[n] for n in _WEIGHTS])
```

```python
import functools
import math

import jax
import jax.numpy as jnp
from jax import lax
from jax.experimental import pallas as pl
from jax.experimental.pallas import tpu as pltpu

F32 = jnp.float32
BF16 = jnp.bfloat16
EPS = 1e-6
NEG = -1e30
WINDOW = 128
HEAD_DIM = 64
KV_RATIO = 8
SSM_GROUP = 16
STATE = 64
PIECE = 128
PSTATES = 512
DEPTH = 2
ADAM_LR, ADAM_B1, ADAM_B2, ADAM_EPS, ADAM_WD, ADAM_STEP = 0.001, 0.9, 0.999, 1e-08, 0.01, 10
MIB = 1024 * 1024
_MATMUL_VMEM_BUDGET = 40 * MIB
MESH = pl.DeviceIdType.MESH


def _cparams(sem=None, vmem_mib=48):
    return pltpu.CompilerParams(dimension_semantics=sem, vmem_limit_bytes=vmem_mib * MIB)


def _gelu(x):
    c = math.sqrt(2.0 / math.pi)
    return 0.5 * x * (1.0 + jnp.tanh(c * (x + 0.044715 * (x * x * x))))


def _gelu_and_grad(x):
    c = math.sqrt(2.0 / math.pi)
    x2 = x * x
    t = jnp.tanh(c * (x + 0.044715 * (x2 * x)))
    half = 0.5 * (1.0 + t)
    return x * half, half + 0.5 * x * (1.0 - t * t) * c * (1.0 + 3.0 * 0.044715 * x2)


def _gelu_grad(x):
    return _gelu_and_grad(x)[1]


def _sigmoid(x):
    return 1.0 / (1.0 + jnp.exp(-x))


def _matmul(a, b, *, m, n, k, tm, tn, tk, out_dtype, name, ta=False, tb=False, a_stack=0, b_stack=0, o_stack=0):
    assert m % tm == 0 and n % tn == 0 and k % tk == 0, (name, m, n, k, tm, tn, tk)
    nk = k // tk

    if a_stack:
        assert not ta and (k // a_stack) % tk == 0
        per = (k // a_stack) // tk
        a_spec = pl.BlockSpec((None, tm, tk), lambda i, j, kk: (kk // per, i, kk % per))
    elif ta:
        a_spec = pl.BlockSpec((tk, tm), lambda i, j, kk: (kk, i))
    else:
        a_spec = pl.BlockSpec((tm, tk), lambda i, j, kk: (i, kk))
    if b_stack and tb:
        perb = (k // b_stack) // tk
        b_spec = pl.BlockSpec((None, tn, tk), lambda i, j, kk: (kk // perb, j, kk % perb))
    elif b_stack:
        perb = (n // b_stack) // tn
        b_spec = pl.BlockSpec((None, tk, tn), lambda i, j, kk: (j // perb, kk, j % perb))
    elif tb:
        b_spec = pl.BlockSpec((tn, tk), lambda i, j, kk: (j, kk))
    else:
        b_spec = pl.BlockSpec((tk, tn), lambda i, j, kk: (kk, j))
    if o_stack:
        pero = (n // o_stack) // tn
        o_spec = pl.BlockSpec((None, tm, tn), lambda i, j, kk: (j // pero, i, j % pero))
        o_shape = jax.ShapeDtypeStruct((o_stack, m, n // o_stack), out_dtype)
    else:
        o_spec = pl.BlockSpec((tm, tn), lambda i, j, kk: (i, j))
        o_shape = jax.ShapeDtypeStruct((m, n), out_dtype)
    dims = (((0 if ta else 1,), (1 if tb else 0,)), ((), ()))

    def body(a_ref, b_ref, o_ref, *acc):
        p = lax.dot_general(a_ref[...].astype(BF16), b_ref[...].astype(BF16), dims, preferred_element_type=F32)
        if nk == 1:
            o_ref[...] = p.astype(o_ref.dtype)
        else:
            acc_ref = acc[0]
            kk = pl.program_id(2)

            @pl.when(kk == 0)
            def _():
                acc_ref[...] = p

            @pl.when(kk > 0)
            def _():
                acc_ref[...] += p

            @pl.when(kk == nk - 1)
            def _():
                o_ref[...] = acc_ref[...].astype(o_ref.dtype)

    return pl.pallas_call(
        body, name=name, grid=(m // tm, n // tn, nk), in_specs=[a_spec, b_spec], out_specs=o_spec, out_shape=o_shape,
        scratch_shapes=[] if nk == 1 else [pltpu.VMEM((tm, tn), F32)],
        compiler_params=_cparams(("parallel", "parallel", "arbitrary"), 56),
    )(a, b)


def _row(d):
    return pl.BlockSpec((1, d), lambda i: (0, 0))


def _tok(tm, d):
    return pl.BlockSpec((tm, d), lambda i: (i, 0))


def _pre_fwd(x, g, sc, sh, name):
    s, d = x.shape
    tm = min(256, s)

    def body(x_ref, g_ref, sc_ref, sh_ref, h_ref):
        xv = x_ref[...]
        r = lax.rsqrt(jnp.mean(xv * xv, axis=-1, keepdims=True) + EPS)
        h_ref[...] = (((xv * r) * g_ref[...]) * (1.0 + sc_ref[...]) + sh_ref[...]).astype(BF16)

    return pl.pallas_call(
        body, name=name, grid=(s // tm,), in_specs=[_tok(tm, d), _row(d), _row(d), _row(d)], out_specs=_tok(tm, d),
        out_shape=jax.ShapeDtypeStruct((s, d), BF16), compiler_params=_cparams(("parallel",)),
    )(x, g, sc, sh)


def _post_fwd(x, o, g, gt, name):
    s, d = x.shape
    tm = min(256, s)

    def body(x_ref, o_ref, g_ref, gt_ref, y_ref):
        ov = o_ref[...]
        r = lax.rsqrt(jnp.mean(ov * ov, axis=-1, keepdims=True) + EPS)
        y_ref[...] = x_ref[...] + (1.0 + gt_ref[...]) * ((ov * r) * g_ref[...])

    return pl.pallas_call(
        body, name=name, grid=(s // tm,), in_specs=[_tok(tm, d), _tok(tm, d), _row(d), _row(d)], out_specs=_tok(tm, d),
        out_shape=jax.ShapeDtypeStruct((s, d), F32), compiler_params=_cparams(("parallel",)),
    )(x, o, g, gt)


def _post_bwd(dxo, o, g, gt, name):
    s, d = o.shape
    tm = min(256, s)

    def body(dx_ref, o_ref, g_ref, gt_ref, do_ref, dgt_ref, dg_ref):
        i = pl.program_id(0)
        dx = dx_ref[...]
        ov = o_ref[...]
        gv = g_ref[...]
        r = lax.rsqrt(jnp.mean(ov * ov, axis=-1, keepdims=True) + EPS)
        oh = ov * r
        dn = dx * (1.0 + gt_ref[...])
        e = dn * gv
        do_ref[...] = (r * (e - oh * jnp.mean(e * oh, axis=-1, keepdims=True))).astype(BF16)
        p_gt = jnp.sum(dx * (oh * gv), axis=0, keepdims=True)
        p_g = jnp.sum(dn * oh, axis=0, keepdims=True)

        @pl.when(i == 0)
        def _():
            dgt_ref[...] = p_gt
            dg_ref[...] = p_g

        @pl.when(i > 0)
        def _():
            dgt_ref[...] += p_gt
            dg_ref[...] += p_g

    row = jax.ShapeDtypeStruct((1, d), F32)
    return pl.pallas_call(
        body, name=name, grid=(s // tm,), in_specs=[_tok(tm, d), _tok(tm, d), _row(d), _row(d)],
        out_specs=(_tok(tm, d), _row(d), _row(d)), out_shape=(jax.ShapeDtypeStruct((s, d), BF16), row, row),
        compiler_params=_cparams(("arbitrary",)),
    )(dxo, o, g, gt)


def _pre_bwd(dres, dh, x, g, sc, name):
    s, d = x.shape
    tm = min(256, s)

    def body(dres_ref, dh_ref, x_ref, g_ref, sc_ref, dx_ref, dsh_ref, dsc_ref, dg_ref):
        i = pl.program_id(0)
        dh_v = dh_ref[...]
        xv = x_ref[...]
        gv = g_ref[...]
        one_sc = 1.0 + sc_ref[...]
        r = lax.rsqrt(jnp.mean(xv * xv, axis=-1, keepdims=True) + EPS)
        xh = xv * r
        e = dh_v * one_sc * gv
        dx_ref[...] = dres_ref[...] + r * (e - xh * jnp.mean(e * xh, axis=-1, keepdims=True))
        p_sh = jnp.sum(dh_v, axis=0, keepdims=True)
        p_sc = jnp.sum(dh_v * (xh * gv), axis=0, keepdims=True)
        p_g = jnp.sum(dh_v * one_sc * xh, axis=0, keepdims=True)

        @pl.when(i == 0)
        def _():
            dsh_ref[...] = p_sh
            dsc_ref[...] = p_sc
            dg_ref[...] = p_g

        @pl.when(i > 0)
        def _():
            dsh_ref[...] += p_sh
            dsc_ref[...] += p_sc
            dg_ref[...] += p_g

    row = jax.ShapeDtypeStruct((1, d), F32)
    return pl.pallas_call(
        body, name=name, grid=(s // tm,), in_specs=[_tok(tm, d), _tok(tm, d), _tok(tm, d), _row(d), _row(d)],
        out_specs=(_tok(tm, d), _row(d), _row(d), _row(d)), out_shape=(jax.ShapeDtypeStruct((s, d), F32), row, row, row),
        compiler_params=_cparams(("arbitrary",)),
    )(dres, dh, x, g, sc)


def _loss_head(y, tgt, name):
    s, d = y.shape
    tm = min(256, s)

    def body(y_ref, t_ref, dy_ref, l_ref):
        i = pl.program_id(0)
        err = y_ref[...] - t_ref[...]
        dy_ref[...] = err * (1.0 / d)
        part = jnp.zeros((1, 128), F32) + jnp.sum(err * err)

        @pl.when(i == 0)
        def _():
            l_ref[...] = part

        @pl.when(i > 0)
        def _():
            l_ref[...] += part

    return pl.pallas_call(
        body, name=name, grid=(s // tm,), in_specs=[_tok(tm, d), _tok(tm, d)],
        out_specs=(_tok(tm, d), pl.BlockSpec((1, 128), lambda i: (0, 0))),
        out_shape=(jax.ShapeDtypeStruct((s, d), F32), jax.ShapeDtypeStruct((1, 128), F32)),
        compiler_params=_cparams(("arbitrary",)),
    )(y, tgt)


def _gelu_fwd(y, name):
    s, u = y.shape
    tm = min(512, s)

    def body(y_ref, z_ref):
        z_ref[...] = _gelu(y_ref[...]).astype(BF16)

    return pl.pallas_call(
        body, name=name, grid=(s // tm,), in_specs=[_tok(tm, u)], out_specs=_tok(tm, u),
        out_shape=jax.ShapeDtypeStruct((s, u), BF16), compiler_params=_cparams(("parallel",)),
    )(y)


def _merge_fwd(attn, y, gl, ga, gs, name):
    s, aw = attn.shape
    uw = y.shape[1]
    tm = min(256, s)

    def body(a_ref, y_ref, gl_ref, ga_ref, gs_ref, m_ref):
        av = a_ref[...]
        ra = lax.rsqrt(jnp.mean(av * av, axis=-1, keepdims=True) + EPS)
        m_ref[:, :aw] = ((av * ra) * ga_ref[...]).astype(BF16)
        ssm = _gelu(y_ref[...]) * _sigmoid(gl_ref[...])
        rs = lax.rsqrt(jnp.mean(ssm * ssm, axis=-1, keepdims=True) + EPS)
        m_ref[:, aw:] = ((ssm * rs) * gs_ref[...]).astype(BF16)

    return pl.pallas_call(
        body, name=name, grid=(s // tm,), in_specs=[_tok(tm, aw), _tok(tm, uw), _tok(tm, uw), _row(aw), _row(uw)],
        out_specs=_tok(tm, aw + uw), out_shape=jax.ShapeDtypeStruct((s, aw + uw), BF16),
        compiler_params=_cparams(("parallel",)),
    )(attn, y, gl, ga, gs)


def _merge_bwd(dm, attn, y, gl, ga, gs, name):
    s, aw = attn.shape
    uw = y.shape[1]
    tm = min(256, s)

    def body(dm_ref, a_ref, y_ref, gl_ref, ga_ref, gs_ref, da_ref, dgl_ref, dz_ref, dga_ref, dgs_ref):
        i = pl.program_id(0)
        av = a_ref[...]
        dma = dm_ref[:, :aw]
        ra = lax.rsqrt(jnp.mean(av * av, axis=-1, keepdims=True) + EPS)
        ah = av * ra
        e = dma * ga_ref[...]
        da_ref[...] = (ra * (e - ah * jnp.mean(e * ah, axis=-1, keepdims=True))).astype(BF16)
        p_ga = jnp.sum(dma * ah, axis=0, keepdims=True)

        z = _gelu(y_ref[...])
        sig = _sigmoid(gl_ref[...])
        ssm = z * sig
        dms = dm_ref[:, aw:]
        rs = lax.rsqrt(jnp.mean(ssm * ssm, axis=-1, keepdims=True) + EPS)
        sh = ssm * rs
        e2 = dms * gs_ref[...]
        dssm = rs * (e2 - sh * jnp.mean(e2 * sh, axis=-1, keepdims=True))
        dz_ref[...] = dssm * sig
        dgl_ref[...] = (dssm * z * sig * (1.0 - sig)).astype(BF16)
        p_gs = jnp.sum(dms * sh, axis=0, keepdims=True)

        @pl.when(i == 0)
        def _():
            dga_ref[...] = p_ga
            dgs_ref[...] = p_gs

        @pl.when(i > 0)
        def _():
            dga_ref[...] += p_ga
            dgs_ref[...] += p_gs

    return pl.pallas_call(
        body, name=name, grid=(s // tm,),
        in_specs=[_tok(tm, aw + uw), _tok(tm, aw), _tok(tm, uw), _tok(tm, uw), _row(aw), _row(uw)],
        out_specs=(_tok(tm, aw), _tok(tm, uw), _tok(tm, uw), _row(aw), _row(uw)),
        out_shape=(jax.ShapeDtypeStruct((s, aw), BF16), jax.ShapeDtypeStruct((s, uw), BF16), jax.ShapeDtypeStruct((s, uw), F32),
                   jax.ShapeDtypeStruct((1, aw), F32), jax.ShapeDtypeStruct((1, uw), F32)),
        compiler_params=_cparams(("arbitrary",)),
    )(dm, attn, y, gl, ga, gs)


def _gelu_bwd(dzd, dz2, y, name):
    s, u = y.shape
    tm = min(512, s)

    def body(a_ref, b_ref, y_ref, o_ref):
        o_ref[...] = (a_ref[...] + b_ref[...]) * _gelu_grad(y_ref[...])

    return pl.pallas_call(
        body, name=name, grid=(s // tm,), in_specs=[_tok(tm, u), _tok(tm, u), _tok(tm, u)], out_specs=_tok(tm, u),
        out_shape=jax.ShapeDtypeStruct((s, u), F32), compiler_params=_cparams(("parallel",)),
    )(dzd, dz2, y)


def _attn_scores(qh, kb, sink, valid):
    s = lax.dot_general(qh, kb, (((1,), (1,)), ((), ())), preferred_element_type=F32) * (HEAD_DIM ** -0.5)
    s = jnp.where(valid, s, NEG)
    m = jnp.maximum(jnp.max(s, axis=-1, keepdims=True), sink)
    e = jnp.exp(s - m)
    esink = jnp.exp(sink - m)
    den = jnp.sum(e, axis=-1, keepdims=True) + esink
    return e / den, esink / den


def _attn_valid(i):
    qi = lax.broadcasted_iota(jnp.int32, (WINDOW, 2 * WINDOW), 0)
    kj = lax.broadcasted_iota(jnp.int32, (WINDOW, 2 * WINDOW), 1)
    return (kj > qi) & (kj <= qi + WINDOW) & ((kj >= WINDOW) | (i > 0))


def _attn_specs(aw, uw, kvw):
    qblk = uw // aw
    kvblk = (uw + aw) // (2 * kvw)
    assert uw % aw == 0 and (uw + aw) % (2 * kvw) == 0
    return [
        pl.BlockSpec(memory_space=pltpu.SMEM),
        pl.BlockSpec((WINDOW, aw), lambda i: (i, qblk)),
        pl.BlockSpec((WINDOW, 2 * kvw), lambda i: (i, kvblk)),
        pl.BlockSpec((WINDOW, 2 * kvw), lambda i: (jnp.maximum(i - 1, 0), kvblk)),
    ]


def _attn_fwd(proj, sinks, aw, uw, name):
    s = proj.shape[0]
    nq = aw // HEAD_DIM
    nkv = nq // KV_RATIO
    kvw = nkv * HEAD_DIM

    def body(sink_ref, q_ref, kvc_ref, kvp_ref, o_ref):
        valid = _attn_valid(pl.program_id(0))
        q = q_ref[...]
        kvc = kvc_ref[...]
        kvp = kvp_ref[...]
        for hk in range(nkv):
            kb = jnp.concatenate([kvp[:, hk * HEAD_DIM:(hk + 1) * HEAD_DIM], kvc[:, hk * HEAD_DIM:(hk + 1) * HEAD_DIM]], axis=0)
            vb = jnp.concatenate([kvp[:, kvw + hk * HEAD_DIM:kvw + (hk + 1) * HEAD_DIM],
                                  kvc[:, kvw + hk * HEAD_DIM:kvw + (hk + 1) * HEAD_DIM]], axis=0)
            for g in range(KV_RATIO):
                hq = hk * KV_RATIO + g
                p, _ = _attn_scores(q[:, hq * HEAD_DIM:(hq + 1) * HEAD_DIM], kb, sink_ref[hq], valid)
                o_ref[:, hq * HEAD_DIM:(hq + 1) * HEAD_DIM] = jnp.dot(p.astype(BF16), vb, preferred_element_type=F32)

    return pl.pallas_call(
        body, name=name, grid=(s // WINDOW,), in_specs=_attn_specs(aw, uw, kvw),
        out_specs=pl.BlockSpec((WINDOW, aw), lambda i: (i, 0)), out_shape=jax.ShapeDtypeStruct((s, aw), F32),
        compiler_params=_cparams(("parallel",)),
    )(sinks, proj, proj, proj)


def _attn_bwd(proj, sinks, attn, dattn, aw, uw, name):
    s = proj.shape[0]
    nq = aw // HEAD_DIM
    nkv = nq // KV_RATIO
    kvw = nkv * HEAD_DIM
    hd = HEAD_DIM

    def body(sink_ref, q_ref, kvc_ref, kvp_ref, o_ref, do_ref, dq_ref, dc_ref, dp_ref, ds_ref):
        i = pl.program_id(0)
        valid = _attn_valid(i)
        q = q_ref[...]
        kvc = kvc_ref[...]
        kvp = kvp_ref[...]
        lane = lax.broadcasted_iota(jnp.int32, (1, nq), 1)
        dsink = jnp.zeros((1, nq), F32)
        for hk in range(nkv):
            kb = jnp.concatenate([kvp[:, hk * hd:(hk + 1) * hd], kvc[:, hk * hd:(hk + 1) * hd]], axis=0)
            vb = jnp.concatenate([kvp[:, kvw + hk * hd:kvw + (hk + 1) * hd], kvc[:, kvw + hk * hd:kvw + (hk + 1) * hd]], axis=0)
            dkb = jnp.zeros((2 * WINDOW, hd), F32)
            dvb = jnp.zeros((2 * WINDOW, hd), F32)
            for g in range(KV_RATIO):
                hq = hk * KV_RATIO + g
                qh = q[:, hq * hd:(hq + 1) * hd]
                p, psink = _attn_scores(qh, kb, sink_ref[hq], valid)
                do_h = do_ref[:, hq * hd:(hq + 1) * hd]
                delta = jnp.sum(do_h.astype(F32) * o_ref[:, hq * hd:(hq + 1) * hd], axis=-1, keepdims=True)
                dpv = lax.dot_general(do_h, vb, (((1,), (1,)), ((), ())), preferred_element_type=F32)
                dsb = (p * (dpv - delta) * (hd ** -0.5)).astype(BF16)
                dq_ref[:, hq * hd:(hq + 1) * hd] = jnp.dot(dsb, kb, preferred_element_type=F32).astype(BF16)
                dkb = dkb + lax.dot_general(dsb, qh, (((0,), (0,)), ((), ())), preferred_element_type=F32)
                dvb = dvb + lax.dot_general(p.astype(BF16), do_h, (((0,), (0,)), ((), ())), preferred_element_type=F32)
                dsink = dsink + jnp.where(lane == hq, -jnp.sum(psink * delta), 0.0)
            dp_ref[:, hk * hd:(hk + 1) * hd] = dkb[:WINDOW]
            dc_ref[:, hk * hd:(hk + 1) * hd] = dkb[WINDOW:]
            dp_ref[:, kvw + hk * hd:kvw + (hk + 1) * hd] = dvb[:WINDOW]
            dc_ref[:, kvw + hk * hd:kvw + (hk + 1) * hd] = dvb[WINDOW:]

        @pl.when(i == 0)
        def _():
            ds_ref[...] = dsink

        @pl.when(i > 0)
        def _():
            ds_ref[...] += dsink

    blk_a = pl.BlockSpec((WINDOW, aw), lambda i: (i, 0))
    blk_kv = pl.BlockSpec((WINDOW, 2 * kvw), lambda i: (i, 0))
    return pl.pallas_call(
        body, name=name, grid=(s // WINDOW,), in_specs=_attn_specs(aw, uw, kvw) + [blk_a, blk_a],
        out_specs=(blk_a, blk_kv, blk_kv, pl.BlockSpec((1, nq), lambda i: (0, 0))),
        out_shape=(jax.ShapeDtypeStruct((s, aw), BF16), jax.ShapeDtypeStruct((s, 2 * kvw), F32),
                   jax.ShapeDtypeStruct((s, 2 * kvw), F32), jax.ShapeDtypeStruct((1, nq), F32)),
        compiler_params=_cparams(("arbitrary",)),
    )(sinks, proj, proj, proj, attn, dattn)


def _assemble_dproj(du, dq, dkv_cur, dkv_prev, name):
    s, uw = du.shape
    aw = dq.shape[1]
    kv2 = dkv_cur.shape[1]
    nb = s // WINDOW

    def body(du_ref, dq_ref, dc_ref, dp_ref, o_ref):
        i = pl.program_id(0)
        o_ref[:, :uw] = du_ref[...].astype(BF16)
        o_ref[:, uw:uw + aw] = dq_ref[...]
        nxt = jnp.where(i < nb - 1, 1.0, 0.0)
        o_ref[:, uw + aw:] = (dc_ref[...] + nxt * dp_ref[...]).astype(BF16)

    return pl.pallas_call(
        body, name=name, grid=(nb,),
        in_specs=[_tok(WINDOW, uw), _tok(WINDOW, aw), _tok(WINDOW, kv2),
                  pl.BlockSpec((WINDOW, kv2), lambda i: (jnp.minimum(i + 1, nb - 1), 0))],
        out_specs=_tok(WINDOW, uw + aw + kv2), out_shape=jax.ShapeDtypeStruct((s, uw + aw + kv2), BF16),
        compiler_params=_cparams(("parallel",)),
    )(du, dq, dkv_cur, dkv_prev)


def _zoh(lr, li, ls, btr, bti):
    dt = jnp.exp(ls)
    mag = jnp.exp(lr * dt)
    ang = li * dt
    ar = mag * jnp.cos(ang)
    ai = mag * jnp.sin(ang)
    den = lr * lr + li * li
    fr = ((ar - 1.0) * lr + ai * li) / den
    fi = (ai * lr - (ar - 1.0) * li) / den
    return ar, ai, fr[None] * btr - fi[None] * bti, fr[None] * bti + fi[None] * btr


def _ssm_prep(lr, li, ls, btr, bti, name):
    def body(lr_ref, li_ref, ls_ref, btr_ref, bti_ref, ar_ref, ai_ref, bbr_ref, bbi_ref):
        ar, ai, bbr, bbi = _zoh(lr_ref[...], li_ref[...], ls_ref[...], btr_ref[...], bti_ref[...])
        ar_ref[...] = ar
        ai_ref[...] = ai
        bbr_ref[...] = bbr
        bbi_ref[...] = bbi

    s2 = jax.ShapeDtypeStruct(lr.shape, F32)
    s3 = jax.ShapeDtypeStruct(btr.shape, F32)
    return pl.pallas_call(body, name=name, out_shape=(s2, s2, s3, s3))(lr, li, ls, btr, bti)


def _ssm_prep_bwd(lr, li, ls, btr, bti, dar, dai, dbbr, dbbi, name):
    def body(lr_ref, li_ref, ls_ref, btr_ref, bti_ref, dar_ref, dai_ref, dbbr_ref, dbbi_ref, o1, o2, o3, o4, o5):
        _, vjp = jax.vjp(_zoh, lr_ref[...], li_ref[...], ls_ref[...], btr_ref[...], bti_ref[...])
        g = vjp((dar_ref[...], dai_ref[...], dbbr_ref[...], dbbi_ref[...]))
        for o, v in zip((o1, o2, o3, o4, o5), g):
            o[...] = v

    s2 = jax.ShapeDtypeStruct(lr.shape, F32)
    s3 = jax.ShapeDtypeStruct(btr.shape, F32)
    return pl.pallas_call(body, name=name, out_shape=(s2, s2, s2, s3, s3))(lr, li, ls, btr, bti, dar, dai, dbbr, dbbi)


def _state_tiles(ref):
    return [ref[:, cb * 128:(cb + 1) * 128] for cb in range(4)]


def _gather_rows(ref_re, ref_im, r, t):
    return jnp.concatenate([ref_re.at[cb][pl.ds(r, t, stride=8), :] for cb in range(4)]
                           + [ref_im.at[cb][pl.ds(r, t, stride=8), :] for cb in range(4)], axis=1)


def _scatter_rows(ref_re, ref_im, r, t, val):
    for cb in range(4):
        ref_re.at[cb][pl.ds(r, t, stride=8), :] = val[:, cb * 128:(cb + 1) * 128]
        ref_im.at[cb][pl.ds(r, t, stride=8), :] = val[:, PSTATES + cb * 128:PSTATES + (cb + 1) * 128]


def _ssm_fwd(proj, bp, cp, a_re, a_im, dvec, uw, name, t=128):
    s = proj.shape[0]
    npc = uw // PIECE
    assert npc == 8 and s % t == 0

    def body(u_ref, bp_ref, cp_ref, ar_ref, ai_ref, d_ref, y_ref, xr_ref, xi_ref, cr_ref, ci_ref):
        i = pl.program_id(0)

        @pl.when(i == 0)
        def _():
            cr_ref[...] = jnp.zeros_like(cr_ref)
            ci_ref[...] = jnp.zeros_like(ci_ref)

        for r in range(npc):
            bu = jnp.dot(u_ref[:, r * PIECE:(r + 1) * PIECE], bp_ref[r], preferred_element_type=F32)
            _scatter_rows(xr_ref, xi_ref, r, t, bu)
        ar = _state_tiles(ar_ref)
        ai = _state_tiles(ai_ref)

        def step(tt, carry):
            xr, xi = carry
            off = pl.multiple_of(tt * 8, 8)
            nr, ni = [], []
            for cb in range(4):
                vr = ar[cb] * xr[cb] - ai[cb] * xi[cb] + xr_ref[cb, pl.ds(off, 8), :]
                vi = ar[cb] * xi[cb] + ai[cb] * xr[cb] + xi_ref[cb, pl.ds(off, 8), :]
                xr_ref[cb, pl.ds(off, 8), :] = vr
                xi_ref[cb, pl.ds(off, 8), :] = vi
                nr.append(vr)
                ni.append(vi)
            return tuple(nr), tuple(ni)

        xr, xi = lax.fori_loop(0, t, step, (tuple(_state_tiles(cr_ref)), tuple(_state_tiles(ci_ref))), unroll=4)
        for cb in range(4):
            cr_ref[:, cb * 128:(cb + 1) * 128] = xr[cb]
            ci_ref[:, cb * 128:(cb + 1) * 128] = xi[cb]
        for r in range(npc):
            xs = _gather_rows(xr_ref, xi_ref, r, t).astype(BF16)
            y_ref[:, r * PIECE:(r + 1) * PIECE] = (
                jnp.dot(xs, cp_ref[r], preferred_element_type=F32)
                + d_ref[:, r * PIECE:(r + 1) * PIECE] * u_ref[:, r * PIECE:(r + 1) * PIECE].astype(F32))

    full3 = lambda shp: pl.BlockSpec(shp, lambda i: (0, 0, 0))
    full2 = lambda shp: pl.BlockSpec(shp, lambda i: (0, 0))
    xs_spec = pl.BlockSpec((4, t * 8, 128), lambda i: (0, i, 0))
    xs_shape = jax.ShapeDtypeStruct((4, s * 8, 128), F32)
    return pl.pallas_call(
        body, name=name, grid=(s // t,),
        in_specs=[pl.BlockSpec((t, uw), lambda i: (i, 0)), full3(bp.shape), full3(cp.shape), full2(a_re.shape), full2(a_im.shape),
                  full2(dvec.shape)],
        out_specs=(pl.BlockSpec((t, uw), lambda i: (i, 0)), xs_spec, xs_spec),
        out_shape=(jax.ShapeDtypeStruct((s, uw), F32), xs_shape, xs_shape),
        scratch_shapes=[pltpu.VMEM((8, PSTATES), F32), pltpu.VMEM((8, PSTATES), F32)],
        compiler_params=_cparams(("arbitrary",), 56),
    )(proj, bp, cp, a_re, a_im, dvec)


def _ssm_bwd(dy, proj, xs_re, xs_im, cpt, bpt, a_re, a_im, dvec, uw, name, t=128):
    s = proj.shape[0]
    npc = uw // PIECE
    nt = s // t
    assert npc == 8 and s % t == 0

    def body(dy_ref, u_ref, xr_ref, xi_ref, hr_ref, hi_ref, cpt_ref, bpt_ref, ar_ref, ai_ref, d_ref,
             du_ref, dbp_ref, dcp_ref, dar_ref, dai_ref, dd_ref, gr_ref, gi_ref, lr_ref, li_ref):
        i = pl.program_id(0)

        @pl.when(i == 0)
        def _():
            lr_ref[...] = jnp.zeros_like(lr_ref)
            li_ref[...] = jnp.zeros_like(li_ref)
            dbp_ref[...] = jnp.zeros_like(dbp_ref)
            dcp_ref[...] = jnp.zeros_like(dcp_ref)
            dar_ref[...] = jnp.zeros_like(dar_ref)
            dai_ref[...] = jnp.zeros_like(dai_ref)
            dd_ref[...] = jnp.zeros_like(dd_ref)

        dyb = dy_ref[...].astype(BF16)
        for r in range(npc):
            gx = jnp.dot(dyb[:, r * PIECE:(r + 1) * PIECE], cpt_ref[r], preferred_element_type=F32)
            _scatter_rows(gr_ref, gi_ref, r, t, gx)
        ar = _state_tiles(ar_ref)
        ai = _state_tiles(ai_ref)

        def adjoint(off, lam_r, lam_i, xpr, xpi, acc_r, acc_i):
            nr, ni, qr, qi = [], [], [], []
            for cb in range(4):
                vr = gr_ref[cb, pl.ds(off, 8), :] + ar[cb] * lam_r[cb] + ai[cb] * lam_i[cb]
                vi = gi_ref[cb, pl.ds(off, 8), :] + ar[cb] * lam_i[cb] - ai[cb] * lam_r[cb]
                gr_ref[cb, pl.ds(off, 8), :] = vr
                gi_ref[cb, pl.ds(off, 8), :] = vi
                nr.append(vr)
                ni.append(vi)
                qr.append(acc_r[cb] + vr * xpr[cb] + vi * xpi[cb])
                qi.append(acc_i[cb] + vi * xpr[cb] - vr * xpi[cb])
            return tuple(nr), tuple(ni), tuple(qr), tuple(qi)

        def step(j, carry):
            lam_r, lam_i, acc_r, acc_i = carry
            tt = t - 1 - j
            off = pl.multiple_of(tt * 8, 8)
            offp = pl.multiple_of(tt * 8 - 8, 8)
            xpr = [xr_ref[cb, pl.ds(offp, 8), :] for cb in range(4)]
            xpi = [xi_ref[cb, pl.ds(offp, 8), :] for cb in range(4)]
            return adjoint(off, lam_r, lam_i, xpr, xpi, acc_r, acc_i)

        zero4 = tuple(jnp.zeros((8, 128), F32) for _ in range(4))
        carry = lax.fori_loop(0, t - 1, step, (tuple(_state_tiles(lr_ref)), tuple(_state_tiles(li_ref)), zero4, zero4), unroll=4)
        has_prev = jnp.where(i < nt - 1, 1.0, 0.0)
        xpr = [hr_ref[cb] * has_prev for cb in range(4)]
        xpi = [hi_ref[cb] * has_prev for cb in range(4)]
        lam_r, lam_i, acc_r, acc_i = adjoint(0, carry[0], carry[1], xpr, xpi, carry[2], carry[3])
        for cb in range(4):
            lr_ref[:, cb * 128:(cb + 1) * 128] = lam_r[cb]
            li_ref[:, cb * 128:(cb + 1) * 128] = lam_i[cb]
            dar_ref[:, cb * 128:(cb + 1) * 128] += acc_r[cb]
            dai_ref[:, cb * 128:(cb + 1) * 128] += acc_i[cb]

        dyv = dy_ref[...]
        uv = u_ref[...]
        dd_ref[...] += jnp.sum(dyv * uv.astype(F32), axis=0, keepdims=True)
        for r in range(npc):
            lam = _gather_rows(gr_ref, gi_ref, r, t).astype(BF16)
            sl = slice(r * PIECE, (r + 1) * PIECE)
            du_ref[:, sl] = jnp.dot(lam, bpt_ref[r], preferred_element_type=F32) + d_ref[:, sl] * dyv[:, sl]
            dbp_ref[r] += lax.dot_general(uv[:, sl], lam, (((0,), (0,)), ((), ())), preferred_element_type=F32)
            xs = _gather_rows(xr_ref, xi_ref, r, t).astype(BF16)
            dcp_ref[r] += lax.dot_general(xs, dyb[:, sl], (((0,), (0,)), ((), ())), preferred_element_type=F32)

    rev = lambda i: (nt - 1 - i, 0)
    full3 = lambda shp: pl.BlockSpec(shp, lambda i: (0, 0, 0))
    full2 = lambda shp: pl.BlockSpec(shp, lambda i: (0, 0))
    xs_spec = pl.BlockSpec((4, t * 8, 128), lambda i: (0, nt - 1 - i, 0))
    halo_spec = pl.BlockSpec((4, 8, 128), lambda i: (0, jnp.maximum((nt - 1 - i) * t - 1, 0), 0))
    st = jax.ShapeDtypeStruct((8, PSTATES), F32)
    return pl.pallas_call(
        body, name=name, grid=(nt,),
        in_specs=[pl.BlockSpec((t, uw), rev), pl.BlockSpec((t, uw), rev), xs_spec, xs_spec, halo_spec, halo_spec,
                  full3(cpt.shape), full3(bpt.shape), full2(a_re.shape), full2(a_im.shape), full2(dvec.shape)],
        out_specs=(pl.BlockSpec((t, uw), rev), full3((npc, PIECE, 2 * PSTATES)), full3((npc, 2 * PSTATES, PIECE)),
                   full2((8, PSTATES)), full2((8, PSTATES)), full2((1, uw))),
        out_shape=(jax.ShapeDtypeStruct((s, uw), F32), jax.ShapeDtypeStruct((npc, PIECE, 2 * PSTATES), F32),
                   jax.ShapeDtypeStruct((npc, 2 * PSTATES, PIECE), F32), st, st, jax.ShapeDtypeStruct((1, uw), F32)),
        scratch_shapes=[pltpu.VMEM((4, t * 8, 128), F32), pltpu.VMEM((4, t * 8, 128), F32),
                        pltpu.VMEM((8, PSTATES), F32), pltpu.VMEM((8, PSTATES), F32)],
        compiler_params=_cparams(("arbitrary",), 56),
    )(dy, proj, xs_re, xs_im, xs_re, xs_im, cpt, bpt, a_re, a_im, dvec)


def _conv3(x, h6, h7, w, b):
    row = lax.broadcasted_iota(jnp.int32, x.shape, 0)
    x1 = jnp.where(row == 0, h7, pltpu.roll(x, 1, 0))
    x2 = jnp.where(row == 0, h6, jnp.where(row == 1, h7, pltpu.roll(x, 2, 0)))
    return ((b + x2 * w[0:1]) + x1 * w[1:2]) + x * w[2:3], x1, x2


def _ffn_tiles(s, f):
    tm = min(256, s)
    tn = f // 4 if (f // 4) % 128 == 0 else f
    return tm, tn


def _conv_glu_fwd(up0, cw, cb, name):
    _, s, f = up0.shape
    tm, tn = _ffn_tiles(s, f)
    hb = tm // 8

    def body(x_ref, h_ref, w_ref, b_ref, a_ref):
        first = jnp.where(pl.program_id(0) > 0, 1.0, 0.0)
        ups = []
        for p in range(2):
            h = h_ref[p].astype(F32) * first
            ups.append(_conv3(x_ref[p].astype(F32), h[6:7], h[7:8], w_ref[p], b_ref[p])[0])
        a_ref[...] = (_gelu(ups[1]) * ups[0]).astype(BF16)

    return pl.pallas_call(
        body, name=name, grid=(s // tm, f // tn),
        in_specs=[pl.BlockSpec((2, tm, tn), lambda i, j: (0, i, j)),
                  pl.BlockSpec((2, 8, tn), lambda i, j: (0, jnp.maximum(i * hb - 1, 0), j)),
                  pl.BlockSpec((2, 3, tn), lambda i, j: (0, 0, j)), pl.BlockSpec((2, 1, tn), lambda i, j: (0, 0, j))],
        out_specs=pl.BlockSpec((tm, tn), lambda i, j: (i, j)), out_shape=jax.ShapeDtypeStruct((s, f), BF16),
        compiler_params=_cparams(("parallel", "parallel")),
    )(up0, up0, cw, cb)


def _ffn_bwd_gate(da, up0, cw, cb, name):
    _, s, f = up0.shape
    tm, tn = _ffn_tiles(s, f)
    hb = tm // 8

    def body(da_ref, x_ref, h_ref, w_ref, b_ref, d_ref, dw_ref, db_ref):
        i = pl.program_id(1)
        first = jnp.where(i > 0, 1.0, 0.0)
        ups, taps = [], []
        for p in range(2):
            h = h_ref[p].astype(F32) * first
            xv = x_ref[p].astype(F32)
            up, x1, x2 = _conv3(xv, h[6:7], h[7:8], w_ref[p], b_ref[p])
            ups.append(up)
            taps.append((x2, x1, xv))
        dav = da_ref[...]
        gate, dgate = _gelu_and_grad(ups[1])
        douts = (dav * gate, dav * ups[0] * dgate)

        @pl.when(i == 0)
        def _():
            dw_ref[...] = jnp.zeros_like(dw_ref)
            db_ref[...] = jnp.zeros_like(db_ref)

        for p in range(2):
            d_ref[p] = douts[p].astype(BF16)
            db_ref[p] += jnp.sum(douts[p], axis=0, keepdims=True)
            for kk in range(3):
                dw_ref[p, kk:kk + 1, :] += jnp.sum(douts[p] * taps[p][kk], axis=0, keepdims=True)

    return pl.pallas_call(
        body, name=name, grid=(f // tn, s // tm),
        in_specs=[pl.BlockSpec((tm, tn), lambda j, i: (i, j)), pl.BlockSpec((2, tm, tn), lambda j, i: (0, i, j)),
                  pl.BlockSpec((2, 8, tn), lambda j, i: (0, jnp.maximum(i * hb - 1, 0), j)),
                  pl.BlockSpec((2, 3, tn), lambda j, i: (0, 0, j)), pl.BlockSpec((2, 1, tn), lambda j, i: (0, 0, j))],
        out_specs=(pl.BlockSpec((2, tm, tn), lambda j, i: (0, i, j)), pl.BlockSpec((2, 3, tn), lambda j, i: (0, 0, j)),
                   pl.BlockSpec((2, 1, tn), lambda j, i: (0, 0, j))),
        out_shape=(jax.ShapeDtypeStruct((2, s, f), BF16), jax.ShapeDtypeStruct((2, 3, f), F32), jax.ShapeDtypeStruct((2, 1, f), F32)),
        compiler_params=_cparams(("parallel", "arbitrary")),
    )(da, up0, up0, cw, cb)


def _conv_bwd(dup, cw, name):
    _, s, f = dup.shape
    tm, tn = _ffn_tiles(s, f)
    hb = tm // 8
    nb = s // tm

    def body(d_ref, h_ref, w_ref, o_ref):
        last = jnp.where(pl.program_id(0) < nb - 1, 1.0, 0.0)
        row = lax.broadcasted_iota(jnp.int32, (tm, tn), 0)
        for p in range(2):
            d = d_ref[p].astype(F32)
            h = h_ref[p].astype(F32) * last
            d1 = jnp.where(row == tm - 1, h[0:1], pltpu.roll(d, tm - 1, 0))
            d2 = jnp.where(row == tm - 1, h[1:2], jnp.where(row == tm - 2, h[0:1], pltpu.roll(d, tm - 2, 0)))
            w = w_ref[p]
            o_ref[p] = (d * w[2:3] + d1 * w[1:2] + d2 * w[0:1]).astype(BF16)

    return pl.pallas_call(
        body, name=name, grid=(nb, f // tn),
        in_specs=[pl.BlockSpec((2, tm, tn), lambda i, j: (0, i, j)),
                  pl.BlockSpec((2, 8, tn), lambda i, j: (0, jnp.minimum((i + 1) * hb, s // 8 - 1), j)),
                  pl.BlockSpec((2, 3, tn), lambda i, j: (0, 0, j))],
        out_specs=pl.BlockSpec((2, tm, tn), lambda i, j: (0, i, j)), out_shape=jax.ShapeDtypeStruct((2, s, f), BF16),
        compiler_params=_cparams(("parallel", "parallel")),
    )(dup, dup, cw)


def _ada_part(c_all, w_ada, name):
    nb, d = c_all.shape
    depth, _, cols = w_ada.shape
    tn = 1024 if cols % 1024 == 0 else cols

    def body(c_ref, w_ref, o_ref, ca_ref):
        cv = c_ref[...]
        ca = cv * _sigmoid(cv)
        ca_ref[...] = ca
        o_ref[...] = jnp.dot(ca.astype(BF16), w_ref[...].astype(BF16), preferred_element_type=F32)

    return pl.pallas_call(
        body, name=name, grid=(depth, cols // tn),
        in_specs=[pl.BlockSpec((nb, d), lambda l, j: (0, 0)), pl.BlockSpec((None, d, tn), lambda l, j: (l, 0, j))],
        out_specs=(pl.BlockSpec((None, nb, tn), lambda l, j: (l, 0, j)), pl.BlockSpec((nb, d), lambda l, j: (0, 0))),
        out_shape=(jax.ShapeDtypeStruct((depth, nb, cols), F32), jax.ShapeDtypeStruct((nb, d), F32)),
        compiler_params=_cparams(("arbitrary", "arbitrary")),
    )(c_all, w_ada)


def _ada_select(gath, b_ada, name):
    depth, n6 = b_ada.shape
    cols = gath.shape[1]

    def body(g_ref, b_ref, o_ref):
        me = 4 * lax.axis_index("x") + 2 * lax.axis_index("y") + lax.axis_index("c")
        for l in range(depth):
            for j in range(n6 // cols):
                row = (2 * j) * (8 * depth) + l * 8 + me
                o_ref[l:l + 1, j * cols:(j + 1) * cols] = g_ref[pl.ds(row, 1), :] + b_ref[l:l + 1, j * cols:(j + 1) * cols]

    return pl.pallas_call(body, name=name, out_shape=jax.ShapeDtypeStruct((depth, n6), F32))(gath, b_ada)


def _wada_grad(ca_t, d_sel, name):
    d, nb = ca_t.shape
    cols = d_sel.shape[1]
    tm = min(256, d)

    def body(a_ref, g_ref, o_ref):
        acc = a_ref[:, 0:1] * g_ref[0:1, :]
        for b in range(1, nb):
            acc = acc + a_ref[:, b:b + 1] * g_ref[b:b + 1, :]
        o_ref[...] = acc

    return pl.pallas_call(
        body, name=name, grid=(d // tm,), in_specs=[pl.BlockSpec((tm, nb), lambda i: (i, 0)), pl.BlockSpec((nb, cols), lambda i: (0, 0))],
        out_specs=pl.BlockSpec((tm, cols), lambda i: (i, 0)), out_shape=jax.ShapeDtypeStruct((d, cols), F32),
        compiler_params=_cparams(("parallel",)),
    )(ca_t, d_sel)


def _block_rows(r, c):
    tr = r
    for cand in (2048, 1024, 512, 256, 128, 64, 32, 16, 8):
        if r % cand == 0 and cand * c * 4 <= MIB:
            tr = cand
            break
    else:
        for cand in (8, 16, 32):
            if r % cand == 0:
                tr = cand
                break
    return tr


def _adamw(w, g, m, v, name):
    r, c = w.shape
    tr = _block_rows(r, c)
    c1 = 1.0 - ADAM_B1 ** ADAM_STEP
    c2 = 1.0 - ADAM_B2 ** ADAM_STEP

    def body(w_ref, g_ref, m_ref, v_ref, d_ref, nm_ref, nv_ref):
        gv = g_ref[...]
        nm = ADAM_B1 * m_ref[...] + (1.0 - ADAM_B1) * gv
        nv = ADAM_B2 * v_ref[...] + (1.0 - ADAM_B2) * (gv * gv)
        d_ref[...] = -ADAM_LR * ((nm / c1) / (jnp.sqrt(nv / c2) + ADAM_EPS) + ADAM_WD * w_ref[...])
        nm_ref[...] = nm
        nv_ref[...] = nv

    spec = pl.BlockSpec((tr, c), lambda i: (i, 0))
    shp = jax.ShapeDtypeStruct((r, c), F32)
    return pl.pallas_call(
        body, name=name, grid=(r // tr,), in_specs=[spec] * 4, out_specs=(spec,) * 3, out_shape=(shp,) * 3,
        compiler_params=_cparams(("parallel",)),
    )(w, g, m, v)


def _sum_slots(x, nslots, name, out_dtype=F32):
    r = x.shape[0] // nslots
    c = x.shape[1]
    tr = _block_rows(r, c)
    nbk = r // tr

    def body(*refs):
        acc = refs[0][...].astype(F32)
        for k in range(1, nslots):
            acc = acc + refs[k][...].astype(F32)
        refs[nslots][...] = acc.astype(out_dtype)

    specs = [pl.BlockSpec((tr, c), functools.partial(lambda k, i: (k * nbk + i, 0), k)) for k in range(nslots)]
    return pl.pallas_call(
        body, name=name, grid=(nbk,), in_specs=specs, out_specs=pl.BlockSpec((tr, c), lambda i: (i, 0)),
        out_shape=jax.ShapeDtypeStruct((r, c), out_dtype), compiler_params=_cparams(("parallel",)),
    )(*([x] * nslots))


def _pick(dim, prefs):
    for p in prefs:
        if dim % p == 0:
            return p
    return dim


def _mm(a, b, m, n, k, name, out_dtype, **kw):
    tm = _pick(m, (1408, 1024, 512, 256, 128))
    tn = _pick(n, (1408, 1152, 1024, 512, 256, 128))
    kdiv = k // max(kw.get("a_stack", 0), kw.get("b_stack", 0) if kw.get("tb") else 0, 1)
    osize = jnp.dtype(out_dtype).itemsize
    tk = kdiv
    for cut in (1, 2, 4, 8, 16):
        tk = kdiv // cut
        vmem = 2 * 2 * tk * (tm + tn) + tm * tn * (2 * osize + 4 + (4 if tk < k else 0))
        if kdiv % cut == 0 and tk % 128 == 0 and vmem <= _MATMUL_VMEM_BUDGET:
            break
    return _matmul(a, b, m=m, n=n, k=k, tm=tm, tn=tn, tk=tk, out_dtype=out_dtype, name=name, **kw)


def _ssm_layout(p):
    g, st = p["lam_re"].shape
    npc = g * st // PSTATES
    lr = p["lam_re"].reshape(npc, PSTATES)
    li = p["lam_im"].reshape(npc, PSTATES)
    ls = jnp.broadcast_to(p["log_step"][:, None], (g, st)).reshape(npc, PSTATES)
    btr = jnp.transpose(p["ssm_b_re"], (2, 0, 1)).reshape(SSM_GROUP, npc, PSTATES)
    bti = jnp.transpose(p["ssm_b_im"], (2, 0, 1)).reshape(SSM_GROUP, npc, PSTATES)
    return lr, li, ls, btr, bti


def _ssm_pieces(bbr, bbi, c_re, c_im):
    npc = bbr.shape[1]
    gl = PSTATES // STATE
    eye = jnp.eye(gl, dtype=bool)

    def b_piece(bb):
        t = jnp.transpose(bb.reshape(SSM_GROUP, npc, gl, STATE), (1, 2, 0, 3))
        full = jnp.where(eye[None, :, None, :, None], t[:, :, :, None, :], 0.0)
        return full.reshape(npc, gl * SSM_GROUP, PSTATES)

    def c_piece(cc):
        t = jnp.transpose(cc.reshape(npc, gl, SSM_GROUP, STATE), (0, 1, 3, 2))
        full = jnp.where(eye[None, :, None, :, None], t[:, :, :, None, :], 0.0)
        return full.reshape(npc, PSTATES, gl * SSM_GROUP)

    bp = jnp.concatenate([b_piece(bbr), b_piece(bbi)], axis=2).astype(BF16)
    cp = jnp.concatenate([c_piece(c_re), c_piece(-c_im)], axis=1).astype(BF16)
    return bp, cp, jnp.swapaxes(bp, 1, 2), jnp.swapaxes(cp, 1, 2)


def _ssm_unpieces(dbp, dcp):
    npc = dbp.shape[0]
    gl = PSTATES // STATE
    idx = jnp.arange(gl)

    def b_diag(x):
        d = x.reshape(npc, gl, SSM_GROUP, gl, STATE)[:, idx, :, idx, :]
        return jnp.transpose(d, (2, 1, 0, 3)).reshape(SSM_GROUP, npc, PSTATES)

    def c_diag(x):
        d = x.reshape(npc, gl, STATE, gl, SSM_GROUP)[:, idx, :, idx, :]
        return jnp.transpose(d, (1, 0, 3, 2)).reshape(npc * gl, SSM_GROUP, STATE)

    return b_diag(dbp[:, :, :PSTATES]), b_diag(dbp[:, :, PSTATES:]), c_diag(dcp[:, :PSTATES, :]), -c_diag(dcp[:, PSTATES:, :])


def _layer_fwd(l, x, ada6, w, p):
    s, d = x.shape
    sh_m, sc_m, gt_m, sh_f, sc_f, gt_f = ada6
    uw = w["w_glu"].shape[0]
    aw = w["w_out"].shape[0] - uw
    ncol = w["w_in"].shape[1]
    f = w["w_down"].shape[0]
    row = lambda v: v.reshape(1, -1)
    n = lambda t: f"l{l}_{t}"

    h = _pre_fwd(x, row(p["g_pre_mix"]), sc_m, sh_m, n("pre_mix"))
    proj = _mm(h, w["w_in"], s, ncol, d, n("proj"), BF16)
    attn = _attn_fwd(proj, p["attn_sinks"], aw, uw, n("attn_fwd"))
    zin = _ssm_layout(p)
    a_re, a_im, bbr, bbi = _ssm_prep(*zin, n("ssm_prep"))
    bp, cp, bpt, cpt = _ssm_pieces(bbr, bbi, p["ssm_c_re"], p["ssm_c_im"])
    dvec = p["ssm_d"].reshape(1, uw)
    y, xs_re, xs_im = _ssm_fwd(proj, bp, cp, a_re, a_im, dvec, uw, n("ssm_fwd"))
    z = _gelu_fwd(y, n("gelu_fwd"))
    gl = _mm(z, w["w_glu"], s, uw, uw, n("glu"), F32)
    merged = _merge_fwd(attn, y, gl, row(p["g_attn_out"]), row(p["g_ssm_out"]), n("merge_fwd"))
    mix = _mm(merged, w["w_out"], s, d, aw + uw, n("out_proj"), F32)
    x1 = _post_fwd(x, mix, row(p["g_post_mix"]), gt_m, n("post_mix"))

    h2 = _pre_fwd(x1, row(p["g_pre_ffn"]), sc_f, sh_f, n("pre_ffn"))
    up0 = _mm(h2, w["w_up"], s, 2 * f, d, n("up_proj"), BF16, b_stack=w["w_up"].shape[0], o_stack=2)
    cw2 = jnp.transpose(p["conv_w"].reshape(3, 2, f), (1, 0, 2))
    cb2 = p["conv_b"].reshape(2, 1, f)
    act = _conv_glu_fwd(up0, cw2, cb2, n("conv_glu"))
    ff = _mm(act, w["w_down"], s, d, f, n("down_proj"), F32)
    x2 = _post_fwd(x1, ff, row(p["g_post_ffn"]), gt_f, n("post_ffn"))
    saved = dict(x=x, h=h, proj=proj, attn=attn, zin=zin, a_re=a_re, a_im=a_im, bpt=bpt, cpt=cpt, dvec=dvec, y=y, xs_re=xs_re,
                 xs_im=xs_im, z=z, gl=gl, merged=merged, mix=mix, x1=x1, h2=h2, up0=up0, cw2=cw2, cb2=cb2, act=act, ff=ff)
    return x2, saved


def _layer_bwd(l, dx2, ada6, w, p, sv):
    s, d = dx2.shape
    sh_m, sc_m, gt_m, sh_f, sc_f, gt_f = ada6
    uw = w["w_glu"].shape[0]
    aw = w["w_out"].shape[0] - uw
    ncol = w["w_in"].shape[1]
    f = w["w_down"].shape[0]
    nst = w["w_up"].shape[0]
    row = lambda v: v.reshape(1, -1)
    n = lambda t: f"l{l}_{t}"
    gw, gs = {}, {}

    dff, dgt_f, gs["g_post_ffn"] = _post_bwd(dx2, sv["ff"], row(p["g_post_ffn"]), gt_f, n("post_ffn_bwd"))
    gw["w_down"] = _mm(sv["act"], dff, f, d, s, n("down_dw"), BF16, ta=True)
    dact = _mm(dff, w["w_down"], s, f, d, n("down_dx"), F32, tb=True)
    dup, dcw2, dcb2 = _ffn_bwd_gate(dact, sv["up0"], sv["cw2"], sv["cb2"], n("ffn_gate_bwd"))
    gs["conv_w"] = jnp.transpose(dcw2, (1, 0, 2)).reshape(3, 2 * f)
    gs["conv_b"] = dcb2.reshape(2 * f)
    dup0 = _conv_bwd(dup, sv["cw2"], n("conv_bwd"))
    gw["w_up"] = _mm(sv["h2"], dup0, d, 2 * f, s, n("up_dw"), BF16, ta=True, b_stack=2, o_stack=nst)
    dh2 = _mm(dup0, w["w_up"], s, d, 2 * f, n("up_dx"), F32, tb=True, a_stack=2, b_stack=nst)
    dx1, dsh_f, dsc_f, gs["g_pre_ffn"] = _pre_bwd(dx2, dh2, sv["x1"], row(p["g_pre_ffn"]), sc_f, n("pre_ffn_bwd"))

    dmix, dgt_m, gs["g_post_mix"] = _post_bwd(dx1, sv["mix"], row(p["g_post_mix"]), gt_m, n("post_mix_bwd"))
    gw["w_out"] = _mm(sv["merged"], dmix, aw + uw, d, s, n("out_dw"), BF16, ta=True)
    dmerged = _mm(dmix, w["w_out"], s, aw + uw, d, n("out_dx"), F32, tb=True)
    dattn, dgl, dzd, gs["g_attn_out"], gs["g_ssm_out"] = _merge_bwd(
        dmerged, sv["attn"], sv["y"], sv["gl"], row(p["g_attn_out"]), row(p["g_ssm_out"]), n("merge_bwd"))
    gw["w_glu"] = _mm(sv["z"], dgl, uw, uw, s, n("glu_dw"), BF16, ta=True)
    dz2 = _mm(dgl, w["w_glu"], s, uw, uw, n("glu_dx"), F32, tb=True)
    dy = _gelu_bwd(dzd, dz2, sv["y"], n("gelu_bwd"))
    du, dbp, dcp, dar, dai, dd = _ssm_bwd(dy, sv["proj"], sv["xs_re"], sv["xs_im"], sv["cpt"], sv["bpt"], sv["a_re"], sv["a_im"],
                                          sv["dvec"], uw, n("ssm_bwd"))
    dq, dkv_c, dkv_p, dsinks = _attn_bwd(sv["proj"], p["attn_sinks"], sv["attn"], dattn, aw, uw, n("attn_bwd"))
    dproj = _assemble_dproj(du, dq, dkv_c, dkv_p, n("dproj"))
    gw["w_in"] = _mm(sv["h"], dproj, d, ncol, s, n("in_dw"), BF16, ta=True)
    dh = _mm(dproj, w["w_in"], s, d, ncol, n("in_dx"), F32, tb=True)
    dx0, dsh_m, dsc_m, gs["g_pre_mix"] = _pre_bwd(dx1, dh, sv["x"], row(p["g_pre_mix"]), sc_m, n("pre_mix_bwd"))

    dbbr, dbbi, dc_re, dc_im = _ssm_unpieces(dbp, dcp)
    dlr, dli, dls, dbtr, dbti = _ssm_prep_bwd(*sv["zin"], dar, dai, dbbr, dbbi, n("ssm_prep_bwd"))
    g, st = p["lam_re"].shape
    gs["lam_re"] = dlr.reshape(g, st)
    gs["lam_im"] = dli.reshape(g, st)
    gs["log_step"] = jnp.sum(dls.reshape(g, st), axis=1)
    gs["ssm_b_re"] = jnp.transpose(dbtr.reshape(SSM_GROUP, g, st), (1, 2, 0))
    gs["ssm_b_im"] = jnp.transpose(dbti.reshape(SSM_GROUP, g, st), (1, 2, 0))
    gs["ssm_c_re"] = dc_re
    gs["ssm_c_im"] = dc_im
    gs["ssm_d"] = dd.reshape(p["ssm_d"].shape)
    gs["attn_sinks"] = dsinks.reshape(-1)
    gs["b_ada"] = jnp.concatenate([dsh_m, dsc_m, dgt_m, dsh_f, dsc_f, dgt_f], axis=1).reshape(-1)
    for key in ("g_post_ffn", "g_pre_ffn", "g_post_mix", "g_attn_out", "g_ssm_out", "g_pre_mix"):
        gs[key] = gs[key].reshape(-1)
    return dx0, gw, gs


def _local_step(x, tgt, ada, wl, pl_small):
    d = x.shape[1]
    depth = len(wl)
    ada6 = [[ada[l:l + 1, k * d:(k + 1) * d] for k in range(6)] for l in range(depth)]
    saved = []
    h = x
    for l in range(depth):
        h, sv = _layer_fwd(l, h, ada6[l], wl[l], pl_small[l])
        saved.append(sv)
    dy, lsum = _loss_head(h, tgt, "loss_head")
    loss = 0.5 * lsum[0, 0] / d
    gws, gss = [None] * depth, [None] * depth
    dx = dy
    for l in reversed(range(depth)):
        dx, gws[l], gss[l] = _layer_bwd(l, dx, ada6[l], wl[l], pl_small[l], saved[l])
    return loss, dx, gws, gss


_ANY = pl.BlockSpec(memory_space=pl.ANY)


def _mesh_pos():
    return lax.axis_index("x"), lax.axis_index("y"), lax.axis_index("c")


def _other_chips(x, y):
    return [(1 - x, y), (x, 1 - y), (1 - x, 1 - y)]


def _remote(src, dst, send_sems, recv_sems, k, to):
    return pltpu.make_async_remote_copy(src_ref=src, dst_ref=dst, send_sem=send_sems.at[k], recv_sem=recv_sems.at[k],
                                        device_id=to, device_id_type=MESH)


def _allgather8(xs, name):
    m, n = xs.shape

    def body(x_ref, out_ref, send_sems, recv_sems):
        x, y, c = _mesh_pos()
        me, sibling = (x, y, c), (x, y, 1 - c)
        chips = _other_chips(x, y)

        def rows(px, py, pc):
            return out_ref.at[pl.ds((4 * px + 2 * py + pc) * m, m), :]

        def copy(k, block, to, src=None):
            return _remote(rows(*block) if src is None else src, rows(*block), send_sems, recv_sems, k, to)

        first = [copy(0, me, sibling, src=x_ref)]
        first += [copy(1 + j, me, (*chip, c), src=x_ref) for j, chip in enumerate(chips)]
        for cp in first:
            cp.start()
        passed = [copy(4 + j, (*chip, c), sibling) for j, chip in enumerate(chips)]
        for j, chip in enumerate(chips):
            copy(1 + j, (*chip, c), me).wait_recv()
            passed[j].start()
        copy(0, sibling, me).wait_recv()
        for j, chip in enumerate(chips):
            copy(4 + j, (*chip, 1 - c), me).wait_recv()
        for cp in first + passed:
            cp.wait_send()

    out = pl.pallas_call(
        body, name=name, out_shape=jax.ShapeDtypeStruct((8 * m, n), xs.dtype), in_specs=[_ANY], out_specs=_ANY,
        scratch_shapes=[pltpu.SemaphoreType.DMA((7,)), pltpu.SemaphoreType.DMA((7,))],
    )(xs)
    x, y, c = _mesh_pos()
    return lax.dynamic_update_slice(out, xs, ((4 * x + 2 * y + c) * m, 0))


def _half_rows(ref_rows, half, align):
    h = ref_rows // 2
    return pl.ds(pl.multiple_of(half * h, align), h)


def _allgather_chips(shards, name):
    nw = len(shards)

    def body(*refs):
        ins, outs = refs[:nw], refs[nw:2 * nw]
        send_sems, recv_sems = refs[2 * nw:]
        x, y, c = _mesh_pos()
        mine = 2 * x + y
        sibling = (x, y, 1 - c)
        chips = _other_chips(x, y)

        def blk(w, chip_idx, half):
            return outs[w].at[chip_idx, _half_rows(shards[w].shape[0], half, 16), :]

        sends = []
        for w in range(nw):
            for q, chip in enumerate(chips):
                src = ins[w].at[_half_rows(shards[w].shape[0], c, 16), :]
                sends.append(_remote(src, blk(w, mine, c), send_sems, recv_sems, 7 * w + q, (*chip, c)))
                sends[-1].start()
        for w in range(nw):
            sends.append(_remote(ins[w], outs[w].at[mine], send_sems, recv_sems, 7 * w + 6, sibling))
            sends[-1].start()
        for w in range(nw):
            for q, chip in enumerate(chips):
                landed = blk(w, 2 * chip[0] + chip[1], c)
                _remote(landed, landed, send_sems, recv_sems, 7 * w + q, (x, y, c)).wait_recv()
                sends.append(_remote(landed, landed, send_sems, recv_sems, 7 * w + 3 + q, sibling))
                sends[-1].start()
        for w in range(nw):
            own = outs[w].at[mine]
            _remote(own, own, send_sems, recv_sems, 7 * w + 6, (x, y, c)).wait_recv()
            for q, chip in enumerate(chips):
                other = blk(w, 2 * chip[0] + chip[1], 1 - c)
                _remote(other, other, send_sems, recv_sems, 7 * w + 3 + q, (x, y, c)).wait_recv()
        for cp in sends:
            cp.wait_send()

    return pl.pallas_call(
        body, name=name, out_shape=[jax.ShapeDtypeStruct((4,) + s.shape, s.dtype) for s in shards],
        in_specs=[_ANY] * nw, out_specs=[_ANY] * nw,
        scratch_shapes=[pltpu.SemaphoreType.DMA((7 * nw,)), pltpu.SemaphoreType.DMA((7 * nw,))],
    )(*shards)


def _exchange_halves(gs, name):
    nw = len(gs)

    def body(*refs):
        ins, outs = refs[:nw], refs[nw:2 * nw]
        send_sems, recv_sems = refs[2 * nw:]
        x, y, c = _mesh_pos()
        cps = []
        for w in range(nw):
            src = ins[w].at[:, _half_rows(gs[w].shape[1], 1 - c, 16), :]
            cps.append(_remote(src, outs[w], send_sems, recv_sems, w, (x, y, 1 - c)))
            cps[-1].start()
        for cp in cps:
            cp.wait_recv()
        for cp in cps:
            cp.wait_send()

    return pl.pallas_call(
        body, name=name, out_shape=[jax.ShapeDtypeStruct((4, g.shape[1] // 2, g.shape[2]), g.dtype) for g in gs],
        in_specs=[_ANY] * nw, out_specs=[_ANY] * nw,
        scratch_shapes=[pltpu.SemaphoreType.DMA((nw,)), pltpu.SemaphoreType.DMA((nw,))],
    )(*gs)


def _add_half(g, recv, cidx, name):
    _, r, c = g.shape
    h = r // 2
    tr = _block_rows(h, c)
    nbh = h // tr
    assert tr % 16 == 0

    def body(c_ref, g_ref, r_ref, o_ref):
        o_ref[...] = (g_ref[...].astype(F32) + r_ref[...].astype(F32)).astype(BF16)

    grid_spec = pltpu.PrefetchScalarGridSpec(
        num_scalar_prefetch=1, grid=(4, nbh),
        in_specs=[pl.BlockSpec((None, tr, c), lambda s, i, cr: (s, cr[0] * nbh + i, 0)),
                  pl.BlockSpec((None, tr, c), lambda s, i, cr: (s, i, 0))],
        out_specs=pl.BlockSpec((None, tr, c), lambda s, i, cr: (s, i, 0)))
    return pl.pallas_call(
        body, name=name, grid_spec=grid_spec, out_shape=jax.ShapeDtypeStruct((4, h, c), BF16),
        compiler_params=_cparams(("parallel", "parallel")),
    )(cidx, g, recv)


def _scatter_chips(ps, name):
    nw = len(ps)

    def body(*refs):
        ins, outs = refs[:nw], refs[nw:2 * nw]
        send_sems, recv_sems = refs[2 * nw:]
        x, y, c = _mesh_pos()
        mine = 2 * x + y
        chips = _other_chips(x, y)
        sends = []
        for w in range(nw):
            for q, chip in enumerate(chips):
                sends.append(_remote(ins[w].at[2 * chip[0] + chip[1]], outs[w].at[mine], send_sems, recv_sems, 3 * w + q, (*chip, c)))
                sends[-1].start()
        for w in range(nw):
            for q, chip in enumerate(chips):
                landed = outs[w].at[2 * chip[0] + chip[1]]
                _remote(landed, landed, send_sems, recv_sems, 3 * w + q, (x, y, c)).wait_recv()
        for cp in sends:
            cp.wait_send()

    return pl.pallas_call(
        body, name=name, out_shape=[jax.ShapeDtypeStruct(p.shape, p.dtype) for p in ps],
        in_specs=[_ANY] * nw, out_specs=[_ANY] * nw,
        scratch_shapes=[pltpu.SemaphoreType.DMA((3 * nw,)), pltpu.SemaphoreType.DMA((3 * nw,))],
    )(*ps)


def _swap_with_sibling(xs, name):
    nf = len(xs)

    def body(*refs):
        ins, outs = refs[:nf], refs[nf:2 * nf]
        send_sems, recv_sems = refs[2 * nf:]
        x, y, c = _mesh_pos()
        cps = [_remote(ins[k], outs[k], send_sems, recv_sems, k, (x, y, 1 - c)) for k in range(nf)]
        for cp in cps:
            cp.start()
        for cp in cps:
            cp.wait_recv()
        for cp in cps:
            cp.wait_send()

    return pl.pallas_call(
        body, name=name, out_shape=[jax.ShapeDtypeStruct(t.shape, t.dtype) for t in xs], in_specs=[_ANY] * nf, out_specs=[_ANY] * nf,
        scratch_shapes=[pltpu.SemaphoreType.DMA((nf,)), pltpu.SemaphoreType.DMA((nf,))],
    )(*xs)


_BIG = ("w_in", "w_glu", "w_out", "w_up", "w_down")
_SMALL = ("b_ada", "g_pre_mix", "g_post_mix", "attn_sinks", "lam_re", "lam_im", "log_step", "ssm_b_re", "ssm_b_im", "ssm_c_re",
          "ssm_c_im", "ssm_d", "g_attn_out", "g_ssm_out", "g_pre_ffn", "g_post_ffn", "conv_b")
_WEIGHTS = ("w_ada", "b_ada", "g_pre_mix", "g_post_mix", "w_in", "attn_sinks", "lam_re", "lam_im", "log_step", "ssm_b_re", "ssm_b_im",
            "ssm_c_re", "ssm_c_im", "ssm_d", "w_glu", "g_attn_out", "g_ssm_out", "w_out", "g_pre_ffn", "g_post_ffn", "w_up", "conv_w",
            "conv_b", "w_down")
_LANES = 1024


def _pack(parts, rows_to):
    flat = jnp.concatenate([p.reshape(-1) for p in parts])
    per = _LANES * rows_to
    total = -(-flat.shape[0] // per) * per
    return jnp.pad(flat, (0, total - flat.shape[0])).reshape(total // _LANES, _LANES)


def _unpack(packed, shapes):
    flat = packed.reshape(-1)
    out, off = [], 0
    for shp in shapes:
        size = math.prod(shp)
        out.append(flat[off:off + size].reshape(shp))
        off += size
    return out


def kernel(x, c, w_ada, b_ada, g_pre_mix, g_post_mix, w_in, attn_sinks, lam_re, lam_im, log_step, ssm_b_re, ssm_b_im, ssm_c_re, ssm_c_im, ssm_d, w_glu, g_attn_out, g_ssm_out, w_out, g_pre_ffn, g_post_ffn, w_up, conv_w, conv_b, w_down, loss_target, m_w_ada, m_b_ada, m_g_pre_mix, m_g_post_mix, m_w_in, m_attn_sinks, m_lam_re, m_lam_im, m_log_step, m_ssm_b_re, m_ssm_b_im, m_ssm_c_re, m_ssm_c_im, m_ssm_d, m_w_glu, m_g_attn_out, m_g_ssm_out, m_w_out, m_g_pre_ffn, m_g_post_ffn, m_w_up, m_conv_w, m_conv_b, m_w_down, v_w_ada, v_b_ada, v_g_pre_mix, v_g_post_mix, v_w_in, v_attn_sinks, v_lam_re, v_lam_im, v_log_step, v_ssm_b_re, v_ssm_b_im, v_ssm_c_re, v_ssm_c_im, v_ssm_d, v_w_glu, v_g_attn_out, v_g_ssm_out, v_w_out, v_g_pre_ffn, v_g_post_ffn, v_w_up, v_conv_w, v_conv_b, v_w_down):
    given = dict(locals())
    wts = {n: given[n] for n in _WEIGHTS}
    mom = {n: given["m_" + n] for n in _WEIGHTS}
    var = {n: given["v_" + n] for n in _WEIGHTS}
    depth, d, ada_cols = w_ada.shape
    nchips = 4
    xi, yi, ci = lax.axis_index("x"), lax.axis_index("y"), lax.axis_index("c")
    chip = 2 * xi + yi
    cidx = jnp.reshape(ci, (1,)).astype(jnp.int32)

    cw_cols = conv_w.shape[2]
    vec = _pack([c, conv_w], 8)
    g1 = _allgather8(vec, "ag_cond").reshape(8, -1)
    c_all = g1[:, :d]
    cw_sh = g1[0::2, d:d + depth * 3 * cw_cols].reshape(nchips, depth, 3, cw_cols)
    conv_w_full = jnp.transpose(cw_sh, (1, 2, 0, 3)).reshape(depth, 3, nchips * cw_cols)

    ada_part, c_act = _ada_part(c_all, w_ada, "ada_part")
    g2 = _allgather8(ada_part.reshape(depth * 8, ada_cols), "ag_ada")
    ada = _ada_select(g2, b_ada, "ada_select")

    shards = [wts[n][l].astype(BF16) for l in range(depth) for n in _BIG]
    gathered = _allgather_chips(shards, "ag_weights")
    wl, ps = [], []
    for l in range(depth):
        gw = dict(zip(_BIG, gathered[l * len(_BIG):(l + 1) * len(_BIG)]))
        w_in_full = jnp.transpose(gw["w_in"], (1, 0, 2)).reshape(d, -1)
        uw = nchips * gw["w_glu"].shape[1]
        split = w_in_full.shape[1] - uw
        wl.append(dict(
            w_in=jnp.concatenate([w_in_full[:, split:], w_in_full[:, :split]], axis=1),
            w_glu=gw["w_glu"].reshape(-1, gw["w_glu"].shape[2]), w_out=gw["w_out"].reshape(-1, gw["w_out"].shape[2]),
            w_up=gw["w_up"], w_down=gw["w_down"].reshape(-1, gw["w_down"].shape[2])))
        small = {n: wts[n][l] for n in _SMALL if n != "b_ada"}
        small["conv_w"] = conv_w_full[l]
        ps.append(small)

    loss_sum, grad_x, gws, gss = _local_step(x[0], loss_target[0], ada, wl, ps)
    loss = lax.psum(loss_sum, ("x", "y", "c"))

    stacks = []
    for l in range(depth):
        for n in _BIG:
            g = gws[l][n]
            if n == "w_in":
                uw = wl[l]["w_glu"].shape[0]
                g = jnp.concatenate([g[:, uw:], g[:, :uw]], axis=1)
                g = jnp.transpose(g.reshape(d, nchips, -1), (1, 0, 2))
            elif n != "w_up":
                g = g.reshape(nchips, g.shape[0] // nchips, g.shape[1])
            stacks.append(g)
    from_sibling = _exchange_halves(stacks, "rs_sibling")
    partials = [_add_half(stacks[k], from_sibling[k], cidx, f"rs_add_sibling_{k}") for k in range(len(stacks))]
    from_chips = _scatter_chips(partials, "rs_chips")
    from_chips = [lax.dynamic_update_slice(t, lax.dynamic_slice_in_dim(p, chip, 1, axis=0), (chip, 0, 0))
                  for t, p in zip(from_chips, partials)]
    reduced = [_sum_slots(t.reshape(-1, t.shape[2]), nchips, f"rs_sum_chips_{k}") for k, t in enumerate(from_chips)]
    theirs = _swap_with_sibling(reduced, "rs_share")
    nb = len(_BIG)
    big_grads = {}
    for w, n in enumerate(_BIG):
        own = jnp.stack([reduced[l * nb + w] for l in range(depth)])
        oth = jnp.stack([theirs[l * nb + w] for l in range(depth)])
        big_grads[n] = jnp.where(ci == 0, jnp.concatenate([own, oth], axis=1), jnp.concatenate([oth, own], axis=1))

    small_parts = [jnp.stack([gss[l][n] for l in range(depth)]) for n in _SMALL]
    pack_small = _pack(small_parts, 8)
    pack_cw = _pack([jnp.stack([gss[l]["conv_w"] for l in range(depth)])], 8)
    rows_small = pack_small.shape[0]
    mine = jnp.concatenate([pack_small, pack_cw], axis=0)
    g3 = _allgather8(mine, "ag_small")
    total = _sum_slots(g3, 8, "sum_small")
    grads = dict(big_grads)
    for n, v in zip(_SMALL, _unpack(total[:rows_small], [wts[n].shape for n in _SMALL])):
        grads[n] = v
    conv_w_grad = _unpack(total[rows_small:], [(depth, 3, nchips * cw_cols)])[0]
    grads["conv_w"] = lax.dynamic_slice_in_dim(conv_w_grad, chip * cw_cols, cw_cols, axis=2)

    d_ada_all = g3.reshape(8, -1)[:, :depth * 6 * d].reshape(8, depth, 6 * d)
    ca_t = jnp.transpose(c_act)
    grads["w_ada"] = jnp.stack([
        _wada_grad(ca_t, lax.dynamic_slice_in_dim(d_ada_all[:, l], chip * ada_cols, ada_cols, axis=1), f"w_ada_grad_{l}")
        for l in range(depth)])

    delta, new_m, new_v = {}, {}, {}
    for n in ("w_ada",) + _BIG + ("conv_w",):
        shp = wts[n].shape
        two_d = lambda t: t.reshape(-1, shp[-1])
        dl, nm, nv = _adamw(two_d(wts[n]), two_d(grads[n]), two_d(mom[n]), two_d(var[n]), f"adamw_{n}")
        delta[n], new_m[n], new_v[n] = dl.reshape(shp), nm.reshape(shp), nv.reshape(shp)
    packs = [_pack([t[n] for n in _SMALL], 8) for t in (wts, mom, var)]
    outs = _adamw(packs[0], total[:rows_small], packs[1], packs[2], "adamw_small")
    shapes = [wts[n].shape for n in _SMALL]
    for dst, packed in zip((delta, new_m, new_v), outs):
        for n, v in zip(_SMALL, _unpack(packed, shapes)):
            dst[n] = v

    return (loss, grad_x[None], *[grads[n] for n in _WEIGHTS], *[delta[n] for n in _WEIGHTS],
            *[new_m[n] for n in _WEIGHTS], *[new_v[n] for n in _WEIGHTS])
```

```python
import functools
import math

import jax
import jax.numpy as jnp
from jax import lax
from jax.experimental import pallas as pl
from jax.experimental.pallas import tpu as pltpu

F32 = jnp.float32
BF16 = jnp.bfloat16
EPS = 1e-6
NEG = -1e30
WINDOW = 128
HEAD_DIM = 64
KV_RATIO = 8
SSM_GROUP = 16
STATE = 64
PIECE = 128
PSTATES = 512
DEPTH = 2
ADAM_LR, ADAM_B1, ADAM_B2, ADAM_EPS, ADAM_WD, ADAM_STEP = 0.001, 0.9, 0.999, 1e-08, 0.01, 10
MIB = 1024 * 1024
_MATMUL_VMEM_BUDGET = 40 * MIB
MESH = pl.DeviceIdType.MESH


def _cparams(sem=None, vmem_mib=48):
    return pltpu.CompilerParams(dimension_semantics=sem, vmem_limit_bytes=vmem_mib * MIB)


def _gelu(x):
    c = math.sqrt(2.0 / math.pi)
    return 0.5 * x * (1.0 + jnp.tanh(c * (x + 0.044715 * (x * x * x))))


def _gelu_and_grad(x):
    c = math.sqrt(2.0 / math.pi)
    x2 = x * x
    t = jnp.tanh(c * (x + 0.044715 * (x2 * x)))
    half = 0.5 * (1.0 + t)
    return x * half, half + 0.5 * x * (1.0 - t * t) * c * (1.0 + 3.0 * 0.044715 * x2)


def _gelu_grad(x):
    return _gelu_and_grad(x)[1]


def _sigmoid(x):
    return 1.0 / (1.0 + jnp.exp(-x))


def _matmul(a, b, *, m, n, k, tm, tn, tk, out_dtype, name, ta=False, tb=False, a_stack=0, b_stack=0, o_stack=0):
    assert m % tm == 0 and n % tn == 0 and k % tk == 0, (name, m, n, k, tm, tn, tk)
    nk = k // tk

    if a_stack:
        assert not ta and (k // a_stack) % tk == 0
        per = (k // a_stack) // tk
        a_spec = pl.BlockSpec((None, tm, tk), lambda i, j, kk: (kk // per, i, kk % per))
    elif ta:
        a_spec = pl.BlockSpec((tk, tm), lambda i, j, kk: (kk, i))
    else:
        a_spec = pl.BlockSpec((tm, tk), lambda i, j, kk: (i, kk))
    if b_stack and tb:
        perb = (k // b_stack) // tk
        b_spec = pl.BlockSpec((None, tn, tk), lambda i, j, kk: (kk // perb, j, kk % perb))
    elif b_stack:
        perb = (n // b_stack) // tn
        b_spec = pl.BlockSpec((None, tk, tn), lambda i, j, kk: (j // perb, kk, j % perb))
    elif tb:
        b_spec = pl.BlockSpec((tn, tk), lambda i, j, kk: (j, kk))
    else:
        b_spec = pl.BlockSpec((tk, tn), lambda i, j, kk: (kk, j))
    if o_stack:
        pero = (n // o_stack) // tn
        o_spec = pl.BlockSpec((None, tm, tn), lambda i, j, kk: (j // pero, i, j % pero))
        o_shape = jax.ShapeDtypeStruct((o_stack, m, n // o_stack), out_dtype)
    else:
        o_spec = pl.BlockSpec((tm, tn), lambda i, j, kk: (i, j))
        o_shape = jax.ShapeDtypeStruct((m, n), out_dtype)
    dims = (((0 if ta else 1,), (1 if tb else 0,)), ((), ()))

    def body(a_ref, b_ref, o_ref, *acc):
        p = lax.dot_general(a_ref[...].astype(BF16), b_ref[...].astype(BF16), dims, preferred_element_type=F32)
        if nk == 1:
            o_ref[...] = p.astype(o_ref.dtype)
        else:
            acc_ref = acc[0]
            kk = pl.program_id(2)

            @pl.when(kk == 0)
            def _():
                acc_ref[...] = p

            @pl.when(kk > 0)
            def _():
                acc_ref[...] += p

            @pl.when(kk == nk - 1)
            def _():
                o_ref[...] = acc_ref[...].astype(o_ref.dtype)

    return pl.pallas_call(
        body, name=name, grid=(m // tm, n // tn, nk), in_specs=[a_spec, b_spec], out_specs=o_spec, out_shape=o_shape,
        scratch_shapes=[] if nk == 1 else [pltpu.VMEM((tm, tn), F32)],
        compiler_params=_cparams(("parallel", "parallel", "arbitrary"), 56),
    )(a, b)


def _row(d):
    return pl.BlockSpec((1, d), lambda i: (0, 0))


def _tok(tm, d):
    return pl.BlockSpec((tm, d), lambda i: (i, 0))


def _pre_fwd(x, g, sc, sh, name):
    s, d = x.shape
    tm = min(256, s)

    def body(x_ref, g_ref, sc_ref, sh_ref, h_ref):
        xv = x_ref[...]
        r = lax.rsqrt(jnp.mean(xv * xv, axis=-1, keepdims=True) + EPS)
        h_ref[...] = (((xv * r) * g_ref[...]) * (1.0 + sc_ref[...]) + sh_ref[...]).astype(BF16)

    return pl.pallas_call(
        body, name=name, grid=(s // tm,), in_specs=[_tok(tm, d), _row(d), _row(d), _row(d)], out_specs=_tok(tm, d),
        out_shape=jax.ShapeDtypeStruct((s, d), BF16), compiler_params=_cparams(("parallel",)),
    )(x, g, sc, sh)


def _post_fwd(x, o, g, gt, name):
    s, d = x.shape
    tm = min(256, s)

    def body(x_ref, o_ref, g_ref, gt_ref, y_ref):
        ov = o_ref[...]
        r = lax.rsqrt(jnp.mean(ov * ov, axis=-1, keepdims=True) + EPS)
        y_ref[...] = x_ref[...] + (1.0 + gt_ref[...]) * ((ov * r) * g_ref[...])

    return pl.pallas_call(
        body, name=name, grid=(s // tm,), in_specs=[_tok(tm, d), _tok(tm, d), _row(d), _row(d)], out_specs=_tok(tm, d),
        out_shape=jax.ShapeDtypeStruct((s, d), F32), compiler_params=_cparams(("parallel",)),
    )(x, o, g, gt)


def _post_bwd(dxo, o, g, gt, name):
    s, d = o.shape
    tm = min(256, s)

    def body(dx_ref, o_ref, g_ref, gt_ref, do_ref, dgt_ref, dg_ref):
        i = pl.program_id(0)
        dx = dx_ref[...]
        ov = o_ref[...]
        gv = g_ref[...]
        r = lax.rsqrt(jnp.mean(ov * ov, axis=-1, keepdims=True) + EPS)
        oh = ov * r
        dn = dx * (1.0 + gt_ref[...])
        e = dn * gv
        do_ref[...] = (r * (e - oh * jnp.mean(e * oh, axis=-1, keepdims=True))).astype(BF16)
        p_gt = jnp.sum(dx * (oh * gv), axis=0, keepdims=True)
        p_g = jnp.sum(dn * oh, axis=0, keepdims=True)

        @pl.when(i == 0)
        def _():
            dgt_ref[...] = p_gt
            dg_ref[...] = p_g

        @pl.when(i > 0)
        def _():
            dgt_ref[...] += p_gt
            dg_ref[...] += p_g

    row = jax.ShapeDtypeStruct((1, d), F32)
    return pl.pallas_call(
        body, name=name, grid=(s // tm,), in_specs=[_tok(tm, d), _tok(tm, d), _row(d), _row(d)],
        out_specs=(_tok(tm, d), _row(d), _row(d)), out_shape=(jax.ShapeDtypeStruct((s, d), BF16), row, row),
        compiler_params=_cparams(("arbitrary",)),
    )(dxo, o, g, gt)


def _pre_bwd(dres, dh, x, g, sc, name):
    s, d = x.shape
    tm = min(256, s)

    def body(dres_ref, dh_ref, x_ref, g_ref, sc_ref, dx_ref, dsh_ref, dsc_ref, dg_ref):
        i = pl.program_id(0)
        dh_v = dh_ref[...]
        xv = x_ref[...]
        gv = g_ref[...]
        one_sc = 1.0 + sc_ref[...]
        r = lax.rsqrt(jnp.mean(xv * xv, axis=-1, keepdims=True) + EPS)
        xh = xv * r
        e = dh_v * one_sc * gv
        dx_ref[...] = dres_ref[...] + r * (e - xh * jnp.mean(e * xh, axis=-1, keepdims=True))
        p_sh = jnp.sum(dh_v, axis=0, keepdims=True)
        p_sc = jnp.sum(dh_v * (xh * gv), axis=0, keepdims=True)
        p_g = jnp.sum(dh_v * one_sc * xh, axis=0, keepdims=True)

        @pl.when(i == 0)
        def _():
            dsh_ref[...] = p_sh
            dsc_ref[...] = p_sc
            dg_ref[...] = p_g

        @pl.when(i > 0)
        def _():
            dsh_ref[...] += p_sh
            dsc_ref[...] += p_sc
            dg_ref[...] += p_g

    row = jax.ShapeDtypeStruct((1, d), F32)
    return pl.pallas_call(
        body, name=name, grid=(s // tm,), in_specs=[_tok(tm, d), _tok(tm, d), _tok(tm, d), _row(d), _row(d)],
        out_specs=(_tok(tm, d), _row(d), _row(d), _row(d)), out_shape=(jax.ShapeDtypeStruct((s, d), F32), row, row, row),
        compiler_params=_cparams(("arbitrary",)),
    )(dres, dh, x, g, sc)


def _loss_head(y, tgt, name):
    s, d = y.shape
    tm = min(256, s)

    def body(y_ref, t_ref, dy_ref, l_ref):
        i = pl.program_id(0)
        err = y_ref[...] - t_ref[...]
        dy_ref[...] = err * (1.0 / d)
        part = jnp.zeros((1, 128), F32) + jnp.sum(err * err)

        @pl.when(i == 0)
        def _():
            l_ref[...] = part

        @pl.when(i > 0)
        def _():
            l_ref[...] += part

    return pl.pallas_call(
        body, name=name, grid=(s // tm,), in_specs=[_tok(tm, d), _tok(tm, d)],
        out_specs=(_tok(tm, d), pl.BlockSpec((1, 128), lambda i: (0, 0))),
        out_shape=(jax.ShapeDtypeStruct((s, d), F32), jax.ShapeDtypeStruct((1, 128), F32)),
        compiler_params=_cparams(("arbitrary",)),
    )(y, tgt)


def _gelu_fwd(y, name):
    s, u = y.shape
    tm = min(512, s)

    def body(y_ref, z_ref):
        z_ref[...] = _gelu(y_ref[...]).astype(BF16)

    return pl.pallas_call(
        body, name=name, grid=(s // tm,), in_specs=[_tok(tm, u)], out_specs=_tok(tm, u),
        out_shape=jax.ShapeDtypeStruct((s, u), BF16), compiler_params=_cparams(("parallel",)),
    )(y)


def _merge_fwd(attn, y, gl, ga, gs, name):
    s, aw = attn.shape
    uw = y.shape[1]
    tm = min(256, s)

    def body(a_ref, y_ref, gl_ref, ga_ref, gs_ref, m_ref):
        av = a_ref[...]
        ra = lax.rsqrt(jnp.mean(av * av, axis=-1, keepdims=True) + EPS)
        m_ref[:, :aw] = ((av * ra) * ga_ref[...]).astype(BF16)
        ssm = _gelu(y_ref[...]) * _sigmoid(gl_ref[...])
        rs = lax.rsqrt(jnp.mean(ssm * ssm, axis=-1, keepdims=True) + EPS)
        m_ref[:, aw:] = ((ssm * rs) * gs_ref[...]).astype(BF16)

    return pl.pallas_call(
        body, name=name, grid=(s // tm,), in_specs=[_tok(tm, aw), _tok(tm, uw), _tok(tm, uw), _row(aw), _row(uw)],
        out_specs=_tok(tm, aw + uw), out_shape=jax.ShapeDtypeStruct((s, aw + uw), BF16),
        compiler_params=_cparams(("parallel",)),
    )(attn, y, gl, ga, gs)


def _merge_bwd(dm, attn, y, gl, ga, gs, name):
    s, aw = attn.shape
    uw = y.shape[1]
    tm = min(256, s)

    def body(dm_ref, a_ref, y_ref, gl_ref, ga_ref, gs_ref, da_ref, dgl_ref, dz_ref, dga_ref, dgs_ref):
        i = pl.program_id(0)
        av = a_ref[...]
        dma = dm_ref[:, :aw]
        ra = lax.rsqrt(jnp.mean(av * av, axis=-1, keepdims=True) + EPS)
        ah = av * ra
        e = dma * ga_ref[...]
        da_ref[...] = (ra * (e - ah * jnp.mean(e * ah, axis=-1, keepdims=True))).astype(BF16)
        p_ga = jnp.sum(dma * ah, axis=0, keepdims=True)

        z = _gelu(y_ref[...])
        sig = _sigmoid(gl_ref[...])
        ssm = z * sig
        dms = dm_ref[:, aw:]
        rs = lax.rsqrt(jnp.mean(ssm * ssm, axis=-1, keepdims=True) + EPS)
        sh = ssm * rs
        e2 = dms * gs_ref[...]
        dssm = rs * (e2 - sh * jnp.mean(e2 * sh, axis=-1, keepdims=True))
        dz_ref[...] = dssm * sig
        dgl_ref[...] = (dssm * z * sig * (1.0 - sig)).astype(BF16)
        p_gs = jnp.sum(dms * sh, axis=0, keepdims=True)

        @pl.when(i == 0)
        def _():
            dga_ref[...] = p_ga
            dgs_ref[...] = p_gs

        @pl.when(i > 0)
        def _():
            dga_ref[...] += p_ga
            dgs_ref[...] += p_gs

    return pl.pallas_call(
        body, name=name, grid=(s // tm,),
        in_specs=[_tok(tm, aw + uw), _tok(tm, aw), _tok(tm, uw), _tok(tm, uw), _row(aw), _row(uw)],
        out_specs=(_tok(tm, aw), _tok(tm, uw), _tok(tm, uw), _row(aw), _row(uw)),
        out_shape=(jax.ShapeDtypeStruct((s, aw), BF16), jax.ShapeDtypeStruct((s, uw), BF16), jax.ShapeDtypeStruct((s, uw), F32),
                   jax.ShapeDtypeStruct((1, aw), F32), jax.ShapeDtypeStruct((1, uw), F32)),
        compiler_params=_cparams(("arbitrary",)),
    )(dm, attn, y, gl, ga, gs)


def _gelu_bwd(dzd, dz2, y, name):
    s, u = y.shape
    tm = min(512, s)

    def body(a_ref, b_ref, y_ref, o_ref):
        o_ref[...] = (a_ref[...] + b_ref[...]) * _gelu_grad(y_ref[...])

    return pl.pallas_call(
        body, name=name, grid=(s // tm,), in_specs=[_tok(tm, u), _tok(tm, u), _tok(tm, u)], out_specs=_tok(tm, u),
        out_shape=jax.ShapeDtypeStruct((s, u), F32), compiler_params=_cparams(("parallel",)),
    )(dzd, dz2, y)


def _attn_scores(qh, kb, sink, valid):
    s = lax.dot_general(qh, kb, (((1,), (1,)), ((), ())), preferred_element_type=F32) * (HEAD_DIM ** -0.5)
    s = jnp.where(valid, s, NEG)
    m = jnp.maximum(jnp.max(s, axis=-1, keepdims=True), sink)
    e = jnp.exp(s - m)
    esink = jnp.exp(sink - m)
    den = jnp.sum(e, axis=-1, keepdims=True) + esink
    return e / den, esink / den


def _attn_valid(i):
    qi = lax.broadcasted_iota(jnp.int32, (WINDOW, 2 * WINDOW), 0)
    kj = lax.broadcasted_iota(jnp.int32, (WINDOW, 2 * WINDOW), 1)
    return (kj > qi) & (kj <= qi + WINDOW) & ((kj >= WINDOW) | (i > 0))


def _attn_specs(aw, uw, kvw):
    qblk = uw // aw
    kvblk = (uw + aw) // (2 * kvw)
    assert uw % aw == 0 and (uw + aw) % (2 * kvw) == 0
    return [
        pl.BlockSpec(memory_space=pltpu.SMEM),
        pl.BlockSpec((WINDOW, aw), lambda i: (i, qblk)),
        pl.BlockSpec((WINDOW, 2 * kvw), lambda i: (i, kvblk)),
        pl.BlockSpec((WINDOW, 2 * kvw), lambda i: (jnp.maximum(i - 1, 0), kvblk)),
    ]


def _attn_fwd(proj, sinks, aw, uw, name):
    s = proj.shape[0]
    nq = aw // HEAD_DIM
    nkv = nq // KV_RATIO
    kvw = nkv * HEAD_DIM

    def body(sink_ref, q_ref, kvc_ref, kvp_ref, o_ref):
        valid = _attn_valid(pl.program_id(0))
        q = q_ref[...]
        kvc = kvc_ref[...]
        kvp = kvp_ref[...]
        for hk in range(nkv):
            kb = jnp.concatenate([kvp[:, hk * HEAD_DIM:(hk + 1) * HEAD_DIM], kvc[:, hk * HEAD_DIM:(hk + 1) * HEAD_DIM]], axis=0)
            vb = jnp.concatenate([kvp[:, kvw + hk * HEAD_DIM:kvw + (hk + 1) * HEAD_DIM],
                                  kvc[:, kvw + hk * HEAD_DIM:kvw + (hk + 1) * HEAD_DIM]], axis=0)
            for g in range(KV_RATIO):
                hq = hk * KV_RATIO + g
                p, _ = _attn_scores(q[:, hq * HEAD_DIM:(hq + 1) * HEAD_DIM], kb, sink_ref[hq], valid)
                o_ref[:, hq * HEAD_DIM:(hq + 1) * HEAD_DIM] = jnp.dot(p.astype(BF16), vb, preferred_element_type=F32)

    return pl.pallas_call(
        body, name=name, grid=(s // WINDOW,), in_specs=_attn_specs(aw, uw, kvw),
        out_specs=pl.BlockSpec((WINDOW, aw), lambda i: (i, 0)), out_shape=jax.ShapeDtypeStruct((s, aw), F32),
        compiler_params=_cparams(("parallel",)),
    )(sinks, proj, proj, proj)


def _attn_bwd(proj, sinks, attn, dattn, aw, uw, name):
    s = proj.shape[0]
    nq = aw // HEAD_DIM
    nkv = nq // KV_RATIO
    kvw = nkv * HEAD_DIM
    hd = HEAD_DIM

    def body(sink_ref, q_ref, kvc_ref, kvp_ref, o_ref, do_ref, dq_ref, dc_ref, dp_ref, ds_ref):
        i = pl.program_id(0)
        valid = _attn_valid(i)
        q = q_ref[...]
        kvc = kvc_ref[...]
        kvp = kvp_ref[...]
        lane = lax.broadcasted_iota(jnp.int32, (1, nq), 1)
        dsink = jnp.zeros((1, nq), F32)
        for hk in range(nkv):
            kb = jnp.concatenate([kvp[:, hk * hd:(hk + 1) * hd], kvc[:, hk * hd:(hk + 1) * hd]], axis=0)
            vb = jnp.concatenate([kvp[:, kvw + hk * hd:kvw + (hk + 1) * hd], kvc[:, kvw + hk * hd:kvw + (hk + 1) * hd]], axis=0)
            dkb = jnp.zeros((2 * WINDOW, hd), F32)
            dvb = jnp.zeros((2 * WINDOW, hd), F32)
            for g in range(KV_RATIO):
                hq = hk * KV_RATIO + g
                qh = q[:, hq * hd:(hq + 1) * hd]
                p, psink = _attn_scores(qh, kb, sink_ref[hq], valid)
                do_h = do_ref[:, hq * hd:(hq + 1) * hd]
                delta = jnp.sum(do_h.astype(F32) * o_ref[:, hq * hd:(hq + 1) * hd], axis=-1, keepdims=True)
                dpv = lax.dot_general(do_h, vb, (((1,), (1,)), ((), ())), preferred_element_type=F32)
                dsb = (p * (dpv - delta) * (hd ** -0.5)).astype(BF16)
                dq_ref[:, hq * hd:(hq + 1) * hd] = jnp.dot(dsb, kb, preferred_element_type=F32).astype(BF16)
                dkb = dkb + lax.dot_general(dsb, qh, (((0,), (0,)), ((), ())), preferred_element_type=F32)
                dvb = dvb + lax.dot_general(p.astype(BF16), do_h, (((0,), (0,)), ((), ())), preferred_element_type=F32)
                dsink = dsink + jnp.where(lane == hq, -jnp.sum(psink * delta), 0.0)
            dp_ref[:, hk * hd:(hk + 1) * hd] = dkb[:WINDOW]
            dc_ref[:, hk * hd:(hk + 1) * hd] = dkb[WINDOW:]
            dp_ref[:, kvw + hk * hd:kvw + (hk + 1) * hd] = dvb[:WINDOW]
            dc_ref[:, kvw + hk * hd:kvw + (hk + 1) * hd] = dvb[WINDOW:]

        @pl.when(i == 0)
        def _():
            ds_ref[...] = dsink

        @pl.when(i > 0)
        def _():
            ds_ref[...] += dsink

    blk_a = pl.BlockSpec((WINDOW, aw), lambda i: (i, 0))
    blk_kv = pl.BlockSpec((WINDOW, 2 * kvw), lambda i: (i, 0))
    return pl.pallas_call(
        body, name=name, grid=(s // WINDOW,), in_specs=_attn_specs(aw, uw, kvw) + [blk_a, blk_a],
        out_specs=(blk_a, blk_kv, blk_kv, pl.BlockSpec((1, nq), lambda i: (0, 0))),
        out_shape=(jax.ShapeDtypeStruct((s, aw), BF16), jax.ShapeDtypeStruct((s, 2 * kvw), F32),
                   jax.ShapeDtypeStruct((s, 2 * kvw), F32), jax.ShapeDtypeStruct((1, nq), F32)),
        compiler_params=_cparams(("arbitrary",)),
    )(sinks, proj, proj, proj, attn, dattn)


def _assemble_dproj(du, dq, dkv_cur, dkv_prev, name):
    s, uw = du.shape
    aw = dq.shape[1]
    kv2 = dkv_cur.shape[1]
    nb = s // WINDOW

    def body(du_ref, dq_ref, dc_ref, dp_ref, o_ref):
        i = pl.program_id(0)
        o_ref[:, :uw] = du_ref[...].astype(BF16)
        o_ref[:, uw:uw + aw] = dq_ref[...]
        nxt = jnp.where(i < nb - 1, 1.0, 0.0)
        o_ref[:, uw + aw:] = (dc_ref[...] + nxt * dp_ref[...]).astype(BF16)

    return pl.pallas_call(
        body, name=name, grid=(nb,),
        in_specs=[_tok(WINDOW, uw), _tok(WINDOW, aw), _tok(WINDOW, kv2),
                  pl.BlockSpec((WINDOW, kv2), lambda i: (jnp.minimum(i + 1, nb - 1), 0))],
        out_specs=_tok(WINDOW, uw + aw + kv2), out_shape=jax.ShapeDtypeStruct((s, uw + aw + kv2), BF16),
        compiler_params=_cparams(("parallel",)),
    )(du, dq, dkv_cur, dkv_prev)


def _zoh(lr, li, ls, btr, bti):
    dt = jnp.exp(ls)
    mag = jnp.exp(lr * dt)
    ang = li * dt
    ar = mag * jnp.cos(ang)
    ai = mag * jnp.sin(ang)
    den = lr * lr + li * li
    fr = ((ar - 1.0) * lr + ai * li) / den
    fi = (ai * lr - (ar - 1.0) * li) / den
    return ar, ai, fr[None] * btr - fi[None] * bti, fr[None] * bti + fi[None] * btr


def _ssm_prep(lr, li, ls, btr, bti, name):
    def body(lr_ref, li_ref, ls_ref, btr_ref, bti_ref, ar_ref, ai_ref, bbr_ref, bbi_ref):
        ar, ai, bbr, bbi = _zoh(lr_ref[...], li_ref[...], ls_ref[...], btr_ref[...], bti_ref[...])
        ar_ref[...] = ar
        ai_ref[...] = ai
        bbr_ref[...] = bbr
        bbi_ref[...] = bbi

    s2 = jax.ShapeDtypeStruct(lr.shape, F32)
    s3 = jax.ShapeDtypeStruct(btr.shape, F32)
    return pl.pallas_call(body, name=name, out_shape=(s2, s2, s3, s3))(lr, li, ls, btr, bti)


def _ssm_prep_bwd(lr, li, ls, btr, bti, dar, dai, dbbr, dbbi, name):
    def body(lr_ref, li_ref, ls_ref, btr_ref, bti_ref, dar_ref, dai_ref, dbbr_ref, dbbi_ref, o1, o2, o3, o4, o5):
        _, vjp = jax.vjp(_zoh, lr_ref[...], li_ref[...], ls_ref[...], btr_ref[...], bti_ref[...])
        g = vjp((dar_ref[...], dai_ref[...], dbbr_ref[...], dbbi_ref[...]))
        for o, v in zip((o1, o2, o3, o4, o5), g):
            o[...] = v

    s2 = jax.ShapeDtypeStruct(lr.shape, F32)
    s3 = jax.ShapeDtypeStruct(btr.shape, F32)
    return pl.pallas_call(body, name=name, out_shape=(s2, s2, s2, s3, s3))(lr, li, ls, btr, bti, dar, dai, dbbr, dbbi)


def _state_tiles(ref):
    return [ref[:, cb * 128:(cb + 1) * 128] for cb in range(4)]


def _gather_rows(ref_re, ref_im, r, t):
    return jnp.concatenate([ref_re.at[cb][pl.ds(r, t, stride=8), :] for cb in range(4)]
                           + [ref_im.at[cb][pl.ds(r, t, stride=8), :] for cb in range(4)], axis=1)


def _scatter_rows(ref_re, ref_im, r, t, val):
    for cb in range(4):
        ref_re.at[cb][pl.ds(r, t, stride=8), :] = val[:, cb * 128:(cb + 1) * 128]
        ref_im.at[cb][pl.ds(r, t, stride=8), :] = val[:, PSTATES + cb * 128:PSTATES + (cb + 1) * 128]


def _ssm_fwd(proj, bp, cp, a_re, a_im, dvec, uw, name, t=128):
    s = proj.shape[0]
    npc = uw // PIECE
    assert npc == 8 and s % t == 0

    def body(u_ref, bp_ref, cp_ref, ar_ref, ai_ref, d_ref, y_ref, xr_ref, xi_ref, cr_ref, ci_ref):
        i = pl.program_id(0)

        @pl.when(i == 0)
        def _():
            cr_ref[...] = jnp.zeros_like(cr_ref)
            ci_ref[...] = jnp.zeros_like(ci_ref)

        for r in range(npc):
            bu = jnp.dot(u_ref[:, r * PIECE:(r + 1) * PIECE], bp_ref[r], preferred_element_type=F32)
            _scatter_rows(xr_ref, xi_ref, r, t, bu)
        ar = _state_tiles(ar_ref)
        ai = _state_tiles(ai_ref)

        def step(tt, carry):
            xr, xi = carry
            off = pl.multiple_of(tt * 8, 8)
            nr, ni = [], []
            for cb in range(4):
                vr = ar[cb] * xr[cb] - ai[cb] * xi[cb] + xr_ref[cb, pl.ds(off, 8), :]
                vi = ar[cb] * xi[cb] + ai[cb] * xr[cb] + xi_ref[cb, pl.ds(off, 8), :]
                xr_ref[cb, pl.ds(off, 8), :] = vr
                xi_ref[cb, pl.ds(off, 8), :] = vi
                nr.append(vr)
                ni.append(vi)
            return tuple(nr), tuple(ni)

        xr, xi = lax.fori_loop(0, t, step, (tuple(_state_tiles(cr_ref)), tuple(_state_tiles(ci_ref))), unroll=4)
        for cb in range(4):
            cr_ref[:, cb * 128:(cb + 1) * 128] = xr[cb]
            ci_ref[:, cb * 128:(cb + 1) * 128] = xi[cb]
        for r in range(npc):
            xs = _gather_rows(xr_ref, xi_ref, r, t).astype(BF16)
            y_ref[:, r * PIECE:(r + 1) * PIECE] = (
                jnp.dot(xs, cp_ref[r], preferred_element_type=F32)
                + d_ref[:, r * PIECE:(r + 1) * PIECE] * u_ref[:, r * PIECE:(r + 1) * PIECE].astype(F32))

    full3 = lambda shp: pl.BlockSpec(shp, lambda i: (0, 0, 0))
    full2 = lambda shp: pl.BlockSpec(shp, lambda i: (0, 0))
    xs_spec = pl.BlockSpec((4, t * 8, 128), lambda i: (0, i, 0))
    xs_shape = jax.ShapeDtypeStruct((4, s * 8, 128), F32)
    return pl.pallas_call(
        body, name=name, grid=(s // t,),
        in_specs=[pl.BlockSpec((t, uw), lambda i: (i, 0)), full3(bp.shape), full3(cp.shape), full2(a_re.shape), full2(a_im.shape),
                  full2(dvec.shape)],
        out_specs=(pl.BlockSpec((t, uw), lambda i: (i, 0)), xs_spec, xs_spec),
        out_shape=(jax.ShapeDtypeStruct((s, uw), F32), xs_shape, xs_shape),
        scratch_shapes=[pltpu.VMEM((8, PSTATES), F32), pltpu.VMEM((8, PSTATES), F32)],
        compiler_params=_cparams(("arbitrary",), 56),
    )(proj, bp, cp, a_re, a_im, dvec)


def _ssm_bwd(dy, proj, xs_re, xs_im, cpt, bpt, a_re, a_im, dvec, uw, name, t=128):
    s = proj.shape[0]
    npc = uw // PIECE
    nt = s // t
    assert npc == 8 and s % t == 0

    def body(dy_ref, u_ref, xr_ref, xi_ref, hr_ref, hi_ref, cpt_ref, bpt_ref, ar_ref, ai_ref, d_ref,
             du_ref, dbp_ref, dcp_ref, dar_ref, dai_ref, dd_ref, gr_ref, gi_ref, lr_ref, li_ref):
        i = pl.program_id(0)

        @pl.when(i == 0)
        def _():
            lr_ref[...] = jnp.zeros_like(lr_ref)
            li_ref[...] = jnp.zeros_like(li_ref)
            dbp_ref[...] = jnp.zeros_like(dbp_ref)
            dcp_ref[...] = jnp.zeros_like(dcp_ref)
            dar_ref[...] = jnp.zeros_like(dar_ref)
            dai_ref[...] = jnp.zeros_like(dai_ref)
            dd_ref[...] = jnp.zeros_like(dd_ref)

        dyb = dy_ref[...].astype(BF16)
        for r in range(npc):
            gx = jnp.dot(dyb[:, r * PIECE:(r + 1) * PIECE], cpt_ref[r], preferred_element_type=F32)
            _scatter_rows(gr_ref, gi_ref, r, t, gx)
        ar = _state_tiles(ar_ref)
        ai = _state_tiles(ai_ref)

        def adjoint(off, lam_r, lam_i, xpr, xpi, acc_r, acc_i):
            nr, ni, qr, qi = [], [], [], []
            for cb in range(4):
                vr = gr_ref[cb, pl.ds(off, 8), :] + ar[cb] * lam_r[cb] + ai[cb] * lam_i[cb]
                vi = gi_ref[cb, pl.ds(off, 8), :] + ar[cb] * lam_i[cb] - ai[cb] * lam_r[cb]
                gr_ref[cb, pl.ds(off, 8), :] = vr
                gi_ref[cb, pl.ds(off, 8), :] = vi
                nr.append(vr)
                ni.append(vi)
                qr.append(acc_r[cb] + vr * xpr[cb] + vi * xpi[cb])
                qi.append(acc_i[cb] + vi * xpr[cb] - vr * xpi[cb])
            return tuple(nr), tuple(ni), tuple(qr), tuple(qi)

        def step(j, carry):
            lam_r, lam_i, acc_r, acc_i = carry
            tt = t - 1 - j
            off = pl.multiple_of(tt * 8, 8)
            offp = pl.multiple_of(tt * 8 - 8, 8)
            xpr = [xr_ref[cb, pl.ds(offp, 8), :] for cb in range(4)]
            xpi = [xi_ref[cb, pl.ds(offp, 8), :] for cb in range(4)]
            return adjoint(off, lam_r, lam_i, xpr, xpi, acc_r, acc_i)

        zero4 = tuple(jnp.zeros((8, 128), F32) for _ in range(4))
        carry = lax.fori_loop(0, t - 1, step, (tuple(_state_tiles(lr_ref)), tuple(_state_tiles(li_ref)), zero4, zero4), unroll=4)
        has_prev = jnp.where(i < nt - 1, 1.0, 0.0)
        xpr = [hr_ref[cb] * has_prev for cb in range(4)]
        xpi = [hi_ref[cb] * has_prev for cb in range(4)]
        lam_r, lam_i, acc_r, acc_i = adjoint(0, carry[0], carry[1], xpr, xpi, carry[2], carry[3])
        for cb in range(4):
            lr_ref[:, cb * 128:(cb + 1) * 128] = lam_r[cb]
            li_ref[:, cb * 128:(cb + 1) * 128] = lam_i[cb]
            dar_ref[:, cb * 128:(cb + 1) * 128] += acc_r[cb]
            dai_ref[:, cb * 128:(cb + 1) * 128] += acc_i[cb]

        dyv = dy_ref[...]
        uv = u_ref[...]
        dd_ref[...] += jnp.sum(dyv * uv.astype(F32), axis=0, keepdims=True)
        for r in range(npc):
            lam = _gather_rows(gr_ref, gi_ref, r, t).astype(BF16)
            sl = slice(r * PIECE, (r + 1) * PIECE)
            du_ref[:, sl] = jnp.dot(lam, bpt_ref[r], preferred_element_type=F32) + d_ref[:, sl] * dyv[:, sl]
            dbp_ref[r] += lax.dot_general(uv[:, sl], lam, (((0,), (0,)), ((), ())), preferred_element_type=F32)
            xs = _gather_rows(xr_ref, xi_ref, r, t).astype(BF16)
            dcp_ref[r] += lax.dot_general(xs, dyb[:, sl], (((0,), (0,)), ((), ())), preferred_element_type=F32)

    rev = lambda i: (nt - 1 - i, 0)
    full3 = lambda shp: pl.BlockSpec(shp, lambda i: (0, 0, 0))
    full2 = lambda shp: pl.BlockSpec(shp, lambda i: (0, 0))
    xs_spec = pl.BlockSpec((4, t * 8, 128), lambda i: (0, nt - 1 - i, 0))
    halo_spec = pl.BlockSpec((4, 8, 128), lambda i: (0, jnp.maximum((nt - 1 - i) * t - 1, 0), 0))
    st = jax.ShapeDtypeStruct((8, PSTATES), F32)
    return pl.pallas_call(
        body, name=name, grid=(nt,),
        in_specs=[pl.BlockSpec((t, uw), rev), pl.BlockSpec((t, uw), rev), xs_spec, xs_spec, halo_spec, halo_spec,
                  full3(cpt.shape), full3(bpt.shape), full2(a_re.shape), full2(a_im.shape), full2(dvec.shape)],
        out_specs=(pl.BlockSpec((t, uw), rev), full3((npc, PIECE, 2 * PSTATES)), full3((npc, 2 * PSTATES, PIECE)),
                   full2((8, PSTATES)), full2((8, PSTATES)), full2((1, uw))),
        out_shape=(jax.ShapeDtypeStruct((s, uw), F32), jax.ShapeDtypeStruct((npc, PIECE, 2 * PSTATES), F32),
                   jax.ShapeDtypeStruct((npc, 2 * PSTATES, PIECE), F32), st, st, jax.ShapeDtypeStruct((1, uw), F32)),
        scratch_shapes=[pltpu.VMEM((4, t * 8, 128), F32), pltpu.VMEM((4, t * 8, 128), F32),
                        pltpu.VMEM((8, PSTATES), F32), pltpu.VMEM((8, PSTATES), F32)],
        compiler_params=_cparams(("arbitrary",), 56),
    )(dy, proj, xs_re, xs_im, xs_re, xs_im, cpt, bpt, a_re, a_im, dvec)


def _conv3(x, h6, h7, w, b):
    row = lax.broadcasted_iota(jnp.int32, x.shape, 0)
    x1 = jnp.where(row == 0, h7, pltpu.roll(x, 1, 0))
    x2 = jnp.where(row == 0, h6, jnp.where(row == 1, h7, pltpu.roll(x, 2, 0)))
    return ((b + x2 * w[0:1]) + x1 * w[1:2]) + x * w[2:3], x1, x2


def _ffn_tiles(s, f):
    tm = min(256, s)
    tn = f // 4 if (f // 4) % 128 == 0 else f
    return tm, tn


def _conv_glu_fwd(up0, cw, cb, name):
    _, s, f = up0.shape
    tm, tn = _ffn_tiles(s, f)
    hb = tm // 8

    def body(x_ref, h_ref, w_ref, b_ref, a_ref):
        first = jnp.where(pl.program_id(0) > 0, 1.0, 0.0)
        ups = []
        for p in range(2):
            h = h_ref[p].astype(F32) * first
            ups.append(_conv3(x_ref[p].astype(F32), h[6:7], h[7:8], w_ref[p], b_ref[p])[0])
        a_ref[...] = (_gelu(ups[1]) * ups[0]).astype(BF16)

    return pl.pallas_call(
        body, name=name, grid=(s // tm, f // tn),
        in_specs=[pl.BlockSpec((2, tm, tn), lambda i, j: (0, i, j)),
                  pl.BlockSpec((2, 8, tn), lambda i, j: (0, jnp.maximum(i * hb - 1, 0), j)),
                  pl.BlockSpec((2, 3, tn), lambda i, j: (0, 0, j)), pl.BlockSpec((2, 1, tn), lambda i, j: (0, 0, j))],
        out_specs=pl.BlockSpec((tm, tn), lambda i, j: (i, j)), out_shape=jax.ShapeDtypeStruct((s, f), BF16),
        compiler_params=_cparams(("parallel", "parallel")),
    )(up0, up0, cw, cb)


def _ffn_bwd_gate(da, up0, cw, cb, name):
    _, s, f = up0.shape
    tm, tn = _ffn_tiles(s, f)
    hb = tm // 8

    def body(da_ref, x_ref, h_ref, w_ref, b_ref, d_ref, dw_ref, db_ref):
        i = pl.program_id(1)
        first = jnp.where(i > 0, 1.0, 0.0)
        ups, taps = [], []
        for p in range(2):
            h = h_ref[p].astype(F32) * first
            xv = x_ref[p].astype(F32)
            up, x1, x2 = _conv3(xv, h[6:7], h[7:8], w_ref[p], b_ref[p])
            ups.append(up)
            taps.append((x2, x1, xv))
        dav = da_ref[...]
        gate, dgate = _gelu_and_grad(ups[1])
        douts = (dav * gate, dav * ups[0] * dgate)

        @pl.when(i == 0)
        def _():
            dw_ref[...] = jnp.zeros_like(dw_ref)
            db_ref[...] = jnp.zeros_like(db_ref)

        for p in range(2):
            d_ref[p] = douts[p].astype(BF16)
            db_ref[p] += jnp.sum(douts[p], axis=0, keepdims=True)
            for kk in range(3):
                dw_ref[p, kk:kk + 1, :] += jnp.sum(douts[p] * taps[p][kk], axis=0, keepdims=True)

    return pl.pallas_call(
        body, name=name, grid=(f // tn, s // tm),
        in_specs=[pl.BlockSpec((tm, tn), lambda j, i: (i, j)), pl.BlockSpec((2, tm, tn), lambda j, i: (0, i, j)),
                  pl.BlockSpec((2, 8, tn), lambda j, i: (0, jnp.maximum(i * hb - 1, 0), j)),
                  pl.BlockSpec((2, 3, tn), lambda j, i: (0, 0, j)), pl.BlockSpec((2, 1, tn), lambda j, i: (0, 0, j))],
        out_specs=(pl.BlockSpec((2, tm, tn), lambda j, i: (0, i, j)), pl.BlockSpec((2, 3, tn), lambda j, i: (0, 0, j)),
                   pl.BlockSpec((2, 1, tn), lambda j, i: (0, 0, j))),
        out_shape=(jax.ShapeDtypeStruct((2, s, f), BF16), jax.ShapeDtypeStruct((2, 3, f), F32), jax.ShapeDtypeStruct((2, 1, f), F32)),
        compiler_params=_cparams(("parallel", "arbitrary")),
    )(da, up0, up0, cw, cb)


def _conv_bwd(dup, cw, name):
    _, s, f = dup.shape
    tm, tn = _ffn_tiles(s, f)
    hb = tm // 8
    nb = s // tm

    def body(d_ref, h_ref, w_ref, o_ref):
        last = jnp.where(pl.program_id(0) < nb - 1, 1.0, 0.0)
        row = lax.broadcasted_iota(jnp.int32, (tm, tn), 0)
        for p in range(2):
            d = d_ref[p].astype(F32)
            h = h_ref[p].astype(F32) * last
            d1 = jnp.where(row == tm - 1, h[0:1], pltpu.roll(d, tm - 1, 0))
            d2 = jnp.where(row == tm - 1, h[1:2], jnp.where(row == tm - 2, h[0:1], pltpu.roll(d, tm - 2, 0)))
            w = w_ref[p]
            o_ref[p] = (d * w[2:3] + d1 * w[1:2] + d2 * w[0:1]).astype(BF16)

    return pl.pallas_call(
        body, name=name, grid=(nb, f // tn),
        in_specs=[pl.BlockSpec((2, tm, tn), lambda i, j: (0, i, j)),
                  pl.BlockSpec((2, 8, tn), lambda i, j: (0, jnp.minimum((i + 1) * hb, s // 8 - 1), j)),
                  pl.BlockSpec((2, 3, tn), lambda i, j: (0, 0, j))],
        out_specs=pl.BlockSpec((2, tm, tn), lambda i, j: (0, i, j)), out_shape=jax.ShapeDtypeStruct((2, s, f), BF16),
        compiler_params=_cparams(("parallel", "parallel")),
    )(dup, dup, cw)


def _ada_part(c_all, w_ada, name):
    nb, d = c_all.shape
    depth, _, cols = w_ada.shape
    tn = 1024 if cols % 1024 == 0 else cols

    def body(c_ref, w_ref, o_ref, ca_ref):
        cv = c_ref[...]
        ca = cv * _sigmoid(cv)
        ca_ref[...] = ca
        o_ref[...] = jnp.dot(ca.astype(BF16), w_ref[...].astype(BF16), preferred_element_type=F32)

    return pl.pallas_call(
        body, name=name, grid=(depth, cols // tn),
        in_specs=[pl.BlockSpec((nb, d), lambda l, j: (0, 0)), pl.BlockSpec((None, d, tn), lambda l, j: (l, 0, j))],
        out_specs=(pl.BlockSpec((None, nb, tn), lambda l, j: (l, 0, j)), pl.BlockSpec((nb, d), lambda l, j: (0, 0))),
        out_shape=(jax.ShapeDtypeStruct((depth, nb, cols), F32), jax.ShapeDtypeStruct((nb, d), F32)),
        compiler_params=_cparams(("arbitrary", "arbitrary")),
    )(c_all, w_ada)


def _ada_select(gath, b_ada, name):
    depth, n6 = b_ada.shape
    cols = gath.shape[1]

    def body(g_ref, b_ref, o_ref):
        me = 4 * lax.axis_index("x") + 2 * lax.axis_index("y") + lax.axis_index("c")
        for l in range(depth):
            for j in range(n6 // cols):
                row = (2 * j) * (8 * depth) + l * 8 + me
                o_ref[l:l + 1, j * cols:(j + 1) * cols] = g_ref[pl.ds(row, 1), :] + b_ref[l:l + 1, j * cols:(j + 1) * cols]

    return pl.pallas_call(body, name=name, out_shape=jax.ShapeDtypeStruct((depth, n6), F32))(gath, b_ada)


def _wada_grad(ca_t, d_sel, name):
    d, nb = ca_t.shape
    cols = d_sel.shape[1]
    tm = min(256, d)

    def body(a_ref, g_ref, o_ref):
        acc = a_ref[:, 0:1] * g_ref[0:1, :]
        for b in range(1, nb):
            acc = acc + a_ref[:, b:b + 1] * g_ref[b:b + 1, :]
        o_ref[...] = acc

    return pl.pallas_call(
        body, name=name, grid=(d // tm,), in_specs=[pl.BlockSpec((tm, nb), lambda i: (i, 0)), pl.BlockSpec((nb, cols), lambda i: (0, 0))],
        out_specs=pl.BlockSpec((tm, cols), lambda i: (i, 0)), out_shape=jax.ShapeDtypeStruct((d, cols), F32),
        compiler_params=_cparams(("parallel",)),
    )(ca_t, d_sel)


def _block_rows(r, c):
    tr = r
    for cand in (2048, 1024, 512, 256, 128, 64, 32, 16, 8):
        if r % cand == 0 and cand * c * 4 <= MIB:
            tr = cand
            break
    else:
        for cand in (8, 16, 32):
            if r % cand == 0:
                tr = cand
                break
    return tr


def _adamw(w, g, m, v, name):
    r, c = w.shape
    tr = _block_rows(r, c)
    c1 = 1.0 - ADAM_B1 ** ADAM_STEP
    c2 = 1.0 - ADAM_B2 ** ADAM_STEP

    def body(w_ref, g_ref, m_ref, v_ref, d_ref, nm_ref, nv_ref):
        gv = g_ref[...]
        nm = ADAM_B1 * m_ref[...] + (1.0 - ADAM_B1) * gv
        nv = ADAM_B2 * v_ref[...] + (1.0 - ADAM_B2) * (gv * gv)
        d_ref[...] = -ADAM_LR * ((nm / c1) / (jnp.sqrt(nv / c2) + ADAM_EPS) + ADAM_WD * w_ref[...])
        nm_ref[...] = nm
        nv_ref[...] = nv

    spec = pl.BlockSpec((tr, c), lambda i: (i, 0))
    shp = jax.ShapeDtypeStruct((r, c), F32)
    return pl.pallas_call(
        body, name=name, grid=(r // tr,), in_specs=[spec] * 4, out_specs=(spec,) * 3, out_shape=(shp,) * 3,
        compiler_params=_cparams(("parallel",)),
    )(w, g, m, v)


def _sum_slots(x, nslots, name, out_dtype=F32):
    r = x.shape[0] // nslots
    c = x.shape[1]
    tr = _block_rows(r, c)
    nbk = r // tr

    def body(*refs):
        acc = refs[0][...].astype(F32)
        for k in range(1, nslots):
            acc = acc + refs[k][...].astype(F32)
        refs[nslots][...] = acc.astype(out_dtype)

    specs = [pl.BlockSpec((tr, c), functools.partial(lambda k, i: (k * nbk + i, 0), k)) for k in range(nslots)]
    return pl.pallas_call(
        body, name=name, grid=(nbk,), in_specs=specs, out_specs=pl.BlockSpec((tr, c), lambda i: (i, 0)),
        out_shape=jax.ShapeDtypeStruct((r, c), out_dtype), compiler_params=_cparams(("parallel",)),
    )(*([x] * nslots))


def _pick(dim, prefs):
    for p in prefs:
        if dim % p == 0:
            return p
    return dim


def _mm(a, b, m, n, k, name, out_dtype, **kw):
    tm = _pick(m, (1408, 1024, 512, 256, 128))
    tn = _pick(n, (1408, 1152, 1024, 512, 256, 128))
    kdiv = k // max(kw.get("a_stack", 0), kw.get("b_stack", 0) if kw.get("tb") else 0, 1)
    osize = jnp.dtype(out_dtype).itemsize
    tk = kdiv
    for cut in (1, 2, 4, 8, 16):
        tk = kdiv // cut
        vmem = 2 * 2 * tk * (tm + tn) + tm * tn * (2 * osize + 4 + (4 if tk < k else 0))
        if kdiv % cut == 0 and tk % 128 == 0 and vmem <= _MATMUL_VMEM_BUDGET:
            break
    return _matmul(a, b, m=m, n=n, k=k, tm=tm, tn=tn, tk=tk, out_dtype=out_dtype, name=name, **kw)


def _ssm_layout(p):
    g, st = p["lam_re"].shape
    npc = g * st // PSTATES
    lr = p["lam_re"].reshape(npc, PSTATES)
    li = p["lam_im"].reshape(npc, PSTATES)
    ls = jnp.broadcast_to(p["log_step"][:, None], (g, st)).reshape(npc, PSTATES)
    btr = jnp.transpose(p["ssm_b_re"], (2, 0, 1)).reshape(SSM_GROUP, npc, PSTATES)
    bti = jnp.transpose(p["ssm_b_im"], (2, 0, 1)).reshape(SSM_GROUP, npc, PSTATES)
    return lr, li, ls, btr, bti


def _ssm_pieces(bbr, bbi, c_re, c_im):
    npc = bbr.shape[1]
    gl = PSTATES // STATE
    eye = jnp.eye(gl, dtype=bool)

    def b_piece(bb):
        t = jnp.transpose(bb.reshape(SSM_GROUP, npc, gl, STATE), (1, 2, 0, 3))
        full = jnp.where(eye[None, :, None, :, None], t[:, :, :, None, :], 0.0)
        return full.reshape(npc, gl * SSM_GROUP, PSTATES)

    def c_piece(cc):
        t = jnp.transpose(cc.reshape(npc, gl, SSM_GROUP, STATE), (0, 1, 3, 2))
        full = jnp.where(eye[None, :, None, :, None], t[:, :, :, None, :], 0.0)
        return full.reshape(npc, PSTATES, gl * SSM_GROUP)

    bp = jnp.concatenate([b_piece(bbr), b_piece(bbi)], axis=2).astype(BF16)
    cp = jnp.concatenate([c_piece(c_re), c_piece(-c_im)], axis=1).astype(BF16)
    return bp, cp, jnp.swapaxes(bp, 1, 2), jnp.swapaxes(cp, 1, 2)


def _ssm_unpieces(dbp, dcp):
    npc = dbp.shape[0]
    gl = PSTATES // STATE
    idx = jnp.arange(gl)

    def b_diag(x):
        d = x.reshape(npc, gl, SSM_GROUP, gl, STATE)[:, idx, :, idx, :]
        return jnp.transpose(d, (2, 1, 0, 3)).reshape(SSM_GROUP, npc, PSTATES)

    def c_diag(x):
        d = x.reshape(npc, gl, STATE, gl, SSM_GROUP)[:, idx, :, idx, :]
        return jnp.transpose(d, (1, 0, 3, 2)).reshape(npc * gl, SSM_GROUP, STATE)

    return b_diag(dbp[:, :, :PSTATES]), b_diag(dbp[:, :, PSTATES:]), c_diag(dcp[:, :PSTATES, :]), -c_diag(dcp[:, PSTATES:, :])


class _LayerWeights:
    def __init__(self, fetch):
        self._fetch = fetch
        self._got = {}

    def group(self, g, after=None):
        if g not in self._got:
            self._got[g] = self._fetch(g, after)
        return self._got[g]


def _layer_fwd(l, x, ada6, weights, p):
    s, d = x.shape
    sh_m, sc_m, gt_m, sh_f, sc_f, gt_f = ada6
    w = dict(weights.group("mix", x))
    uw = w["w_glu"].shape[0]
    aw = w["w_out"].shape[0] - uw
    ncol = w["w_in"].shape[1]
    row = lambda v: v.reshape(1, -1)
    n = lambda t: f"l{l}_{t}"

    h = _pre_fwd(x, row(p["g_pre_mix"]), sc_m, sh_m, n("pre_mix"))
    proj = _mm(h, w["w_in"], s, ncol, d, n("proj"), BF16)
    attn = _attn_fwd(proj, p["attn_sinks"], aw, uw, n("attn_fwd"))
    zin = _ssm_layout(p)
    a_re, a_im, bbr, bbi = _ssm_prep(*zin, n("ssm_prep"))
    bp, cp, bpt, cpt = _ssm_pieces(bbr, bbi, p["ssm_c_re"], p["ssm_c_im"])
    dvec = p["ssm_d"].reshape(1, uw)
    y, xs_re, xs_im = _ssm_fwd(proj, bp, cp, a_re, a_im, dvec, uw, n("ssm_fwd"))
    z = _gelu_fwd(y, n("gelu_fwd"))
    gl = _mm(z, w["w_glu"], s, uw, uw, n("glu"), F32)
    merged = _merge_fwd(attn, y, gl, row(p["g_attn_out"]), row(p["g_ssm_out"]), n("merge_fwd"))
    mix = _mm(merged, w["w_out"], s, d, aw + uw, n("out_proj"), F32)
    x1 = _post_fwd(x, mix, row(p["g_post_mix"]), gt_m, n("post_mix"))

    w.update(weights.group("ffn", x1))
    f = w["w_down"].shape[0]
    h2 = _pre_fwd(x1, row(p["g_pre_ffn"]), sc_f, sh_f, n("pre_ffn"))
    up0 = _mm(h2, w["w_up"], s, 2 * f, d, n("up_proj"), BF16, b_stack=w["w_up"].shape[0], o_stack=2)
    cw2 = jnp.transpose(p["conv_w"].reshape(3, 2, f), (1, 0, 2))
    cb2 = p["conv_b"].reshape(2, 1, f)
    act = _conv_glu_fwd(up0, cw2, cb2, n("conv_glu"))
    ff = _mm(act, w["w_down"], s, d, f, n("down_proj"), F32)
    x2 = _post_fwd(x1, ff, row(p["g_post_ffn"]), gt_f, n("post_ffn"))
    saved = dict(x=x, h=h, proj=proj, attn=attn, zin=zin, a_re=a_re, a_im=a_im, bpt=bpt, cpt=cpt, dvec=dvec, y=y, xs_re=xs_re,
                 xs_im=xs_im, z=z, gl=gl, merged=merged, mix=mix, x1=x1, h2=h2, up0=up0, cw2=cw2, cb2=cb2, act=act, ff=ff)
    return x2, saved


def _layer_bwd(l, dx2, ada6, weights, p, sv, on_grads):
    s, d = dx2.shape
    sh_m, sc_m, gt_m, sh_f, sc_f, gt_f = ada6
    w = {**weights.group("mix"), **weights.group("ffn")}
    uw = w["w_glu"].shape[0]
    aw = w["w_out"].shape[0] - uw
    ncol = w["w_in"].shape[1]
    f = w["w_down"].shape[0]
    nst = w["w_up"].shape[0]
    row = lambda v: v.reshape(1, -1)
    n = lambda t: f"l{l}_{t}"
    gw, gs = {}, {}

    dff, dgt_f, gs["g_post_ffn"] = _post_bwd(dx2, sv["ff"], row(p["g_post_ffn"]), gt_f, n("post_ffn_bwd"))
    gw["w_down"] = _mm(sv["act"], dff, f, d, s, n("down_dw"), BF16, ta=True)
    dact = _mm(dff, w["w_down"], s, f, d, n("down_dx"), F32, tb=True)
    dup, dcw2, dcb2 = _ffn_bwd_gate(dact, sv["up0"], sv["cw2"], sv["cb2"], n("ffn_gate_bwd"))
    gs["conv_w"] = jnp.transpose(dcw2, (1, 0, 2)).reshape(3, 2 * f)
    gs["conv_b"] = dcb2.reshape(2 * f)
    dup0 = _conv_bwd(dup, sv["cw2"], n("conv_bwd"))
    gw["w_up"] = _mm(sv["h2"], dup0, d, 2 * f, s, n("up_dw"), BF16, ta=True, b_stack=2, o_stack=nst)
    dh2 = _mm(dup0, w["w_up"], s, d, 2 * f, n("up_dx"), F32, tb=True, a_stack=2, b_stack=nst)
    dx1, dsh_f, dsc_f, gs["g_pre_ffn"] = _pre_bwd(dx2, dh2, sv["x1"], row(p["g_pre_ffn"]), sc_f, n("pre_ffn_bwd"))
    token = on_grads(l, "ffn", {k: gw.pop(k) for k in ("w_up", "w_down")})
    if token is not None:
        gt_m = gt_m + token[0:1, 0:1]

    dmix, dgt_m, gs["g_post_mix"] = _post_bwd(dx1, sv["mix"], row(p["g_post_mix"]), gt_m, n("post_mix_bwd"))
    gw["w_out"] = _mm(sv["merged"], dmix, aw + uw, d, s, n("out_dw"), BF16, ta=True)
    dmerged = _mm(dmix, w["w_out"], s, aw + uw, d, n("out_dx"), F32, tb=True)
    dattn, dgl, dzd, gs["g_attn_out"], gs["g_ssm_out"] = _merge_bwd(
        dmerged, sv["attn"], sv["y"], sv["gl"], row(p["g_attn_out"]), row(p["g_ssm_out"]), n("merge_bwd"))
    gw["w_glu"] = _mm(sv["z"], dgl, uw, uw, s, n("glu_dw"), BF16, ta=True)
    dz2 = _mm(dgl, w["w_glu"], s, uw, uw, n("glu_dx"), F32, tb=True)
    dy = _gelu_bwd(dzd, dz2, sv["y"], n("gelu_bwd"))
    du, dbp, dcp, dar, dai, dd = _ssm_bwd(dy, sv["proj"], sv["xs_re"], sv["xs_im"], sv["cpt"], sv["bpt"], sv["a_re"], sv["a_im"],
                                          sv["dvec"], uw, n("ssm_bwd"))
    dq, dkv_c, dkv_p, dsinks = _attn_bwd(sv["proj"], p["attn_sinks"], sv["attn"], dattn, aw, uw, n("attn_bwd"))
    dproj = _assemble_dproj(du, dq, dkv_c, dkv_p, n("dproj"))
    gw["w_in"] = _mm(sv["h"], dproj, d, ncol, s, n("in_dw"), BF16, ta=True)
    dh = _mm(dproj, w["w_in"], s, d, ncol, n("in_dx"), F32, tb=True)
    dx0, dsh_m, dsc_m, gs["g_pre_mix"] = _pre_bwd(dx1, dh, sv["x"], row(p["g_pre_mix"]), sc_m, n("pre_mix_bwd"))
    token = on_grads(l, "mix", {k: gw.pop(k) for k in ("w_in", "w_glu", "w_out")})

    dbbr, dbbi, dc_re, dc_im = _ssm_unpieces(dbp, dcp)
    dlr, dli, dls, dbtr, dbti = _ssm_prep_bwd(*sv["zin"], dar, dai, dbbr, dbbi, n("ssm_prep_bwd"))
    g, st = p["lam_re"].shape
    gs["lam_re"] = dlr.reshape(g, st)
    gs["lam_im"] = dli.reshape(g, st)
    gs["log_step"] = jnp.sum(dls.reshape(g, st), axis=1)
    gs["ssm_b_re"] = jnp.transpose(dbtr.reshape(SSM_GROUP, g, st), (1, 2, 0))
    gs["ssm_b_im"] = jnp.transpose(dbti.reshape(SSM_GROUP, g, st), (1, 2, 0))
    gs["ssm_c_re"] = dc_re
    gs["ssm_c_im"] = dc_im
    gs["ssm_d"] = dd.reshape(p["ssm_d"].shape)
    gs["attn_sinks"] = dsinks.reshape(-1)
    gs["b_ada"] = jnp.concatenate([dsh_m, dsc_m, dgt_m, dsh_f, dsc_f, dgt_f], axis=1).reshape(-1)
    for key in ("g_post_ffn", "g_pre_ffn", "g_post_mix", "g_attn_out", "g_ssm_out", "g_pre_mix"):
        gs[key] = gs[key].reshape(-1)
    return dx0, gs, token


def _local_step(x, tgt, ada, wl, pl_small, on_grads):
    d = x.shape[1]
    depth = len(wl)
    ada6 = [[ada[l:l + 1, k * d:(k + 1) * d] for k in range(6)] for l in range(depth)]
    saved = []
    h = x
    for l in range(depth):
        h, sv = _layer_fwd(l, h, ada6[l], wl[l], pl_small[l])
        saved.append(sv)
    dy, lsum = _loss_head(h, tgt, "loss_head")
    loss = 0.5 * lsum[0, 0] / d
    gss = [None] * depth
    dx = dy
    for l in reversed(range(depth)):
        dx, gss[l], token = _layer_bwd(l, dx, ada6[l], wl[l], pl_small[l], saved[l], on_grads)
        if token is not None and l > 0:
            ada6[l - 1][5] = ada6[l - 1][5] + token[0:1, 0:1]
    return loss, dx, gss


_ANY = pl.BlockSpec(memory_space=pl.ANY)


def _mesh_pos():
    return lax.axis_index("x"), lax.axis_index("y"), lax.axis_index("c")


def _other_chips(x, y):
    return [(1 - x, y), (x, 1 - y), (1 - x, 1 - y)]


def _remote(src, dst, send_sems, recv_sems, k, to):
    return pltpu.make_async_remote_copy(src_ref=src, dst_ref=dst, send_sem=send_sems.at[k], recv_sem=recv_sems.at[k],
                                        device_id=to, device_id_type=MESH)


def _allgather8(xs, name):
    m, n = xs.shape

    def body(x_ref, out_ref, send_sems, recv_sems):
        x, y, c = _mesh_pos()
        me, sibling = (x, y, c), (x, y, 1 - c)
        chips = _other_chips(x, y)

        def rows(px, py, pc):
            return out_ref.at[pl.ds((4 * px + 2 * py + pc) * m, m), :]

        def copy(k, block, to, src=None):
            return _remote(rows(*block) if src is None else src, rows(*block), send_sems, recv_sems, k, to)

        first = [copy(0, me, sibling, src=x_ref)]
        first += [copy(1 + j, me, (*chip, c), src=x_ref) for j, chip in enumerate(chips)]
        for cp in first:
            cp.start()
        passed = [copy(4 + j, (*chip, c), sibling) for j, chip in enumerate(chips)]
        for j, chip in enumerate(chips):
            copy(1 + j, (*chip, c), me).wait_recv()
            passed[j].start()
        copy(0, sibling, me).wait_recv()
        for j, chip in enumerate(chips):
            copy(4 + j, (*chip, 1 - c), me).wait_recv()
        for cp in first + passed:
            cp.wait_send()

    out = pl.pallas_call(
        body, name=name, out_shape=jax.ShapeDtypeStruct((8 * m, n), xs.dtype), in_specs=[_ANY], out_specs=_ANY,
        scratch_shapes=[pltpu.SemaphoreType.DMA((7,)), pltpu.SemaphoreType.DMA((7,))],
    )(xs)
    x, y, c = _mesh_pos()
    return lax.dynamic_update_slice(out, xs, ((4 * x + 2 * y + c) * m, 0))


def _half_rows(ref_rows, half, align):
    h = ref_rows // 2
    return pl.ds(pl.multiple_of(half * h, align), h)


_HBM = pl.BlockSpec(memory_space=pltpu.HBM)
_SEMS = pl.BlockSpec(memory_space=pltpu.SEMAPHORE)
_EFFECT = pltpu.SideEffectType.DATAFLOW_SIDE_EFFECTING
_TOKEN = jax.ShapeDtypeStruct((8, 128), F32)


def _in_hbm(arrays):
    return [pltpu.with_memory_space_constraint(a, pltpu.HBM) for a in arrays]


def _chip_copy(kind, srcs, lands, w, q, chip, mine, c, send, recv, k):
    if kind == "gather":
        rows = _half_rows(srcs[w].shape[0], c, 16)
        return _remote(srcs[w].at[rows, :], lands[w].at[mine, rows, :], send, recv, k, (*chip, c))
    return _remote(srcs[w].at[2 * chip[0] + chip[1]], lands[w].at[mine], send, recv, k, (*chip, c))


def _chip_landing(kind, srcs, lands, w, chip, c):
    if kind == "gather":
        return lands[w].at[2 * chip[0] + chip[1], _half_rows(srcs[w].shape[0], c, 16), :]
    return lands[w].at[2 * chip[0] + chip[1]]


def _ici_start(kind, srcs, groups, name, after=None):
    nw, ng = len(srcs), len(groups)
    lands = [lax.empty((4,) + s.shape if kind == "gather" else s.shape, s.dtype) for s in srcs]
    extra = [] if after is None else [after]

    def body(*refs):
        ins, lnd = refs[:nw], refs[nw:2 * nw]
        sems = refs[2 * nw + len(extra):2 * nw + len(extra) + 2 * ng]
        token = refs[-1]
        x, y, c = _mesh_pos()
        for g, members in enumerate(groups):
            for j, w in enumerate(members):
                for q, chip in enumerate(_other_chips(x, y)):
                    _chip_copy(kind, ins, lnd, w, q, chip, 2 * x + y, c, sems[2 * g], sems[2 * g + 1], 3 * j + q).start()
        token[...] = jnp.zeros_like(token)

    sem_shapes = [pltpu.SemaphoreType.DMA((3 * len(members),)) for members in groups for _ in range(2)]
    out = pl.pallas_call(
        body, name=name,
        out_shape=(*sem_shapes, *[pltpu.HBM(s.shape, s.dtype) for s in srcs], *[pltpu.HBM(t.shape, t.dtype) for t in lands], _TOKEN),
        in_specs=[_HBM] * (2 * nw) + [_ANY] * len(extra),
        out_specs=(*([_SEMS] * (2 * ng)), *([_HBM] * (2 * nw)), pl.BlockSpec(memory_space=pltpu.VMEM)),
        input_output_aliases={i: 2 * ng + i for i in range(2 * nw)},
        compiler_params=pltpu.CompilerParams(has_side_effects=_EFFECT),
    )(*_in_hbm(srcs), *_in_hbm(lands), *extra)
    sems = [(out[2 * g], out[2 * g + 1]) for g in range(ng)]
    return sems, list(out[2 * ng:2 * ng + nw]), list(out[2 * ng + nw:2 * ng + 2 * nw]), out[-1]


def _ici_wait(kind, sems, srcs, lands, after, name):
    nm = len(srcs)

    def body(*refs):
        ins, lnd = refs[:nm], refs[nm:2 * nm]
        send, recv = refs[2 * nm], refs[2 * nm + 1]
        x, y, c = _mesh_pos()
        for j in range(nm):
            for q, chip in enumerate(_other_chips(x, y)):
                _chip_copy(kind, ins, lnd, j, q, chip, 2 * x + y, c, send, recv, 3 * j + q).wait_send()
                landed = _chip_landing(kind, ins, lnd, j, chip, c)
                _remote(landed, landed, send, recv, 3 * j + q, (x, y, c)).wait_recv()

    out = pl.pallas_call(
        body, name=name, out_shape=(*[pltpu.HBM(s.shape, s.dtype) for s in srcs], *[pltpu.HBM(t.shape, t.dtype) for t in lands]),
        in_specs=[_HBM] * (2 * nm) + [_SEMS, _SEMS, _ANY], out_specs=tuple([_HBM] * (2 * nm)),
        input_output_aliases={i: i for i in range(2 * nm)},
        compiler_params=pltpu.CompilerParams(has_side_effects=_EFFECT),
    )(*srcs, *lands, sems[0], sems[1], after)
    return list(out[:nm]), list(out[nm:])


def _gather_finish(shards, lands, name):
    nw = len(shards)

    def body(*refs):
        ins, lnd, outs = refs[:nw], refs[nw:2 * nw], refs[2 * nw:3 * nw]
        send_sems, recv_sems = refs[3 * nw:]
        x, y, c = _mesh_pos()
        mine = 2 * x + y
        sibling = (x, y, 1 - c)
        chips = _other_chips(x, y)

        def blk(ref, chip_idx, half):
            return ref.at[chip_idx, _half_rows(ref.shape[1], half, 16), :]

        sends = []
        for w in range(nw):
            for q, chip in enumerate(chips):
                k = 2 * chip[0] + chip[1]
                sends.append(_remote(blk(lnd[w], k, c), blk(outs[w], k, c), send_sems, recv_sems, 4 * w + q, sibling))
            sends.append(_remote(ins[w], outs[w].at[mine], send_sems, recv_sems, 4 * w + 3, sibling))
        for cp in sends:
            cp.start()
        for w in range(nw):
            for q, chip in enumerate(chips):
                other = blk(outs[w], 2 * chip[0] + chip[1], 1 - c)
                _remote(other, other, send_sems, recv_sems, 4 * w + q, (x, y, c)).wait_recv()
            own = outs[w].at[mine]
            _remote(own, own, send_sems, recv_sems, 4 * w + 3, (x, y, c)).wait_recv()
        for cp in sends:
            cp.wait_send()

    return pl.pallas_call(
        body, name=name, out_shape=[jax.ShapeDtypeStruct(t.shape, t.dtype) for t in lands],
        in_specs=[_ANY] * (2 * nw), out_specs=[_ANY] * nw, input_output_aliases={nw + w: w for w in range(nw)},
        scratch_shapes=[pltpu.SemaphoreType.DMA((4 * nw,)), pltpu.SemaphoreType.DMA((4 * nw,))],
    )(*shards, *lands)


def _exchange_halves(gs, name):
    nw = len(gs)

    def body(*refs):
        ins, outs = refs[:nw], refs[nw:2 * nw]
        send_sems, recv_sems = refs[2 * nw:]
        x, y, c = _mesh_pos()
        cps = []
        for w in range(nw):
            src = ins[w].at[:, _half_rows(gs[w].shape[1], 1 - c, 16), :]
            cps.append(_remote(src, outs[w], send_sems, recv_sems, w, (x, y, 1 - c)))
            cps[-1].start()
        for cp in cps:
            cp.wait_recv()
        for cp in cps:
            cp.wait_send()

    return pl.pallas_call(
        body, name=name, out_shape=[jax.ShapeDtypeStruct((4, g.shape[1] // 2, g.shape[2]), g.dtype) for g in gs],
        in_specs=[_ANY] * nw, out_specs=[_ANY] * nw,
        scratch_shapes=[pltpu.SemaphoreType.DMA((nw,)), pltpu.SemaphoreType.DMA((nw,))],
    )(*gs)


def _add_half(g, recv, cidx, name):
    _, r, c = g.shape
    h = r // 2
    tr = _block_rows(h, c)
    nbh = h // tr
    assert tr % 16 == 0

    def body(c_ref, g_ref, r_ref, o_ref):
        o_ref[...] = (g_ref[...].astype(F32) + r_ref[...].astype(F32)).astype(BF16)

    grid_spec = pltpu.PrefetchScalarGridSpec(
        num_scalar_prefetch=1, grid=(4, nbh),
        in_specs=[pl.BlockSpec((None, tr, c), lambda s, i, cr: (s, cr[0] * nbh + i, 0)),
                  pl.BlockSpec((None, tr, c), lambda s, i, cr: (s, i, 0))],
        out_specs=pl.BlockSpec((None, tr, c), lambda s, i, cr: (s, i, 0)))
    return pl.pallas_call(
        body, name=name, grid_spec=grid_spec, out_shape=jax.ShapeDtypeStruct((4, h, c), BF16),
        compiler_params=_cparams(("parallel", "parallel")),
    )(cidx, g, recv)


def _swap_with_sibling(xs, name):
    nf = len(xs)

    def body(*refs):
        ins, outs = refs[:nf], refs[nf:2 * nf]
        send_sems, recv_sems = refs[2 * nf:]
        x, y, c = _mesh_pos()
        cps = [_remote(ins[k], outs[k], send_sems, recv_sems, k, (x, y, 1 - c)) for k in range(nf)]
        for cp in cps:
            cp.start()
        for cp in cps:
            cp.wait_recv()
        for cp in cps:
            cp.wait_send()

    return pl.pallas_call(
        body, name=name, out_shape=[jax.ShapeDtypeStruct(t.shape, t.dtype) for t in xs], in_specs=[_ANY] * nf, out_specs=[_ANY] * nf,
        scratch_shapes=[pltpu.SemaphoreType.DMA((nf,)), pltpu.SemaphoreType.DMA((nf,))],
    )(*xs)


_BIG = ("w_in", "w_glu", "w_out", "w_up", "w_down")
_GROUPS = {"mix": ("w_in", "w_glu", "w_out"), "ffn": ("w_up", "w_down")}
_SMALL = ("b_ada", "g_pre_mix", "g_post_mix", "attn_sinks", "lam_re", "lam_im", "log_step", "ssm_b_re", "ssm_b_im", "ssm_c_re",
          "ssm_c_im", "ssm_d", "g_attn_out", "g_ssm_out", "g_pre_ffn", "g_post_ffn", "conv_b")
_WEIGHTS = ("w_ada", "b_ada", "g_pre_mix", "g_post_mix", "w_in", "attn_sinks", "lam_re", "lam_im", "log_step", "ssm_b_re", "ssm_b_im",
            "ssm_c_re", "ssm_c_im", "ssm_d", "w_glu", "g_attn_out", "g_ssm_out", "w_out", "g_pre_ffn", "g_post_ffn", "w_up", "conv_w",
            "conv_b", "w_down")
_LANES = 1024


def _pack(parts, rows_to):
    flat = jnp.concatenate([p.reshape(-1) for p in parts])
    per = _LANES * rows_to
    total = -(-flat.shape[0] // per) * per
    return jnp.pad(flat, (0, total - flat.shape[0])).reshape(total // _LANES, _LANES)


def _unpack(packed, shapes):
    flat = packed.reshape(-1)
    out, off = [], 0
    for shp in shapes:
        size = math.prod(shp)
        out.append(flat[off:off + size].reshape(shp))
        off += size
    return out


def kernel(x, c, w_ada, b_ada, g_pre_mix, g_post_mix, w_in, attn_sinks, lam_re, lam_im, log_step, ssm_b_re, ssm_b_im, ssm_c_re, ssm_c_im, ssm_d, w_glu, g_attn_out, g_ssm_out, w_out, g_pre_ffn, g_post_ffn, w_up, conv_w, conv_b, w_down, loss_target, m_w_ada, m_b_ada, m_g_pre_mix, m_g_post_mix, m_w_in, m_attn_sinks, m_lam_re, m_lam_im, m_log_step, m_ssm_b_re, m_ssm_b_im, m_ssm_c_re, m_ssm_c_im, m_ssm_d, m_w_glu, m_g_attn_out, m_g_ssm_out, m_w_out, m_g_pre_ffn, m_g_post_ffn, m_w_up, m_conv_w, m_conv_b, m_w_down, v_w_ada, v_b_ada, v_g_pre_mix, v_g_post_mix, v_w_in, v_attn_sinks, v_lam_re, v_lam_im, v_log_step, v_ssm_b_re, v_ssm_b_im, v_ssm_c_re, v_ssm_c_im, v_ssm_d, v_w_glu, v_g_attn_out, v_g_ssm_out, v_w_out, v_g_pre_ffn, v_g_post_ffn, v_w_up, v_conv_w, v_conv_b, v_w_down):
    given = dict(locals())
    wts = {n: given[n] for n in _WEIGHTS}
    mom = {n: given["m_" + n] for n in _WEIGHTS}
    var = {n: given["v_" + n] for n in _WEIGHTS}
    depth, d, ada_cols = w_ada.shape
    nchips = 4
    xi, yi, ci = lax.axis_index("x"), lax.axis_index("y"), lax.axis_index("c")
    chip = 2 * xi + yi
    cidx = jnp.reshape(ci, (1,)).astype(jnp.int32)

    cw_cols = conv_w.shape[2]
    vec = _pack([c, conv_w], 8)
    g1 = _allgather8(vec, "ag_cond").reshape(8, -1)
    c_all = g1[:, :d]
    cw_sh = g1[0::2, d:d + depth * 3 * cw_cols].reshape(nchips, depth, 3, cw_cols)
    conv_w_full = jnp.transpose(cw_sh, (1, 2, 0, 3)).reshape(depth, 3, nchips * cw_cols)

    ada_part, c_act = _ada_part(c_all, w_ada, "ada_part")
    g2 = _allgather8(ada_part.reshape(depth * 8, ada_cols), "ag_ada")
    ada = _ada_select(g2, b_ada, "ada_select")

    order = [(l, g) for l in range(depth) for g in _GROUPS]
    members = {key: [wts[n][key[0]].astype(BF16) for n in _GROUPS[key[1]]] for key in order}
    flat = [s for key in order for s in members[key]]
    index, at = {}, 0
    for key in order:
        index[key] = list(range(at, at + len(members[key])))
        at += len(members[key])
    ag_sems, ag_srcs, ag_lands, ag_token = _ici_start("gather", flat, [index[key] for key in order], "ag_start", after=ada)
    ada = ada + ag_token[0:1, 0:1]

    def fetch(l, g, after):
        pos, ids = order.index((l, g)), index[(l, g)]
        srcs, lands = [ag_srcs[i] for i in ids], [ag_lands[i] for i in ids]
        srcs, lands = _ici_wait("gather", ag_sems[pos], srcs, lands, ag_token if after is None else after, f"ag_wait_l{l}_{g}")
        got = dict(zip(_GROUPS[g], _gather_finish(srcs, lands, f"ag_finish_l{l}_{g}")))
        if g == "ffn":
            return dict(w_up=got["w_up"], w_down=got["w_down"].reshape(-1, got["w_down"].shape[2]))
        w_in_full = jnp.transpose(got["w_in"], (1, 0, 2)).reshape(d, -1)
        split = w_in_full.shape[1] - nchips * got["w_glu"].shape[1]
        return dict(w_in=jnp.concatenate([w_in_full[:, split:], w_in_full[:, :split]], axis=1),
                    w_glu=got["w_glu"].reshape(-1, got["w_glu"].shape[2]), w_out=got["w_out"].reshape(-1, got["w_out"].shape[2]))

    wl = [_LayerWeights(functools.partial(fetch, l)) for l in range(depth)]
    wl[0].group("mix")
    ps = []
    for l in range(depth):
        small = {n: wts[n][l] for n in _SMALL if n != "b_ada"}
        small["conv_w"] = conv_w_full[l]
        ps.append(small)

    in_flight = {}

    def on_grads(l, g, gw):
        stacks = []
        for n in _GROUPS[g]:
            t = gw[n]
            if n == "w_in":
                uw = nchips * wts["w_glu"].shape[1]
                t = jnp.concatenate([t[:, uw:], t[:, :uw]], axis=1)
                t = jnp.transpose(t.reshape(d, nchips, -1), (1, 0, 2))
            elif n != "w_up":
                t = t.reshape(nchips, t.shape[0] // nchips, t.shape[1])
            stacks.append(t)
        from_sibling = _exchange_halves(stacks, f"rs_sibling_l{l}_{g}")
        partials = [_add_half(s, r, cidx, f"rs_add_l{l}_{n}") for s, r, n in zip(stacks, from_sibling, _GROUPS[g])]
        sems, srcs, lands, token = _ici_start("scatter", partials, [list(range(len(partials)))], f"rs_start_l{l}_{g}")
        in_flight[(l, g)] = (sems[0], srcs, lands)
        return token

    loss_sum, grad_x, gss = _local_step(x[0], loss_target[0], ada, wl, ps, on_grads)
    loss = lax.psum(loss_sum, ("x", "y", "c"))

    reduced = {}
    for key in reversed(order):
        l, g = key
        sems, srcs, lands = in_flight[key]
        sent, landed = _ici_wait("scatter", sems, srcs, lands, grad_x, f"rs_wait_l{l}_{g}")
        for n, t, p in zip(_GROUPS[g], landed, sent):
            t = lax.dynamic_update_slice(t, lax.dynamic_slice_in_dim(p, chip, 1, axis=0), (chip, 0, 0))
            reduced[(n, l)] = _sum_slots(t.reshape(-1, t.shape[2]), nchips, f"rs_sum_l{l}_{n}")
    keys = [(n, l) for n in _BIG for l in range(depth)]
    theirs = dict(zip(keys, _swap_with_sibling([reduced[k] for k in keys], "rs_share")))
    big_grads = {}
    for n in _BIG:
        own = jnp.stack([reduced[(n, l)] for l in range(depth)])
        oth = jnp.stack([theirs[(n, l)] for l in range(depth)])
        big_grads[n] = jnp.where(ci == 0, jnp.concatenate([own, oth], axis=1), jnp.concatenate([oth, own], axis=1))

    small_parts = [jnp.stack([gss[l][n] for l in range(depth)]) for n in _SMALL]
    pack_small = _pack(small_parts, 8)
    pack_cw = _pack([jnp.stack([gss[l]["conv_w"] for l in range(depth)])], 8)
    rows_small = pack_small.shape[0]
    mine = jnp.concatenate([pack_small, pack_cw], axis=0)
    g3 = _allgather8(mine, "ag_small")
    total = _sum_slots(g3, 8, "sum_small")
    grads = dict(big_grads)
    for n, v in zip(_SMALL, _unpack(total[:rows_small], [wts[n].shape for n in _SMALL])):
        grads[n] = v
    conv_w_grad = _unpack(total[rows_small:], [(depth, 3, nchips * cw_cols)])[0]
    grads["conv_w"] = lax.dynamic_slice_in_dim(conv_w_grad, chip * cw_cols, cw_cols, axis=2)

    d_ada_all = g3.reshape(8, -1)[:, :depth * 6 * d].reshape(8, depth, 6 * d)
    ca_t = jnp.transpose(c_act)
    grads["w_ada"] = jnp.stack([
        _wada_grad(ca_t, lax.dynamic_slice_in_dim(d_ada_all[:, l], chip * ada_cols, ada_cols, axis=1), f"w_ada_grad_{l}")
        for l in range(depth)])

    delta, new_m, new_v = {}, {}, {}
    for n in ("w_ada",) + _BIG + ("conv_w",):
        shp = wts[n].shape
        two_d = lambda t: t.reshape(-1, shp[-1])
        dl, nm, nv = _adamw(two_d(wts[n]), two_d(grads[n]), two_d(mom[n]), two_d(var[n]), f"adamw_{n}")
        delta[n], new_m[n], new_v[n] = dl.reshape(shp), nm.reshape(shp), nv.reshape(shp)
    packs = [_pack([t[n] for n in _SMALL], 8) for t in (wts, mom, var)]
    outs = _adamw(packs[0], total[:rows_small], packs[1], packs[2], "adamw_small")
    shapes = [wts[n].shape for n in _SMALL]
    for dst, packed in zip((delta, new_m, new_v), outs):
        for n, v in zip(_SMALL, _unpack(packed, shapes)):
            dst[n] = v

    return (loss, grad_x[None], *[grads[n] for n in _WEIGHTS], *[delta[n] for n in _WEIGHTS],
            *[new_m[n] for n in _WEIGHTS], *[new_v[n] for n in _WEIGHTS])
```

```python
import functools
import math

import jax
import jax.numpy as jnp
from jax import lax
from jax.experimental import pallas as pl
from jax.experimental.pallas import tpu as pltpu

F32 = jnp.float32
BF16 = jnp.bfloat16
EPS = 1e-6
NEG = -1e30
WINDOW = 128
HEAD_DIM = 64
KV_RATIO = 8
SSM_GROUP = 16
STATE = 64
PIECE = 128
PSTATES = 512
DEPTH = 2
ADAM_LR, ADAM_B1, ADAM_B2, ADAM_EPS, ADAM_WD, ADAM_STEP = 0.001, 0.9, 0.999, 1e-08, 0.01, 10
MIB = 1024 * 1024
_MATMUL_VMEM_BUDGET = 40 * MIB
MESH = pl.DeviceIdType.MESH


def _cparams(sem=None, vmem_mib=48):
    return pltpu.CompilerParams(dimension_semantics=sem, vmem_limit_bytes=vmem_mib * MIB)


def _gelu(x):
    c = math.sqrt(2.0 / math.pi)
    return 0.5 * x * (1.0 + jnp.tanh(c * (x + 0.044715 * (x * x * x))))


def _gelu_and_grad(x):
    c = math.sqrt(2.0 / math.pi)
    x2 = x * x
    t = jnp.tanh(c * (x + 0.044715 * (x2 * x)))
    half = 0.5 * (1.0 + t)
    return x * half, half + 0.5 * x * (1.0 - t * t) * c * (1.0 + 3.0 * 0.044715 * x2)


def _gelu_grad(x):
    return _gelu_and_grad(x)[1]


def _sigmoid(x):
    return 1.0 / (1.0 + jnp.exp(-x))


def _matmul(a, b, *, m, n, k, tm, tn, tk, out_dtype, name, ta=False, tb=False, a_stack=0, b_stack=0, o_stack=0):
    assert m % tm == 0 and n % tn == 0 and k % tk == 0, (name, m, n, k, tm, tn, tk)
    nk = k // tk

    if a_stack:
        assert not ta and (k // a_stack) % tk == 0
        per = (k // a_stack) // tk
        a_spec = pl.BlockSpec((None, tm, tk), lambda i, j, kk: (kk // per, i, kk % per))
    elif ta:
        a_spec = pl.BlockSpec((tk, tm), lambda i, j, kk: (kk, i))
    else:
        a_spec = pl.BlockSpec((tm, tk), lambda i, j, kk: (i, kk))
    if b_stack and tb:
        perb = (k // b_stack) // tk
        b_spec = pl.BlockSpec((None, tn, tk), lambda i, j, kk: (kk // perb, j, kk % perb))
    elif b_stack:
        perb = (n // b_stack) // tn
        b_spec = pl.BlockSpec((None, tk, tn), lambda i, j, kk: (j // perb, kk, j % perb))
    elif tb:
        b_spec = pl.BlockSpec((tn, tk), lambda i, j, kk: (j, kk))
    else:
        b_spec = pl.BlockSpec((tk, tn), lambda i, j, kk: (kk, j))
    if o_stack:
        pero = (n // o_stack) // tn
        o_spec = pl.BlockSpec((None, tm, tn), lambda i, j, kk: (j // pero, i, j % pero))
        o_shape = jax.ShapeDtypeStruct((o_stack, m, n // o_stack), out_dtype)
    else:
        o_spec = pl.BlockSpec((tm, tn), lambda i, j, kk: (i, j))
        o_shape = jax.ShapeDtypeStruct((m, n), out_dtype)
    dims = (((0 if ta else 1,), (1 if tb else 0,)), ((), ()))

    def body(a_ref, b_ref, o_ref, *acc):
        p = lax.dot_general(a_ref[...].astype(BF16), b_ref[...].astype(BF16), dims, preferred_element_type=F32)
        if nk == 1:
            o_ref[...] = p.astype(o_ref.dtype)
        else:
            acc_ref = acc[0]
            kk = pl.program_id(2)

            @pl.when(kk == 0)
            def _():
                acc_ref[...] = p

            @pl.when(kk > 0)
            def _():
                acc_ref[...] += p

            @pl.when(kk == nk - 1)
            def _():
                o_ref[...] = acc_ref[...].astype(o_ref.dtype)

    return pl.pallas_call(
        body, name=name, grid=(m // tm, n // tn, nk), in_specs=[a_spec, b_spec], out_specs=o_spec, out_shape=o_shape,
        scratch_shapes=[] if nk == 1 else [pltpu.VMEM((tm, tn), F32)],
        compiler_params=_cparams(("parallel", "parallel", "arbitrary"), 56),
    )(a, b)


def _row(d):
    return pl.BlockSpec((1, d), lambda i: (0, 0))


def _tok(tm, d):
    return pl.BlockSpec((tm, d), lambda i: (i, 0))


def _pre_fwd(x, g, sc, sh, name):
    s, d = x.shape
    tm = min(256, s)

    def body(x_ref, g_ref, sc_ref, sh_ref, h_ref):
        xv = x_ref[...]
        r = lax.rsqrt(jnp.mean(xv * xv, axis=-1, keepdims=True) + EPS)
        h_ref[...] = (((xv * r) * g_ref[...]) * (1.0 + sc_ref[...]) + sh_ref[...]).astype(BF16)

    return pl.pallas_call(
        body, name=name, grid=(s // tm,), in_specs=[_tok(tm, d), _row(d), _row(d), _row(d)], out_specs=_tok(tm, d),
        out_shape=jax.ShapeDtypeStruct((s, d), BF16), compiler_params=_cparams(("parallel",)),
    )(x, g, sc, sh)


def _post_fwd(x, o, g, gt, name):
    s, d = x.shape
    tm = min(256, s)

    def body(x_ref, o_ref, g_ref, gt_ref, y_ref):
        ov = o_ref[...]
        r = lax.rsqrt(jnp.mean(ov * ov, axis=-1, keepdims=True) + EPS)
        y_ref[...] = x_ref[...] + (1.0 + gt_ref[...]) * ((ov * r) * g_ref[...])

    return pl.pallas_call(
        body, name=name, grid=(s // tm,), in_specs=[_tok(tm, d), _tok(tm, d), _row(d), _row(d)], out_specs=_tok(tm, d),
        out_shape=jax.ShapeDtypeStruct((s, d), F32), compiler_params=_cparams(("parallel",)),
    )(x, o, g, gt)


def _post_bwd(dxo, o, g, gt, name):
    s, d = o.shape
    tm = min(256, s)

    def body(dx_ref, o_ref, g_ref, gt_ref, do_ref, dgt_ref, dg_ref):
        i = pl.program_id(0)
        dx = dx_ref[...]
        ov = o_ref[...]
        gv = g_ref[...]
        r = lax.rsqrt(jnp.mean(ov * ov, axis=-1, keepdims=True) + EPS)
        oh = ov * r
        dn = dx * (1.0 + gt_ref[...])
        e = dn * gv
        do_ref[...] = (r * (e - oh * jnp.mean(e * oh, axis=-1, keepdims=True))).astype(BF16)
        p_gt = jnp.sum(dx * (oh * gv), axis=0, keepdims=True)
        p_g = jnp.sum(dn * oh, axis=0, keepdims=True)

        @pl.when(i == 0)
        def _():
            dgt_ref[...] = p_gt
            dg_ref[...] = p_g

        @pl.when(i > 0)
        def _():
            dgt_ref[...] += p_gt
            dg_ref[...] += p_g

    row = jax.ShapeDtypeStruct((1, d), F32)
    return pl.pallas_call(
        body, name=name, grid=(s // tm,), in_specs=[_tok(tm, d), _tok(tm, d), _row(d), _row(d)],
        out_specs=(_tok(tm, d), _row(d), _row(d)), out_shape=(jax.ShapeDtypeStruct((s, d), BF16), row, row),
        compiler_params=_cparams(("arbitrary",)),
    )(dxo, o, g, gt)


def _pre_bwd(dres, dh, x, g, sc, name):
    s, d = x.shape
    tm = min(256, s)

    def body(dres_ref, dh_ref, x_ref, g_ref, sc_ref, dx_ref, dsh_ref, dsc_ref, dg_ref):
        i = pl.program_id(0)
        dh_v = dh_ref[...]
        xv = x_ref[...]
        gv = g_ref[...]
        one_sc = 1.0 + sc_ref[...]
        r = lax.rsqrt(jnp.mean(xv * xv, axis=-1, keepdims=True) + EPS)
        xh = xv * r
        e = dh_v * one_sc * gv
        dx_ref[...] = dres_ref[...] + r * (e - xh * jnp.mean(e * xh, axis=-1, keepdims=True))
        p_sh = jnp.sum(dh_v, axis=0, keepdims=True)
        p_sc = jnp.sum(dh_v * (xh * gv), axis=0, keepdims=True)
        p_g = jnp.sum(dh_v * one_sc * xh, axis=0, keepdims=True)

        @pl.when(i == 0)
        def _():
            dsh_ref[...] = p_sh
            dsc_ref[...] = p_sc
            dg_ref[...] = p_g

        @pl.when(i > 0)
        def _():
            dsh_ref[...] += p_sh
            dsc_ref[...] += p_sc
            dg_ref[...] += p_g

    row = jax.ShapeDtypeStruct((1, d), F32)
    return pl.pallas_call(
        body, name=name, grid=(s // tm,), in_specs=[_tok(tm, d), _tok(tm, d), _tok(tm, d), _row(d), _row(d)],
        out_specs=(_tok(tm, d), _row(d), _row(d), _row(d)), out_shape=(jax.ShapeDtypeStruct((s, d), F32), row, row, row),
        compiler_params=_cparams(("arbitrary",)),
    )(dres, dh, x, g, sc)


def _loss_head(y, tgt, name):
    s, d = y.shape
    tm = min(256, s)

    def body(y_ref, t_ref, dy_ref, l_ref):
        i = pl.program_id(0)
        err = y_ref[...] - t_ref[...]
        dy_ref[...] = err * (1.0 / d)
        part = jnp.zeros((1, 128), F32) + jnp.sum(err * err)

        @pl.when(i == 0)
        def _():
            l_ref[...] = part

        @pl.when(i > 0)
        def _():
            l_ref[...] += part

    return pl.pallas_call(
        body, name=name, grid=(s // tm,), in_specs=[_tok(tm, d), _tok(tm, d)],
        out_specs=(_tok(tm, d), pl.BlockSpec((1, 128), lambda i: (0, 0))),
        out_shape=(jax.ShapeDtypeStruct((s, d), F32), jax.ShapeDtypeStruct((1, 128), F32)),
        compiler_params=_cparams(("arbitrary",)),
    )(y, tgt)


def _gelu_fwd(y, name):
    s, u = y.shape
    tm = min(512, s)

    def body(y_ref, z_ref):
        z_ref[...] = _gelu(y_ref[...]).astype(BF16)

    return pl.pallas_call(
        body, name=name, grid=(s // tm,), in_specs=[_tok(tm, u)], out_specs=_tok(tm, u),
        out_shape=jax.ShapeDtypeStruct((s, u), BF16), compiler_params=_cparams(("parallel",)),
    )(y)


def _merge_fwd(attn, y, gl, ga, gs, name):
    s, aw = attn.shape
    uw = y.shape[1]
    tm = min(256, s)

    def body(a_ref, y_ref, gl_ref, ga_ref, gs_ref, m_ref):
        av = a_ref[...]
        ra = lax.rsqrt(jnp.mean(av * av, axis=-1, keepdims=True) + EPS)
        m_ref[:, :aw] = ((av * ra) * ga_ref[...]).astype(BF16)
        ssm = _gelu(y_ref[...]) * _sigmoid(gl_ref[...])
        rs = lax.rsqrt(jnp.mean(ssm * ssm, axis=-1, keepdims=True) + EPS)
        m_ref[:, aw:] = ((ssm * rs) * gs_ref[...]).astype(BF16)

    return pl.pallas_call(
        body, name=name, grid=(s // tm,), in_specs=[_tok(tm, aw), _tok(tm, uw), _tok(tm, uw), _row(aw), _row(uw)],
        out_specs=_tok(tm, aw + uw), out_shape=jax.ShapeDtypeStruct((s, aw + uw), BF16),
        compiler_params=_cparams(("parallel",)),
    )(attn, y, gl, ga, gs)


def _merge_bwd(dm, attn, y, gl, ga, gs, name):
    s, aw = attn.shape
    uw = y.shape[1]
    tm = min(256, s)

    def body(dm_ref, a_ref, y_ref, gl_ref, ga_ref, gs_ref, da_ref, dgl_ref, dz_ref, dga_ref, dgs_ref):
        i = pl.program_id(0)
        av = a_ref[...]
        dma = dm_ref[:, :aw]
        ra = lax.rsqrt(jnp.mean(av * av, axis=-1, keepdims=True) + EPS)
        ah = av * ra
        e = dma * ga_ref[...]
        da_ref[...] = (ra * (e - ah * jnp.mean(e * ah, axis=-1, keepdims=True))).astype(BF16)
        p_ga = jnp.sum(dma * ah, axis=0, keepdims=True)

        z = _gelu(y_ref[...])
        sig = _sigmoid(gl_ref[...])
        ssm = z * sig
        dms = dm_ref[:, aw:]
        rs = lax.rsqrt(jnp.mean(ssm * ssm, axis=-1, keepdims=True) + EPS)
        sh = ssm * rs
        e2 = dms * gs_ref[...]
        dssm = rs * (e2 - sh * jnp.mean(e2 * sh, axis=-1, keepdims=True))
        dz_ref[...] = dssm * sig
        dgl_ref[...] = (dssm * z * sig * (1.0 - sig)).astype(BF16)
        p_gs = jnp.sum(dms * sh, axis=0, keepdims=True)

        @pl.when(i == 0)
        def _():
            dga_ref[...] = p_ga
            dgs_ref[...] = p_gs

        @pl.when(i > 0)
        def _():
            dga_ref[...] += p_ga
            dgs_ref[...] += p_gs

    return pl.pallas_call(
        body, name=name, grid=(s // tm,),
        in_specs=[_tok(tm, aw + uw), _tok(tm, aw), _tok(tm, uw), _tok(tm, uw), _row(aw), _row(uw)],
        out_specs=(_tok(tm, aw), _tok(tm, uw), _tok(tm, uw), _row(aw), _row(uw)),
        out_shape=(jax.ShapeDtypeStruct((s, aw), BF16), jax.ShapeDtypeStruct((s, uw), BF16), jax.ShapeDtypeStruct((s, uw), F32),
                   jax.ShapeDtypeStruct((1, aw), F32), jax.ShapeDtypeStruct((1, uw), F32)),
        compiler_params=_cparams(("arbitrary",)),
    )(dm, attn, y, gl, ga, gs)


def _gelu_bwd(dzd, dz2, y, name):
    s, u = y.shape
    tm = min(512, s)

    def body(a_ref, b_ref, y_ref, o_ref):
        o_ref[...] = (a_ref[...] + b_ref[...]) * _gelu_grad(y_ref[...])

    return pl.pallas_call(
        body, name=name, grid=(s // tm,), in_specs=[_tok(tm, u), _tok(tm, u), _tok(tm, u)], out_specs=_tok(tm, u),
        out_shape=jax.ShapeDtypeStruct((s, u), F32), compiler_params=_cparams(("parallel",)),
    )(dzd, dz2, y)


def _attn_scores(qh, kb, sink, valid):
    s = lax.dot_general(qh, kb, (((1,), (1,)), ((), ())), preferred_element_type=F32) * (HEAD_DIM ** -0.5)
    s = jnp.where(valid, s, NEG)
    m = jnp.maximum(jnp.max(s, axis=-1, keepdims=True), sink)
    e = jnp.exp(s - m)
    esink = jnp.exp(sink - m)
    den = jnp.sum(e, axis=-1, keepdims=True) + esink
    return e / den, esink / den


def _attn_valid(i):
    qi = lax.broadcasted_iota(jnp.int32, (KV_RATIO * WINDOW, 2 * WINDOW), 0) % WINDOW
    kj = lax.broadcasted_iota(jnp.int32, (KV_RATIO * WINDOW, 2 * WINDOW), 1)
    return (kj > qi) & (kj <= qi + WINDOW) & ((kj >= WINDOW) | (i > 0))


def _stack_heads(ref, hk):
    return jnp.concatenate([ref[:, (hk * KV_RATIO + g) * HEAD_DIM:(hk * KV_RATIO + g + 1) * HEAD_DIM] for g in range(KV_RATIO)], axis=0)


def _stack_sinks(sink_ref, hk):
    return jnp.concatenate([jnp.full((WINDOW, 1), sink_ref[hk * KV_RATIO + g], F32) for g in range(KV_RATIO)], axis=0)


def _band(kvp, kvc, off):
    return jnp.concatenate([kvp[:, off:off + HEAD_DIM], kvc[:, off:off + HEAD_DIM]], axis=0)


def _attn_specs(aw, uw, kvw):
    qblk = uw // aw
    kvblk = (uw + aw) // (2 * kvw)
    assert uw % aw == 0 and (uw + aw) % (2 * kvw) == 0
    return [
        pl.BlockSpec(memory_space=pltpu.SMEM),
        pl.BlockSpec((WINDOW, aw), lambda i: (i, qblk)),
        pl.BlockSpec((WINDOW, 2 * kvw), lambda i: (i, kvblk)),
        pl.BlockSpec((WINDOW, 2 * kvw), lambda i: (jnp.maximum(i - 1, 0), kvblk)),
    ]


def _attn_fwd(proj, sinks, aw, uw, name):
    s = proj.shape[0]
    nq = aw // HEAD_DIM
    nkv = nq // KV_RATIO
    kvw = nkv * HEAD_DIM

    def body(sink_ref, q_ref, kvc_ref, kvp_ref, o_ref):
        valid = _attn_valid(pl.program_id(0))[:WINDOW]
        q = q_ref[...]
        kvc = kvc_ref[...]
        kvp = kvp_ref[...]
        for hk in range(nkv):
            kb = _band(kvp, kvc, hk * HEAD_DIM)
            vb = _band(kvp, kvc, kvw + hk * HEAD_DIM)
            for g in range(KV_RATIO):
                hq = hk * KV_RATIO + g
                p, _ = _attn_scores(q[:, hq * HEAD_DIM:(hq + 1) * HEAD_DIM], kb, sink_ref[hq], valid)
                o_ref[:, hq * HEAD_DIM:(hq + 1) * HEAD_DIM] = jnp.dot(p.astype(BF16), vb, preferred_element_type=F32)

    return pl.pallas_call(
        body, name=name, grid=(s // WINDOW,), in_specs=_attn_specs(aw, uw, kvw),
        out_specs=pl.BlockSpec((WINDOW, aw), lambda i: (i, 0)), out_shape=jax.ShapeDtypeStruct((s, aw), F32),
        compiler_params=_cparams(("parallel",)),
    )(sinks, proj, proj, proj)


def _attn_bwd(proj, sinks, attn, dattn, aw, uw, name):
    s = proj.shape[0]
    nq = aw // HEAD_DIM
    nkv = nq // KV_RATIO
    kvw = nkv * HEAD_DIM
    hd = HEAD_DIM

    def body(sink_ref, q_ref, kvc_ref, kvp_ref, o_ref, do_ref, dq_ref, dc_ref, dp_ref, ds_ref):
        i = pl.program_id(0)
        valid = _attn_valid(i)
        kvc = kvc_ref[...]
        kvp = kvp_ref[...]
        lane = lax.broadcasted_iota(jnp.int32, (1, nq), 1)
        dsink = jnp.zeros((1, nq), F32)
        for hk in range(nkv):
            kb = _band(kvp, kvc, hk * hd)
            vb = _band(kvp, kvc, kvw + hk * hd)
            qs = _stack_heads(q_ref, hk)
            dos = _stack_heads(do_ref, hk)
            p, psink = _attn_scores(qs, kb, _stack_sinks(sink_ref, hk), valid)
            delta = jnp.sum(dos.astype(F32) * _stack_heads(o_ref, hk), axis=-1, keepdims=True)
            dpv = lax.dot_general(dos, vb, (((1,), (1,)), ((), ())), preferred_element_type=F32)
            dsb = (p * (dpv - delta) * (hd ** -0.5)).astype(BF16)
            dqs = jnp.dot(dsb, kb, preferred_element_type=F32).astype(BF16)
            dkb = lax.dot_general(dsb, qs, (((0,), (0,)), ((), ())), preferred_element_type=F32)
            dvb = lax.dot_general(p.astype(BF16), dos, (((0,), (0,)), ((), ())), preferred_element_type=F32)
            sink_term = psink * delta
            for g in range(KV_RATIO):
                hq = hk * KV_RATIO + g
                dq_ref[:, hq * hd:(hq + 1) * hd] = dqs[g * WINDOW:(g + 1) * WINDOW]
                dsink = dsink + jnp.where(lane == hq, -jnp.sum(sink_term[g * WINDOW:(g + 1) * WINDOW]), 0.0)
            dp_ref[:, hk * hd:(hk + 1) * hd] = dkb[:WINDOW]
            dc_ref[:, hk * hd:(hk + 1) * hd] = dkb[WINDOW:]
            dp_ref[:, kvw + hk * hd:kvw + (hk + 1) * hd] = dvb[:WINDOW]
            dc_ref[:, kvw + hk * hd:kvw + (hk + 1) * hd] = dvb[WINDOW:]

        @pl.when(i == 0)
        def _():
            ds_ref[...] = dsink

        @pl.when(i > 0)
        def _():
            ds_ref[...] += dsink

    blk_a = pl.BlockSpec((WINDOW, aw), lambda i: (i, 0))
    blk_kv = pl.BlockSpec((WINDOW, 2 * kvw), lambda i: (i, 0))
    return pl.pallas_call(
        body, name=name, grid=(s // WINDOW,), in_specs=_attn_specs(aw, uw, kvw) + [blk_a, blk_a],
        out_specs=(blk_a, blk_kv, blk_kv, pl.BlockSpec((1, nq), lambda i: (0, 0))),
        out_shape=(jax.ShapeDtypeStruct((s, aw), BF16), jax.ShapeDtypeStruct((s, 2 * kvw), F32),
                   jax.ShapeDtypeStruct((s, 2 * kvw), F32), jax.ShapeDtypeStruct((1, nq), F32)),
        compiler_params=_cparams(("arbitrary",)),
    )(sinks, proj, proj, proj, attn, dattn)


def _assemble_dproj(du, dq, dkv_cur, dkv_prev, name):
    s, uw = du.shape
    aw = dq.shape[1]
    kv2 = dkv_cur.shape[1]
    nb = s // WINDOW

    def body(du_ref, dq_ref, dc_ref, dp_ref, o_ref):
        i = pl.program_id(0)
        o_ref[:, :uw] = du_ref[...].astype(BF16)
        o_ref[:, uw:uw + aw] = dq_ref[...]
        nxt = jnp.where(i < nb - 1, 1.0, 0.0)
        o_ref[:, uw + aw:] = (dc_ref[...] + nxt * dp_ref[...]).astype(BF16)

    return pl.pallas_call(
        body, name=name, grid=(nb,),
        in_specs=[_tok(WINDOW, uw), _tok(WINDOW, aw), _tok(WINDOW, kv2),
                  pl.BlockSpec((WINDOW, kv2), lambda i: (jnp.minimum(i + 1, nb - 1), 0))],
        out_specs=_tok(WINDOW, uw + aw + kv2), out_shape=jax.ShapeDtypeStruct((s, uw + aw + kv2), BF16),
        compiler_params=_cparams(("parallel",)),
    )(du, dq, dkv_cur, dkv_prev)


def _zoh(lr, li, ls, btr, bti):
    dt = jnp.exp(ls)
    mag = jnp.exp(lr * dt)
    ang = li * dt
    ar = mag * jnp.cos(ang)
    ai = mag * jnp.sin(ang)
    den = lr * lr + li * li
    fr = ((ar - 1.0) * lr + ai * li) / den
    fi = (ai * lr - (ar - 1.0) * li) / den
    return ar, ai, fr[None] * btr - fi[None] * bti, fr[None] * bti + fi[None] * btr


def _ssm_prep(lr, li, ls, btr, bti, name):
    def body(lr_ref, li_ref, ls_ref, btr_ref, bti_ref, ar_ref, ai_ref, bbr_ref, bbi_ref):
        ar, ai, bbr, bbi = _zoh(lr_ref[...], li_ref[...], ls_ref[...], btr_ref[...], bti_ref[...])
        ar_ref[...] = ar
        ai_ref[...] = ai
        bbr_ref[...] = bbr
        bbi_ref[...] = bbi

    s2 = jax.ShapeDtypeStruct(lr.shape, F32)
    s3 = jax.ShapeDtypeStruct(btr.shape, F32)
    return pl.pallas_call(body, name=name, out_shape=(s2, s2, s3, s3))(lr, li, ls, btr, bti)


def _ssm_prep_bwd(lr, li, ls, btr, bti, dar, dai, dbbr, dbbi, name):
    def body(lr_ref, li_ref, ls_ref, btr_ref, bti_ref, dar_ref, dai_ref, dbbr_ref, dbbi_ref, o1, o2, o3, o4, o5):
        _, vjp = jax.vjp(_zoh, lr_ref[...], li_ref[...], ls_ref[...], btr_ref[...], bti_ref[...])
        g = vjp((dar_ref[...], dai_ref[...], dbbr_ref[...], dbbi_ref[...]))
        for o, v in zip((o1, o2, o3, o4, o5), g):
            o[...] = v

    s2 = jax.ShapeDtypeStruct(lr.shape, F32)
    s3 = jax.ShapeDtypeStruct(btr.shape, F32)
    return pl.pallas_call(body, name=name, out_shape=(s2, s2, s2, s3, s3))(lr, li, ls, btr, bti, dar, dai, dbbr, dbbi)


def _state_tiles(ref):
    return [ref[:, cb * 128:(cb + 1) * 128] for cb in range(4)]


def _gather_rows(ref_re, ref_im, r, t):
    return jnp.concatenate([ref_re.at[cb][pl.ds(r, t, stride=8), :] for cb in range(4)]
                           + [ref_im.at[cb][pl.ds(r, t, stride=8), :] for cb in range(4)], axis=1)


def _scatter_rows(ref_re, ref_im, r, t, val):
    for cb in range(4):
        ref_re.at[cb][pl.ds(r, t, stride=8), :] = val[:, cb * 128:(cb + 1) * 128]
        ref_im.at[cb][pl.ds(r, t, stride=8), :] = val[:, PSTATES + cb * 128:PSTATES + (cb + 1) * 128]


def _ssm_fwd(proj, bp, cp, a_re, a_im, dvec, uw, name, t=128):
    s = proj.shape[0]
    npc = uw // PIECE
    assert npc == 8 and s % t == 0

    def body(u_ref, bp_ref, cp_ref, ar_ref, ai_ref, d_ref, y_ref, xr_ref, xi_ref, cr_ref, ci_ref):
        i = pl.program_id(0)

        @pl.when(i == 0)
        def _():
            cr_ref[...] = jnp.zeros_like(cr_ref)
            ci_ref[...] = jnp.zeros_like(ci_ref)

        for r in range(npc):
            bu = jnp.dot(u_ref[:, r * PIECE:(r + 1) * PIECE], bp_ref[r], preferred_element_type=F32)
            _scatter_rows(xr_ref, xi_ref, r, t, bu)
        ar = _state_tiles(ar_ref)
        ai = _state_tiles(ai_ref)

        def step(tt, carry):
            xr, xi = carry
            off = pl.multiple_of(tt * 8, 8)
            nr, ni = [], []
            for cb in range(4):
                vr = ar[cb] * xr[cb] - ai[cb] * xi[cb] + xr_ref[cb, pl.ds(off, 8), :]
                vi = ar[cb] * xi[cb] + ai[cb] * xr[cb] + xi_ref[cb, pl.ds(off, 8), :]
                xr_ref[cb, pl.ds(off, 8), :] = vr
                xi_ref[cb, pl.ds(off, 8), :] = vi
                nr.append(vr)
                ni.append(vi)
            return tuple(nr), tuple(ni)

        xr, xi = lax.fori_loop(0, t, step, (tuple(_state_tiles(cr_ref)), tuple(_state_tiles(ci_ref))), unroll=4)
        for cb in range(4):
            cr_ref[:, cb * 128:(cb + 1) * 128] = xr[cb]
            ci_ref[:, cb * 128:(cb + 1) * 128] = xi[cb]
        for r in range(npc):
            xs = _gather_rows(xr_ref, xi_ref, r, t).astype(BF16)
            y_ref[:, r * PIECE:(r + 1) * PIECE] = (
                jnp.dot(xs, cp_ref[r], preferred_element_type=F32)
                + d_ref[:, r * PIECE:(r + 1) * PIECE] * u_ref[:, r * PIECE:(r + 1) * PIECE].astype(F32))

    full3 = lambda shp: pl.BlockSpec(shp, lambda i: (0, 0, 0))
    full2 = lambda shp: pl.BlockSpec(shp, lambda i: (0, 0))
    xs_spec = pl.BlockSpec((4, t * 8, 128), lambda i: (0, i, 0))
    xs_shape = jax.ShapeDtypeStruct((4, s * 8, 128), F32)
    return pl.pallas_call(
        body, name=name, grid=(s // t,),
        in_specs=[pl.BlockSpec((t, uw), lambda i: (i, 0)), full3(bp.shape), full3(cp.shape), full2(a_re.shape), full2(a_im.shape),
                  full2(dvec.shape)],
        out_specs=(pl.BlockSpec((t, uw), lambda i: (i, 0)), xs_spec, xs_spec),
        out_shape=(jax.ShapeDtypeStruct((s, uw), F32), xs_shape, xs_shape),
        scratch_shapes=[pltpu.VMEM((8, PSTATES), F32), pltpu.VMEM((8, PSTATES), F32)],
        compiler_params=_cparams(("arbitrary",), 56),
    )(proj, bp, cp, a_re, a_im, dvec)


def _ssm_bwd(dy, proj, xs_re, xs_im, cpt, bpt, a_re, a_im, dvec, uw, name, t=128):
    s = proj.shape[0]
    npc = uw // PIECE
    nt = s // t
    assert npc == 8 and s % t == 0

    def body(dy_ref, u_ref, xr_ref, xi_ref, hr_ref, hi_ref, cpt_ref, bpt_ref, ar_ref, ai_ref, d_ref,
             du_ref, dbp_ref, dcp_ref, dar_ref, dai_ref, dd_ref, gr_ref, gi_ref, lr_ref, li_ref):
        i = pl.program_id(0)

        @pl.when(i == 0)
        def _():
            lr_ref[...] = jnp.zeros_like(lr_ref)
            li_ref[...] = jnp.zeros_like(li_ref)
            dbp_ref[...] = jnp.zeros_like(dbp_ref)
            dcp_ref[...] = jnp.zeros_like(dcp_ref)
            dar_ref[...] = jnp.zeros_like(dar_ref)
            dai_ref[...] = jnp.zeros_like(dai_ref)
            dd_ref[...] = jnp.zeros_like(dd_ref)

        dyb = dy_ref[...].astype(BF16)
        for r in range(npc):
            gx = jnp.dot(dyb[:, r * PIECE:(r + 1) * PIECE], cpt_ref[r], preferred_element_type=F32)
            _scatter_rows(gr_ref, gi_ref, r, t, gx)
        ar = _state_tiles(ar_ref)
        ai = _state_tiles(ai_ref)

        def adjoint(off, lam_r, lam_i, xpr, xpi, acc_r, acc_i):
            nr, ni, qr, qi = [], [], [], []
            for cb in range(4):
                vr = gr_ref[cb, pl.ds(off, 8), :] + ar[cb] * lam_r[cb] + ai[cb] * lam_i[cb]
                vi = gi_ref[cb, pl.ds(off, 8), :] + ar[cb] * lam_i[cb] - ai[cb] * lam_r[cb]
                gr_ref[cb, pl.ds(off, 8), :] = vr
                gi_ref[cb, pl.ds(off, 8), :] = vi
                nr.append(vr)
                ni.append(vi)
                qr.append(acc_r[cb] + vr * xpr[cb] + vi * xpi[cb])
                qi.append(acc_i[cb] + vi * xpr[cb] - vr * xpi[cb])
            return tuple(nr), tuple(ni), tuple(qr), tuple(qi)

        def step(j, carry):
            lam_r, lam_i, acc_r, acc_i = carry
            tt = t - 1 - j
            off = pl.multiple_of(tt * 8, 8)
            offp = pl.multiple_of(tt * 8 - 8, 8)
            xpr = [xr_ref[cb, pl.ds(offp, 8), :] for cb in range(4)]
            xpi = [xi_ref[cb, pl.ds(offp, 8), :] for cb in range(4)]
            return adjoint(off, lam_r, lam_i, xpr, xpi, acc_r, acc_i)

        zero4 = tuple(jnp.zeros((8, 128), F32) for _ in range(4))
        carry = lax.fori_loop(0, t - 1, step, (tuple(_state_tiles(lr_ref)), tuple(_state_tiles(li_ref)), zero4, zero4), unroll=4)
        has_prev = jnp.where(i < nt - 1, 1.0, 0.0)
        xpr = [hr_ref[cb] * has_prev for cb in range(4)]
        xpi = [hi_ref[cb] * has_prev for cb in range(4)]
        lam_r, lam_i, acc_r, acc_i = adjoint(0, carry[0], carry[1], xpr, xpi, carry[2], carry[3])
        for cb in range(4):
            lr_ref[:, cb * 128:(cb + 1) * 128] = lam_r[cb]
            li_ref[:, cb * 128:(cb + 1) * 128] = lam_i[cb]
            dar_ref[:, cb * 128:(cb + 1) * 128] += acc_r[cb]
            dai_ref[:, cb * 128:(cb + 1) * 128] += acc_i[cb]

        dyv = dy_ref[...]
        uv = u_ref[...]
        dd_ref[...] += jnp.sum(dyv * uv.astype(F32), axis=0, keepdims=True)
        for r in range(npc):
            lam = _gather_rows(gr_ref, gi_ref, r, t).astype(BF16)
            sl = slice(r * PIECE, (r + 1) * PIECE)
            du_ref[:, sl] = jnp.dot(lam, bpt_ref[r], preferred_element_type=F32) + d_ref[:, sl] * dyv[:, sl]
            dbp_ref[r] += lax.dot_general(uv[:, sl], lam, (((0,), (0,)), ((), ())), preferred_element_type=F32)
            xs = _gather_rows(xr_ref, xi_ref, r, t).astype(BF16)
            dcp_ref[r] += lax.dot_general(xs, dyb[:, sl], (((0,), (0,)), ((), ())), preferred_element_type=F32)

    rev = lambda i: (nt - 1 - i, 0)
    full3 = lambda shp: pl.BlockSpec(shp, lambda i: (0, 0, 0))
    full2 = lambda shp: pl.BlockSpec(shp, lambda i: (0, 0))
    xs_spec = pl.BlockSpec((4, t * 8, 128), lambda i: (0, nt - 1 - i, 0))
    halo_spec = pl.BlockSpec((4, 8, 128), lambda i: (0, jnp.maximum((nt - 1 - i) * t - 1, 0), 0))
    st = jax.ShapeDtypeStruct((8, PSTATES), F32)
    return pl.pallas_call(
        body, name=name, grid=(nt,),
        in_specs=[pl.BlockSpec((t, uw), rev), pl.BlockSpec((t, uw), rev), xs_spec, xs_spec, halo_spec, halo_spec,
                  full3(cpt.shape), full3(bpt.shape), full2(a_re.shape), full2(a_im.shape), full2(dvec.shape)],
        out_specs=(pl.BlockSpec((t, uw), rev), full3((npc, PIECE, 2 * PSTATES)), full3((npc, 2 * PSTATES, PIECE)),
                   full2((8, PSTATES)), full2((8, PSTATES)), full2((1, uw))),
        out_shape=(jax.ShapeDtypeStruct((s, uw), F32), jax.ShapeDtypeStruct((npc, PIECE, 2 * PSTATES), F32),
                   jax.ShapeDtypeStruct((npc, 2 * PSTATES, PIECE), F32), st, st, jax.ShapeDtypeStruct((1, uw), F32)),
        scratch_shapes=[pltpu.VMEM((4, t * 8, 128), F32), pltpu.VMEM((4, t * 8, 128), F32),
                        pltpu.VMEM((8, PSTATES), F32), pltpu.VMEM((8, PSTATES), F32)],
        compiler_params=_cparams(("arbitrary",), 56),
    )(dy, proj, xs_re, xs_im, xs_re, xs_im, cpt, bpt, a_re, a_im, dvec)


_RC = 16
_LC = 128


def _taps_before(cur, prev):
    row = lax.broadcasted_iota(jnp.int32, cur.shape, 0)
    x1 = jnp.where(row == 0, pltpu.roll(prev, 1, 0), pltpu.roll(cur, 1, 0))
    x2 = jnp.where(row < 2, pltpu.roll(prev, 2, 0), pltpu.roll(cur, 2, 0))
    return x1, x2


def _conv3(cur, prev, w, b):
    x1, x2 = _taps_before(cur, prev)
    return ((b + x2 * w[0:1]) + x1 * w[1:2]) + cur * w[2:3], (x2, x1, cur)


def _ffn_tiles(s, f):
    tm = min(256, s)
    tn = f // 4 if (f // 4) % _LC == 0 else f
    assert tm % _RC == 0 and tn % _LC == 0
    return tm, tn


def _conv_glu_fwd(up0, cw, cb, name):
    _, s, f = up0.shape
    tm, tn = _ffn_tiles(s, f)
    hb = tm // _RC

    def body(x_ref, h_ref, w_ref, b_ref, a_ref):
        first = jnp.where(pl.program_id(0) > 0, 1.0, 0.0)
        row = lax.broadcasted_iota(jnp.int32, (tm, tn), 0)
        ups = []
        for p in range(2):
            h = h_ref[p].astype(F32) * first
            h6, h7 = h[_RC - 2:_RC - 1], h[_RC - 1:_RC]
            xv = x_ref[p].astype(F32)
            x1 = jnp.where(row == 0, h7, pltpu.roll(xv, 1, 0))
            x2 = jnp.where(row == 0, h6, jnp.where(row == 1, h7, pltpu.roll(xv, 2, 0)))
            w = w_ref[p]
            ups.append(((b_ref[p] + x2 * w[0:1]) + x1 * w[1:2]) + xv * w[2:3])
        a_ref[...] = (_gelu(ups[1]) * ups[0]).astype(BF16)

    return pl.pallas_call(
        body, name=name, grid=(s // tm, f // tn),
        in_specs=[pl.BlockSpec((2, tm, tn), lambda i, j: (0, i, j)),
                  pl.BlockSpec((2, _RC, tn), lambda i, j: (0, jnp.maximum(i * hb - 1, 0), j)),
                  pl.BlockSpec((2, 3, tn), lambda i, j: (0, 0, j)), pl.BlockSpec((2, 1, tn), lambda i, j: (0, 0, j))],
        out_specs=pl.BlockSpec((tm, tn), lambda i, j: (i, j)), out_shape=jax.ShapeDtypeStruct((s, f), BF16),
        compiler_params=_cparams(("parallel", "parallel")),
    )(up0, up0, cw, cb)


def _ffn_bwd_gate(da, up0, cw, cb, name):
    _, s, f = up0.shape
    tm, tn = _ffn_tiles(s, f)
    hb = tm // _RC

    def body(da_ref, x_ref, h_ref, w_ref, b_ref, d_ref, dw_ref, db_ref):
        i = pl.program_id(1)
        first = jnp.where(i > 0, 1.0, 0.0)

        @pl.when(i == 0)
        def _():
            dw_ref[...] = jnp.zeros_like(dw_ref)
            db_ref[...] = jnp.zeros_like(db_ref)

        for lc in range(tn // _LC):
            lanes = slice(lc * _LC, (lc + 1) * _LC)
            w = [w_ref[p, :, lanes] for p in range(2)]
            b = [b_ref[p, :, lanes] for p in range(2)]

            def step(rc, carry):
                prev, acc = carry
                rows = pl.ds(pl.multiple_of(rc * _RC, _RC), _RC)
                cur = tuple(x_ref[p, rows, lanes].astype(F32) for p in range(2))
                ups, taps = zip(*[_conv3(cur[p], prev[p], w[p], b[p]) for p in range(2)])
                dav = da_ref[rows, lanes]
                gate, dgate = _gelu_and_grad(ups[1])
                douts = (dav * gate, dav * ups[0] * dgate)
                new_acc = []
                for p in range(2):
                    d_ref[p, rows, lanes] = douts[p].astype(BF16)
                    new_acc.append(tuple([acc[p][kk] + douts[p] * taps[p][kk] for kk in range(3)] + [acc[p][3] + douts[p]]))
                return cur, tuple(new_acc)

            zero = jnp.zeros((_RC, _LC), F32)
            init = (tuple(h_ref[p, :, lanes].astype(F32) * first for p in range(2)), ((zero,) * 4, (zero,) * 4))
            _, acc = lax.fori_loop(0, tm // _RC, step, init, unroll=2)
            for p in range(2):
                for kk in range(3):
                    dw_ref[p, kk:kk + 1, lanes] += jnp.sum(acc[p][kk], axis=0, keepdims=True)
                db_ref[p, :, lanes] += jnp.sum(acc[p][3], axis=0, keepdims=True)

    return pl.pallas_call(
        body, name=name, grid=(f // tn, s // tm),
        in_specs=[pl.BlockSpec((tm, tn), lambda j, i: (i, j)), pl.BlockSpec((2, tm, tn), lambda j, i: (0, i, j)),
                  pl.BlockSpec((2, _RC, tn), lambda j, i: (0, jnp.maximum(i * hb - 1, 0), j)),
                  pl.BlockSpec((2, 3, tn), lambda j, i: (0, 0, j)), pl.BlockSpec((2, 1, tn), lambda j, i: (0, 0, j))],
        out_specs=(pl.BlockSpec((2, tm, tn), lambda j, i: (0, i, j)), pl.BlockSpec((2, 3, tn), lambda j, i: (0, 0, j)),
                   pl.BlockSpec((2, 1, tn), lambda j, i: (0, 0, j))),
        out_shape=(jax.ShapeDtypeStruct((2, s, f), BF16), jax.ShapeDtypeStruct((2, 3, f), F32), jax.ShapeDtypeStruct((2, 1, f), F32)),
        compiler_params=_cparams(("parallel", "arbitrary")),
    )(da, up0, up0, cw, cb)


def _conv_bwd(dup, cw, name):
    _, s, f = dup.shape
    tm, tn = _ffn_tiles(s, f)
    hb = tm // _RC
    nb = s // tm

    def body(d_ref, h_ref, w_ref, o_ref):
        last = jnp.where(pl.program_id(0) < nb - 1, 1.0, 0.0)
        row = lax.broadcasted_iota(jnp.int32, (tm, tn), 0)
        for p in range(2):
            d = d_ref[p].astype(F32)
            h = h_ref[p].astype(F32) * last
            d1 = jnp.where(row == tm - 1, h[0:1], pltpu.roll(d, tm - 1, 0))
            d2 = jnp.where(row == tm - 1, h[1:2], jnp.where(row == tm - 2, h[0:1], pltpu.roll(d, tm - 2, 0)))
            w = w_ref[p]
            o_ref[p] = (d * w[2:3] + d1 * w[1:2] + d2 * w[0:1]).astype(BF16)

    return pl.pallas_call(
        body, name=name, grid=(nb, f // tn),
        in_specs=[pl.BlockSpec((2, tm, tn), lambda i, j: (0, i, j)),
                  pl.BlockSpec((2, _RC, tn), lambda i, j: (0, jnp.minimum((i + 1) * hb, s // _RC - 1), j)),
                  pl.BlockSpec((2, 3, tn), lambda i, j: (0, 0, j))],
        out_specs=pl.BlockSpec((2, tm, tn), lambda i, j: (0, i, j)), out_shape=jax.ShapeDtypeStruct((2, s, f), BF16),
        compiler_params=_cparams(("parallel", "parallel")),
    )(dup, dup, cw)


def _ada_part(c_all, w_ada, name):
    nb, d = c_all.shape
    depth, _, cols = w_ada.shape
    tn = 1024 if cols % 1024 == 0 else cols

    def body(c_ref, w_ref, o_ref, ca_ref):
        cv = c_ref[...]
        ca = cv * _sigmoid(cv)
        ca_ref[...] = ca
        o_ref[...] = jnp.dot(ca.astype(BF16), w_ref[...].astype(BF16), preferred_element_type=F32)

    return pl.pallas_call(
        body, name=name, grid=(depth, cols // tn),
        in_specs=[pl.BlockSpec((nb, d), lambda l, j: (0, 0)), pl.BlockSpec((None, d, tn), lambda l, j: (l, 0, j))],
        out_specs=(pl.BlockSpec((None, nb, tn), lambda l, j: (l, 0, j)), pl.BlockSpec((nb, d), lambda l, j: (0, 0))),
        out_shape=(jax.ShapeDtypeStruct((depth, nb, cols), F32), jax.ShapeDtypeStruct((nb, d), F32)),
        compiler_params=_cparams(("arbitrary", "arbitrary")),
    )(c_all, w_ada)


def _ada_select(gath, b_ada, name):
    depth, n6 = b_ada.shape
    cols = gath.shape[1]

    def body(g_ref, b_ref, o_ref):
        me = 4 * lax.axis_index("x") + 2 * lax.axis_index("y") + lax.axis_index("c")
        for l in range(depth):
            for j in range(n6 // cols):
                row = (2 * j) * (8 * depth) + l * 8 + me
                o_ref[l:l + 1, j * cols:(j + 1) * cols] = g_ref[pl.ds(row, 1), :] + b_ref[l:l + 1, j * cols:(j + 1) * cols]

    return pl.pallas_call(body, name=name, out_shape=jax.ShapeDtypeStruct((depth, n6), F32))(gath, b_ada)


def _wada_grad(ca_t, d_sel, name):
    d, nb = ca_t.shape
    depth, _, cols = d_sel.shape
    tm = min(256, d)

    def body(a_ref, g_ref, o_ref):
        acc = a_ref[:, 0:1] * g_ref[0:1, :]
        for b in range(1, nb):
            acc = acc + a_ref[:, b:b + 1] * g_ref[b:b + 1, :]
        o_ref[...] = acc

    return pl.pallas_call(
        body, name=name, grid=(depth, d // tm),
        in_specs=[pl.BlockSpec((tm, nb), lambda l, i: (i, 0)), pl.BlockSpec((None, nb, cols), lambda l, i: (l, 0, 0))],
        out_specs=pl.BlockSpec((None, tm, cols), lambda l, i: (l, i, 0)), out_shape=jax.ShapeDtypeStruct((depth, d, cols), F32),
        compiler_params=_cparams(("parallel", "parallel")),
    )(ca_t, d_sel)


def _block_rows(r, c):
    tr = r
    for cand in (2048, 1024, 512, 256, 128, 64, 32, 16, 8):
        if r % cand == 0 and cand * c * 4 <= MIB:
            tr = cand
            break
    else:
        for cand in (8, 16, 32):
            if r % cand == 0:
                tr = cand
                break
    return tr


def _adamw(w, g, m, v, name):
    nl, r, c = w.shape
    tr = _block_rows(r, c)
    c1 = 1.0 - ADAM_B1 ** ADAM_STEP
    c2 = 1.0 - ADAM_B2 ** ADAM_STEP

    def body(w_ref, g_ref, m_ref, v_ref, d_ref, nm_ref, nv_ref):
        gv = g_ref[...]
        nm = ADAM_B1 * m_ref[...] + (1.0 - ADAM_B1) * gv
        nv = ADAM_B2 * v_ref[...] + (1.0 - ADAM_B2) * (gv * gv)
        d_ref[...] = -ADAM_LR * ((nm / c1) / (jnp.sqrt(nv / c2) + ADAM_EPS) + ADAM_WD * w_ref[...])
        nm_ref[...] = nm
        nv_ref[...] = nv

    spec = pl.BlockSpec((None, tr, c), lambda l, i: (l, i, 0))
    shp = jax.ShapeDtypeStruct((nl, r, c), F32)
    return pl.pallas_call(
        body, name=name, grid=(nl, r // tr), in_specs=[spec] * 4, out_specs=(spec,) * 3, out_shape=(shp,) * 3,
        compiler_params=_cparams(("parallel", "parallel")),
    )(w, g, m, v)


def _sum_slots(x, nslots, name, out_dtype=F32):
    r = x.shape[0] // nslots
    c = x.shape[1]
    tr = _block_rows(r, c)
    nbk = r // tr

    def body(*refs):
        acc = refs[0][...].astype(F32)
        for k in range(1, nslots):
            acc = acc + refs[k][...].astype(F32)
        refs[nslots][...] = acc.astype(out_dtype)

    specs = [pl.BlockSpec((tr, c), functools.partial(lambda k, i: (k * nbk + i, 0), k)) for k in range(nslots)]
    return pl.pallas_call(
        body, name=name, grid=(nbk,), in_specs=specs, out_specs=pl.BlockSpec((tr, c), lambda i: (i, 0)),
        out_shape=jax.ShapeDtypeStruct((r, c), out_dtype), compiler_params=_cparams(("parallel",)),
    )(*([x] * nslots))


def _pick(dim, prefs):
    for p in prefs:
        if dim % p == 0:
            return p
    return dim


def _mm(a, b, m, n, k, name, out_dtype, **kw):
    tm = _pick(m, (1408, 1024, 512, 256, 128))
    tn = _pick(n, (1408, 1152, 1024, 512, 256, 128))
    kdiv = k // max(kw.get("a_stack", 0), kw.get("b_stack", 0) if kw.get("tb") else 0, 1)
    osize = jnp.dtype(out_dtype).itemsize
    tk = kdiv
    for cut in (1, 2, 4, 8, 16):
        tk = kdiv // cut
        vmem = 2 * 2 * tk * (tm + tn) + tm * tn * (2 * osize + 4 + (4 if tk < k else 0))
        if kdiv % cut == 0 and tk % 128 == 0 and vmem <= _MATMUL_VMEM_BUDGET:
            break
    return _matmul(a, b, m=m, n=n, k=k, tm=tm, tn=tn, tk=tk, out_dtype=out_dtype, name=name, **kw)


def _ssm_layout(p):
    g, st = p["lam_re"].shape
    npc = g * st // PSTATES
    lr = p["lam_re"].reshape(npc, PSTATES)
    li = p["lam_im"].reshape(npc, PSTATES)
    ls = jnp.broadcast_to(p["log_step"][:, None], (g, st)).reshape(npc, PSTATES)
    btr = jnp.transpose(p["ssm_b_re"], (2, 0, 1)).reshape(SSM_GROUP, npc, PSTATES)
    bti = jnp.transpose(p["ssm_b_im"], (2, 0, 1)).reshape(SSM_GROUP, npc, PSTATES)
    return lr, li, ls, btr, bti


def _ssm_pieces(bbr, bbi, c_re, c_im):
    npc = bbr.shape[1]
    gl = PSTATES // STATE
    eye = jnp.eye(gl, dtype=bool)

    def b_piece(bb):
        t = jnp.transpose(bb.reshape(SSM_GROUP, npc, gl, STATE), (1, 2, 0, 3))
        full = jnp.where(eye[None, :, None, :, None], t[:, :, :, None, :], 0.0)
        return full.reshape(npc, gl * SSM_GROUP, PSTATES)

    def c_piece(cc):
        t = jnp.transpose(cc.reshape(npc, gl, SSM_GROUP, STATE), (0, 1, 3, 2))
        full = jnp.where(eye[None, :, None, :, None], t[:, :, :, None, :], 0.0)
        return full.reshape(npc, PSTATES, gl * SSM_GROUP)

    bp = jnp.concatenate([b_piece(bbr), b_piece(bbi)], axis=2).astype(BF16)
    cp = jnp.concatenate([c_piece(c_re), c_piece(-c_im)], axis=1).astype(BF16)
    return bp, cp, jnp.swapaxes(bp, 1, 2), jnp.swapaxes(cp, 1, 2)


def _ssm_unpieces(dbp, dcp):
    npc = dbp.shape[0]
    gl = PSTATES // STATE
    idx = jnp.arange(gl)

    def b_diag(x):
        d = x.reshape(npc, gl, SSM_GROUP, gl, STATE)[:, idx, :, idx, :]
        return jnp.transpose(d, (2, 1, 0, 3)).reshape(SSM_GROUP, npc, PSTATES)

    def c_diag(x):
        d = x.reshape(npc, gl, STATE, gl, SSM_GROUP)[:, idx, :, idx, :]
        return jnp.transpose(d, (1, 0, 3, 2)).reshape(npc * gl, SSM_GROUP, STATE)

    return b_diag(dbp[:, :, :PSTATES]), b_diag(dbp[:, :, PSTATES:]), c_diag(dcp[:, :PSTATES, :]), -c_diag(dcp[:, PSTATES:, :])


class _LayerWeights:
    def __init__(self, fetch):
        self._fetch = fetch
        self._got = {}

    def group(self, g, after=None):
        if g not in self._got:
            self._got[g] = self._fetch(g, after)
        return self._got[g]


def _layer_fwd(l, x, ada6, weights, p):
    s, d = x.shape
    sh_m, sc_m, gt_m, sh_f, sc_f, gt_f = ada6
    w = dict(weights.group("mix", x))
    uw = w["w_glu"].shape[0]
    aw = w["w_out"].shape[0] - uw
    ncol = w["w_in"].shape[1]
    row = lambda v: v.reshape(1, -1)
    n = lambda t: f"l{l}_{t}"

    h = _pre_fwd(x, row(p["g_pre_mix"]), sc_m, sh_m, n("pre_mix"))
    proj = _mm(h, w["w_in"], s, ncol, d, n("proj"), BF16)
    attn = _attn_fwd(proj, p["attn_sinks"], aw, uw, n("attn_fwd"))
    zin = _ssm_layout(p)
    a_re, a_im, bbr, bbi = _ssm_prep(*zin, n("ssm_prep"))
    bp, cp, bpt, cpt = _ssm_pieces(bbr, bbi, p["ssm_c_re"], p["ssm_c_im"])
    dvec = p["ssm_d"].reshape(1, uw)
    y, xs_re, xs_im = _ssm_fwd(proj, bp, cp, a_re, a_im, dvec, uw, n("ssm_fwd"), t=256 if s % 256 == 0 else 128)
    z = _gelu_fwd(y, n("gelu_fwd"))
    gl = _mm(z, w["w_glu"], s, uw, uw, n("glu"), F32)
    merged = _merge_fwd(attn, y, gl, row(p["g_attn_out"]), row(p["g_ssm_out"]), n("merge_fwd"))
    mix = _mm(merged, w["w_out"], s, d, aw + uw, n("out_proj"), F32)
    x1 = _post_fwd(x, mix, row(p["g_post_mix"]), gt_m, n("post_mix"))

    w.update(weights.group("ffn", x1))
    f = w["w_down"].shape[0]
    h2 = _pre_fwd(x1, row(p["g_pre_ffn"]), sc_f, sh_f, n("pre_ffn"))
    up0 = _mm(h2, w["w_up"], s, 2 * f, d, n("up_proj"), BF16, b_stack=w["w_up"].shape[0], o_stack=2)
    cw2 = jnp.transpose(p["conv_w"].reshape(3, 2, f), (1, 0, 2))
    cb2 = p["conv_b"].reshape(2, 1, f)
    act = _conv_glu_fwd(up0, cw2, cb2, n("conv_glu"))
    ff = _mm(act, w["w_down"], s, d, f, n("down_proj"), F32)
    x2 = _post_fwd(x1, ff, row(p["g_post_ffn"]), gt_f, n("post_ffn"))
    saved = dict(x=x, h=h, proj=proj, attn=attn, zin=zin, a_re=a_re, a_im=a_im, bpt=bpt, cpt=cpt, dvec=dvec, y=y, xs_re=xs_re,
                 xs_im=xs_im, z=z, gl=gl, merged=merged, mix=mix, x1=x1, h2=h2, up0=up0, cw2=cw2, cb2=cb2, act=act, ff=ff)
    return x2, saved


def _layer_bwd(l, dx2, ada6, weights, p, sv, on_grads):
    s, d = dx2.shape
    sh_m, sc_m, gt_m, sh_f, sc_f, gt_f = ada6
    w = {**weights.group("mix"), **weights.group("ffn")}
    uw = w["w_glu"].shape[0]
    aw = w["w_out"].shape[0] - uw
    ncol = w["w_in"].shape[1]
    f = w["w_down"].shape[0]
    nst = w["w_up"].shape[0]
    row = lambda v: v.reshape(1, -1)
    n = lambda t: f"l{l}_{t}"
    gw, gs = {}, {}

    dff, dgt_f, gs["g_post_ffn"] = _post_bwd(dx2, sv["ff"], row(p["g_post_ffn"]), gt_f, n("post_ffn_bwd"))
    gw["w_down"] = _mm(sv["act"], dff, f, d, s, n("down_dw"), BF16, ta=True)
    dact = _mm(dff, w["w_down"], s, f, d, n("down_dx"), F32, tb=True)
    dup, dcw2, dcb2 = _ffn_bwd_gate(dact, sv["up0"], sv["cw2"], sv["cb2"], n("ffn_gate_bwd"))
    gs["conv_w"] = jnp.transpose(dcw2, (1, 0, 2)).reshape(3, 2 * f)
    gs["conv_b"] = dcb2.reshape(2 * f)
    dup0 = _conv_bwd(dup, sv["cw2"], n("conv_bwd"))
    gw["w_up"] = _mm(sv["h2"], dup0, d, 2 * f, s, n("up_dw"), BF16, ta=True, b_stack=2, o_stack=nst)
    dh2 = _mm(dup0, w["w_up"], s, d, 2 * f, n("up_dx"), F32, tb=True, a_stack=2, b_stack=nst)
    dx1, dsh_f, dsc_f, gs["g_pre_ffn"] = _pre_bwd(dx2, dh2, sv["x1"], row(p["g_pre_ffn"]), sc_f, n("pre_ffn_bwd"))
    token = on_grads(l, "ffn", {k: gw.pop(k) for k in ("w_up", "w_down")})
    if token is not None:
        gt_m = gt_m + token[0:1, 0:1]

    dmix, dgt_m, gs["g_post_mix"] = _post_bwd(dx1, sv["mix"], row(p["g_post_mix"]), gt_m, n("post_mix_bwd"))
    gw["w_out"] = _mm(sv["merged"], dmix, aw + uw, d, s, n("out_dw"), BF16, ta=True)
    dmerged = _mm(dmix, w["w_out"], s, aw + uw, d, n("out_dx"), F32, tb=True)
    dattn, dgl, dzd, gs["g_attn_out"], gs["g_ssm_out"] = _merge_bwd(
        dmerged, sv["attn"], sv["y"], sv["gl"], row(p["g_attn_out"]), row(p["g_ssm_out"]), n("merge_bwd"))
    gw["w_glu"] = _mm(sv["z"], dgl, uw, uw, s, n("glu_dw"), BF16, ta=True)
    dz2 = _mm(dgl, w["w_glu"], s, uw, uw, n("glu_dx"), F32, tb=True)
    dy = _gelu_bwd(dzd, dz2, sv["y"], n("gelu_bwd"))
    du, dbp, dcp, dar, dai, dd = _ssm_bwd(dy, sv["proj"], sv["xs_re"], sv["xs_im"], sv["cpt"], sv["bpt"], sv["a_re"], sv["a_im"],
                                          sv["dvec"], uw, n("ssm_bwd"))
    dq, dkv_c, dkv_p, dsinks = _attn_bwd(sv["proj"], p["attn_sinks"], sv["attn"], dattn, aw, uw, n("attn_bwd"))
    dproj = _assemble_dproj(du, dq, dkv_c, dkv_p, n("dproj"))
    gw["w_in"] = _mm(sv["h"], dproj, d, ncol, s, n("in_dw"), BF16, ta=True)
    dh = _mm(dproj, w["w_in"], s, d, ncol, n("in_dx"), F32, tb=True)
    dx0, dsh_m, dsc_m, gs["g_pre_mix"] = _pre_bwd(dx1, dh, sv["x"], row(p["g_pre_mix"]), sc_m, n("pre_mix_bwd"))
    token = on_grads(l, "mix", {k: gw.pop(k) for k in ("w_in", "w_glu", "w_out")})

    dbbr, dbbi, dc_re, dc_im = _ssm_unpieces(dbp, dcp)
    dlr, dli, dls, dbtr, dbti = _ssm_prep_bwd(*sv["zin"], dar, dai, dbbr, dbbi, n("ssm_prep_bwd"))
    g, st = p["lam_re"].shape
    gs["lam_re"] = dlr.reshape(g, st)
    gs["lam_im"] = dli.reshape(g, st)
    gs["log_step"] = jnp.sum(dls.reshape(g, st), axis=1)
    gs["ssm_b_re"] = jnp.transpose(dbtr.reshape(SSM_GROUP, g, st), (1, 2, 0))
    gs["ssm_b_im"] = jnp.transpose(dbti.reshape(SSM_GROUP, g, st), (1, 2, 0))
    gs["ssm_c_re"] = dc_re
    gs["ssm_c_im"] = dc_im
    gs["ssm_d"] = dd.reshape(p["ssm_d"].shape)
    gs["attn_sinks"] = dsinks.reshape(-1)
    gs["b_ada"] = jnp.concatenate([dsh_m, dsc_m, dgt_m, dsh_f, dsc_f, dgt_f], axis=1).reshape(-1)
    for key in ("g_post_ffn", "g_pre_ffn", "g_post_mix", "g_attn_out", "g_ssm_out", "g_pre_mix"):
        gs[key] = gs[key].reshape(-1)
    return dx0, gs, token


def _local_step(x, tgt, ada, wl, pl_small, on_grads):
    d = x.shape[1]
    depth = len(wl)
    ada6 = [[ada[l:l + 1, k * d:(k + 1) * d] for k in range(6)] for l in range(depth)]
    saved = []
    h = x
    for l in range(depth):
        h, sv = _layer_fwd(l, h, ada6[l], wl[l], pl_small[l])
        saved.append(sv)
    dy, lsum = _loss_head(h, tgt, "loss_head")
    loss = 0.5 * lsum[0, 0] / d
    gss = [None] * depth
    dx = dy
    for l in reversed(range(depth)):
        dx, gss[l], token = _layer_bwd(l, dx, ada6[l], wl[l], pl_small[l], saved[l], on_grads)
        if token is not None and l > 0:
            ada6[l - 1][5] = ada6[l - 1][5] + token[0:1, 0:1]
    return loss, dx, gss


_ANY = pl.BlockSpec(memory_space=pl.ANY)


def _mesh_pos():
    return lax.axis_index("x"), lax.axis_index("y"), lax.axis_index("c")


def _other_chips(x, y):
    return [(1 - x, y), (x, 1 - y), (1 - x, 1 - y)]


def _remote(src, dst, send_sems, recv_sems, k, to):
    return pltpu.make_async_remote_copy(src_ref=src, dst_ref=dst, send_sem=send_sems.at[k], recv_sem=recv_sems.at[k],
                                        device_id=to, device_id_type=MESH)


def _allgather8(xs, name):
    m, n = xs.shape

    def body(x_ref, out_ref, send_sems, recv_sems):
        x, y, c = _mesh_pos()
        me, sibling = (x, y, c), (x, y, 1 - c)
        chips = _other_chips(x, y)

        def rows(px, py, pc):
            return out_ref.at[pl.ds((4 * px + 2 * py + pc) * m, m), :]

        def copy(k, block, to, src=None):
            return _remote(rows(*block) if src is None else src, rows(*block), send_sems, recv_sems, k, to)

        first = [copy(0, me, sibling, src=x_ref)]
        first += [copy(1 + j, me, (*chip, c), src=x_ref) for j, chip in enumerate(chips)]
        for cp in first:
            cp.start()
        passed = [copy(4 + j, (*chip, c), sibling) for j, chip in enumerate(chips)]
        for j, chip in enumerate(chips):
            copy(1 + j, (*chip, c), me).wait_recv()
            passed[j].start()
        copy(0, sibling, me).wait_recv()
        for j, chip in enumerate(chips):
            copy(4 + j, (*chip, 1 - c), me).wait_recv()
        for cp in first + passed:
            cp.wait_send()

    out = pl.pallas_call(
        body, name=name, out_shape=jax.ShapeDtypeStruct((8 * m, n), xs.dtype), in_specs=[_ANY], out_specs=_ANY,
        scratch_shapes=[pltpu.SemaphoreType.DMA((7,)), pltpu.SemaphoreType.DMA((7,))],
    )(xs)
    x, y, c = _mesh_pos()
    return lax.dynamic_update_slice(out, xs, ((4 * x + 2 * y + c) * m, 0))


def _half_rows(ref_rows, half, align):
    h = ref_rows // 2
    return pl.ds(pl.multiple_of(half * h, align), h)


_HBM = pl.BlockSpec(memory_space=pltpu.HBM)
_SEMS = pl.BlockSpec(memory_space=pltpu.SEMAPHORE)
_EFFECT = pltpu.SideEffectType.DATAFLOW_SIDE_EFFECTING
_TOKEN = jax.ShapeDtypeStruct((8, 128), F32)


def _in_hbm(arrays):
    return [pltpu.with_memory_space_constraint(a, pltpu.HBM) for a in arrays]


def _chip_copy(kind, srcs, lands, w, q, chip, mine, c, send, recv, k):
    if kind == "gather":
        rows = _half_rows(srcs[w].shape[0], c, 16)
        return _remote(srcs[w].at[rows, :], lands[w].at[mine, rows, :], send, recv, k, (*chip, c))
    return _remote(srcs[w].at[2 * chip[0] + chip[1]], lands[w].at[mine], send, recv, k, (*chip, c))


def _chip_landing(kind, srcs, lands, w, chip, c):
    if kind == "gather":
        return lands[w].at[2 * chip[0] + chip[1], _half_rows(srcs[w].shape[0], c, 16), :]
    return lands[w].at[2 * chip[0] + chip[1]]


def _ici_start(kind, srcs, groups, name, after=None):
    nw, ng = len(srcs), len(groups)
    lands = [lax.empty((4,) + s.shape if kind == "gather" else s.shape, s.dtype) for s in srcs]
    extra = [] if after is None else [after]

    def body(*refs):
        ins, lnd = refs[:nw], refs[nw:2 * nw]
        sems = refs[2 * nw + len(extra):2 * nw + len(extra) + 2 * ng]
        token = refs[-1]
        x, y, c = _mesh_pos()
        for g, members in enumerate(groups):
            for j, w in enumerate(members):
                for q, chip in enumerate(_other_chips(x, y)):
                    _chip_copy(kind, ins, lnd, w, q, chip, 2 * x + y, c, sems[2 * g], sems[2 * g + 1], 3 * j + q).start()
        token[...] = jnp.zeros_like(token)

    sem_shapes = [pltpu.SemaphoreType.DMA((3 * len(members),)) for members in groups for _ in range(2)]
    out = pl.pallas_call(
        body, name=name,
        out_shape=(*sem_shapes, *[pltpu.HBM(s.shape, s.dtype) for s in srcs], *[pltpu.HBM(t.shape, t.dtype) for t in lands], _TOKEN),
        in_specs=[_HBM] * (2 * nw) + [_ANY] * len(extra),
        out_specs=(*([_SEMS] * (2 * ng)), *([_HBM] * (2 * nw)), pl.BlockSpec(memory_space=pltpu.VMEM)),
        input_output_aliases={i: 2 * ng + i for i in range(2 * nw)},
        compiler_params=pltpu.CompilerParams(has_side_effects=_EFFECT),
    )(*_in_hbm(srcs), *_in_hbm(lands), *extra)
    sems = [(out[2 * g], out[2 * g + 1]) for g in range(ng)]
    return sems, list(out[2 * ng:2 * ng + nw]), list(out[2 * ng + nw:2 * ng + 2 * nw]), out[-1]


def _ici_wait(kind, sems, srcs, lands, after, name):
    nm = len(srcs)

    def body(*refs):
        ins, lnd = refs[:nm], refs[nm:2 * nm]
        send, recv = refs[2 * nm], refs[2 * nm + 1]
        x, y, c = _mesh_pos()
        for j in range(nm):
            for q, chip in enumerate(_other_chips(x, y)):
                _chip_copy(kind, ins, lnd, j, q, chip, 2 * x + y, c, send, recv, 3 * j + q).wait_send()
                landed = _chip_landing(kind, ins, lnd, j, chip, c)
                _remote(landed, landed, send, recv, 3 * j + q, (x, y, c)).wait_recv()

    out = pl.pallas_call(
        body, name=name, out_shape=(*[pltpu.HBM(s.shape, s.dtype) for s in srcs], *[pltpu.HBM(t.shape, t.dtype) for t in lands]),
        in_specs=[_HBM] * (2 * nm) + [_SEMS, _SEMS, _ANY], out_specs=tuple([_HBM] * (2 * nm)),
        input_output_aliases={i: i for i in range(2 * nm)},
        compiler_params=pltpu.CompilerParams(has_side_effects=_EFFECT),
    )(*srcs, *lands, sems[0], sems[1], after)
    return list(out[:nm]), list(out[nm:])


def _gather_finish(shards, lands, name):
    nw = len(shards)

    def body(*refs):
        ins, lnd, outs = refs[:nw], refs[nw:2 * nw], refs[2 * nw:3 * nw]
        send_sems, recv_sems = refs[3 * nw:]
        x, y, c = _mesh_pos()
        mine = 2 * x + y
        sibling = (x, y, 1 - c)
        chips = _other_chips(x, y)

        def blk(ref, chip_idx, half):
            return ref.at[chip_idx, _half_rows(ref.shape[1], half, 16), :]

        sends = []
        for w in range(nw):
            for q, chip in enumerate(chips):
                k = 2 * chip[0] + chip[1]
                sends.append(_remote(blk(lnd[w], k, c), blk(outs[w], k, c), send_sems, recv_sems, 4 * w + q, sibling))
            sends.append(_remote(ins[w], outs[w].at[mine], send_sems, recv_sems, 4 * w + 3, sibling))
        for cp in sends:
            cp.start()
        for w in range(nw):
            for q, chip in enumerate(chips):
                other = blk(outs[w], 2 * chip[0] + chip[1], 1 - c)
                _remote(other, other, send_sems, recv_sems, 4 * w + q, (x, y, c)).wait_recv()
            own = outs[w].at[mine]
            _remote(own, own, send_sems, recv_sems, 4 * w + 3, (x, y, c)).wait_recv()
        for cp in sends:
            cp.wait_send()

    return pl.pallas_call(
        body, name=name, out_shape=[jax.ShapeDtypeStruct(t.shape, t.dtype) for t in lands],
        in_specs=[_ANY] * (2 * nw), out_specs=[_ANY] * nw, input_output_aliases={nw + w: w for w in range(nw)},
        scratch_shapes=[pltpu.SemaphoreType.DMA((4 * nw,)), pltpu.SemaphoreType.DMA((4 * nw,))],
    )(*shards, *lands)


def _exchange_halves(gs, name):
    nw = len(gs)

    def body(*refs):
        ins, outs = refs[:nw], refs[nw:2 * nw]
        send_sems, recv_sems = refs[2 * nw:]
        x, y, c = _mesh_pos()
        cps = []
        for w in range(nw):
            src = ins[w].at[:, _half_rows(gs[w].shape[1], 1 - c, 16), :]
            cps.append(_remote(src, outs[w], send_sems, recv_sems, w, (x, y, 1 - c)))
            cps[-1].start()
        for cp in cps:
            cp.wait_recv()
        for cp in cps:
            cp.wait_send()

    return pl.pallas_call(
        body, name=name, out_shape=[jax.ShapeDtypeStruct((4, g.shape[1] // 2, g.shape[2]), g.dtype) for g in gs],
        in_specs=[_ANY] * nw, out_specs=[_ANY] * nw,
        scratch_shapes=[pltpu.SemaphoreType.DMA((nw,)), pltpu.SemaphoreType.DMA((nw,))],
    )(*gs)


def _add_half(g, recv, cidx, name):
    _, r, c = g.shape
    h = r // 2
    tr = _block_rows(h, c)
    nbh = h // tr
    assert tr % 16 == 0

    def body(c_ref, g_ref, r_ref, o_ref):
        o_ref[...] = (g_ref[...].astype(F32) + r_ref[...].astype(F32)).astype(BF16)

    grid_spec = pltpu.PrefetchScalarGridSpec(
        num_scalar_prefetch=1, grid=(4, nbh),
        in_specs=[pl.BlockSpec((None, tr, c), lambda s, i, cr: (s, cr[0] * nbh + i, 0)),
                  pl.BlockSpec((None, tr, c), lambda s, i, cr: (s, i, 0))],
        out_specs=pl.BlockSpec((None, tr, c), lambda s, i, cr: (s, i, 0)))
    return pl.pallas_call(
        body, name=name, grid_spec=grid_spec, out_shape=jax.ShapeDtypeStruct((4, h, c), BF16),
        compiler_params=_cparams(("parallel", "parallel")),
    )(cidx, g, recv)


def _swap_with_sibling(xs, name):
    nf = len(xs)

    def body(*refs):
        ins, outs = refs[:nf], refs[nf:2 * nf]
        send_sems, recv_sems = refs[2 * nf:]
        x, y, c = _mesh_pos()
        cps = [_remote(ins[k], outs[k], send_sems, recv_sems, k, (x, y, 1 - c)) for k in range(nf)]
        for cp in cps:
            cp.start()
        for cp in cps:
            cp.wait_recv()
        for cp in cps:
            cp.wait_send()

    return pl.pallas_call(
        body, name=name, out_shape=[jax.ShapeDtypeStruct(t.shape, t.dtype) for t in xs], in_specs=[_ANY] * nf, out_specs=[_ANY] * nf,
        scratch_shapes=[pltpu.SemaphoreType.DMA((nf,)), pltpu.SemaphoreType.DMA((nf,))],
    )(*xs)


_BIG = ("w_in", "w_glu", "w_out", "w_up", "w_down")
_GROUPS = {"mix": ("w_in", "w_glu", "w_out"), "ffn": ("w_up", "w_down")}
_SMALL = ("b_ada", "g_pre_mix", "g_post_mix", "attn_sinks", "lam_re", "lam_im", "log_step", "ssm_b_re", "ssm_b_im", "ssm_c_re",
          "ssm_c_im", "ssm_d", "g_attn_out", "g_ssm_out", "g_pre_ffn", "g_post_ffn", "conv_b")
_WEIGHTS = ("w_ada", "b_ada", "g_pre_mix", "g_post_mix", "w_in", "attn_sinks", "lam_re", "lam_im", "log_step", "ssm_b_re", "ssm_b_im",
            "ssm_c_re", "ssm_c_im", "ssm_d", "w_glu", "g_attn_out", "g_ssm_out", "w_out", "g_pre_ffn", "g_post_ffn", "w_up", "conv_w",
            "conv_b", "w_down")
_LANES = 1024


def _pack(parts, rows_to):
    flat = jnp.concatenate([p.reshape(-1) for p in parts])
    per = _LANES * rows_to
    total = -(-flat.shape[0] // per) * per
    return jnp.pad(flat, (0, total - flat.shape[0])).reshape(total // _LANES, _LANES)


def _unpack(packed, shapes):
    flat = packed.reshape(-1)
    out, off = [], 0
    for shp in shapes:
        size = math.prod(shp)
        out.append(flat[off:off + size].reshape(shp))
        off += size
    return out


def kernel(x, c, w_ada, b_ada, g_pre_mix, g_post_mix, w_in, attn_sinks, lam_re, lam_im, log_step, ssm_b_re, ssm_b_im, ssm_c_re, ssm_c_im, ssm_d, w_glu, g_attn_out, g_ssm_out, w_out, g_pre_ffn, g_post_ffn, w_up, conv_w, conv_b, w_down, loss_target, m_w_ada, m_b_ada, m_g_pre_mix, m_g_post_mix, m_w_in, m_attn_sinks, m_lam_re, m_lam_im, m_log_step, m_ssm_b_re, m_ssm_b_im, m_ssm_c_re, m_ssm_c_im, m_ssm_d, m_w_glu, m_g_attn_out, m_g_ssm_out, m_w_out, m_g_pre_ffn, m_g_post_ffn, m_w_up, m_conv_w, m_conv_b, m_w_down, v_w_ada, v_b_ada, v_g_pre_mix, v_g_post_mix, v_w_in, v_attn_sinks, v_lam_re, v_lam_im, v_log_step, v_ssm_b_re, v_ssm_b_im, v_ssm_c_re, v_ssm_c_im, v_ssm_d, v_w_glu, v_g_attn_out, v_g_ssm_out, v_w_out, v_g_pre_ffn, v_g_post_ffn, v_w_up, v_conv_w, v_conv_b, v_w_down):
    given = dict(locals())
    wts = {n: given[n] for n in _WEIGHTS}
    mom = {n: given["m_" + n] for n in _WEIGHTS}
    var = {n: given["v_" + n] for n in _WEIGHTS}
    depth, d, ada_cols = w_ada.shape
    nchips = 4
    xi, yi, ci = lax.axis_index("x"), lax.axis_index("y"), lax.axis_index("c")
    chip = 2 * xi + yi
    cidx = jnp.reshape(ci, (1,)).astype(jnp.int32)

    cw_cols = conv_w.shape[2]
    vec = _pack([c, conv_w], 8)
    g1 = _allgather8(vec, "ag_cond").reshape(8, -1)
    c_all = g1[:, :d]
    cw_sh = g1[0::2, d:d + depth * 3 * cw_cols].reshape(nchips, depth, 3, cw_cols)
    conv_w_full = jnp.transpose(cw_sh, (1, 2, 0, 3)).reshape(depth, 3, nchips * cw_cols)

    ada_part, c_act = _ada_part(c_all, w_ada, "ada_part")
    g2 = _allgather8(ada_part.reshape(depth * 8, ada_cols), "ag_ada")
    ada = _ada_select(g2, b_ada, "ada_select")

    order = [(l, g) for l in range(depth) for g in _GROUPS]
    members = {key: [wts[n][key[0]].astype(BF16) for n in _GROUPS[key[1]]] for key in order}
    flat = [s for key in order for s in members[key]]
    index, at = {}, 0
    for key in order:
        index[key] = list(range(at, at + len(members[key])))
        at += len(members[key])
    ag_sems, ag_srcs, ag_lands, ag_token = _ici_start("gather", flat, [index[key] for key in order], "ag_start", after=ada)
    ada = ada + ag_token[0:1, 0:1]

    def fetch(l, g, after):
        pos, ids = order.index((l, g)), index[(l, g)]
        srcs, lands = [ag_srcs[i] for i in ids], [ag_lands[i] for i in ids]
        srcs, lands = _ici_wait("gather", ag_sems[pos], srcs, lands, ag_token if after is None else after, f"ag_wait_l{l}_{g}")
        got = dict(zip(_GROUPS[g], _gather_finish(srcs, lands, f"ag_finish_l{l}_{g}")))
        if g == "ffn":
            return dict(w_up=got["w_up"], w_down=got["w_down"].reshape(-1, got["w_down"].shape[2]))
        w_in_full = jnp.transpose(got["w_in"], (1, 0, 2)).reshape(d, -1)
        split = w_in_full.shape[1] - nchips * got["w_glu"].shape[1]
        return dict(w_in=jnp.concatenate([w_in_full[:, split:], w_in_full[:, :split]], axis=1),
                    w_glu=got["w_glu"].reshape(-1, got["w_glu"].shape[2]), w_out=got["w_out"].reshape(-1, got["w_out"].shape[2]))

    wl = [_LayerWeights(functools.partial(fetch, l)) for l in range(depth)]
    wl[0].group("mix")
    ps = []
    for l in range(depth):
        small = {n: wts[n][l] for n in _SMALL if n != "b_ada"}
        small["conv_w"] = conv_w_full[l]
        ps.append(small)

    in_flight = {}

    def on_grads(l, g, gw):
        stacks = []
        for n in _GROUPS[g]:
            t = gw[n]
            if n == "w_in":
                uw = nchips * wts["w_glu"].shape[1]
                t = jnp.concatenate([t[:, uw:], t[:, :uw]], axis=1)
                t = jnp.transpose(t.reshape(d, nchips, -1), (1, 0, 2))
            elif n != "w_up":
                t = t.reshape(nchips, t.shape[0] // nchips, t.shape[1])
            stacks.append(t)
        from_sibling = _exchange_halves(stacks, f"rs_sibling_l{l}_{g}")
        partials = [_add_half(s, r, cidx, f"rs_add_l{l}_{n}") for s, r, n in zip(stacks, from_sibling, _GROUPS[g])]
        sems, srcs, lands, token = _ici_start("scatter", partials, [list(range(len(partials)))], f"rs_start_l{l}_{g}")
        in_flight[(l, g)] = (sems[0], srcs, lands)
        return token

    loss_sum, grad_x, gss = _local_step(x[0], loss_target[0], ada, wl, ps, on_grads)
    loss = lax.psum(loss_sum, ("x", "y", "c"))

    reduced = {}
    for key in reversed(order):
        l, g = key
        sems, srcs, lands = in_flight[key]
        sent, landed = _ici_wait("scatter", sems, srcs, lands, grad_x, f"rs_wait_l{l}_{g}")
        for n, t, p in zip(_GROUPS[g], landed, sent):
            t = lax.dynamic_update_slice(t, lax.dynamic_slice_in_dim(p, chip, 1, axis=0), (chip, 0, 0))
            reduced[(n, l)] = _sum_slots(t.reshape(-1, t.shape[2]), nchips, f"rs_sum_l{l}_{n}")
    keys = [(n, l) for n in _BIG for l in range(depth)]
    theirs = dict(zip(keys, _swap_with_sibling([reduced[k] for k in keys], "rs_share")))
    big_grads = {}
    for n in _BIG:
        own = jnp.stack([reduced[(n, l)] for l in range(depth)])
        oth = jnp.stack([theirs[(n, l)] for l in range(depth)])
        big_grads[n] = jnp.where(ci == 0, jnp.concatenate([own, oth], axis=1), jnp.concatenate([oth, own], axis=1))

    small_parts = [jnp.stack([gss[l][n] for l in range(depth)]) for n in _SMALL]
    pack_small = _pack(small_parts, 8)
    pack_cw = _pack([jnp.stack([gss[l]["conv_w"] for l in range(depth)])], 8)
    rows_small = pack_small.shape[0]
    mine = jnp.concatenate([pack_small, pack_cw], axis=0)
    g3 = _allgather8(mine, "ag_small")
    total = _sum_slots(g3, 8, "sum_small")
    grads = dict(big_grads)
    for n, v in zip(_SMALL, _unpack(total[:rows_small], [wts[n].shape for n in _SMALL])):
        grads[n] = v
    conv_w_grad = _unpack(total[rows_small:], [(depth, 3, nchips * cw_cols)])[0]
    grads["conv_w"] = lax.dynamic_slice_in_dim(conv_w_grad, chip * cw_cols, cw_cols, axis=2)

    ada_rows = depth * 6 * d // _LANES
    d_ada_all = g3.reshape(8, -1, _LANES)[:, :ada_rows].reshape(8, depth, 6 * d)
    d_sel = lax.dynamic_slice_in_dim(jnp.transpose(d_ada_all, (1, 0, 2)), chip * ada_cols, ada_cols, axis=2)
    grads["w_ada"] = _wada_grad(jnp.transpose(c_act), d_sel, "w_ada_grad")

    delta, new_m, new_v = {}, {}, {}
    for n in _WEIGHTS:
        shp = wts[n].shape
        view = (lambda t: t) if len(shp) == 3 else (lambda t: t.reshape(1, -1, shp[-1]))
        outs = _adamw(view(wts[n]), view(grads[n]), view(mom[n]), view(var[n]), f"adamw_{n}")
        delta[n], new_m[n], new_v[n] = [t.reshape(shp) for t in outs]

    return (loss, grad_x[None], *[grads[n] for n in _WEIGHTS], *[delta[n] for n in _WEIGHTS],
            *[new_m[n] for n in _WEIGHTS], *[new_v[n] for n in _WEIGHTS])
```

```python
import functools
import math

import jax
import jax.numpy as jnp
from jax import lax
from jax.experimental import pallas as pl
from jax.experimental.pallas import tpu as pltpu

F32 = jnp.float32
BF16 = jnp.bfloat16
EPS = 1e-6
NEG = -1e30
WINDOW = 128
HEAD_DIM = 64
KV_RATIO = 8
SSM_GROUP = 16
STATE = 64
PIECE = 128
PSTATES = 512
DEPTH = 2
ADAM_LR, ADAM_B1, ADAM_B2, ADAM_EPS, ADAM_WD, ADAM_STEP = 0.001, 0.9, 0.999, 1e-08, 0.01, 10
MIB = 1024 * 1024
_MATMUL_VMEM_BUDGET = 40 * MIB
MESH = pl.DeviceIdType.MESH


def _cparams(sem=None, vmem_mib=48):
    return pltpu.CompilerParams(dimension_semantics=sem, vmem_limit_bytes=vmem_mib * MIB)


def _gelu(x):
    c = math.sqrt(2.0 / math.pi)
    return 0.5 * x * (1.0 + jnp.tanh(c * (x + 0.044715 * (x * x * x))))


def _gelu_and_grad(x):
    c = math.sqrt(2.0 / math.pi)
    x2 = x * x
    t = jnp.tanh(c * (x + 0.044715 * (x2 * x)))
    half = 0.5 * (1.0 + t)
    return x * half, half + 0.5 * x * (1.0 - t * t) * c * (1.0 + 3.0 * 0.044715 * x2)


def _gelu_grad(x):
    return _gelu_and_grad(x)[1]


def _sigmoid(x):
    return 1.0 / (1.0 + jnp.exp(-x))


def _matmul(a, b, *, m, n, k, tm, tn, tk, out_dtype, name, ta=False, tb=False, a_stack=0, b_stack=0, o_stack=0):
    assert m % tm == 0 and n % tn == 0 and k % tk == 0, (name, m, n, k, tm, tn, tk)
    nk = k // tk

    if a_stack:
        assert not ta and (k // a_stack) % tk == 0
        per = (k // a_stack) // tk
        a_spec = pl.BlockSpec((None, tm, tk), lambda i, j, kk: (kk // per, i, kk % per))
    elif ta:
        a_spec = pl.BlockSpec((tk, tm), lambda i, j, kk: (kk, i))
    else:
        a_spec = pl.BlockSpec((tm, tk), lambda i, j, kk: (i, kk))
    if b_stack and tb:
        perb = (k // b_stack) // tk
        b_spec = pl.BlockSpec((None, tn, tk), lambda i, j, kk: (kk // perb, j, kk % perb))
    elif b_stack:
        perb = (n // b_stack) // tn
        b_spec = pl.BlockSpec((None, tk, tn), lambda i, j, kk: (j // perb, kk, j % perb))
    elif tb:
        b_spec = pl.BlockSpec((tn, tk), lambda i, j, kk: (j, kk))
    else:
        b_spec = pl.BlockSpec((tk, tn), lambda i, j, kk: (kk, j))
    if o_stack:
        pero = (n // o_stack) // tn
        o_spec = pl.BlockSpec((None, tm, tn), lambda i, j, kk: (j // pero, i, j % pero))
        o_shape = jax.ShapeDtypeStruct((o_stack, m, n // o_stack), out_dtype)
    else:
        o_spec = pl.BlockSpec((tm, tn), lambda i, j, kk: (i, j))
        o_shape = jax.ShapeDtypeStruct((m, n), out_dtype)
    dims = (((0 if ta else 1,), (1 if tb else 0,)), ((), ()))

    def body(a_ref, b_ref, o_ref, *acc):
        p = lax.dot_general(a_ref[...].astype(BF16), b_ref[...].astype(BF16), dims, preferred_element_type=F32)
        if nk == 1:
            o_ref[...] = p.astype(o_ref.dtype)
        else:
            acc_ref = acc[0]
            kk = pl.program_id(2)

            @pl.when(kk == 0)
            def _():
                acc_ref[...] = p

            @pl.when(kk > 0)
            def _():
                acc_ref[...] += p

            @pl.when(kk == nk - 1)
            def _():
                o_ref[...] = acc_ref[...].astype(o_ref.dtype)

    return pl.pallas_call(
        body, name=name, grid=(m // tm, n // tn, nk), in_specs=[a_spec, b_spec], out_specs=o_spec, out_shape=o_shape,
        scratch_shapes=[] if nk == 1 else [pltpu.VMEM((tm, tn), F32)],
        compiler_params=_cparams(("parallel", "parallel", "arbitrary"), 56),
    )(a, b)


def _row(d):
    return pl.BlockSpec((1, d), lambda i: (0, 0))


def _tok(tm, d):
    return pl.BlockSpec((tm, d), lambda i: (i, 0))


def _pre_fwd(x, g, sc, sh, name):
    s, d = x.shape
    tm = min(256, s)

    def body(x_ref, g_ref, sc_ref, sh_ref, h_ref):
        xv = x_ref[...]
        r = lax.rsqrt(jnp.mean(xv * xv, axis=-1, keepdims=True) + EPS)
        h_ref[...] = (((xv * r) * g_ref[...]) * (1.0 + sc_ref[...]) + sh_ref[...]).astype(BF16)

    return pl.pallas_call(
        body, name=name, grid=(s // tm,), in_specs=[_tok(tm, d), _row(d), _row(d), _row(d)], out_specs=_tok(tm, d),
        out_shape=jax.ShapeDtypeStruct((s, d), BF16), compiler_params=_cparams(("parallel",)),
    )(x, g, sc, sh)


def _post_fwd(x, o, g, gt, name):
    s, d = x.shape
    tm = min(256, s)

    def body(x_ref, o_ref, g_ref, gt_ref, y_ref):
        ov = o_ref[...]
        r = lax.rsqrt(jnp.mean(ov * ov, axis=-1, keepdims=True) + EPS)
        y_ref[...] = x_ref[...] + (1.0 + gt_ref[...]) * ((ov * r) * g_ref[...])

    return pl.pallas_call(
        body, name=name, grid=(s // tm,), in_specs=[_tok(tm, d), _tok(tm, d), _row(d), _row(d)], out_specs=_tok(tm, d),
        out_shape=jax.ShapeDtypeStruct((s, d), F32), compiler_params=_cparams(("parallel",)),
    )(x, o, g, gt)


def _post_bwd(dxo, o, g, gt, name):
    s, d = o.shape
    tm = min(256, s)

    def body(dx_ref, o_ref, g_ref, gt_ref, do_ref, dgt_ref, dg_ref):
        i = pl.program_id(0)
        dx = dx_ref[...]
        ov = o_ref[...]
        gv = g_ref[...]
        r = lax.rsqrt(jnp.mean(ov * ov, axis=-1, keepdims=True) + EPS)
        oh = ov * r
        dn = dx * (1.0 + gt_ref[...])
        e = dn * gv
        do_ref[...] = (r * (e - oh * jnp.mean(e * oh, axis=-1, keepdims=True))).astype(BF16)
        p_gt = jnp.sum(dx * (oh * gv), axis=0, keepdims=True)
        p_g = jnp.sum(dn * oh, axis=0, keepdims=True)

        @pl.when(i == 0)
        def _():
            dgt_ref[...] = p_gt
            dg_ref[...] = p_g

        @pl.when(i > 0)
        def _():
            dgt_ref[...] += p_gt
            dg_ref[...] += p_g

    row = jax.ShapeDtypeStruct((1, d), F32)
    return pl.pallas_call(
        body, name=name, grid=(s // tm,), in_specs=[_tok(tm, d), _tok(tm, d), _row(d), _row(d)],
        out_specs=(_tok(tm, d), _row(d), _row(d)), out_shape=(jax.ShapeDtypeStruct((s, d), BF16), row, row),
        compiler_params=_cparams(("arbitrary",)),
    )(dxo, o, g, gt)


def _pre_bwd(dres, dh, x, g, sc, name):
    s, d = x.shape
    tm = min(256, s)

    def body(dres_ref, dh_ref, x_ref, g_ref, sc_ref, dx_ref, dsh_ref, dsc_ref, dg_ref):
        i = pl.program_id(0)
        dh_v = dh_ref[...]
        xv = x_ref[...]
        gv = g_ref[...]
        one_sc = 1.0 + sc_ref[...]
        r = lax.rsqrt(jnp.mean(xv * xv, axis=-1, keepdims=True) + EPS)
        xh = xv * r
        e = dh_v * one_sc * gv
        dx_ref[...] = dres_ref[...] + r * (e - xh * jnp.mean(e * xh, axis=-1, keepdims=True))
        p_sh = jnp.sum(dh_v, axis=0, keepdims=True)
        p_sc = jnp.sum(dh_v * (xh * gv), axis=0, keepdims=True)
        p_g = jnp.sum(dh_v * one_sc * xh, axis=0, keepdims=True)

        @pl.when(i == 0)
        def _():
            dsh_ref[...] = p_sh
            dsc_ref[...] = p_sc
            dg_ref[...] = p_g

        @pl.when(i > 0)
        def _():
            dsh_ref[...] += p_sh
            dsc_ref[...] += p_sc
            dg_ref[...] += p_g

    row = jax.ShapeDtypeStruct((1, d), F32)
    return pl.pallas_call(
        body, name=name, grid=(s // tm,), in_specs=[_tok(tm, d), _tok(tm, d), _tok(tm, d), _row(d), _row(d)],
        out_specs=(_tok(tm, d), _row(d), _row(d), _row(d)), out_shape=(jax.ShapeDtypeStruct((s, d), F32), row, row, row),
        compiler_params=_cparams(("arbitrary",)),
    )(dres, dh, x, g, sc)


def _loss_head(y, tgt, name):
    s, d = y.shape
    tm = min(256, s)

    def body(y_ref, t_ref, dy_ref, l_ref):
        i = pl.program_id(0)
        err = y_ref[...] - t_ref[...]
        dy_ref[...] = err * (1.0 / d)
        part = jnp.zeros((1, 128), F32) + jnp.sum(err * err)

        @pl.when(i == 0)
        def _():
            l_ref[...] = part

        @pl.when(i > 0)
        def _():
            l_ref[...] += part

    return pl.pallas_call(
        body, name=name, grid=(s // tm,), in_specs=[_tok(tm, d), _tok(tm, d)],
        out_specs=(_tok(tm, d), pl.BlockSpec((1, 128), lambda i: (0, 0))),
        out_shape=(jax.ShapeDtypeStruct((s, d), F32), jax.ShapeDtypeStruct((1, 128), F32)),
        compiler_params=_cparams(("arbitrary",)),
    )(y, tgt)


def _gelu_fwd(y, name):
    s, u = y.shape
    tm = min(512, s)

    def body(y_ref, z_ref):
        z_ref[...] = _gelu(y_ref[...]).astype(BF16)

    return pl.pallas_call(
        body, name=name, grid=(s // tm,), in_specs=[_tok(tm, u)], out_specs=_tok(tm, u),
        out_shape=jax.ShapeDtypeStruct((s, u), BF16), compiler_params=_cparams(("parallel",)),
    )(y)


def _merge_fwd(attn, y, gl, ga, gs, name):
    s, aw = attn.shape
    uw = y.shape[1]
    tm = min(256, s)

    def body(a_ref, y_ref, gl_ref, ga_ref, gs_ref, m_ref):
        av = a_ref[...]
        ra = lax.rsqrt(jnp.mean(av * av, axis=-1, keepdims=True) + EPS)
        m_ref[:, :aw] = ((av * ra) * ga_ref[...]).astype(BF16)
        ssm = _gelu(y_ref[...]) * _sigmoid(gl_ref[...])
        rs = lax.rsqrt(jnp.mean(ssm * ssm, axis=-1, keepdims=True) + EPS)
        m_ref[:, aw:] = ((ssm * rs) * gs_ref[...]).astype(BF16)

    return pl.pallas_call(
        body, name=name, grid=(s // tm,), in_specs=[_tok(tm, aw), _tok(tm, uw), _tok(tm, uw), _row(aw), _row(uw)],
        out_specs=_tok(tm, aw + uw), out_shape=jax.ShapeDtypeStruct((s, aw + uw), BF16),
        compiler_params=_cparams(("parallel",)),
    )(attn, y, gl, ga, gs)


def _merge_bwd(dm, attn, y, gl, ga, gs, name):
    s, aw = attn.shape
    uw = y.shape[1]
    tm = min(256, s)

    def body(dm_ref, a_ref, y_ref, gl_ref, ga_ref, gs_ref, da_ref, dgl_ref, dz_ref, dga_ref, dgs_ref):
        i = pl.program_id(0)
        av = a_ref[...]
        dma = dm_ref[:, :aw]
        ra = lax.rsqrt(jnp.mean(av * av, axis=-1, keepdims=True) + EPS)
        ah = av * ra
        e = dma * ga_ref[...]
        da_ref[...] = (ra * (e - ah * jnp.mean(e * ah, axis=-1, keepdims=True))).astype(BF16)
        p_ga = jnp.sum(dma * ah, axis=0, keepdims=True)

        z = _gelu(y_ref[...])
        sig = _sigmoid(gl_ref[...])
        ssm = z * sig
        dms = dm_ref[:, aw:]
        rs = lax.rsqrt(jnp.mean(ssm * ssm, axis=-1, keepdims=True) + EPS)
        sh = ssm * rs
        e2 = dms * gs_ref[...]
        dssm = rs * (e2 - sh * jnp.mean(e2 * sh, axis=-1, keepdims=True))
        dz_ref[...] = dssm * sig
        dgl_ref[...] = (dssm * z * sig * (1.0 - sig)).astype(BF16)
        p_gs = jnp.sum(dms * sh, axis=0, keepdims=True)

        @pl.when(i == 0)
        def _():
            dga_ref[...] = p_ga
            dgs_ref[...] = p_gs

        @pl.when(i > 0)
        def _():
            dga_ref[...] += p_ga
            dgs_ref[...] += p_gs

    return pl.pallas_call(
        body, name=name, grid=(s // tm,),
        in_specs=[_tok(tm, aw + uw), _tok(tm, aw), _tok(tm, uw), _tok(tm, uw), _row(aw), _row(uw)],
        out_specs=(_tok(tm, aw), _tok(tm, uw), _tok(tm, uw), _row(aw), _row(uw)),
        out_shape=(jax.ShapeDtypeStruct((s, aw), BF16), jax.ShapeDtypeStruct((s, uw), BF16), jax.ShapeDtypeStruct((s, uw), F32),
                   jax.ShapeDtypeStruct((1, aw), F32), jax.ShapeDtypeStruct((1, uw), F32)),
        compiler_params=_cparams(("arbitrary",)),
    )(dm, attn, y, gl, ga, gs)


def _gelu_bwd(dzd, dz2, y, name):
    s, u = y.shape
    tm = min(512, s)

    def body(a_ref, b_ref, y_ref, o_ref):
        o_ref[...] = (a_ref[...] + b_ref[...]) * _gelu_grad(y_ref[...])

    return pl.pallas_call(
        body, name=name, grid=(s // tm,), in_specs=[_tok(tm, u), _tok(tm, u), _tok(tm, u)], out_specs=_tok(tm, u),
        out_shape=jax.ShapeDtypeStruct((s, u), F32), compiler_params=_cparams(("parallel",)),
    )(dzd, dz2, y)


def _attn_scores(qh, kb, sink, valid):
    s = lax.dot_general(qh, kb, (((1,), (1,)), ((), ())), preferred_element_type=F32) * (HEAD_DIM ** -0.5)
    s = jnp.where(valid, s, NEG)
    m = jnp.maximum(jnp.max(s, axis=-1, keepdims=True), sink)
    e = jnp.exp(s - m)
    esink = jnp.exp(sink - m)
    den = jnp.sum(e, axis=-1, keepdims=True) + esink
    return e / den, esink / den


def _attn_valid(i):
    qi = lax.broadcasted_iota(jnp.int32, (KV_RATIO * WINDOW, 2 * WINDOW), 0) % WINDOW
    kj = lax.broadcasted_iota(jnp.int32, (KV_RATIO * WINDOW, 2 * WINDOW), 1)
    return (kj > qi) & (kj <= qi + WINDOW) & ((kj >= WINDOW) | (i > 0))


def _stack_heads(ref, hk):
    return jnp.concatenate([ref[:, (hk * KV_RATIO + g) * HEAD_DIM:(hk * KV_RATIO + g + 1) * HEAD_DIM] for g in range(KV_RATIO)], axis=0)


def _stack_sinks(sink_ref, hk):
    return jnp.concatenate([jnp.full((WINDOW, 1), sink_ref[hk * KV_RATIO + g], F32) for g in range(KV_RATIO)], axis=0)


def _band(kvp, kvc, off):
    return jnp.concatenate([kvp[:, off:off + HEAD_DIM], kvc[:, off:off + HEAD_DIM]], axis=0)


def _attn_specs(aw, uw, kvw):
    qblk = uw // aw
    kvblk = (uw + aw) // (2 * kvw)
    assert uw % aw == 0 and (uw + aw) % (2 * kvw) == 0
    return [
        pl.BlockSpec(memory_space=pltpu.SMEM),
        pl.BlockSpec((WINDOW, aw), lambda i: (i, qblk)),
        pl.BlockSpec((WINDOW, 2 * kvw), lambda i: (i, kvblk)),
        pl.BlockSpec((WINDOW, 2 * kvw), lambda i: (jnp.maximum(i - 1, 0), kvblk)),
    ]


def _attn_fwd(proj, sinks, aw, uw, name):
    s = proj.shape[0]
    nq = aw // HEAD_DIM
    nkv = nq // KV_RATIO
    kvw = nkv * HEAD_DIM

    def body(sink_ref, q_ref, kvc_ref, kvp_ref, o_ref):
        valid = _attn_valid(pl.program_id(0))[:WINDOW]
        q = q_ref[...]
        kvc = kvc_ref[...]
        kvp = kvp_ref[...]
        for hk in range(nkv):
            kb = _band(kvp, kvc, hk * HEAD_DIM)
            vb = _band(kvp, kvc, kvw + hk * HEAD_DIM)
            for g in range(KV_RATIO):
                hq = hk * KV_RATIO + g
                p, _ = _attn_scores(q[:, hq * HEAD_DIM:(hq + 1) * HEAD_DIM], kb, sink_ref[hq], valid)
                o_ref[:, hq * HEAD_DIM:(hq + 1) * HEAD_DIM] = jnp.dot(p.astype(BF16), vb, preferred_element_type=F32)

    return pl.pallas_call(
        body, name=name, grid=(s // WINDOW,), in_specs=_attn_specs(aw, uw, kvw),
        out_specs=pl.BlockSpec((WINDOW, aw), lambda i: (i, 0)), out_shape=jax.ShapeDtypeStruct((s, aw), F32),
        compiler_params=_cparams(("parallel",)),
    )(sinks, proj, proj, proj)


def _attn_bwd(proj, sinks, attn, dattn, aw, uw, name):
    s = proj.shape[0]
    nq = aw // HEAD_DIM
    nkv = nq // KV_RATIO
    kvw = nkv * HEAD_DIM
    hd = HEAD_DIM

    def body(sink_ref, q_ref, kvc_ref, kvp_ref, o_ref, do_ref, dq_ref, dc_ref, dp_ref, ds_ref):
        i = pl.program_id(0)
        valid = _attn_valid(i)
        kvc = kvc_ref[...]
        kvp = kvp_ref[...]
        lane = lax.broadcasted_iota(jnp.int32, (1, nq), 1)
        dsink = jnp.zeros((1, nq), F32)
        for hk in range(nkv):
            kb = _band(kvp, kvc, hk * hd)
            vb = _band(kvp, kvc, kvw + hk * hd)
            qs = _stack_heads(q_ref, hk)
            dos = _stack_heads(do_ref, hk)
            p, psink = _attn_scores(qs, kb, _stack_sinks(sink_ref, hk), valid)
            delta = jnp.sum(dos.astype(F32) * _stack_heads(o_ref, hk), axis=-1, keepdims=True)
            dpv = lax.dot_general(dos, vb, (((1,), (1,)), ((), ())), preferred_element_type=F32)
            dsb = (p * (dpv - delta) * (hd ** -0.5)).astype(BF16)
            dqs = jnp.dot(dsb, kb, preferred_element_type=F32).astype(BF16)
            dkb = lax.dot_general(dsb, qs, (((0,), (0,)), ((), ())), preferred_element_type=F32)
            dvb = lax.dot_general(p.astype(BF16), dos, (((0,), (0,)), ((), ())), preferred_element_type=F32)
            sink_term = psink * delta
            for g in range(KV_RATIO):
                hq = hk * KV_RATIO + g
                dq_ref[:, hq * hd:(hq + 1) * hd] = dqs[g * WINDOW:(g + 1) * WINDOW]
                dsink = dsink + jnp.where(lane == hq, -jnp.sum(sink_term[g * WINDOW:(g + 1) * WINDOW]), 0.0)
            dp_ref[:, hk * hd:(hk + 1) * hd] = dkb[:WINDOW]
            dc_ref[:, hk * hd:(hk + 1) * hd] = dkb[WINDOW:]
            dp_ref[:, kvw + hk * hd:kvw + (hk + 1) * hd] = dvb[:WINDOW]
            dc_ref[:, kvw + hk * hd:kvw + (hk + 1) * hd] = dvb[WINDOW:]

        @pl.when(i == 0)
        def _():
            ds_ref[...] = dsink

        @pl.when(i > 0)
        def _():
            ds_ref[...] += dsink

    blk_a = pl.BlockSpec((WINDOW, aw), lambda i: (i, 0))
    blk_kv = pl.BlockSpec((WINDOW, 2 * kvw), lambda i: (i, 0))
    return pl.pallas_call(
        body, name=name, grid=(s // WINDOW,), in_specs=_attn_specs(aw, uw, kvw) + [blk_a, blk_a],
        out_specs=(blk_a, blk_kv, blk_kv, pl.BlockSpec((1, nq), lambda i: (0, 0))),
        out_shape=(jax.ShapeDtypeStruct((s, aw), BF16), jax.ShapeDtypeStruct((s, 2 * kvw), F32),
                   jax.ShapeDtypeStruct((s, 2 * kvw), F32), jax.ShapeDtypeStruct((1, nq), F32)),
        compiler_params=_cparams(("arbitrary",)),
    )(sinks, proj, proj, proj, attn, dattn)


def _assemble_dproj(du, dq, dkv_cur, dkv_prev, name):
    s, uw = du.shape
    aw = dq.shape[1]
    kv2 = dkv_cur.shape[1]
    nb = s // WINDOW

    def body(du_ref, dq_ref, dc_ref, dp_ref, o_ref):
        i = pl.program_id(0)
        o_ref[:, :uw] = du_ref[...].astype(BF16)
        o_ref[:, uw:uw + aw] = dq_ref[...]
        nxt = jnp.where(i < nb - 1, 1.0, 0.0)
        o_ref[:, uw + aw:] = (dc_ref[...] + nxt * dp_ref[...]).astype(BF16)

    return pl.pallas_call(
        body, name=name, grid=(nb,),
        in_specs=[_tok(WINDOW, uw), _tok(WINDOW, aw), _tok(WINDOW, kv2),
                  pl.BlockSpec((WINDOW, kv2), lambda i: (jnp.minimum(i + 1, nb - 1), 0))],
        out_specs=_tok(WINDOW, uw + aw + kv2), out_shape=jax.ShapeDtypeStruct((s, uw + aw + kv2), BF16),
        compiler_params=_cparams(("parallel",)),
    )(du, dq, dkv_cur, dkv_prev)


def _zoh(lr, li, ls, btr, bti):
    dt = jnp.exp(ls)
    mag = jnp.exp(lr * dt)
    ang = li * dt
    ar = mag * jnp.cos(ang)
    ai = mag * jnp.sin(ang)
    den = lr * lr + li * li
    fr = ((ar - 1.0) * lr + ai * li) / den
    fi = (ai * lr - (ar - 1.0) * li) / den
    return ar, ai, fr[None] * btr - fi[None] * bti, fr[None] * bti + fi[None] * btr


def _ssm_prep(lr, li, ls, btr, bti, name):
    def body(lr_ref, li_ref, ls_ref, btr_ref, bti_ref, ar_ref, ai_ref, bbr_ref, bbi_ref):
        ar, ai, bbr, bbi = _zoh(lr_ref[...], li_ref[...], ls_ref[...], btr_ref[...], bti_ref[...])
        ar_ref[...] = ar
        ai_ref[...] = ai
        bbr_ref[...] = bbr
        bbi_ref[...] = bbi

    s2 = jax.ShapeDtypeStruct(lr.shape, F32)
    s3 = jax.ShapeDtypeStruct(btr.shape, F32)
    return pl.pallas_call(body, name=name, out_shape=(s2, s2, s3, s3))(lr, li, ls, btr, bti)


def _ssm_prep_bwd(lr, li, ls, btr, bti, dar, dai, dbbr, dbbi, name):
    def body(lr_ref, li_ref, ls_ref, btr_ref, bti_ref, dar_ref, dai_ref, dbbr_ref, dbbi_ref, o1, o2, o3, o4, o5):
        _, vjp = jax.vjp(_zoh, lr_ref[...], li_ref[...], ls_ref[...], btr_ref[...], bti_ref[...])
        g = vjp((dar_ref[...], dai_ref[...], dbbr_ref[...], dbbi_ref[...]))
        for o, v in zip((o1, o2, o3, o4, o5), g):
            o[...] = v

    s2 = jax.ShapeDtypeStruct(lr.shape, F32)
    s3 = jax.ShapeDtypeStruct(btr.shape, F32)
    return pl.pallas_call(body, name=name, out_shape=(s2, s2, s2, s3, s3))(lr, li, ls, btr, bti, dar, dai, dbbr, dbbi)


def _state_tiles(ref):
    return [ref[:, cb * 128:(cb + 1) * 128] for cb in range(4)]


def _gather_rows(ref_re, ref_im, r, t):
    return jnp.concatenate([ref_re.at[cb][pl.ds(r, t, stride=8), :] for cb in range(4)]
                           + [ref_im.at[cb][pl.ds(r, t, stride=8), :] for cb in range(4)], axis=1)


def _scatter_rows(ref_re, ref_im, r, t, val):
    for cb in range(4):
        ref_re.at[cb][pl.ds(r, t, stride=8), :] = val[:, cb * 128:(cb + 1) * 128]
        ref_im.at[cb][pl.ds(r, t, stride=8), :] = val[:, PSTATES + cb * 128:PSTATES + (cb + 1) * 128]


def _ssm_fwd(proj, bp, cp, a_re, a_im, dvec, uw, name, t=128):
    s = proj.shape[0]
    npc = uw // PIECE
    assert npc == 8 and s % t == 0

    def body(u_ref, bp_ref, cp_ref, ar_ref, ai_ref, d_ref, y_ref, xr_ref, xi_ref, cr_ref, ci_ref):
        i = pl.program_id(0)

        @pl.when(i == 0)
        def _():
            cr_ref[...] = jnp.zeros_like(cr_ref)
            ci_ref[...] = jnp.zeros_like(ci_ref)

        for r in range(npc):
            bu = jnp.dot(u_ref[:, r * PIECE:(r + 1) * PIECE], bp_ref[r], preferred_element_type=F32)
            _scatter_rows(xr_ref, xi_ref, r, t, bu)
        ar = _state_tiles(ar_ref)
        ai = _state_tiles(ai_ref)

        def step(tt, carry):
            xr, xi = carry
            off = pl.multiple_of(tt * 8, 8)
            nr, ni = [], []
            for cb in range(4):
                vr = ar[cb] * xr[cb] - ai[cb] * xi[cb] + xr_ref[cb, pl.ds(off, 8), :]
                vi = ar[cb] * xi[cb] + ai[cb] * xr[cb] + xi_ref[cb, pl.ds(off, 8), :]
                xr_ref[cb, pl.ds(off, 8), :] = vr
                xi_ref[cb, pl.ds(off, 8), :] = vi
                nr.append(vr)
                ni.append(vi)
            return tuple(nr), tuple(ni)

        xr, xi = lax.fori_loop(0, t, step, (tuple(_state_tiles(cr_ref)), tuple(_state_tiles(ci_ref))), unroll=4)
        for cb in range(4):
            cr_ref[:, cb * 128:(cb + 1) * 128] = xr[cb]
            ci_ref[:, cb * 128:(cb + 1) * 128] = xi[cb]
        for r in range(npc):
            xs = _gather_rows(xr_ref, xi_ref, r, t).astype(BF16)
            y_ref[:, r * PIECE:(r + 1) * PIECE] = (
                jnp.dot(xs, cp_ref[r], preferred_element_type=F32)
                + d_ref[:, r * PIECE:(r + 1) * PIECE] * u_ref[:, r * PIECE:(r + 1) * PIECE].astype(F32))

    full3 = lambda shp: pl.BlockSpec(shp, lambda i: (0, 0, 0))
    full2 = lambda shp: pl.BlockSpec(shp, lambda i: (0, 0))
    xs_spec = pl.BlockSpec((4, t * 8, 128), lambda i: (0, i, 0))
    xs_shape = jax.ShapeDtypeStruct((4, s * 8, 128), F32)
    return pl.pallas_call(
        body, name=name, grid=(s // t,),
        in_specs=[pl.BlockSpec((t, uw), lambda i: (i, 0)), full3(bp.shape), full3(cp.shape), full2(a_re.shape), full2(a_im.shape),
                  full2(dvec.shape)],
        out_specs=(pl.BlockSpec((t, uw), lambda i: (i, 0)), xs_spec, xs_spec),
        out_shape=(jax.ShapeDtypeStruct((s, uw), F32), xs_shape, xs_shape),
        scratch_shapes=[pltpu.VMEM((8, PSTATES), F32), pltpu.VMEM((8, PSTATES), F32)],
        compiler_params=_cparams(("arbitrary",), 56),
    )(proj, bp, cp, a_re, a_im, dvec)


def _ssm_bwd(dy, proj, xs_re, xs_im, cpt, bpt, a_re, a_im, dvec, uw, name, t=128):
    s = proj.shape[0]
    npc = uw // PIECE
    nt = s // t
    assert npc == 8 and s % t == 0

    def body(dy_ref, u_ref, xr_ref, xi_ref, hr_ref, hi_ref, cpt_ref, bpt_ref, ar_ref, ai_ref, d_ref,
             du_ref, dbp_ref, dcp_ref, dar_ref, dai_ref, dd_ref, gr_ref, gi_ref, lr_ref, li_ref):
        i = pl.program_id(0)

        @pl.when(i == 0)
        def _():
            lr_ref[...] = jnp.zeros_like(lr_ref)
            li_ref[...] = jnp.zeros_like(li_ref)
            dbp_ref[...] = jnp.zeros_like(dbp_ref)
            dcp_ref[...] = jnp.zeros_like(dcp_ref)
            dar_ref[...] = jnp.zeros_like(dar_ref)
            dai_ref[...] = jnp.zeros_like(dai_ref)
            dd_ref[...] = jnp.zeros_like(dd_ref)

        dyb = dy_ref[...].astype(BF16)
        for r in range(npc):
            gx = jnp.dot(dyb[:, r * PIECE:(r + 1) * PIECE], cpt_ref[r], preferred_element_type=F32)
            _scatter_rows(gr_ref, gi_ref, r, t, gx)
        ar = _state_tiles(ar_ref)
        ai = _state_tiles(ai_ref)

        def adjoint(off, lam_r, lam_i, xpr, xpi, acc_r, acc_i):
            nr, ni, qr, qi = [], [], [], []
            for cb in range(4):
                vr = gr_ref[cb, pl.ds(off, 8), :] + ar[cb] * lam_r[cb] + ai[cb] * lam_i[cb]
                vi = gi_ref[cb, pl.ds(off, 8), :] + ar[cb] * lam_i[cb] - ai[cb] * lam_r[cb]
                gr_ref[cb, pl.ds(off, 8), :] = vr
                gi_ref[cb, pl.ds(off, 8), :] = vi
                nr.append(vr)
                ni.append(vi)
                qr.append(acc_r[cb] + vr * xpr[cb] + vi * xpi[cb])
                qi.append(acc_i[cb] + vi * xpr[cb] - vr * xpi[cb])
            return tuple(nr), tuple(ni), tuple(qr), tuple(qi)

        def step(j, carry):
            lam_r, lam_i, acc_r, acc_i = carry
            tt = t - 1 - j
            off = pl.multiple_of(tt * 8, 8)
            offp = pl.multiple_of(tt * 8 - 8, 8)
            xpr = [xr_ref[cb, pl.ds(offp, 8), :] for cb in range(4)]
            xpi = [xi_ref[cb, pl.ds(offp, 8), :] for cb in range(4)]
            return adjoint(off, lam_r, lam_i, xpr, xpi, acc_r, acc_i)

        zero4 = tuple(jnp.zeros((8, 128), F32) for _ in range(4))
        carry = lax.fori_loop(0, t - 1, step, (tuple(_state_tiles(lr_ref)), tuple(_state_tiles(li_ref)), zero4, zero4), unroll=4)
        has_prev = jnp.where(i < nt - 1, 1.0, 0.0)
        xpr = [hr_ref[cb] * has_prev for cb in range(4)]
        xpi = [hi_ref[cb] * has_prev for cb in range(4)]
        lam_r, lam_i, acc_r, acc_i = adjoint(0, carry[0], carry[1], xpr, xpi, carry[2], carry[3])
        for cb in range(4):
            lr_ref[:, cb * 128:(cb + 1) * 128] = lam_r[cb]
            li_ref[:, cb * 128:(cb + 1) * 128] = lam_i[cb]
            dar_ref[:, cb * 128:(cb + 1) * 128] += acc_r[cb]
            dai_ref[:, cb * 128:(cb + 1) * 128] += acc_i[cb]

        dyv = dy_ref[...]
        uv = u_ref[...]
        dd_ref[...] += jnp.sum(dyv * uv.astype(F32), axis=0, keepdims=True)
        for r in range(npc):
            lam = _gather_rows(gr_ref, gi_ref, r, t).astype(BF16)
            sl = slice(r * PIECE, (r + 1) * PIECE)
            du_ref[:, sl] = jnp.dot(lam, bpt_ref[r], preferred_element_type=F32) + d_ref[:, sl] * dyv[:, sl]
            dbp_ref[r] += lax.dot_general(uv[:, sl], lam, (((0,), (0,)), ((), ())), preferred_element_type=F32)
            xs = _gather_rows(xr_ref, xi_ref, r, t).astype(BF16)
            dcp_ref[r] += lax.dot_general(xs, dyb[:, sl], (((0,), (0,)), ((), ())), preferred_element_type=F32)

    rev = lambda i: (nt - 1 - i, 0)
    full3 = lambda shp: pl.BlockSpec(shp, lambda i: (0, 0, 0))
    full2 = lambda shp: pl.BlockSpec(shp, lambda i: (0, 0))
    xs_spec = pl.BlockSpec((4, t * 8, 128), lambda i: (0, nt - 1 - i, 0))
    halo_spec = pl.BlockSpec((4, 8, 128), lambda i: (0, jnp.maximum((nt - 1 - i) * t - 1, 0), 0))
    st = jax.ShapeDtypeStruct((8, PSTATES), F32)
    return pl.pallas_call(
        body, name=name, grid=(nt,),
        in_specs=[pl.BlockSpec((t, uw), rev), pl.BlockSpec((t, uw), rev), xs_spec, xs_spec, halo_spec, halo_spec,
                  full3(cpt.shape), full3(bpt.shape), full2(a_re.shape), full2(a_im.shape), full2(dvec.shape)],
        out_specs=(pl.BlockSpec((t, uw), rev), full3((npc, PIECE, 2 * PSTATES)), full3((npc, 2 * PSTATES, PIECE)),
                   full2((8, PSTATES)), full2((8, PSTATES)), full2((1, uw))),
        out_shape=(jax.ShapeDtypeStruct((s, uw), F32), jax.ShapeDtypeStruct((npc, PIECE, 2 * PSTATES), F32),
                   jax.ShapeDtypeStruct((npc, 2 * PSTATES, PIECE), F32), st, st, jax.ShapeDtypeStruct((1, uw), F32)),
        scratch_shapes=[pltpu.VMEM((4, t * 8, 128), F32), pltpu.VMEM((4, t * 8, 128), F32),
                        pltpu.VMEM((8, PSTATES), F32), pltpu.VMEM((8, PSTATES), F32)],
        compiler_params=_cparams(("arbitrary",), 56),
    )(dy, proj, xs_re, xs_im, xs_re, xs_im, cpt, bpt, a_re, a_im, dvec)


_RC = 16
_LC = 128


def _conv3_block(xv, halo, w, b, row):
    h6, h7 = halo[_RC - 2:_RC - 1], halo[_RC - 1:_RC]
    x1 = jnp.where(row == 0, h7, pltpu.roll(xv, 1, 0))
    x2 = jnp.where(row == 0, h6, jnp.where(row == 1, h7, pltpu.roll(xv, 2, 0)))
    return ((b + x2 * w[0:1]) + x1 * w[1:2]) + xv * w[2:3], (x2, x1, xv)


def _ffn_tiles(s, f):
    tm = min(256, s)
    tn = f // 4 if (f // 4) % _LC == 0 else f
    assert tm % _RC == 0 and tn % _LC == 0
    return tm, tn


def _conv_glu_fwd(up0, cw, cb, name):
    _, s, f = up0.shape
    tm, tn = _ffn_tiles(s, f)
    hb = tm // _RC

    def body(x_ref, h_ref, w_ref, b_ref, a_ref):
        first = jnp.where(pl.program_id(0) > 0, 1.0, 0.0)
        row = lax.broadcasted_iota(jnp.int32, (tm, tn), 0)
        ups = [_conv3_block(x_ref[p].astype(F32), h_ref[p].astype(F32) * first, w_ref[p], b_ref[p], row)[0] for p in range(2)]
        a_ref[...] = (_gelu(ups[1]) * ups[0]).astype(BF16)

    return pl.pallas_call(
        body, name=name, grid=(s // tm, f // tn),
        in_specs=[pl.BlockSpec((2, tm, tn), lambda i, j: (0, i, j)),
                  pl.BlockSpec((2, _RC, tn), lambda i, j: (0, jnp.maximum(i * hb - 1, 0), j)),
                  pl.BlockSpec((2, 3, tn), lambda i, j: (0, 0, j)), pl.BlockSpec((2, 1, tn), lambda i, j: (0, 0, j))],
        out_specs=pl.BlockSpec((tm, tn), lambda i, j: (i, j)), out_shape=jax.ShapeDtypeStruct((s, f), BF16),
        compiler_params=_cparams(("parallel", "parallel")),
    )(up0, up0, cw, cb)


def _ffn_bwd_gate(da, up0, cw, cb, name):
    _, s, f = up0.shape
    tm, tn = _ffn_tiles(s, f)
    hb = tm // _RC

    def body(da_ref, x_ref, h_ref, w_ref, b_ref, d_ref, dw_ref, db_ref):
        i = pl.program_id(1)
        first = jnp.where(i > 0, 1.0, 0.0)

        @pl.when(i == 0)
        def _():
            dw_ref[...] = jnp.zeros_like(dw_ref)
            db_ref[...] = jnp.zeros_like(db_ref)

        row = lax.broadcasted_iota(jnp.int32, (tm, tn), 0)
        ups, taps = [], []
        for p in range(2):
            up, tap = _conv3_block(x_ref[p].astype(F32), h_ref[p].astype(F32) * first, w_ref[p], b_ref[p], row)
            ups.append(up)
            taps.append(tap)
        dav = da_ref[...]
        gate, dgate = _gelu_and_grad(ups[1])
        douts = (dav * gate, dav * ups[0] * dgate)
        for p in range(2):
            d_ref[p] = douts[p].astype(BF16)
            db_ref[p] += jnp.sum(douts[p], axis=0, keepdims=True)
            for kk in range(3):
                dw_ref[p, kk:kk + 1, :] += jnp.sum(douts[p] * taps[p][kk], axis=0, keepdims=True)

    return pl.pallas_call(
        body, name=name, grid=(f // tn, s // tm),
        in_specs=[pl.BlockSpec((tm, tn), lambda j, i: (i, j)), pl.BlockSpec((2, tm, tn), lambda j, i: (0, i, j)),
                  pl.BlockSpec((2, _RC, tn), lambda j, i: (0, jnp.maximum(i * hb - 1, 0), j)),
                  pl.BlockSpec((2, 3, tn), lambda j, i: (0, 0, j)), pl.BlockSpec((2, 1, tn), lambda j, i: (0, 0, j))],
        out_specs=(pl.BlockSpec((2, tm, tn), lambda j, i: (0, i, j)), pl.BlockSpec((2, 3, tn), lambda j, i: (0, 0, j)),
                   pl.BlockSpec((2, 1, tn), lambda j, i: (0, 0, j))),
        out_shape=(jax.ShapeDtypeStruct((2, s, f), BF16), jax.ShapeDtypeStruct((2, 3, f), F32), jax.ShapeDtypeStruct((2, 1, f), F32)),
        compiler_params=_cparams(("parallel", "arbitrary")),
    )(da, up0, up0, cw, cb)


def _conv_bwd(dup, cw, name):
    _, s, f = dup.shape
    tm, tn = _ffn_tiles(s, f)
    hb = tm // _RC
    nb = s // tm

    def body(d_ref, h_ref, w_ref, o_ref):
        last = jnp.where(pl.program_id(0) < nb - 1, 1.0, 0.0)
        row = lax.broadcasted_iota(jnp.int32, (tm, tn), 0)
        for p in range(2):
            d = d_ref[p].astype(F32)
            h = h_ref[p].astype(F32) * last
            d1 = jnp.where(row == tm - 1, h[0:1], pltpu.roll(d, tm - 1, 0))
            d2 = jnp.where(row == tm - 1, h[1:2], jnp.where(row == tm - 2, h[0:1], pltpu.roll(d, tm - 2, 0)))
            w = w_ref[p]
            o_ref[p] = (d * w[2:3] + d1 * w[1:2] + d2 * w[0:1]).astype(BF16)

    return pl.pallas_call(
        body, name=name, grid=(nb, f // tn),
        in_specs=[pl.BlockSpec((2, tm, tn), lambda i, j: (0, i, j)),
                  pl.BlockSpec((2, _RC, tn), lambda i, j: (0, jnp.minimum((i + 1) * hb, s // _RC - 1), j)),
                  pl.BlockSpec((2, 3, tn), lambda i, j: (0, 0, j))],
        out_specs=pl.BlockSpec((2, tm, tn), lambda i, j: (0, i, j)), out_shape=jax.ShapeDtypeStruct((2, s, f), BF16),
        compiler_params=_cparams(("parallel", "parallel")),
    )(dup, dup, cw)


def _ada_part(c_all, w_ada, name):
    nb, d = c_all.shape
    depth, _, cols = w_ada.shape
    tn = 1024 if cols % 1024 == 0 else cols

    def body(c_ref, w_ref, o_ref, ca_ref):
        cv = c_ref[...]
        ca = cv * _sigmoid(cv)
        ca_ref[...] = ca
        o_ref[...] = jnp.dot(ca.astype(BF16), w_ref[...].astype(BF16), preferred_element_type=F32)

    return pl.pallas_call(
        body, name=name, grid=(depth, cols // tn),
        in_specs=[pl.BlockSpec((nb, d), lambda l, j: (0, 0)), pl.BlockSpec((None, d, tn), lambda l, j: (l, 0, j))],
        out_specs=(pl.BlockSpec((None, nb, tn), lambda l, j: (l, 0, j)), pl.BlockSpec((nb, d), lambda l, j: (0, 0))),
        out_shape=(jax.ShapeDtypeStruct((depth, nb, cols), F32), jax.ShapeDtypeStruct((nb, d), F32)),
        compiler_params=_cparams(("arbitrary", "arbitrary")),
    )(c_all, w_ada)


def _ada_select(gath, b_ada, name):
    depth, n6 = b_ada.shape
    cols = gath.shape[1]

    def body(g_ref, b_ref, o_ref):
        me = 4 * lax.axis_index("x") + 2 * lax.axis_index("y") + lax.axis_index("c")
        for l in range(depth):
            for j in range(n6 // cols):
                row = (2 * j) * (8 * depth) + l * 8 + me
                o_ref[l:l + 1, j * cols:(j + 1) * cols] = g_ref[pl.ds(row, 1), :] + b_ref[l:l + 1, j * cols:(j + 1) * cols]

    return pl.pallas_call(body, name=name, out_shape=jax.ShapeDtypeStruct((depth, n6), F32))(gath, b_ada)


def _wada_grad(ca_t, d_sel, name):
    d, nb = ca_t.shape
    depth, _, cols = d_sel.shape
    tm = min(256, d)

    def body(a_ref, g_ref, o_ref):
        acc = a_ref[:, 0:1] * g_ref[0:1, :]
        for b in range(1, nb):
            acc = acc + a_ref[:, b:b + 1] * g_ref[b:b + 1, :]
        o_ref[...] = acc

    return pl.pallas_call(
        body, name=name, grid=(depth, d // tm),
        in_specs=[pl.BlockSpec((tm, nb), lambda l, i: (i, 0)), pl.BlockSpec((None, nb, cols), lambda l, i: (l, 0, 0))],
        out_specs=pl.BlockSpec((None, tm, cols), lambda l, i: (l, i, 0)), out_shape=jax.ShapeDtypeStruct((depth, d, cols), F32),
        compiler_params=_cparams(("parallel", "parallel")),
    )(ca_t, d_sel)


def _block_rows(r, c):
    tr = r
    for cand in (2048, 1024, 512, 256, 128, 64, 32, 16, 8):
        if r % cand == 0 and cand * c * 4 <= MIB:
            tr = cand
            break
    else:
        for cand in (8, 16, 32):
            if r % cand == 0:
                tr = cand
                break
    return tr


def _adamw(w, g, m, v, name):
    nl, r, c = w.shape
    tr = _block_rows(r, c)
    c1 = 1.0 - ADAM_B1 ** ADAM_STEP
    c2 = 1.0 - ADAM_B2 ** ADAM_STEP

    def body(w_ref, g_ref, m_ref, v_ref, d_ref, nm_ref, nv_ref):
        gv = g_ref[...]
        nm = ADAM_B1 * m_ref[...] + (1.0 - ADAM_B1) * gv
        nv = ADAM_B2 * v_ref[...] + (1.0 - ADAM_B2) * (gv * gv)
        d_ref[...] = -ADAM_LR * ((nm / c1) / (jnp.sqrt(nv / c2) + ADAM_EPS) + ADAM_WD * w_ref[...])
        nm_ref[...] = nm
        nv_ref[...] = nv

    spec = pl.BlockSpec((None, tr, c), lambda l, i: (l, i, 0))
    shp = jax.ShapeDtypeStruct((nl, r, c), F32)
    return pl.pallas_call(
        body, name=name, grid=(nl, r // tr), in_specs=[spec] * 4, out_specs=(spec,) * 3, out_shape=(shp,) * 3,
        compiler_params=_cparams(("parallel", "parallel")),
    )(w, g, m, v)


def _sum_slots(x, nslots, name, out_dtype=F32):
    r = x.shape[0] // nslots
    c = x.shape[1]
    tr = _block_rows(r, c)
    nbk = r // tr

    def body(*refs):
        acc = refs[0][...].astype(F32)
        for k in range(1, nslots):
            acc = acc + refs[k][...].astype(F32)
        refs[nslots][...] = acc.astype(out_dtype)

    specs = [pl.BlockSpec((tr, c), functools.partial(lambda k, i: (k * nbk + i, 0), k)) for k in range(nslots)]
    return pl.pallas_call(
        body, name=name, grid=(nbk,), in_specs=specs, out_specs=pl.BlockSpec((tr, c), lambda i: (i, 0)),
        out_shape=jax.ShapeDtypeStruct((r, c), out_dtype), compiler_params=_cparams(("parallel",)),
    )(*([x] * nslots))


def _pick(dim, prefs):
    for p in prefs:
        if dim % p == 0:
            return p
    return dim


def _mm(a, b, m, n, k, name, out_dtype, **kw):
    tm = _pick(m, (1408, 1152, 1024, 512, 256, 128))
    tn = _pick(n, (1408, 1152, 1024, 512, 256, 128))
    kdiv = k // max(kw.get("a_stack", 0), kw.get("b_stack", 0) if kw.get("tb") else 0, 1)
    osize = jnp.dtype(out_dtype).itemsize
    tk = kdiv
    for cut in (1, 2, 4, 8, 16):
        tk = kdiv // cut
        vmem = 2 * 2 * tk * (tm + tn) + tm * tn * (2 * osize + 4 + (4 if tk < k else 0))
        if kdiv % cut == 0 and tk % 128 == 0 and vmem <= _MATMUL_VMEM_BUDGET:
            break
    return _matmul(a, b, m=m, n=n, k=k, tm=tm, tn=tn, tk=tk, out_dtype=out_dtype, name=name, **kw)


def _ssm_layout(p):
    g, st = p["lam_re"].shape
    npc = g * st // PSTATES
    lr = p["lam_re"].reshape(npc, PSTATES)
    li = p["lam_im"].reshape(npc, PSTATES)
    ls = jnp.broadcast_to(p["log_step"][:, None], (g, st)).reshape(npc, PSTATES)
    btr = jnp.transpose(p["ssm_b_re"], (2, 0, 1)).reshape(SSM_GROUP, npc, PSTATES)
    bti = jnp.transpose(p["ssm_b_im"], (2, 0, 1)).reshape(SSM_GROUP, npc, PSTATES)
    return lr, li, ls, btr, bti


def _ssm_pieces(bbr, bbi, c_re, c_im):
    npc = bbr.shape[1]
    gl = PSTATES // STATE
    eye = jnp.eye(gl, dtype=bool)

    def b_piece(bb):
        t = jnp.transpose(bb.reshape(SSM_GROUP, npc, gl, STATE), (1, 2, 0, 3))
        full = jnp.where(eye[None, :, None, :, None], t[:, :, :, None, :], 0.0)
        return full.reshape(npc, gl * SSM_GROUP, PSTATES)

    def c_piece(cc):
        t = jnp.transpose(cc.reshape(npc, gl, SSM_GROUP, STATE), (0, 1, 3, 2))
        full = jnp.where(eye[None, :, None, :, None], t[:, :, :, None, :], 0.0)
        return full.reshape(npc, PSTATES, gl * SSM_GROUP)

    bp = jnp.concatenate([b_piece(bbr), b_piece(bbi)], axis=2).astype(BF16)
    cp = jnp.concatenate([c_piece(c_re), c_piece(-c_im)], axis=1).astype(BF16)
    return bp, cp, jnp.swapaxes(bp, 1, 2), jnp.swapaxes(cp, 1, 2)


def _ssm_unpieces(dbp, dcp):
    npc = dbp.shape[0]
    gl = PSTATES // STATE
    idx = jnp.arange(gl)

    def b_diag(x):
        d = x.reshape(npc, gl, SSM_GROUP, gl, STATE)[:, idx, :, idx, :]
        return jnp.transpose(d, (2, 1, 0, 3)).reshape(SSM_GROUP, npc, PSTATES)

    def c_diag(x):
        d = x.reshape(npc, gl, STATE, gl, SSM_GROUP)[:, idx, :, idx, :]
        return jnp.transpose(d, (1, 0, 3, 2)).reshape(npc * gl, SSM_GROUP, STATE)

    return b_diag(dbp[:, :, :PSTATES]), b_diag(dbp[:, :, PSTATES:]), c_diag(dcp[:, :PSTATES, :]), -c_diag(dcp[:, PSTATES:, :])


class _LayerWeights:
    def __init__(self, fetch):
        self._fetch = fetch
        self._got = {}

    def group(self, g, after=None):
        if g not in self._got:
            self._got[g] = self._fetch(g, after)
        return self._got[g]


def _layer_fwd(l, x, ada6, weights, p):
    s, d = x.shape
    sh_m, sc_m, gt_m, sh_f, sc_f, gt_f = ada6
    w = dict(weights.group("mix", x))
    uw = w["w_glu"].shape[0]
    aw = w["w_out"].shape[0] - uw
    ncol = w["w_in"].shape[0]
    row = lambda v: v.reshape(1, -1)
    n = lambda t: f"l{l}_{t}"

    h = _pre_fwd(x, row(p["g_pre_mix"]), sc_m, sh_m, n("pre_mix"))
    proj = _mm(h, w["w_in"], s, ncol, d, n("proj"), BF16, tb=True)
    attn = _attn_fwd(proj, p["attn_sinks"], aw, uw, n("attn_fwd"))
    zin = _ssm_layout(p)
    a_re, a_im, bbr, bbi = _ssm_prep(*zin, n("ssm_prep"))
    bp, cp, bpt, cpt = _ssm_pieces(bbr, bbi, p["ssm_c_re"], p["ssm_c_im"])
    dvec = p["ssm_d"].reshape(1, uw)
    y, xs_re, xs_im = _ssm_fwd(proj, bp, cp, a_re, a_im, dvec, uw, n("ssm_fwd"), t=256 if s % 256 == 0 else 128)
    z = _gelu_fwd(y, n("gelu_fwd"))
    gl = _mm(z, w["w_glu"], s, uw, uw, n("glu"), F32)
    merged = _merge_fwd(attn, y, gl, row(p["g_attn_out"]), row(p["g_ssm_out"]), n("merge_fwd"))
    mix = _mm(merged, w["w_out"], s, d, aw + uw, n("out_proj"), F32)
    x1 = _post_fwd(x, mix, row(p["g_post_mix"]), gt_m, n("post_mix"))

    w.update(weights.group("ffn", x1))
    f = w["w_down"].shape[0]
    h2 = _pre_fwd(x1, row(p["g_pre_ffn"]), sc_f, sh_f, n("pre_ffn"))
    up0 = _mm(h2, w["w_up"], s, 2 * f, d, n("up_proj"), BF16, b_stack=w["w_up"].shape[0], o_stack=2)
    cw2 = jnp.transpose(p["conv_w"].reshape(3, 2, f), (1, 0, 2))
    cb2 = p["conv_b"].reshape(2, 1, f)
    act = _conv_glu_fwd(up0, cw2, cb2, n("conv_glu"))
    ff = _mm(act, w["w_down"], s, d, f, n("down_proj"), F32)
    x2 = _post_fwd(x1, ff, row(p["g_post_ffn"]), gt_f, n("post_ffn"))
    saved = dict(x=x, h=h, proj=proj, attn=attn, zin=zin, a_re=a_re, a_im=a_im, bpt=bpt, cpt=cpt, dvec=dvec, y=y, xs_re=xs_re,
                 xs_im=xs_im, z=z, gl=gl, merged=merged, mix=mix, x1=x1, h2=h2, up0=up0, cw2=cw2, cb2=cb2, act=act, ff=ff)
    return x2, saved


def _layer_bwd(l, dx2, ada6, weights, p, sv, on_grads):
    s, d = dx2.shape
    sh_m, sc_m, gt_m, sh_f, sc_f, gt_f = ada6
    w = {**weights.group("mix"), **weights.group("ffn")}
    uw = w["w_glu"].shape[0]
    aw = w["w_out"].shape[0] - uw
    ncol = w["w_in"].shape[0]
    f = w["w_down"].shape[0]
    nst = w["w_up"].shape[0]
    row = lambda v: v.reshape(1, -1)
    n = lambda t: f"l{l}_{t}"
    gw, gs = {}, {}

    dff, dgt_f, gs["g_post_ffn"] = _post_bwd(dx2, sv["ff"], row(p["g_post_ffn"]), gt_f, n("post_ffn_bwd"))
    gw["w_down"] = _mm(sv["act"], dff, f, d, s, n("down_dw"), BF16, ta=True)
    dact = _mm(dff, w["w_down"], s, f, d, n("down_dx"), F32, tb=True)
    dup, dcw2, dcb2 = _ffn_bwd_gate(dact, sv["up0"], sv["cw2"], sv["cb2"], n("ffn_gate_bwd"))
    gs["conv_w"] = jnp.transpose(dcw2, (1, 0, 2)).reshape(3, 2 * f)
    gs["conv_b"] = dcb2.reshape(2 * f)
    dup0 = _conv_bwd(dup, sv["cw2"], n("conv_bwd"))
    gw["w_up"] = _mm(sv["h2"], dup0, d, 2 * f, s, n("up_dw"), BF16, ta=True, b_stack=2, o_stack=nst)
    dh2 = _mm(dup0, w["w_up"], s, d, 2 * f, n("up_dx"), F32, tb=True, a_stack=2, b_stack=nst)
    dx1, dsh_f, dsc_f, gs["g_pre_ffn"] = _pre_bwd(dx2, dh2, sv["x1"], row(p["g_pre_ffn"]), sc_f, n("pre_ffn_bwd"))
    token = on_grads(l, "ffn", {k: gw.pop(k) for k in ("w_up", "w_down")})
    if token is not None:
        gt_m = gt_m + token[0:1, 0:1]

    dmix, dgt_m, gs["g_post_mix"] = _post_bwd(dx1, sv["mix"], row(p["g_post_mix"]), gt_m, n("post_mix_bwd"))
    gw["w_out"] = _mm(sv["merged"], dmix, aw + uw, d, s, n("out_dw"), BF16, ta=True)
    dmerged = _mm(dmix, w["w_out"], s, aw + uw, d, n("out_dx"), F32, tb=True)
    dattn, dgl, dzd, gs["g_attn_out"], gs["g_ssm_out"] = _merge_bwd(
        dmerged, sv["attn"], sv["y"], sv["gl"], row(p["g_attn_out"]), row(p["g_ssm_out"]), n("merge_bwd"))
    gw["w_glu"] = _mm(sv["z"], dgl, uw, uw, s, n("glu_dw"), BF16, ta=True)
    dz2 = _mm(dgl, w["w_glu"], s, uw, uw, n("glu_dx"), F32, tb=True)
    dy = _gelu_bwd(dzd, dz2, sv["y"], n("gelu_bwd"))
    du, dbp, dcp, dar, dai, dd = _ssm_bwd(dy, sv["proj"], sv["xs_re"], sv["xs_im"], sv["cpt"], sv["bpt"], sv["a_re"], sv["a_im"],
                                          sv["dvec"], uw, n("ssm_bwd"))
    dq, dkv_c, dkv_p, dsinks = _attn_bwd(sv["proj"], p["attn_sinks"], sv["attn"], dattn, aw, uw, n("attn_bwd"))
    dproj = _assemble_dproj(du, dq, dkv_c, dkv_p, n("dproj"))
    gw["w_in"] = _mm(dproj, sv["h"], ncol, d, s, n("in_dw"), BF16, ta=True)
    dh = _mm(dproj, w["w_in"], s, d, ncol, n("in_dx"), F32)
    dx0, dsh_m, dsc_m, gs["g_pre_mix"] = _pre_bwd(dx1, dh, sv["x"], row(p["g_pre_mix"]), sc_m, n("pre_mix_bwd"))
    token = on_grads(l, "mix", {k: gw.pop(k) for k in ("w_in", "w_glu", "w_out")})

    dbbr, dbbi, dc_re, dc_im = _ssm_unpieces(dbp, dcp)
    dlr, dli, dls, dbtr, dbti = _ssm_prep_bwd(*sv["zin"], dar, dai, dbbr, dbbi, n("ssm_prep_bwd"))
    g, st = p["lam_re"].shape
    gs["lam_re"] = dlr.reshape(g, st)
    gs["lam_im"] = dli.reshape(g, st)
    gs["log_step"] = jnp.sum(dls.reshape(g, st), axis=1)
    gs["ssm_b_re"] = jnp.transpose(dbtr.reshape(SSM_GROUP, g, st), (1, 2, 0))
    gs["ssm_b_im"] = jnp.transpose(dbti.reshape(SSM_GROUP, g, st), (1, 2, 0))
    gs["ssm_c_re"] = dc_re
    gs["ssm_c_im"] = dc_im
    gs["ssm_d"] = dd.reshape(p["ssm_d"].shape)
    gs["attn_sinks"] = dsinks.reshape(-1)
    gs["b_ada"] = jnp.concatenate([dsh_m, dsc_m, dgt_m, dsh_f, dsc_f, dgt_f], axis=1).reshape(-1)
    for key in ("g_post_ffn", "g_pre_ffn", "g_post_mix", "g_attn_out", "g_ssm_out", "g_pre_mix"):
        gs[key] = gs[key].reshape(-1)
    return dx0, gs, token


def _local_step(x, tgt, ada, wl, pl_small, on_grads):
    d = x.shape[1]
    depth = len(wl)
    ada6 = [[ada[l:l + 1, k * d:(k + 1) * d] for k in range(6)] for l in range(depth)]
    saved = []
    h = x
    for l in range(depth):
        h, sv = _layer_fwd(l, h, ada6[l], wl[l], pl_small[l])
        saved.append(sv)
    dy, lsum = _loss_head(h, tgt, "loss_head")
    loss = 0.5 * lsum[0, 0] / d
    gss = [None] * depth
    dx = dy
    for l in reversed(range(depth)):
        dx, gss[l], token = _layer_bwd(l, dx, ada6[l], wl[l], pl_small[l], saved[l], on_grads)
        if token is not None and l > 0:
            ada6[l - 1][5] = ada6[l - 1][5] + token[0:1, 0:1]
    return loss, dx, gss


_ANY = pl.BlockSpec(memory_space=pl.ANY)


def _mesh_pos():
    return lax.axis_index("x"), lax.axis_index("y"), lax.axis_index("c")


def _other_chips(x, y):
    return [(1 - x, y), (x, 1 - y), (1 - x, 1 - y)]


def _remote(src, dst, send_sems, recv_sems, k, to):
    return pltpu.make_async_remote_copy(src_ref=src, dst_ref=dst, send_sem=send_sems.at[k], recv_sem=recv_sems.at[k],
                                        device_id=to, device_id_type=MESH)


def _allgather8(xs, name):
    m, n = xs.shape

    def body(x_ref, out_ref, send_sems, recv_sems):
        x, y, c = _mesh_pos()
        me, sibling = (x, y, c), (x, y, 1 - c)
        chips = _other_chips(x, y)

        def rows(px, py, pc):
            return out_ref.at[pl.ds((4 * px + 2 * py + pc) * m, m), :]

        def copy(k, block, to, src=None):
            return _remote(rows(*block) if src is None else src, rows(*block), send_sems, recv_sems, k, to)

        first = [copy(0, me, sibling, src=x_ref)]
        first += [copy(1 + j, me, (*chip, c), src=x_ref) for j, chip in enumerate(chips)]
        for cp in first:
            cp.start()
        passed = [copy(4 + j, (*chip, c), sibling) for j, chip in enumerate(chips)]
        for j, chip in enumerate(chips):
            copy(1 + j, (*chip, c), me).wait_recv()
            passed[j].start()
        copy(0, sibling, me).wait_recv()
        for j, chip in enumerate(chips):
            copy(4 + j, (*chip, 1 - c), me).wait_recv()
        for cp in first + passed:
            cp.wait_send()

    out = pl.pallas_call(
        body, name=name, out_shape=jax.ShapeDtypeStruct((8 * m, n), xs.dtype), in_specs=[_ANY], out_specs=_ANY,
        scratch_shapes=[pltpu.SemaphoreType.DMA((7,)), pltpu.SemaphoreType.DMA((7,))],
    )(xs)
    x, y, c = _mesh_pos()
    return lax.dynamic_update_slice(out, xs, ((4 * x + 2 * y + c) * m, 0))


def _half_rows(ref_rows, half, align):
    h = ref_rows // 2
    return pl.ds(pl.multiple_of(half * h, align), h)


_HBM = pl.BlockSpec(memory_space=pltpu.HBM)
_SEMS = pl.BlockSpec(memory_space=pltpu.SEMAPHORE)
_EFFECT = pltpu.SideEffectType.DATAFLOW_SIDE_EFFECTING
_TOKEN = jax.ShapeDtypeStruct((8, 128), F32)


def _in_hbm(arrays):
    return [pltpu.with_memory_space_constraint(a, pltpu.HBM) for a in arrays]


def _chip_copy(kind, srcs, lands, w, q, chip, mine, c, send, recv, k):
    if kind == "gather":
        rows = _half_rows(srcs[w].shape[0], c, 16)
        return _remote(srcs[w].at[rows, :], lands[w].at[mine, rows, :], send, recv, k, (*chip, c))
    return _remote(srcs[w].at[2 * chip[0] + chip[1]], lands[w].at[mine], send, recv, k, (*chip, c))


def _chip_landing(kind, srcs, lands, w, chip, c):
    if kind == "gather":
        return lands[w].at[2 * chip[0] + chip[1], _half_rows(srcs[w].shape[0], c, 16), :]
    return lands[w].at[2 * chip[0] + chip[1]]


def _ici_start(kind, srcs, groups, name, after=None):
    nw, ng = len(srcs), len(groups)
    lands = [lax.empty((4,) + s.shape if kind == "gather" else s.shape, s.dtype) for s in srcs]
    extra = [] if after is None else [after]

    def body(*refs):
        ins, lnd = refs[:nw], refs[nw:2 * nw]
        sems = refs[2 * nw + len(extra):2 * nw + len(extra) + 2 * ng]
        token = refs[-1]
        x, y, c = _mesh_pos()
        for g, members in enumerate(groups):
            for j, w in enumerate(members):
                for q, chip in enumerate(_other_chips(x, y)):
                    _chip_copy(kind, ins, lnd, w, q, chip, 2 * x + y, c, sems[2 * g], sems[2 * g + 1], 3 * j + q).start()
        token[...] = jnp.zeros_like(token)

    sem_shapes = [pltpu.SemaphoreType.DMA((3 * len(members),)) for members in groups for _ in range(2)]
    out = pl.pallas_call(
        body, name=name,
        out_shape=(*sem_shapes, *[pltpu.HBM(s.shape, s.dtype) for s in srcs], *[pltpu.HBM(t.shape, t.dtype) for t in lands], _TOKEN),
        in_specs=[_HBM] * (2 * nw) + [_ANY] * len(extra),
        out_specs=(*([_SEMS] * (2 * ng)), *([_HBM] * (2 * nw)), pl.BlockSpec(memory_space=pltpu.VMEM)),
        input_output_aliases={i: 2 * ng + i for i in range(2 * nw)},
        compiler_params=pltpu.CompilerParams(has_side_effects=_EFFECT),
    )(*_in_hbm(srcs), *_in_hbm(lands), *extra)
    sems = [(out[2 * g], out[2 * g + 1]) for g in range(ng)]
    return sems, list(out[2 * ng:2 * ng + nw]), list(out[2 * ng + nw:2 * ng + 2 * nw]), out[-1]


def _ici_wait(kind, sems, srcs, lands, after, name):
    nm = len(srcs)

    def body(*refs):
        ins, lnd = refs[:nm], refs[nm:2 * nm]
        send, recv = refs[2 * nm], refs[2 * nm + 1]
        x, y, c = _mesh_pos()
        for j in range(nm):
            for q, chip in enumerate(_other_chips(x, y)):
                _chip_copy(kind, ins, lnd, j, q, chip, 2 * x + y, c, send, recv, 3 * j + q).wait_send()
                landed = _chip_landing(kind, ins, lnd, j, chip, c)
                _remote(landed, landed, send, recv, 3 * j + q, (x, y, c)).wait_recv()

    out = pl.pallas_call(
        body, name=name, out_shape=(*[pltpu.HBM(s.shape, s.dtype) for s in srcs], *[pltpu.HBM(t.shape, t.dtype) for t in lands]),
        in_specs=[_HBM] * (2 * nm) + [_SEMS, _SEMS, _ANY], out_specs=tuple([_HBM] * (2 * nm)),
        input_output_aliases={i: i for i in range(2 * nm)},
        compiler_params=pltpu.CompilerParams(has_side_effects=_EFFECT),
    )(*srcs, *lands, sems[0], sems[1], after)
    return list(out[:nm]), list(out[nm:])


def _gather_finish(shards, lands, name):
    nw = len(shards)

    def body(*refs):
        ins, lnd, outs = refs[:nw], refs[nw:2 * nw], refs[2 * nw:3 * nw]
        send_sems, recv_sems = refs[3 * nw:]
        x, y, c = _mesh_pos()
        mine = 2 * x + y
        sibling = (x, y, 1 - c)
        chips = _other_chips(x, y)

        def blk(ref, chip_idx, half):
            return ref.at[chip_idx, _half_rows(ref.shape[1], half, 16), :]

        sends = []
        for w in range(nw):
            for q, chip in enumerate(chips):
                k = 2 * chip[0] + chip[1]
                sends.append(_remote(blk(lnd[w], k, c), blk(outs[w], k, c), send_sems, recv_sems, 4 * w + q, sibling))
            sends.append(_remote(ins[w], outs[w].at[mine], send_sems, recv_sems, 4 * w + 3, sibling))
        for cp in sends:
            cp.start()
        for w in range(nw):
            for q, chip in enumerate(chips):
                other = blk(outs[w], 2 * chip[0] + chip[1], 1 - c)
                _remote(other, other, send_sems, recv_sems, 4 * w + q, (x, y, c)).wait_recv()
            own = outs[w].at[mine]
            _remote(own, own, send_sems, recv_sems, 4 * w + 3, (x, y, c)).wait_recv()
        for cp in sends:
            cp.wait_send()

    return pl.pallas_call(
        body, name=name, out_shape=[jax.ShapeDtypeStruct(t.shape, t.dtype) for t in lands],
        in_specs=[_ANY] * (2 * nw), out_specs=[_ANY] * nw, input_output_aliases={nw + w: w for w in range(nw)},
        scratch_shapes=[pltpu.SemaphoreType.DMA((4 * nw,)), pltpu.SemaphoreType.DMA((4 * nw,))],
    )(*shards, *lands)


def _exchange_halves(gs, name):
    nw = len(gs)

    def body(*refs):
        ins, outs = refs[:nw], refs[nw:2 * nw]
        send_sems, recv_sems = refs[2 * nw:]
        x, y, c = _mesh_pos()
        cps = []
        for w in range(nw):
            src = ins[w].at[:, _half_rows(gs[w].shape[1], 1 - c, 16), :]
            cps.append(_remote(src, outs[w], send_sems, recv_sems, w, (x, y, 1 - c)))
            cps[-1].start()
        for cp in cps:
            cp.wait_recv()
        for cp in cps:
            cp.wait_send()

    return pl.pallas_call(
        body, name=name, out_shape=[jax.ShapeDtypeStruct((4, g.shape[1] // 2, g.shape[2]), g.dtype) for g in gs],
        in_specs=[_ANY] * nw, out_specs=[_ANY] * nw,
        scratch_shapes=[pltpu.SemaphoreType.DMA((nw,)), pltpu.SemaphoreType.DMA((nw,))],
    )(*gs)


def _add_half(g, recv, cidx, name):
    _, r, c = g.shape
    h = r // 2
    tr = _block_rows(h, c)
    nbh = h // tr
    assert tr % 16 == 0

    def body(c_ref, g_ref, r_ref, o_ref):
        o_ref[...] = (g_ref[...].astype(F32) + r_ref[...].astype(F32)).astype(BF16)

    grid_spec = pltpu.PrefetchScalarGridSpec(
        num_scalar_prefetch=1, grid=(4, nbh),
        in_specs=[pl.BlockSpec((None, tr, c), lambda s, i, cr: (s, cr[0] * nbh + i, 0)),
                  pl.BlockSpec((None, tr, c), lambda s, i, cr: (s, i, 0))],
        out_specs=pl.BlockSpec((None, tr, c), lambda s, i, cr: (s, i, 0)))
    return pl.pallas_call(
        body, name=name, grid_spec=grid_spec, out_shape=jax.ShapeDtypeStruct((4, h, c), BF16),
        compiler_params=_cparams(("parallel", "parallel")),
    )(cidx, g, recv)


def _swap_with_sibling(xs, name):
    nf = len(xs)

    def body(*refs):
        ins, outs = refs[:nf], refs[nf:2 * nf]
        send_sems, recv_sems = refs[2 * nf:]
        x, y, c = _mesh_pos()
        cps = [_remote(ins[k], outs[k], send_sems, recv_sems, k, (x, y, 1 - c)) for k in range(nf)]
        for cp in cps:
            cp.start()
        for cp in cps:
            cp.wait_recv()
        for cp in cps:
            cp.wait_send()

    return pl.pallas_call(
        body, name=name, out_shape=[jax.ShapeDtypeStruct(t.shape, t.dtype) for t in xs], in_specs=[_ANY] * nf, out_specs=[_ANY] * nf,
        scratch_shapes=[pltpu.SemaphoreType.DMA((nf,)), pltpu.SemaphoreType.DMA((nf,))],
    )(*xs)


_BIG = ("w_in", "w_glu", "w_out", "w_up", "w_down")
_GROUPS = {"mix": ("w_in", "w_glu", "w_out"), "ffn": ("w_up", "w_down")}
_SMALL = ("b_ada", "g_pre_mix", "g_post_mix", "attn_sinks", "lam_re", "lam_im", "log_step", "ssm_b_re", "ssm_b_im", "ssm_c_re",
          "ssm_c_im", "ssm_d", "g_attn_out", "g_ssm_out", "g_pre_ffn", "g_post_ffn", "conv_b")
_WEIGHTS = ("w_ada", "b_ada", "g_pre_mix", "g_post_mix", "w_in", "attn_sinks", "lam_re", "lam_im", "log_step", "ssm_b_re", "ssm_b_im",
            "ssm_c_re", "ssm_c_im", "ssm_d", "w_glu", "g_attn_out", "g_ssm_out", "w_out", "g_pre_ffn", "g_post_ffn", "w_up", "conv_w",
            "conv_b", "w_down")
_LANES = 1024


def _pack(parts, rows_to):
    flat = jnp.concatenate([p.reshape(-1) for p in parts])
    per = _LANES * rows_to
    total = -(-flat.shape[0] // per) * per
    return jnp.pad(flat, (0, total - flat.shape[0])).reshape(total // _LANES, _LANES)


def _unpack(packed, shapes):
    flat = packed.reshape(-1)
    out, off = [], 0
    for shp in shapes:
        size = math.prod(shp)
        out.append(flat[off:off + size].reshape(shp))
        off += size
    return out


def kernel(x, c, w_ada, b_ada, g_pre_mix, g_post_mix, w_in, attn_sinks, lam_re, lam_im, log_step, ssm_b_re, ssm_b_im, ssm_c_re, ssm_c_im, ssm_d, w_glu, g_attn_out, g_ssm_out, w_out, g_pre_ffn, g_post_ffn, w_up, conv_w, conv_b, w_down, loss_target, m_w_ada, m_b_ada, m_g_pre_mix, m_g_post_mix, m_w_in, m_attn_sinks, m_lam_re, m_lam_im, m_log_step, m_ssm_b_re, m_ssm_b_im, m_ssm_c_re, m_ssm_c_im, m_ssm_d, m_w_glu, m_g_attn_out, m_g_ssm_out, m_w_out, m_g_pre_ffn, m_g_post_ffn, m_w_up, m_conv_w, m_conv_b, m_w_down, v_w_ada, v_b_ada, v_g_pre_mix, v_g_post_mix, v_w_in, v_attn_sinks, v_lam_re, v_lam_im, v_log_step, v_ssm_b_re, v_ssm_b_im, v_ssm_c_re, v_ssm_c_im, v_ssm_d, v_w_glu, v_g_attn_out, v_g_ssm_out, v_w_out, v_g_pre_ffn, v_g_post_ffn, v_w_up, v_conv_w, v_conv_b, v_w_down):
    given = dict(locals())
    wts = {n: given[n] for n in _WEIGHTS}
    mom = {n: given["m_" + n] for n in _WEIGHTS}
    var = {n: given["v_" + n] for n in _WEIGHTS}
    depth, d, ada_cols = w_ada.shape
    nchips = 4
    xi, yi, ci = lax.axis_index("x"), lax.axis_index("y"), lax.axis_index("c")
    chip = 2 * xi + yi
    cidx = jnp.reshape(ci, (1,)).astype(jnp.int32)

    cw_cols = conv_w.shape[2]
    vec = _pack([c, conv_w], 8)
    g1 = _allgather8(vec, "ag_cond").reshape(8, -1)
    c_all = g1[:, :d]
    cw_sh = g1[0::2, d:d + depth * 3 * cw_cols].reshape(nchips, depth, 3, cw_cols)
    conv_w_full = jnp.transpose(cw_sh, (1, 2, 0, 3)).reshape(depth, 3, nchips * cw_cols)

    ada_part, c_act = _ada_part(c_all, w_ada, "ada_part")
    g2 = _allgather8(ada_part.reshape(depth * 8, ada_cols), "ag_ada")
    ada = _ada_select(g2, b_ada, "ada_select")

    order = [(l, g) for l in range(depth) for g in _GROUPS]
    for table in (wts, mom, var):
        table["w_in"] = jnp.swapaxes(table["w_in"], 1, 2)
    members = {key: [wts[n][key[0]].astype(BF16) for n in _GROUPS[key[1]]] for key in order}
    flat = [s for key in order for s in members[key]]
    index, at = {}, 0
    for key in order:
        index[key] = list(range(at, at + len(members[key])))
        at += len(members[key])
    ag_sems, ag_srcs, ag_lands, ag_token = _ici_start("gather", flat, [index[key] for key in order], "ag_start", after=ada)
    ada = ada + ag_token[0:1, 0:1]

    def fetch(l, g, after):
        pos, ids = order.index((l, g)), index[(l, g)]
        srcs, lands = [ag_srcs[i] for i in ids], [ag_lands[i] for i in ids]
        srcs, lands = _ici_wait("gather", ag_sems[pos], srcs, lands, ag_token if after is None else after, f"ag_wait_l{l}_{g}")
        got = dict(zip(_GROUPS[g], _gather_finish(srcs, lands, f"ag_finish_l{l}_{g}")))
        if g == "ffn":
            return dict(w_up=got["w_up"], w_down=got["w_down"].reshape(-1, got["w_down"].shape[2]))
        w_in_t = got["w_in"].reshape(-1, d)
        split = w_in_t.shape[0] - nchips * got["w_glu"].shape[1]
        return dict(w_in=jnp.concatenate([w_in_t[split:], w_in_t[:split]], axis=0),
                    w_glu=got["w_glu"].reshape(-1, got["w_glu"].shape[2]), w_out=got["w_out"].reshape(-1, got["w_out"].shape[2]))

    wl = [_LayerWeights(functools.partial(fetch, l)) for l in range(depth)]
    wl[0].group("mix")
    ps = []
    for l in range(depth):
        small = {n: wts[n][l] for n in _SMALL if n != "b_ada"}
        small["conv_w"] = conv_w_full[l]
        ps.append(small)

    in_flight = {}

    def on_grads(l, g, gw):
        stacks = []
        for n in _GROUPS[g]:
            t = gw[n]
            if n == "w_in":
                uw = nchips * wts["w_glu"].shape[1]
                t = jnp.concatenate([t[uw:], t[:uw]], axis=0).reshape(nchips, -1, d)
            elif n != "w_up":
                t = t.reshape(nchips, t.shape[0] // nchips, t.shape[1])
            stacks.append(t)
        from_sibling = _exchange_halves(stacks, f"rs_sibling_l{l}_{g}")
        partials = [_add_half(s, r, cidx, f"rs_add_l{l}_{n}") for s, r, n in zip(stacks, from_sibling, _GROUPS[g])]
        sems, srcs, lands, token = _ici_start("scatter", partials, [list(range(len(partials)))], f"rs_start_l{l}_{g}")
        in_flight[(l, g)] = (sems[0], srcs, lands)
        return token

    loss_sum, grad_x, gss = _local_step(x[0], loss_target[0], ada, wl, ps, on_grads)
    loss = lax.psum(loss_sum, ("x", "y", "c"))

    reduced = {}
    for key in reversed(order):
        l, g = key
        sems, srcs, lands = in_flight[key]
        sent, landed = _ici_wait("scatter", sems, srcs, lands, grad_x, f"rs_wait_l{l}_{g}")
        for n, t, p in zip(_GROUPS[g], landed, sent):
            t = lax.dynamic_update_slice(t, lax.dynamic_slice_in_dim(p, chip, 1, axis=0), (chip, 0, 0))
            reduced[(n, l)] = _sum_slots(t.reshape(-1, t.shape[2]), nchips, f"rs_sum_l{l}_{n}")
    keys = [(n, l) for n in _BIG for l in range(depth)]
    theirs = dict(zip(keys, _swap_with_sibling([reduced[k] for k in keys], "rs_share")))
    big_grads = {}
    for n in _BIG:
        own = jnp.stack([reduced[(n, l)] for l in range(depth)])
        oth = jnp.stack([theirs[(n, l)] for l in range(depth)])
        big_grads[n] = jnp.where(ci == 0, jnp.concatenate([own, oth], axis=1), jnp.concatenate([oth, own], axis=1))

    small_parts = [jnp.stack([gss[l][n] for l in range(depth)]) for n in _SMALL]
    pack_small = _pack(small_parts, 8)
    pack_cw = _pack([jnp.stack([gss[l]["conv_w"] for l in range(depth)])], 8)
    rows_small = pack_small.shape[0]
    mine = jnp.concatenate([pack_small, pack_cw], axis=0)
    g3 = _allgather8(mine, "ag_small")
    total = _sum_slots(g3, 8, "sum_small")
    grads = dict(big_grads)
    for n, v in zip(_SMALL, _unpack(total[:rows_small], [wts[n].shape for n in _SMALL])):
        grads[n] = v
    conv_w_grad = _unpack(total[rows_small:], [(depth, 3, nchips * cw_cols)])[0]
    grads["conv_w"] = lax.dynamic_slice_in_dim(conv_w_grad, chip * cw_cols, cw_cols, axis=2)

    ada_rows = depth * 6 * d // _LANES
    d_ada_all = g3.reshape(8, -1, _LANES)[:, :ada_rows].reshape(8, depth, 6 * d)
    d_sel = lax.dynamic_slice_in_dim(jnp.transpose(d_ada_all, (1, 0, 2)), chip * ada_cols, ada_cols, axis=2)
    grads["w_ada"] = _wada_grad(jnp.transpose(c_act), d_sel, "w_ada_grad")

    delta, new_m, new_v = {}, {}, {}
    for n in _WEIGHTS:
        shp = wts[n].shape
        view = (lambda t: t) if len(shp) == 3 else (lambda t: t.reshape(1, -1, shp[-1]))
        outs = _adamw(view(wts[n]), view(grads[n]), view(mom[n]), view(var[n]), f"adamw_{n}")
        delta[n], new_m[n], new_v[n] = [t.reshape(shp) for t in outs]

    for table in (grads, delta, new_m, new_v):
        table["w_in"] = jnp.swapaxes(table["w_in"], 1, 2)
    return (loss, grad_x[None], *[grads[n] for n in _WEIGHTS], *[delta[n] for n in _WEIGHTS],
            *[new_m[n] for n in _WEIGHTS], *[new_v[n] for n in _WEIGHTS])
```

```python
import functools
import math

import jax
import jax.numpy as jnp
from jax import lax
from jax.experimental import pallas as pl
from jax.experimental.pallas import tpu as pltpu

F32 = jnp.float32
BF16 = jnp.bfloat16
EPS = 1e-6
NEG = -1e30
WINDOW = 128
HEAD_DIM = 64
KV_RATIO = 8
SSM_GROUP = 16
STATE = 64
PIECE = 128
PSTATES = 512
DEPTH = 2
ADAM_LR, ADAM_B1, ADAM_B2, ADAM_EPS, ADAM_WD, ADAM_STEP = 0.001, 0.9, 0.999, 1e-08, 0.01, 10
MIB = 1024 * 1024
_MATMUL_VMEM_BUDGET = 40 * MIB
MESH = pl.DeviceIdType.MESH


def _cparams(sem=None, vmem_mib=48):
    return pltpu.CompilerParams(dimension_semantics=sem, vmem_limit_bytes=vmem_mib * MIB)


def _gelu(x):
    c = math.sqrt(2.0 / math.pi)
    return 0.5 * x * (1.0 + jnp.tanh(c * (x + 0.044715 * (x * x * x))))


def _gelu_and_grad(x):
    c = math.sqrt(2.0 / math.pi)
    x2 = x * x
    t = jnp.tanh(c * (x + 0.044715 * (x2 * x)))
    half = 0.5 * (1.0 + t)
    return x * half, half + 0.5 * x * (1.0 - t * t) * c * (1.0 + 3.0 * 0.044715 * x2)


def _gelu_grad(x):
    return _gelu_and_grad(x)[1]


def _sigmoid(x):
    return 1.0 / (1.0 + jnp.exp(-x))


def _matmul(a, b, *, m, n, k, tm, tn, tk, out_dtype, name, ta=False, tb=False, a_stack=0, b_stack=0, o_stack=0):
    assert m % tm == 0 and n % tn == 0 and k % tk == 0, (name, m, n, k, tm, tn, tk)
    nk = k // tk

    if a_stack:
        assert not ta and (k // a_stack) % tk == 0
        per = (k // a_stack) // tk
        a_spec = pl.BlockSpec((None, tm, tk), lambda i, j, kk: (kk // per, i, kk % per))
    elif ta:
        a_spec = pl.BlockSpec((tk, tm), lambda i, j, kk: (kk, i))
    else:
        a_spec = pl.BlockSpec((tm, tk), lambda i, j, kk: (i, kk))
    if b_stack and tb:
        perb = (k // b_stack) // tk
        b_spec = pl.BlockSpec((None, tn, tk), lambda i, j, kk: (kk // perb, j, kk % perb))
    elif b_stack:
        perb = (n // b_stack) // tn
        b_spec = pl.BlockSpec((None, tk, tn), lambda i, j, kk: (j // perb, kk, j % perb))
    elif tb:
        b_spec = pl.BlockSpec((tn, tk), lambda i, j, kk: (j, kk))
    else:
        b_spec = pl.BlockSpec((tk, tn), lambda i, j, kk: (kk, j))
    if o_stack:
        pero = (n // o_stack) // tn
        o_spec = pl.BlockSpec((None, tm, tn), lambda i, j, kk: (j // pero, i, j % pero))
        o_shape = jax.ShapeDtypeStruct((o_stack, m, n // o_stack), out_dtype)
    else:
        o_spec = pl.BlockSpec((tm, tn), lambda i, j, kk: (i, j))
        o_shape = jax.ShapeDtypeStruct((m, n), out_dtype)
    dims = (((0 if ta else 1,), (1 if tb else 0,)), ((), ()))

    def body(a_ref, b_ref, o_ref, *acc):
        p = lax.dot_general(a_ref[...].astype(BF16), b_ref[...].astype(BF16), dims, preferred_element_type=F32)
        if nk == 1:
            o_ref[...] = p.astype(o_ref.dtype)
        else:
            acc_ref = acc[0]
            kk = pl.program_id(2)

            @pl.when(kk == 0)
            def _():
                acc_ref[...] = p

            @pl.when(kk > 0)
            def _():
                acc_ref[...] += p

            @pl.when(kk == nk - 1)
            def _():
                o_ref[...] = acc_ref[...].astype(o_ref.dtype)

    return pl.pallas_call(
        body, name=name, grid=(m // tm, n // tn, nk), in_specs=[a_spec, b_spec], out_specs=o_spec, out_shape=o_shape,
        scratch_shapes=[] if nk == 1 else [pltpu.VMEM((tm, tn), F32)],
        compiler_params=_cparams(("parallel", "parallel", "arbitrary"), 56),
    )(a, b)


def _row(d):
    return pl.BlockSpec((1, d), lambda i: (0, 0))


def _tok(tm, d):
    return pl.BlockSpec((tm, d), lambda i: (i, 0))


def _pre_fwd(x, g, sc, sh, name):
    s, d = x.shape
    tm = min(256, s)

    def body(x_ref, g_ref, sc_ref, sh_ref, h_ref):
        xv = x_ref[...]
        r = lax.rsqrt(jnp.mean(xv * xv, axis=-1, keepdims=True) + EPS)
        h_ref[...] = (((xv * r) * g_ref[...]) * (1.0 + sc_ref[...]) + sh_ref[...]).astype(BF16)

    return pl.pallas_call(
        body, name=name, grid=(s // tm,), in_specs=[_tok(tm, d), _row(d), _row(d), _row(d)], out_specs=_tok(tm, d),
        out_shape=jax.ShapeDtypeStruct((s, d), BF16), compiler_params=_cparams(("parallel",)),
    )(x, g, sc, sh)


def _post_fwd(x, o, g, gt, name):
    s, d = x.shape
    tm = min(256, s)

    def body(x_ref, o_ref, g_ref, gt_ref, y_ref):
        ov = o_ref[...]
        r = lax.rsqrt(jnp.mean(ov * ov, axis=-1, keepdims=True) + EPS)
        y_ref[...] = x_ref[...] + (1.0 + gt_ref[...]) * ((ov * r) * g_ref[...])

    return pl.pallas_call(
        body, name=name, grid=(s // tm,), in_specs=[_tok(tm, d), _tok(tm, d), _row(d), _row(d)], out_specs=_tok(tm, d),
        out_shape=jax.ShapeDtypeStruct((s, d), F32), compiler_params=_cparams(("parallel",)),
    )(x, o, g, gt)


def _post_bwd(dxo, o, g, gt, name):
    s, d = o.shape
    tm = min(256, s)

    def body(dx_ref, o_ref, g_ref, gt_ref, do_ref, dgt_ref, dg_ref):
        i = pl.program_id(0)
        dx = dx_ref[...]
        ov = o_ref[...]
        gv = g_ref[...]
        r = lax.rsqrt(jnp.mean(ov * ov, axis=-1, keepdims=True) + EPS)
        oh = ov * r
        dn = dx * (1.0 + gt_ref[...])
        e = dn * gv
        do_ref[...] = (r * (e - oh * jnp.mean(e * oh, axis=-1, keepdims=True))).astype(BF16)
        p_gt = jnp.sum(dx * (oh * gv), axis=0, keepdims=True)
        p_g = jnp.sum(dn * oh, axis=0, keepdims=True)

        @pl.when(i == 0)
        def _():
            dgt_ref[...] = p_gt
            dg_ref[...] = p_g

        @pl.when(i > 0)
        def _():
            dgt_ref[...] += p_gt
            dg_ref[...] += p_g

    row = jax.ShapeDtypeStruct((1, d), F32)
    return pl.pallas_call(
        body, name=name, grid=(s // tm,), in_specs=[_tok(tm, d), _tok(tm, d), _row(d), _row(d)],
        out_specs=(_tok(tm, d), _row(d), _row(d)), out_shape=(jax.ShapeDtypeStruct((s, d), BF16), row, row),
        compiler_params=_cparams(("arbitrary",)),
    )(dxo, o, g, gt)


def _pre_bwd(dres, dh, x, g, sc, name):
    s, d = x.shape
    tm = min(256, s)

    def body(dres_ref, dh_ref, x_ref, g_ref, sc_ref, dx_ref, dsh_ref, dsc_ref, dg_ref):
        i = pl.program_id(0)
        dh_v = dh_ref[...]
        xv = x_ref[...]
        gv = g_ref[...]
        one_sc = 1.0 + sc_ref[...]
        r = lax.rsqrt(jnp.mean(xv * xv, axis=-1, keepdims=True) + EPS)
        xh = xv * r
        e = dh_v * one_sc * gv
        dx_ref[...] = dres_ref[...] + r * (e - xh * jnp.mean(e * xh, axis=-1, keepdims=True))
        p_sh = jnp.sum(dh_v, axis=0, keepdims=True)
        p_sc = jnp.sum(dh_v * (xh * gv), axis=0, keepdims=True)
        p_g = jnp.sum(dh_v * one_sc * xh, axis=0, keepdims=True)

        @pl.when(i == 0)
        def _():
            dsh_ref[...] = p_sh
            dsc_ref[...] = p_sc
            dg_ref[...] = p_g

        @pl.when(i > 0)
        def _():
            dsh_ref[...] += p_sh
            dsc_ref[...] += p_sc
            dg_ref[...] += p_g

    row = jax.ShapeDtypeStruct((1, d), F32)
    return pl.pallas_call(
        body, name=name, grid=(s // tm,), in_specs=[_tok(tm, d), _tok(tm, d), _tok(tm, d), _row(d), _row(d)],
        out_specs=(_tok(tm, d), _row(d), _row(d), _row(d)), out_shape=(jax.ShapeDtypeStruct((s, d), F32), row, row, row),
        compiler_params=_cparams(("arbitrary",)),
    )(dres, dh, x, g, sc)


def _loss_head(y, tgt, name):
    s, d = y.shape
    tm = min(256, s)

    def body(y_ref, t_ref, dy_ref, l_ref):
        i = pl.program_id(0)
        err = y_ref[...] - t_ref[...]
        dy_ref[...] = err * (1.0 / d)
        part = jnp.zeros((1, 128), F32) + jnp.sum(err * err)

        @pl.when(i == 0)
        def _():
            l_ref[...] = part

        @pl.when(i > 0)
        def _():
            l_ref[...] += part

    return pl.pallas_call(
        body, name=name, grid=(s // tm,), in_specs=[_tok(tm, d), _tok(tm, d)],
        out_specs=(_tok(tm, d), pl.BlockSpec((1, 128), lambda i: (0, 0))),
        out_shape=(jax.ShapeDtypeStruct((s, d), F32), jax.ShapeDtypeStruct((1, 128), F32)),
        compiler_params=_cparams(("arbitrary",)),
    )(y, tgt)


def _gelu_fwd(y, name):
    s, u = y.shape
    tm = min(512, s)

    def body(y_ref, z_ref):
        z_ref[...] = _gelu(y_ref[...]).astype(BF16)

    return pl.pallas_call(
        body, name=name, grid=(s // tm,), in_specs=[_tok(tm, u)], out_specs=_tok(tm, u),
        out_shape=jax.ShapeDtypeStruct((s, u), BF16), compiler_params=_cparams(("parallel",)),
    )(y)


def _merge_fwd(attn, y, gl, ga, gs, name):
    s, aw = attn.shape
    uw = y.shape[1]
    tm = min(256, s)

    def body(a_ref, y_ref, gl_ref, ga_ref, gs_ref, m_ref):
        av = a_ref[...]
        ra = lax.rsqrt(jnp.mean(av * av, axis=-1, keepdims=True) + EPS)
        m_ref[:, :aw] = ((av * ra) * ga_ref[...]).astype(BF16)
        ssm = _gelu(y_ref[...]) * _sigmoid(gl_ref[...])
        rs = lax.rsqrt(jnp.mean(ssm * ssm, axis=-1, keepdims=True) + EPS)
        m_ref[:, aw:] = ((ssm * rs) * gs_ref[...]).astype(BF16)

    return pl.pallas_call(
        body, name=name, grid=(s // tm,), in_specs=[_tok(tm, aw), _tok(tm, uw), _tok(tm, uw), _row(aw), _row(uw)],
        out_specs=_tok(tm, aw + uw), out_shape=jax.ShapeDtypeStruct((s, aw + uw), BF16),
        compiler_params=_cparams(("parallel",)),
    )(attn, y, gl, ga, gs)


def _merge_bwd(dm, attn, y, gl, ga, gs, name):
    s, aw = attn.shape
    uw = y.shape[1]
    tm = min(256, s)

    def body(dm_ref, a_ref, y_ref, gl_ref, ga_ref, gs_ref, da_ref, dgl_ref, dz_ref, dga_ref, dgs_ref):
        i = pl.program_id(0)
        av = a_ref[...]
        dma = dm_ref[:, :aw]
        ra = lax.rsqrt(jnp.mean(av * av, axis=-1, keepdims=True) + EPS)
        ah = av * ra
        e = dma * ga_ref[...]
        da_ref[...] = (ra * (e - ah * jnp.mean(e * ah, axis=-1, keepdims=True))).astype(BF16)
        p_ga = jnp.sum(dma * ah, axis=0, keepdims=True)

        z = _gelu(y_ref[...])
        sig = _sigmoid(gl_ref[...])
        ssm = z * sig
        dms = dm_ref[:, aw:]
        rs = lax.rsqrt(jnp.mean(ssm * ssm, axis=-1, keepdims=True) + EPS)
        sh = ssm * rs
        e2 = dms * gs_ref[...]
        dssm = rs * (e2 - sh * jnp.mean(e2 * sh, axis=-1, keepdims=True))
        dz_ref[...] = dssm * sig
        dgl_ref[...] = (dssm * z * sig * (1.0 - sig)).astype(BF16)
        p_gs = jnp.sum(dms * sh, axis=0, keepdims=True)

        @pl.when(i == 0)
        def _():
            dga_ref[...] = p_ga
            dgs_ref[...] = p_gs

        @pl.when(i > 0)
        def _():
            dga_ref[...] += p_ga
            dgs_ref[...] += p_gs

    return pl.pallas_call(
        body, name=name, grid=(s // tm,),
        in_specs=[_tok(tm, aw + uw), _tok(tm, aw), _tok(tm, uw), _tok(tm, uw), _row(aw), _row(uw)],
        out_specs=(_tok(tm, aw), _tok(tm, uw), _tok(tm, uw), _row(aw), _row(uw)),
        out_shape=(jax.ShapeDtypeStruct((s, aw), BF16), jax.ShapeDtypeStruct((s, uw), BF16), jax.ShapeDtypeStruct((s, uw), F32),
                   jax.ShapeDtypeStruct((1, aw), F32), jax.ShapeDtypeStruct((1, uw), F32)),
        compiler_params=_cparams(("arbitrary",)),
    )(dm, attn, y, gl, ga, gs)


def _gelu_bwd(dzd, dz2, y, name):
    s, u = y.shape
    tm = min(512, s)

    def body(a_ref, b_ref, y_ref, o_ref):
        o_ref[...] = (a_ref[...] + b_ref[...]) * _gelu_grad(y_ref[...])

    return pl.pallas_call(
        body, name=name, grid=(s // tm,), in_specs=[_tok(tm, u), _tok(tm, u), _tok(tm, u)], out_specs=_tok(tm, u),
        out_shape=jax.ShapeDtypeStruct((s, u), F32), compiler_params=_cparams(("parallel",)),
    )(dzd, dz2, y)


def _attn_scores(qh, kb, sink, valid):
    s = lax.dot_general(qh, kb, (((1,), (1,)), ((), ())), preferred_element_type=F32) * (HEAD_DIM ** -0.5)
    s = jnp.where(valid, s, NEG)
    m = jnp.maximum(jnp.max(s, axis=-1, keepdims=True), sink)
    e = jnp.exp(s - m)
    esink = jnp.exp(sink - m)
    den = jnp.sum(e, axis=-1, keepdims=True) + esink
    return e / den, esink / den


def _attn_valid(i):
    qi = lax.broadcasted_iota(jnp.int32, (KV_RATIO * WINDOW, 2 * WINDOW), 0) % WINDOW
    kj = lax.broadcasted_iota(jnp.int32, (KV_RATIO * WINDOW, 2 * WINDOW), 1)
    return (kj > qi) & (kj <= qi + WINDOW) & ((kj >= WINDOW) | (i > 0))


def _stack_heads(ref, hk):
    return jnp.concatenate([ref[:, (hk * KV_RATIO + g) * HEAD_DIM:(hk * KV_RATIO + g + 1) * HEAD_DIM] for g in range(KV_RATIO)], axis=0)


def _stack_sinks(sink_ref, hk):
    return jnp.concatenate([jnp.full((WINDOW, 1), sink_ref[hk * KV_RATIO + g], F32) for g in range(KV_RATIO)], axis=0)


def _band(kvp, kvc, off):
    return jnp.concatenate([kvp[:, off:off + HEAD_DIM], kvc[:, off:off + HEAD_DIM]], axis=0)


def _attn_specs(aw, uw, kvw):
    qblk = uw // aw
    kvblk = (uw + aw) // (2 * kvw)
    assert uw % aw == 0 and (uw + aw) % (2 * kvw) == 0
    return [
        pl.BlockSpec(memory_space=pltpu.SMEM),
        pl.BlockSpec((WINDOW, aw), lambda i: (i, qblk)),
        pl.BlockSpec((WINDOW, 2 * kvw), lambda i: (i, kvblk)),
        pl.BlockSpec((WINDOW, 2 * kvw), lambda i: (jnp.maximum(i - 1, 0), kvblk)),
    ]


def _attn_fwd(proj, sinks, aw, uw, name):
    s = proj.shape[0]
    nq = aw // HEAD_DIM
    nkv = nq // KV_RATIO
    kvw = nkv * HEAD_DIM

    def body(sink_ref, q_ref, kvc_ref, kvp_ref, o_ref):
        valid = _attn_valid(pl.program_id(0))[:WINDOW]
        q = q_ref[...]
        kvc = kvc_ref[...]
        kvp = kvp_ref[...]
        for hk in range(nkv):
            kb = _band(kvp, kvc, hk * HEAD_DIM)
            vb = _band(kvp, kvc, kvw + hk * HEAD_DIM)
            for g in range(KV_RATIO):
                hq = hk * KV_RATIO + g
                p, _ = _attn_scores(q[:, hq * HEAD_DIM:(hq + 1) * HEAD_DIM], kb, sink_ref[hq], valid)
                o_ref[:, hq * HEAD_DIM:(hq + 1) * HEAD_DIM] = jnp.dot(p.astype(BF16), vb, preferred_element_type=F32)

    return pl.pallas_call(
        body, name=name, grid=(s // WINDOW,), in_specs=_attn_specs(aw, uw, kvw),
        out_specs=pl.BlockSpec((WINDOW, aw), lambda i: (i, 0)), out_shape=jax.ShapeDtypeStruct((s, aw), F32),
        compiler_params=_cparams(("parallel",)),
    )(sinks, proj, proj, proj)


def _attn_bwd(proj, sinks, attn, dattn, aw, uw, name):
    s = proj.shape[0]
    nq = aw // HEAD_DIM
    nkv = nq // KV_RATIO
    kvw = nkv * HEAD_DIM
    hd = HEAD_DIM

    def body(sink_ref, q_ref, kvc_ref, kvp_ref, o_ref, do_ref, dq_ref, dc_ref, dp_ref, ds_ref):
        i = pl.program_id(0)
        valid = _attn_valid(i)
        kvc = kvc_ref[...]
        kvp = kvp_ref[...]
        lane = lax.broadcasted_iota(jnp.int32, (1, nq), 1)
        dsink = jnp.zeros((1, nq), F32)
        for hk in range(nkv):
            kb = _band(kvp, kvc, hk * hd)
            vb = _band(kvp, kvc, kvw + hk * hd)
            qs = _stack_heads(q_ref, hk)
            dos = _stack_heads(do_ref, hk)
            p, psink = _attn_scores(qs, kb, _stack_sinks(sink_ref, hk), valid)
            delta = jnp.sum(dos.astype(F32) * _stack_heads(o_ref, hk), axis=-1, keepdims=True)
            dpv = lax.dot_general(dos, vb, (((1,), (1,)), ((), ())), preferred_element_type=F32)
            dsb = (p * (dpv - delta) * (hd ** -0.5)).astype(BF16)
            dqs = jnp.dot(dsb, kb, preferred_element_type=F32).astype(BF16)
            dkb = lax.dot_general(dsb, qs, (((0,), (0,)), ((), ())), preferred_element_type=F32)
            dvb = lax.dot_general(p.astype(BF16), dos, (((0,), (0,)), ((), ())), preferred_element_type=F32)
            sink_term = psink * delta
            for g in range(KV_RATIO):
                hq = hk * KV_RATIO + g
                dq_ref[:, hq * hd:(hq + 1) * hd] = dqs[g * WINDOW:(g + 1) * WINDOW]
                dsink = dsink + jnp.where(lane == hq, -jnp.sum(sink_term[g * WINDOW:(g + 1) * WINDOW]), 0.0)
            dp_ref[:, hk * hd:(hk + 1) * hd] = dkb[:WINDOW]
            dc_ref[:, hk * hd:(hk + 1) * hd] = dkb[WINDOW:]
            dp_ref[:, kvw + hk * hd:kvw + (hk + 1) * hd] = dvb[:WINDOW]
            dc_ref[:, kvw + hk * hd:kvw + (hk + 1) * hd] = dvb[WINDOW:]

        @pl.when(i == 0)
        def _():
            ds_ref[...] = dsink

        @pl.when(i > 0)
        def _():
            ds_ref[...] += dsink

    blk_a = pl.BlockSpec((WINDOW, aw), lambda i: (i, 0))
    blk_kv = pl.BlockSpec((WINDOW, 2 * kvw), lambda i: (i, 0))
    return pl.pallas_call(
        body, name=name, grid=(s // WINDOW,), in_specs=_attn_specs(aw, uw, kvw) + [blk_a, blk_a],
        out_specs=(blk_a, blk_kv, blk_kv, pl.BlockSpec((1, nq), lambda i: (0, 0))),
        out_shape=(jax.ShapeDtypeStruct((s, aw), BF16), jax.ShapeDtypeStruct((s, 2 * kvw), F32),
                   jax.ShapeDtypeStruct((s, 2 * kvw), F32), jax.ShapeDtypeStruct((1, nq), F32)),
        compiler_params=_cparams(("arbitrary",)),
    )(sinks, proj, proj, proj, attn, dattn)


def _assemble_dproj(du, dq, dkv_cur, dkv_prev, name):
    s, uw = du.shape
    aw = dq.shape[1]
    kv2 = dkv_cur.shape[1]
    nb = s // WINDOW

    def body(du_ref, dq_ref, dc_ref, dp_ref, o_ref):
        i = pl.program_id(0)
        o_ref[:, :uw] = du_ref[...].astype(BF16)
        o_ref[:, uw:uw + aw] = dq_ref[...]
        nxt = jnp.where(i < nb - 1, 1.0, 0.0)
        o_ref[:, uw + aw:] = (dc_ref[...] + nxt * dp_ref[...]).astype(BF16)

    return pl.pallas_call(
        body, name=name, grid=(nb,),
        in_specs=[_tok(WINDOW, uw), _tok(WINDOW, aw), _tok(WINDOW, kv2),
                  pl.BlockSpec((WINDOW, kv2), lambda i: (jnp.minimum(i + 1, nb - 1), 0))],
        out_specs=_tok(WINDOW, uw + aw + kv2), out_shape=jax.ShapeDtypeStruct((s, uw + aw + kv2), BF16),
        compiler_params=_cparams(("parallel",)),
    )(du, dq, dkv_cur, dkv_prev)


def _zoh(lr, li, ls, btr, bti):
    dt = jnp.exp(ls)
    mag = jnp.exp(lr * dt)
    ang = li * dt
    ar = mag * jnp.cos(ang)
    ai = mag * jnp.sin(ang)
    den = lr * lr + li * li
    fr = ((ar - 1.0) * lr + ai * li) / den
    fi = (ai * lr - (ar - 1.0) * li) / den
    return ar, ai, fr[None] * btr - fi[None] * bti, fr[None] * bti + fi[None] * btr


def _ssm_prep(lr, li, ls, btr, bti, name):
    def body(lr_ref, li_ref, ls_ref, btr_ref, bti_ref, ar_ref, ai_ref, bbr_ref, bbi_ref):
        ar, ai, bbr, bbi = _zoh(lr_ref[...], li_ref[...], ls_ref[...], btr_ref[...], bti_ref[...])
        ar_ref[...] = ar
        ai_ref[...] = ai
        bbr_ref[...] = bbr
        bbi_ref[...] = bbi

    s2 = jax.ShapeDtypeStruct(lr.shape, F32)
    s3 = jax.ShapeDtypeStruct(btr.shape, F32)
    return pl.pallas_call(body, name=name, out_shape=(s2, s2, s3, s3))(lr, li, ls, btr, bti)


def _ssm_prep_bwd(lr, li, ls, btr, bti, dar, dai, dbbr, dbbi, name):
    def body(lr_ref, li_ref, ls_ref, btr_ref, bti_ref, dar_ref, dai_ref, dbbr_ref, dbbi_ref, o1, o2, o3, o4, o5):
        _, vjp = jax.vjp(_zoh, lr_ref[...], li_ref[...], ls_ref[...], btr_ref[...], bti_ref[...])
        g = vjp((dar_ref[...], dai_ref[...], dbbr_ref[...], dbbi_ref[...]))
        for o, v in zip((o1, o2, o3, o4, o5), g):
            o[...] = v

    s2 = jax.ShapeDtypeStruct(lr.shape, F32)
    s3 = jax.ShapeDtypeStruct(btr.shape, F32)
    return pl.pallas_call(body, name=name, out_shape=(s2, s2, s2, s3, s3))(lr, li, ls, btr, bti, dar, dai, dbbr, dbbi)


def _state_tiles(ref):
    return [ref[:, cb * 128:(cb + 1) * 128] for cb in range(4)]


def _gather_rows(ref_re, ref_im, r, t):
    return jnp.concatenate([ref_re.at[cb][pl.ds(r, t, stride=8), :] for cb in range(4)]
                           + [ref_im.at[cb][pl.ds(r, t, stride=8), :] for cb in range(4)], axis=1)


def _scatter_rows(ref_re, ref_im, r, t, val):
    for cb in range(4):
        ref_re.at[cb][pl.ds(r, t, stride=8), :] = val[:, cb * 128:(cb + 1) * 128]
        ref_im.at[cb][pl.ds(r, t, stride=8), :] = val[:, PSTATES + cb * 128:PSTATES + (cb + 1) * 128]


def _ssm_fwd(proj, bp, cp, a_re, a_im, dvec, uw, name, t=128):
    s = proj.shape[0]
    npc = uw // PIECE
    assert npc == 8 and s % t == 0

    def body(u_ref, bp_ref, cp_ref, ar_ref, ai_ref, d_ref, y_ref, xr_ref, xi_ref, cr_ref, ci_ref):
        i = pl.program_id(0)

        @pl.when(i == 0)
        def _():
            cr_ref[...] = jnp.zeros_like(cr_ref)
            ci_ref[...] = jnp.zeros_like(ci_ref)

        for r in range(npc):
            bu = jnp.dot(u_ref[:, r * PIECE:(r + 1) * PIECE], bp_ref[r], preferred_element_type=F32)
            _scatter_rows(xr_ref, xi_ref, r, t, bu)
        ar = _state_tiles(ar_ref)
        ai = _state_tiles(ai_ref)

        def step(tt, carry):
            xr, xi = carry
            off = pl.multiple_of(tt * 8, 8)
            nr, ni = [], []
            for cb in range(4):
                vr = ar[cb] * xr[cb] - ai[cb] * xi[cb] + xr_ref[cb, pl.ds(off, 8), :]
                vi = ar[cb] * xi[cb] + ai[cb] * xr[cb] + xi_ref[cb, pl.ds(off, 8), :]
                xr_ref[cb, pl.ds(off, 8), :] = vr
                xi_ref[cb, pl.ds(off, 8), :] = vi
                nr.append(vr)
                ni.append(vi)
            return tuple(nr), tuple(ni)

        xr, xi = lax.fori_loop(0, t, step, (tuple(_state_tiles(cr_ref)), tuple(_state_tiles(ci_ref))), unroll=4)
        for cb in range(4):
            cr_ref[:, cb * 128:(cb + 1) * 128] = xr[cb]
            ci_ref[:, cb * 128:(cb + 1) * 128] = xi[cb]
        for r in range(npc):
            xs = _gather_rows(xr_ref, xi_ref, r, t).astype(BF16)
            y_ref[:, r * PIECE:(r + 1) * PIECE] = (
                jnp.dot(xs, cp_ref[r], preferred_element_type=F32)
                + d_ref[:, r * PIECE:(r + 1) * PIECE] * u_ref[:, r * PIECE:(r + 1) * PIECE].astype(F32))

    full3 = lambda shp: pl.BlockSpec(shp, lambda i: (0, 0, 0))
    full2 = lambda shp: pl.BlockSpec(shp, lambda i: (0, 0))
    xs_spec = pl.BlockSpec((4, t * 8, 128), lambda i: (0, i, 0))
    xs_shape = jax.ShapeDtypeStruct((4, s * 8, 128), F32)
    return pl.pallas_call(
        body, name=name, grid=(s // t,),
        in_specs=[pl.BlockSpec((t, uw), lambda i: (i, 0)), full3(bp.shape), full3(cp.shape), full2(a_re.shape), full2(a_im.shape),
                  full2(dvec.shape)],
        out_specs=(pl.BlockSpec((t, uw), lambda i: (i, 0)), xs_spec, xs_spec),
        out_shape=(jax.ShapeDtypeStruct((s, uw), F32), xs_shape, xs_shape),
        scratch_shapes=[pltpu.VMEM((8, PSTATES), F32), pltpu.VMEM((8, PSTATES), F32)],
        compiler_params=_cparams(("arbitrary",), 56),
    )(proj, bp, cp, a_re, a_im, dvec)


def _ssm_bwd(dy, proj, xs_re, xs_im, cpt, bpt, a_re, a_im, dvec, uw, name, t=128):
    s = proj.shape[0]
    npc = uw // PIECE
    nt = s // t
    assert npc == 8 and s % t == 0

    def body(dy_ref, u_ref, xr_ref, xi_ref, hr_ref, hi_ref, cpt_ref, bpt_ref, ar_ref, ai_ref, d_ref,
             du_ref, dbp_ref, dcp_ref, dar_ref, dai_ref, dd_ref, gr_ref, gi_ref, lr_ref, li_ref):
        i = pl.program_id(0)

        @pl.when(i == 0)
        def _():
            lr_ref[...] = jnp.zeros_like(lr_ref)
            li_ref[...] = jnp.zeros_like(li_ref)
            dbp_ref[...] = jnp.zeros_like(dbp_ref)
            dcp_ref[...] = jnp.zeros_like(dcp_ref)
            dar_ref[...] = jnp.zeros_like(dar_ref)
            dai_ref[...] = jnp.zeros_like(dai_ref)
            dd_ref[...] = jnp.zeros_like(dd_ref)

        dyb = dy_ref[...].astype(BF16)
        for r in range(npc):
            gx = jnp.dot(dyb[:, r * PIECE:(r + 1) * PIECE], cpt_ref[r], preferred_element_type=F32)
            _scatter_rows(gr_ref, gi_ref, r, t, gx)
        ar = _state_tiles(ar_ref)
        ai = _state_tiles(ai_ref)

        def adjoint(off, lam_r, lam_i, xpr, xpi, acc_r, acc_i):
            nr, ni, qr, qi = [], [], [], []
            for cb in range(4):
                vr = gr_ref[cb, pl.ds(off, 8), :] + ar[cb] * lam_r[cb] + ai[cb] * lam_i[cb]
                vi = gi_ref[cb, pl.ds(off, 8), :] + ar[cb] * lam_i[cb] - ai[cb] * lam_r[cb]
                gr_ref[cb, pl.ds(off, 8), :] = vr
                gi_ref[cb, pl.ds(off, 8), :] = vi
                nr.append(vr)
                ni.append(vi)
                qr.append(acc_r[cb] + vr * xpr[cb] + vi * xpi[cb])
                qi.append(acc_i[cb] + vi * xpr[cb] - vr * xpi[cb])
            return tuple(nr), tuple(ni), tuple(qr), tuple(qi)

        def step(j, carry):
            lam_r, lam_i, acc_r, acc_i = carry
            tt = t - 1 - j
            off = pl.multiple_of(tt * 8, 8)
            offp = pl.multiple_of(tt * 8 - 8, 8)
            xpr = [xr_ref[cb, pl.ds(offp, 8), :] for cb in range(4)]
            xpi = [xi_ref[cb, pl.ds(offp, 8), :] for cb in range(4)]
            return adjoint(off, lam_r, lam_i, xpr, xpi, acc_r, acc_i)

        zero4 = tuple(jnp.zeros((8, 128), F32) for _ in range(4))
        carry = lax.fori_loop(0, t - 1, step, (tuple(_state_tiles(lr_ref)), tuple(_state_tiles(li_ref)), zero4, zero4), unroll=4)
        has_prev = jnp.where(i < nt - 1, 1.0, 0.0)
        xpr = [hr_ref[cb] * has_prev for cb in range(4)]
        xpi = [hi_ref[cb] * has_prev for cb in range(4)]
        lam_r, lam_i, acc_r, acc_i = adjoint(0, carry[0], carry[1], xpr, xpi, carry[2], carry[3])
        for cb in range(4):
            lr_ref[:, cb * 128:(cb + 1) * 128] = lam_r[cb]
            li_ref[:, cb * 128:(cb + 1) * 128] = lam_i[cb]
            dar_ref[:, cb * 128:(cb + 1) * 128] += acc_r[cb]
            dai_ref[:, cb * 128:(cb + 1) * 128] += acc_i[cb]

        dyv = dy_ref[...]
        uv = u_ref[...]
        dd_ref[...] += jnp.sum(dyv * uv.astype(F32), axis=0, keepdims=True)
        for r in range(npc):
            lam = _gather_rows(gr_ref, gi_ref, r, t).astype(BF16)
            sl = slice(r * PIECE, (r + 1) * PIECE)
            du_ref[:, sl] = jnp.dot(lam, bpt_ref[r], preferred_element_type=F32) + d_ref[:, sl] * dyv[:, sl]
            dbp_ref[r] += lax.dot_general(uv[:, sl], lam, (((0,), (0,)), ((), ())), preferred_element_type=F32)
            xs = _gather_rows(xr_ref, xi_ref, r, t).astype(BF16)
            dcp_ref[r] += lax.dot_general(xs, dyb[:, sl], (((0,), (0,)), ((), ())), preferred_element_type=F32)

    rev = lambda i: (nt - 1 - i, 0)
    full3 = lambda shp: pl.BlockSpec(shp, lambda i: (0, 0, 0))
    full2 = lambda shp: pl.BlockSpec(shp, lambda i: (0, 0))
    xs_spec = pl.BlockSpec((4, t * 8, 128), lambda i: (0, nt - 1 - i, 0))
    halo_spec = pl.BlockSpec((4, 8, 128), lambda i: (0, jnp.maximum((nt - 1 - i) * t - 1, 0), 0))
    st = jax.ShapeDtypeStruct((8, PSTATES), F32)
    return pl.pallas_call(
        body, name=name, grid=(nt,),
        in_specs=[pl.BlockSpec((t, uw), rev), pl.BlockSpec((t, uw), rev), xs_spec, xs_spec, halo_spec, halo_spec,
                  full3(cpt.shape), full3(bpt.shape), full2(a_re.shape), full2(a_im.shape), full2(dvec.shape)],
        out_specs=(pl.BlockSpec((t, uw), rev), full3((npc, PIECE, 2 * PSTATES)), full3((npc, 2 * PSTATES, PIECE)),
                   full2((8, PSTATES)), full2((8, PSTATES)), full2((1, uw))),
        out_shape=(jax.ShapeDtypeStruct((s, uw), F32), jax.ShapeDtypeStruct((npc, PIECE, 2 * PSTATES), F32),
                   jax.ShapeDtypeStruct((npc, 2 * PSTATES, PIECE), F32), st, st, jax.ShapeDtypeStruct((1, uw), F32)),
        scratch_shapes=[pltpu.VMEM((4, t * 8, 128), F32), pltpu.VMEM((4, t * 8, 128), F32),
                        pltpu.VMEM((8, PSTATES), F32), pltpu.VMEM((8, PSTATES), F32)],
        compiler_params=_cparams(("arbitrary",), 56),
    )(dy, proj, xs_re, xs_im, xs_re, xs_im, cpt, bpt, a_re, a_im, dvec)


_RC = 16
_LC = 128


def _conv3_block(xv, halo, w, b, row):
    h6, h7 = halo[_RC - 2:_RC - 1], halo[_RC - 1:_RC]
    x1 = jnp.where(row == 0, h7, pltpu.roll(xv, 1, 0))
    x2 = jnp.where(row == 0, h6, jnp.where(row == 1, h7, pltpu.roll(xv, 2, 0)))
    return ((b + x2 * w[0:1]) + x1 * w[1:2]) + xv * w[2:3], (x2, x1, xv)


def _ffn_tiles(s, f):
    tm = min(256, s)
    tn = f // 4 if (f // 4) % _LC == 0 else f
    assert tm % _RC == 0 and tn % _LC == 0
    return tm, tn


def _conv_glu_fwd(up0, cw, cb, name):
    _, s, f = up0.shape
    tm, tn = _ffn_tiles(s, f)
    hb = tm // _RC

    def body(x_ref, h_ref, w_ref, b_ref, a_ref):
        first = jnp.where(pl.program_id(0) > 0, 1.0, 0.0)
        row = lax.broadcasted_iota(jnp.int32, (tm, tn), 0)
        ups = [_conv3_block(x_ref[p].astype(F32), h_ref[p].astype(F32) * first, w_ref[p], b_ref[p], row)[0] for p in range(2)]
        a_ref[...] = (_gelu(ups[1]) * ups[0]).astype(BF16)

    return pl.pallas_call(
        body, name=name, grid=(s // tm, f // tn),
        in_specs=[pl.BlockSpec((2, tm, tn), lambda i, j: (0, i, j)),
                  pl.BlockSpec((2, _RC, tn), lambda i, j: (0, jnp.maximum(i * hb - 1, 0), j)),
                  pl.BlockSpec((2, 3, tn), lambda i, j: (0, 0, j)), pl.BlockSpec((2, 1, tn), lambda i, j: (0, 0, j))],
        out_specs=pl.BlockSpec((tm, tn), lambda i, j: (i, j)), out_shape=jax.ShapeDtypeStruct((s, f), BF16),
        compiler_params=_cparams(("parallel", "parallel")),
    )(up0, up0, cw, cb)


def _ffn_bwd_gate(da, up0, cw, cb, name):
    _, s, f = up0.shape
    tm, tn = _ffn_tiles(s, f)
    hb = tm // _RC

    def body(da_ref, x_ref, h_ref, w_ref, b_ref, d_ref, dw_ref, db_ref):
        i = pl.program_id(1)
        first = jnp.where(i > 0, 1.0, 0.0)

        @pl.when(i == 0)
        def _():
            dw_ref[...] = jnp.zeros_like(dw_ref)
            db_ref[...] = jnp.zeros_like(db_ref)

        row = lax.broadcasted_iota(jnp.int32, (tm, tn), 0)
        ups, taps = [], []
        for p in range(2):
            up, tap = _conv3_block(x_ref[p].astype(F32), h_ref[p].astype(F32) * first, w_ref[p], b_ref[p], row)
            ups.append(up)
            taps.append(tap)
        dav = da_ref[...]
        gate, dgate = _gelu_and_grad(ups[1])
        douts = (dav * gate, dav * ups[0] * dgate)
        for p in range(2):
            d_ref[p] = douts[p].astype(BF16)
            db_ref[p] += jnp.sum(douts[p], axis=0, keepdims=True)
            for kk in range(3):
                dw_ref[p, kk:kk + 1, :] += jnp.sum(douts[p] * taps[p][kk], axis=0, keepdims=True)

    return pl.pallas_call(
        body, name=name, grid=(f // tn, s // tm),
        in_specs=[pl.BlockSpec((tm, tn), lambda j, i: (i, j)), pl.BlockSpec((2, tm, tn), lambda j, i: (0, i, j)),
                  pl.BlockSpec((2, _RC, tn), lambda j, i: (0, jnp.maximum(i * hb - 1, 0), j)),
                  pl.BlockSpec((2, 3, tn), lambda j, i: (0, 0, j)), pl.BlockSpec((2, 1, tn), lambda j, i: (0, 0, j))],
        out_specs=(pl.BlockSpec((2, tm, tn), lambda j, i: (0, i, j)), pl.BlockSpec((2, 3, tn), lambda j, i: (0, 0, j)),
                   pl.BlockSpec((2, 1, tn), lambda j, i: (0, 0, j))),
        out_shape=(jax.ShapeDtypeStruct((2, s, f), BF16), jax.ShapeDtypeStruct((2, 3, f), F32), jax.ShapeDtypeStruct((2, 1, f), F32)),
        compiler_params=_cparams(("parallel", "arbitrary")),
    )(da, up0, up0, cw, cb)


def _conv_bwd(dup, cw, name):
    _, s, f = dup.shape
    tm, tn = _ffn_tiles(s, f)
    hb = tm // _RC
    nb = s // tm

    def body(d_ref, h_ref, w_ref, o_ref):
        last = jnp.where(pl.program_id(0) < nb - 1, 1.0, 0.0)
        row = lax.broadcasted_iota(jnp.int32, (tm, tn), 0)
        for p in range(2):
            d = d_ref[p].astype(F32)
            h = h_ref[p].astype(F32) * last
            d1 = jnp.where(row == tm - 1, h[0:1], pltpu.roll(d, tm - 1, 0))
            d2 = jnp.where(row == tm - 1, h[1:2], jnp.where(row == tm - 2, h[0:1], pltpu.roll(d, tm - 2, 0)))
            w = w_ref[p]
            o_ref[p] = (d * w[2:3] + d1 * w[1:2] + d2 * w[0:1]).astype(BF16)

    return pl.pallas_call(
        body, name=name, grid=(nb, f // tn),
        in_specs=[pl.BlockSpec((2, tm, tn), lambda i, j: (0, i, j)),
                  pl.BlockSpec((2, _RC, tn), lambda i, j: (0, jnp.minimum((i + 1) * hb, s // _RC - 1), j)),
                  pl.BlockSpec((2, 3, tn), lambda i, j: (0, 0, j))],
        out_specs=pl.BlockSpec((2, tm, tn), lambda i, j: (0, i, j)), out_shape=jax.ShapeDtypeStruct((2, s, f), BF16),
        compiler_params=_cparams(("parallel", "parallel")),
    )(dup, dup, cw)


def _ada_part(c_all, w_ada, name):
    nb, d = c_all.shape
    depth, _, cols = w_ada.shape
    tn = 1024 if cols % 1024 == 0 else cols

    def body(c_ref, w_ref, o_ref, ca_ref):
        cv = c_ref[...]
        ca = cv * _sigmoid(cv)
        ca_ref[...] = ca
        o_ref[...] = jnp.dot(ca.astype(BF16), w_ref[...].astype(BF16), preferred_element_type=F32)

    return pl.pallas_call(
        body, name=name, grid=(depth, cols // tn),
        in_specs=[pl.BlockSpec((nb, d), lambda l, j: (0, 0)), pl.BlockSpec((None, d, tn), lambda l, j: (l, 0, j))],
        out_specs=(pl.BlockSpec((None, nb, tn), lambda l, j: (l, 0, j)), pl.BlockSpec((nb, d), lambda l, j: (0, 0))),
        out_shape=(jax.ShapeDtypeStruct((depth, nb, cols), F32), jax.ShapeDtypeStruct((nb, d), F32)),
        compiler_params=_cparams(("arbitrary", "arbitrary")),
    )(c_all, w_ada)


def _ada_select(gath, b_ada, name):
    depth, n6 = b_ada.shape
    cols = gath.shape[1]

    def body(g_ref, b_ref, o_ref):
        me = 4 * lax.axis_index("x") + 2 * lax.axis_index("y") + lax.axis_index("c")
        for l in range(depth):
            for j in range(n6 // cols):
                row = (2 * j) * (8 * depth) + l * 8 + me
                o_ref[l:l + 1, j * cols:(j + 1) * cols] = g_ref[pl.ds(row, 1), :] + b_ref[l:l + 1, j * cols:(j + 1) * cols]

    return pl.pallas_call(body, name=name, out_shape=jax.ShapeDtypeStruct((depth, n6), F32))(gath, b_ada)


def _wada_grad(ca_t, d_sel, name):
    d, nb = ca_t.shape
    depth, _, cols = d_sel.shape
    tm = min(256, d)

    def body(a_ref, g_ref, o_ref):
        acc = a_ref[:, 0:1] * g_ref[0:1, :]
        for b in range(1, nb):
            acc = acc + a_ref[:, b:b + 1] * g_ref[b:b + 1, :]
        o_ref[...] = acc

    return pl.pallas_call(
        body, name=name, grid=(depth, d // tm),
        in_specs=[pl.BlockSpec((tm, nb), lambda l, i: (i, 0)), pl.BlockSpec((None, nb, cols), lambda l, i: (l, 0, 0))],
        out_specs=pl.BlockSpec((None, tm, cols), lambda l, i: (l, i, 0)), out_shape=jax.ShapeDtypeStruct((depth, d, cols), F32),
        compiler_params=_cparams(("parallel", "parallel")),
    )(ca_t, d_sel)


def _block_rows(r, c):
    tr = r
    for cand in (2048, 1024, 512, 256, 128, 64, 32, 16, 8):
        if r % cand == 0 and cand * c * 4 <= MIB:
            tr = cand
            break
    else:
        for cand in (8, 16, 32):
            if r % cand == 0:
                tr = cand
                break
    return tr


def _adamw(w, g, m, v, name):
    nl, r, c = w.shape
    tr = _block_rows(r, c)
    c1 = 1.0 - ADAM_B1 ** ADAM_STEP
    c2 = 1.0 - ADAM_B2 ** ADAM_STEP

    def body(w_ref, g_ref, m_ref, v_ref, d_ref, nm_ref, nv_ref):
        gv = g_ref[...]
        nm = ADAM_B1 * m_ref[...] + (1.0 - ADAM_B1) * gv
        nv = ADAM_B2 * v_ref[...] + (1.0 - ADAM_B2) * (gv * gv)
        d_ref[...] = -ADAM_LR * ((nm / c1) / (jnp.sqrt(nv / c2) + ADAM_EPS) + ADAM_WD * w_ref[...])
        nm_ref[...] = nm
        nv_ref[...] = nv

    spec = pl.BlockSpec((None, tr, c), lambda l, i: (l, i, 0))
    shp = jax.ShapeDtypeStruct((nl, r, c), F32)
    return pl.pallas_call(
        body, name=name, grid=(nl, r // tr), in_specs=[spec] * 4, out_specs=(spec,) * 3, out_shape=(shp,) * 3,
        compiler_params=_cparams(("parallel", "parallel")),
    )(w, g, m, v)


def _sum_slots(x, nslots, name, out_dtype=F32):
    r = x.shape[0] // nslots
    c = x.shape[1]
    tr = _block_rows(r, c)
    nbk = r // tr

    def body(*refs):
        acc = refs[0][...].astype(F32)
        for k in range(1, nslots):
            acc = acc + refs[k][...].astype(F32)
        refs[nslots][...] = acc.astype(out_dtype)

    specs = [pl.BlockSpec((tr, c), functools.partial(lambda k, i: (k * nbk + i, 0), k)) for k in range(nslots)]
    return pl.pallas_call(
        body, name=name, grid=(nbk,), in_specs=specs, out_specs=pl.BlockSpec((tr, c), lambda i: (i, 0)),
        out_shape=jax.ShapeDtypeStruct((r, c), out_dtype), compiler_params=_cparams(("parallel",)),
    )(*([x] * nslots))


def _sum_slots_into(buf, x, layer, cidx, nslots, name):
    _, r2, c = buf.shape
    h = r2 // 2
    tr = _block_rows(h, c)
    nbk = h // tr

    def body(c_ref, b_ref, *refs):
        acc = refs[0][...].astype(F32)
        for k in range(1, nslots):
            acc = acc + refs[k][...].astype(F32)
        refs[nslots][...] = acc

    specs = [pl.BlockSpec((tr, c), functools.partial(lambda k, i, cr: (k * nbk + i, 0), k)) for k in range(nslots)]
    grid_spec = pltpu.PrefetchScalarGridSpec(
        num_scalar_prefetch=1, grid=(nbk,), in_specs=[_ANY] + specs,
        out_specs=pl.BlockSpec((None, tr, c), lambda i, cr: (layer, cr[0] * nbk + i, 0)))
    return pl.pallas_call(
        body, name=name, grid_spec=grid_spec, out_shape=jax.ShapeDtypeStruct(buf.shape, F32), input_output_aliases={1: 0},
        compiler_params=_cparams(("parallel",)),
    )(cidx, buf, *([x] * nslots))


def _share_halves(bufs, name):
    nw = len(bufs)

    def body(*refs):
        ins, outs = refs[:nw], refs[nw:2 * nw]
        send_sems, recv_sems = refs[2 * nw:]
        x, y, c = _mesh_pos()
        cps = []
        for w in range(nw):
            mine = _half_rows(bufs[w].shape[1], c, 8)
            cps.append(_remote(ins[w].at[:, mine, :], outs[w].at[:, mine, :], send_sems, recv_sems, w, (x, y, 1 - c)))
            cps[-1].start()
        for w in range(nw):
            other = outs[w].at[:, _half_rows(bufs[w].shape[1], 1 - c, 8), :]
            _remote(other, other, send_sems, recv_sems, w, (x, y, c)).wait_recv()
        for cp in cps:
            cp.wait_send()

    return pl.pallas_call(
        body, name=name, out_shape=[jax.ShapeDtypeStruct(t.shape, t.dtype) for t in bufs], in_specs=[_ANY] * nw, out_specs=[_ANY] * nw,
        input_output_aliases={w: w for w in range(nw)},
        scratch_shapes=[pltpu.SemaphoreType.DMA((nw,)), pltpu.SemaphoreType.DMA((nw,))],
    )(*bufs)


def _pick(dim, prefs):
    for p in prefs:
        if dim % p == 0:
            return p
    return dim


def _mm(a, b, m, n, k, name, out_dtype, **kw):
    tm = _pick(m, (1408, 1152, 1024, 512, 256, 128))
    tn = _pick(n, (1408, 1152, 1024, 512, 256, 128))
    kdiv = k // max(kw.get("a_stack", 0), kw.get("b_stack", 0) if kw.get("tb") else 0, 1)
    osize = jnp.dtype(out_dtype).itemsize
    tk = kdiv
    for cut in (1, 2, 4, 8, 16):
        tk = kdiv // cut
        vmem = 2 * 2 * tk * (tm + tn) + tm * tn * (2 * osize + 4 + (4 if tk < k else 0))
        if kdiv % cut == 0 and tk % 128 == 0 and vmem <= _MATMUL_VMEM_BUDGET:
            break
    return _matmul(a, b, m=m, n=n, k=k, tm=tm, tn=tn, tk=tk, out_dtype=out_dtype, name=name, **kw)


def _ssm_layout(p):
    g, st = p["lam_re"].shape
    npc = g * st // PSTATES
    lr = p["lam_re"].reshape(npc, PSTATES)
    li = p["lam_im"].reshape(npc, PSTATES)
    ls = jnp.broadcast_to(p["log_step"][:, None], (g, st)).reshape(npc, PSTATES)
    btr = jnp.transpose(p["ssm_b_re"], (2, 0, 1)).reshape(SSM_GROUP, npc, PSTATES)
    bti = jnp.transpose(p["ssm_b_im"], (2, 0, 1)).reshape(SSM_GROUP, npc, PSTATES)
    return lr, li, ls, btr, bti


def _ssm_pieces(bbr, bbi, c_re, c_im):
    npc = bbr.shape[1]
    gl = PSTATES // STATE
    eye = jnp.eye(gl, dtype=bool)

    def b_piece(bb):
        t = jnp.transpose(bb.reshape(SSM_GROUP, npc, gl, STATE), (1, 2, 0, 3))
        full = jnp.where(eye[None, :, None, :, None], t[:, :, :, None, :], 0.0)
        return full.reshape(npc, gl * SSM_GROUP, PSTATES)

    def c_piece(cc):
        t = jnp.transpose(cc.reshape(npc, gl, SSM_GROUP, STATE), (0, 1, 3, 2))
        full = jnp.where(eye[None, :, None, :, None], t[:, :, :, None, :], 0.0)
        return full.reshape(npc, PSTATES, gl * SSM_GROUP)

    bp = jnp.concatenate([b_piece(bbr), b_piece(bbi)], axis=2).astype(BF16)
    cp = jnp.concatenate([c_piece(c_re), c_piece(-c_im)], axis=1).astype(BF16)
    return bp, cp, jnp.swapaxes(bp, 1, 2), jnp.swapaxes(cp, 1, 2)


def _ssm_unpieces(dbp, dcp):
    npc = dbp.shape[0]
    gl = PSTATES // STATE
    idx = jnp.arange(gl)

    def b_diag(x):
        d = x.reshape(npc, gl, SSM_GROUP, gl, STATE)[:, idx, :, idx, :]
        return jnp.transpose(d, (2, 1, 0, 3)).reshape(SSM_GROUP, npc, PSTATES)

    def c_diag(x):
        d = x.reshape(npc, gl, STATE, gl, SSM_GROUP)[:, idx, :, idx, :]
        return jnp.transpose(d, (1, 0, 3, 2)).reshape(npc * gl, SSM_GROUP, STATE)

    return b_diag(dbp[:, :, :PSTATES]), b_diag(dbp[:, :, PSTATES:]), c_diag(dcp[:, :PSTATES, :]), -c_diag(dcp[:, PSTATES:, :])


class _LayerWeights:
    def __init__(self, fetch):
        self._fetch = fetch
        self._got = {}

    def group(self, g, after=None):
        if g not in self._got:
            self._got[g] = self._fetch(g, after)
        return self._got[g]


def _layer_fwd(l, x, ada6, weights, p):
    s, d = x.shape
    sh_m, sc_m, gt_m, sh_f, sc_f, gt_f = ada6
    w = dict(weights.group("mix", x))
    uw = w["w_glu"].shape[0]
    aw = w["w_out"].shape[0] - uw
    ncol = w["w_in"].shape[0]
    row = lambda v: v.reshape(1, -1)
    n = lambda t: f"l{l}_{t}"

    h = _pre_fwd(x, row(p["g_pre_mix"]), sc_m, sh_m, n("pre_mix"))
    proj = _mm(h, w["w_in"], s, ncol, d, n("proj"), BF16, tb=True)
    attn = _attn_fwd(proj, p["attn_sinks"], aw, uw, n("attn_fwd"))
    zin = _ssm_layout(p)
    a_re, a_im, bbr, bbi = _ssm_prep(*zin, n("ssm_prep"))
    bp, cp, bpt, cpt = _ssm_pieces(bbr, bbi, p["ssm_c_re"], p["ssm_c_im"])
    dvec = p["ssm_d"].reshape(1, uw)
    y, xs_re, xs_im = _ssm_fwd(proj, bp, cp, a_re, a_im, dvec, uw, n("ssm_fwd"), t=256 if s % 256 == 0 else 128)
    z = _gelu_fwd(y, n("gelu_fwd"))
    gl = _mm(z, w["w_glu"], s, uw, uw, n("glu"), F32)
    merged = _merge_fwd(attn, y, gl, row(p["g_attn_out"]), row(p["g_ssm_out"]), n("merge_fwd"))
    mix = _mm(merged, w["w_out"], s, d, aw + uw, n("out_proj"), F32)
    x1 = _post_fwd(x, mix, row(p["g_post_mix"]), gt_m, n("post_mix"))

    w.update(weights.group("ffn", x1))
    f = w["w_down"].shape[0]
    h2 = _pre_fwd(x1, row(p["g_pre_ffn"]), sc_f, sh_f, n("pre_ffn"))
    up0 = _mm(h2, w["w_up"], s, 2 * f, d, n("up_proj"), BF16, b_stack=w["w_up"].shape[0], o_stack=2)
    cw2 = jnp.transpose(p["conv_w"].reshape(3, 2, f), (1, 0, 2))
    cb2 = p["conv_b"].reshape(2, 1, f)
    act = _conv_glu_fwd(up0, cw2, cb2, n("conv_glu"))
    ff = _mm(act, w["w_down"], s, d, f, n("down_proj"), F32)
    x2 = _post_fwd(x1, ff, row(p["g_post_ffn"]), gt_f, n("post_ffn"))
    saved = dict(x=x, h=h, proj=proj, attn=attn, zin=zin, a_re=a_re, a_im=a_im, bpt=bpt, cpt=cpt, dvec=dvec, y=y, xs_re=xs_re,
                 xs_im=xs_im, z=z, gl=gl, merged=merged, mix=mix, x1=x1, h2=h2, up0=up0, cw2=cw2, cb2=cb2, act=act, ff=ff)
    return x2, saved


def _layer_bwd(l, dx2, ada6, weights, p, sv, on_grads):
    s, d = dx2.shape
    sh_m, sc_m, gt_m, sh_f, sc_f, gt_f = ada6
    w = {**weights.group("mix"), **weights.group("ffn")}
    uw = w["w_glu"].shape[0]
    aw = w["w_out"].shape[0] - uw
    ncol = w["w_in"].shape[0]
    f = w["w_down"].shape[0]
    nst = w["w_up"].shape[0]
    row = lambda v: v.reshape(1, -1)
    n = lambda t: f"l{l}_{t}"
    gw, gs = {}, {}

    dff, dgt_f, gs["g_post_ffn"] = _post_bwd(dx2, sv["ff"], row(p["g_post_ffn"]), gt_f, n("post_ffn_bwd"))
    gw["w_down"] = _mm(sv["act"], dff, f, d, s, n("down_dw"), BF16, ta=True)
    dact = _mm(dff, w["w_down"], s, f, d, n("down_dx"), F32, tb=True)
    dup, dcw2, dcb2 = _ffn_bwd_gate(dact, sv["up0"], sv["cw2"], sv["cb2"], n("ffn_gate_bwd"))
    gs["conv_w"] = jnp.transpose(dcw2, (1, 0, 2)).reshape(3, 2 * f)
    gs["conv_b"] = dcb2.reshape(2 * f)
    dup0 = _conv_bwd(dup, sv["cw2"], n("conv_bwd"))
    gw["w_up"] = _mm(sv["h2"], dup0, d, 2 * f, s, n("up_dw"), BF16, ta=True, b_stack=2, o_stack=nst)
    dh2 = _mm(dup0, w["w_up"], s, d, 2 * f, n("up_dx"), F32, tb=True, a_stack=2, b_stack=nst)
    dx1, dsh_f, dsc_f, gs["g_pre_ffn"] = _pre_bwd(dx2, dh2, sv["x1"], row(p["g_pre_ffn"]), sc_f, n("pre_ffn_bwd"))
    token = on_grads(l, "ffn", {k: gw.pop(k) for k in ("w_up", "w_down")})
    if token is not None:
        gt_m = gt_m + token[0:1, 0:1]

    dmix, dgt_m, gs["g_post_mix"] = _post_bwd(dx1, sv["mix"], row(p["g_post_mix"]), gt_m, n("post_mix_bwd"))
    gw["w_out"] = _mm(sv["merged"], dmix, aw + uw, d, s, n("out_dw"), BF16, ta=True)
    dmerged = _mm(dmix, w["w_out"], s, aw + uw, d, n("out_dx"), F32, tb=True)
    dattn, dgl, dzd, gs["g_attn_out"], gs["g_ssm_out"] = _merge_bwd(
        dmerged, sv["attn"], sv["y"], sv["gl"], row(p["g_attn_out"]), row(p["g_ssm_out"]), n("merge_bwd"))
    gw["w_glu"] = _mm(sv["z"], dgl, uw, uw, s, n("glu_dw"), BF16, ta=True)
    dz2 = _mm(dgl, w["w_glu"], s, uw, uw, n("glu_dx"), F32, tb=True)
    dy = _gelu_bwd(dzd, dz2, sv["y"], n("gelu_bwd"))
    du, dbp, dcp, dar, dai, dd = _ssm_bwd(dy, sv["proj"], sv["xs_re"], sv["xs_im"], sv["cpt"], sv["bpt"], sv["a_re"], sv["a_im"],
                                          sv["dvec"], uw, n("ssm_bwd"))
    dq, dkv_c, dkv_p, dsinks = _attn_bwd(sv["proj"], p["attn_sinks"], sv["attn"], dattn, aw, uw, n("attn_bwd"))
    dproj = _assemble_dproj(du, dq, dkv_c, dkv_p, n("dproj"))
    gw["w_in"] = _mm(dproj, sv["h"], ncol, d, s, n("in_dw"), BF16, ta=True)
    dh = _mm(dproj, w["w_in"], s, d, ncol, n("in_dx"), F32)
    dx0, dsh_m, dsc_m, gs["g_pre_mix"] = _pre_bwd(dx1, dh, sv["x"], row(p["g_pre_mix"]), sc_m, n("pre_mix_bwd"))
    token = on_grads(l, "mix", {k: gw.pop(k) for k in ("w_in", "w_glu", "w_out")})

    dbbr, dbbi, dc_re, dc_im = _ssm_unpieces(dbp, dcp)
    dlr, dli, dls, dbtr, dbti = _ssm_prep_bwd(*sv["zin"], dar, dai, dbbr, dbbi, n("ssm_prep_bwd"))
    g, st = p["lam_re"].shape
    gs["lam_re"] = dlr.reshape(g, st)
    gs["lam_im"] = dli.reshape(g, st)
    gs["log_step"] = jnp.sum(dls.reshape(g, st), axis=1)
    gs["ssm_b_re"] = jnp.transpose(dbtr.reshape(SSM_GROUP, g, st), (1, 2, 0))
    gs["ssm_b_im"] = jnp.transpose(dbti.reshape(SSM_GROUP, g, st), (1, 2, 0))
    gs["ssm_c_re"] = dc_re
    gs["ssm_c_im"] = dc_im
    gs["ssm_d"] = dd.reshape(p["ssm_d"].shape)
    gs["attn_sinks"] = dsinks.reshape(-1)
    gs["b_ada"] = jnp.concatenate([dsh_m, dsc_m, dgt_m, dsh_f, dsc_f, dgt_f], axis=1).reshape(-1)
    for key in ("g_post_ffn", "g_pre_ffn", "g_post_mix", "g_attn_out", "g_ssm_out", "g_pre_mix"):
        gs[key] = gs[key].reshape(-1)
    return dx0, gs, token


def _local_step(x, tgt, ada, wl, pl_small, on_grads):
    d = x.shape[1]
    depth = len(wl)
    ada6 = [[ada[l:l + 1, k * d:(k + 1) * d] for k in range(6)] for l in range(depth)]
    saved = []
    h = x
    for l in range(depth):
        h, sv = _layer_fwd(l, h, ada6[l], wl[l], pl_small[l])
        saved.append(sv)
    dy, lsum = _loss_head(h, tgt, "loss_head")
    loss = 0.5 * lsum[0, 0] / d
    gss = [None] * depth
    dx = dy
    for l in reversed(range(depth)):
        dx, gss[l], token = _layer_bwd(l, dx, ada6[l], wl[l], pl_small[l], saved[l], on_grads)
        if token is not None and l > 0:
            ada6[l - 1][5] = ada6[l - 1][5] + token[0:1, 0:1]
    return loss, dx, gss


_ANY = pl.BlockSpec(memory_space=pl.ANY)


def _mesh_pos():
    return lax.axis_index("x"), lax.axis_index("y"), lax.axis_index("c")


def _other_chips(x, y):
    return [(1 - x, y), (x, 1 - y), (1 - x, 1 - y)]


def _remote(src, dst, send_sems, recv_sems, k, to):
    return pltpu.make_async_remote_copy(src_ref=src, dst_ref=dst, send_sem=send_sems.at[k], recv_sem=recv_sems.at[k],
                                        device_id=to, device_id_type=MESH)


def _allgather8(xs, name):
    m, n = xs.shape

    def body(x_ref, out_ref, send_sems, recv_sems):
        x, y, c = _mesh_pos()
        me, sibling = (x, y, c), (x, y, 1 - c)
        chips = _other_chips(x, y)

        def rows(px, py, pc):
            return out_ref.at[pl.ds((4 * px + 2 * py + pc) * m, m), :]

        def copy(k, block, to, src=None):
            return _remote(rows(*block) if src is None else src, rows(*block), send_sems, recv_sems, k, to)

        first = [copy(0, me, sibling, src=x_ref)]
        first += [copy(1 + j, me, (*chip, c), src=x_ref) for j, chip in enumerate(chips)]
        for cp in first:
            cp.start()
        passed = [copy(4 + j, (*chip, c), sibling) for j, chip in enumerate(chips)]
        for j, chip in enumerate(chips):
            copy(1 + j, (*chip, c), me).wait_recv()
            passed[j].start()
        copy(0, sibling, me).wait_recv()
        for j, chip in enumerate(chips):
            copy(4 + j, (*chip, 1 - c), me).wait_recv()
        for cp in first + passed:
            cp.wait_send()

    out = pl.pallas_call(
        body, name=name, out_shape=jax.ShapeDtypeStruct((8 * m, n), xs.dtype), in_specs=[_ANY], out_specs=_ANY,
        scratch_shapes=[pltpu.SemaphoreType.DMA((7,)), pltpu.SemaphoreType.DMA((7,))],
    )(xs)
    x, y, c = _mesh_pos()
    return lax.dynamic_update_slice(out, xs, ((4 * x + 2 * y + c) * m, 0))


def _half_rows(ref_rows, half, align):
    h = ref_rows // 2
    return pl.ds(pl.multiple_of(half * h, align), h)


_HBM = pl.BlockSpec(memory_space=pltpu.HBM)
_SEMS = pl.BlockSpec(memory_space=pltpu.SEMAPHORE)
_EFFECT = pltpu.SideEffectType.DATAFLOW_SIDE_EFFECTING
_TOKEN = jax.ShapeDtypeStruct((8, 128), F32)


def _in_hbm(arrays):
    return [pltpu.with_memory_space_constraint(a, pltpu.HBM) for a in arrays]


def _chip_copy(kind, srcs, lands, w, q, chip, mine, c, send, recv, k):
    if kind == "gather":
        rows = _half_rows(srcs[w].shape[0], c, 16)
        return _remote(srcs[w].at[rows, :], lands[w].at[mine, rows, :], send, recv, k, (*chip, c))
    return _remote(srcs[w].at[2 * chip[0] + chip[1]], lands[w].at[mine], send, recv, k, (*chip, c))


def _chip_landing(kind, srcs, lands, w, chip, c):
    if kind == "gather":
        return lands[w].at[2 * chip[0] + chip[1], _half_rows(srcs[w].shape[0], c, 16), :]
    return lands[w].at[2 * chip[0] + chip[1]]


def _ici_start(kind, srcs, groups, name, after=None):
    nw, ng = len(srcs), len(groups)
    lands = [lax.empty((4,) + s.shape if kind == "gather" else s.shape, s.dtype) for s in srcs]
    extra = [] if after is None else [after]

    def body(*refs):
        ins, lnd = refs[:nw], refs[nw:2 * nw]
        sems = refs[2 * nw + len(extra):2 * nw + len(extra) + 2 * ng]
        token = refs[-1]
        x, y, c = _mesh_pos()
        for g, members in enumerate(groups):
            for j, w in enumerate(members):
                for q, chip in enumerate(_other_chips(x, y)):
                    _chip_copy(kind, ins, lnd, w, q, chip, 2 * x + y, c, sems[2 * g], sems[2 * g + 1], 3 * j + q).start()
        token[...] = jnp.zeros_like(token)

    sem_shapes = [pltpu.SemaphoreType.DMA((3 * len(members),)) for members in groups for _ in range(2)]
    out = pl.pallas_call(
        body, name=name,
        out_shape=(*sem_shapes, *[pltpu.HBM(s.shape, s.dtype) for s in srcs], *[pltpu.HBM(t.shape, t.dtype) for t in lands], _TOKEN),
        in_specs=[_HBM] * (2 * nw) + [_ANY] * len(extra),
        out_specs=(*([_SEMS] * (2 * ng)), *([_HBM] * (2 * nw)), pl.BlockSpec(memory_space=pltpu.VMEM)),
        input_output_aliases={i: 2 * ng + i for i in range(2 * nw)},
        compiler_params=pltpu.CompilerParams(has_side_effects=_EFFECT),
    )(*_in_hbm(srcs), *_in_hbm(lands), *extra)
    sems = [(out[2 * g], out[2 * g + 1]) for g in range(ng)]
    return sems, list(out[2 * ng:2 * ng + nw]), list(out[2 * ng + nw:2 * ng + 2 * nw]), out[-1]


def _ici_wait(kind, sems, srcs, lands, after, name):
    nm = len(srcs)

    def body(*refs):
        ins, lnd = refs[:nm], refs[nm:2 * nm]
        send, recv = refs[2 * nm], refs[2 * nm + 1]
        x, y, c = _mesh_pos()
        for j in range(nm):
            for q, chip in enumerate(_other_chips(x, y)):
                _chip_copy(kind, ins, lnd, j, q, chip, 2 * x + y, c, send, recv, 3 * j + q).wait_send()
                landed = _chip_landing(kind, ins, lnd, j, chip, c)
                _remote(landed, landed, send, recv, 3 * j + q, (x, y, c)).wait_recv()

    out = pl.pallas_call(
        body, name=name, out_shape=(*[pltpu.HBM(s.shape, s.dtype) for s in srcs], *[pltpu.HBM(t.shape, t.dtype) for t in lands]),
        in_specs=[_HBM] * (2 * nm) + [_SEMS, _SEMS, _ANY], out_specs=tuple([_HBM] * (2 * nm)),
        input_output_aliases={i: i for i in range(2 * nm)},
        compiler_params=pltpu.CompilerParams(has_side_effects=_EFFECT),
    )(*srcs, *lands, sems[0], sems[1], after)
    return list(out[:nm]), list(out[nm:])


def _gather_finish(shards, lands, name):
    nw = len(shards)

    def body(*refs):
        ins, lnd, outs = refs[:nw], refs[nw:2 * nw], refs[2 * nw:3 * nw]
        send_sems, recv_sems = refs[3 * nw:]
        x, y, c = _mesh_pos()
        mine = 2 * x + y
        sibling = (x, y, 1 - c)
        chips = _other_chips(x, y)

        def blk(ref, chip_idx, half):
            return ref.at[chip_idx, _half_rows(ref.shape[1], half, 16), :]

        sends = []
        for w in range(nw):
            for q, chip in enumerate(chips):
                k = 2 * chip[0] + chip[1]
                sends.append(_remote(blk(lnd[w], k, c), blk(outs[w], k, c), send_sems, recv_sems, 4 * w + q, sibling))
            sends.append(_remote(ins[w], outs[w].at[mine], send_sems, recv_sems, 4 * w + 3, sibling))
        for cp in sends:
            cp.start()
        for w in range(nw):
            for q, chip in enumerate(chips):
                other = blk(outs[w], 2 * chip[0] + chip[1], 1 - c)
                _remote(other, other, send_sems, recv_sems, 4 * w + q, (x, y, c)).wait_recv()
            own = outs[w].at[mine]
            _remote(own, own, send_sems, recv_sems, 4 * w + 3, (x, y, c)).wait_recv()
        for cp in sends:
            cp.wait_send()

    return pl.pallas_call(
        body, name=name, out_shape=[jax.ShapeDtypeStruct(t.shape, t.dtype) for t in lands],
        in_specs=[_ANY] * (2 * nw), out_specs=[_ANY] * nw, input_output_aliases={nw + w: w for w in range(nw)},
        scratch_shapes=[pltpu.SemaphoreType.DMA((4 * nw,)), pltpu.SemaphoreType.DMA((4 * nw,))],
    )(*shards, *lands)


def _exchange_halves(gs, name):
    nw = len(gs)

    def body(*refs):
        ins, outs = refs[:nw], refs[nw:2 * nw]
        send_sems, recv_sems = refs[2 * nw:]
        x, y, c = _mesh_pos()
        cps = []
        for w in range(nw):
            src = ins[w].at[:, _half_rows(gs[w].shape[1], 1 - c, 16), :]
            cps.append(_remote(src, outs[w], send_sems, recv_sems, w, (x, y, 1 - c)))
            cps[-1].start()
        for cp in cps:
            cp.wait_recv()
        for cp in cps:
            cp.wait_send()

    return pl.pallas_call(
        body, name=name, out_shape=[jax.ShapeDtypeStruct((4, g.shape[1] // 2, g.shape[2]), g.dtype) for g in gs],
        in_specs=[_ANY] * nw, out_specs=[_ANY] * nw,
        scratch_shapes=[pltpu.SemaphoreType.DMA((nw,)), pltpu.SemaphoreType.DMA((nw,))],
    )(*gs)


def _add_half(g, recv, cidx, name):
    _, r, c = g.shape
    h = r // 2
    tr = _block_rows(h, c)
    nbh = h // tr
    assert tr % 16 == 0

    def body(c_ref, g_ref, r_ref, o_ref):
        o_ref[...] = (g_ref[...].astype(F32) + r_ref[...].astype(F32)).astype(BF16)

    grid_spec = pltpu.PrefetchScalarGridSpec(
        num_scalar_prefetch=1, grid=(4, nbh),
        in_specs=[pl.BlockSpec((None, tr, c), lambda s, i, cr: (s, cr[0] * nbh + i, 0)),
                  pl.BlockSpec((None, tr, c), lambda s, i, cr: (s, i, 0))],
        out_specs=pl.BlockSpec((None, tr, c), lambda s, i, cr: (s, i, 0)))
    return pl.pallas_call(
        body, name=name, grid_spec=grid_spec, out_shape=jax.ShapeDtypeStruct((4, h, c), BF16),
        compiler_params=_cparams(("parallel", "parallel")),
    )(cidx, g, recv)


_BIG = ("w_in", "w_glu", "w_out", "w_up", "w_down")
_GROUPS = {"mix": ("w_in", "w_glu", "w_out"), "ffn": ("w_up", "w_down")}
_SMALL = ("b_ada", "g_pre_mix", "g_post_mix", "attn_sinks", "lam_re", "lam_im", "log_step", "ssm_b_re", "ssm_b_im", "ssm_c_re",
          "ssm_c_im", "ssm_d", "g_attn_out", "g_ssm_out", "g_pre_ffn", "g_post_ffn", "conv_b")
_WEIGHTS = ("w_ada", "b_ada", "g_pre_mix", "g_post_mix", "w_in", "attn_sinks", "lam_re", "lam_im", "log_step", "ssm_b_re", "ssm_b_im",
            "ssm_c_re", "ssm_c_im", "ssm_d", "w_glu", "g_attn_out", "g_ssm_out", "w_out", "g_pre_ffn", "g_post_ffn", "w_up", "conv_w",
            "conv_b", "w_down")
_LANES = 1024


def _pack(parts, rows_to):
    flat = jnp.concatenate([p.reshape(-1) for p in parts])
    per = _LANES * rows_to
    total = -(-flat.shape[0] // per) * per
    return jnp.pad(flat, (0, total - flat.shape[0])).reshape(total // _LANES, _LANES)


def _unpack(packed, shapes):
    flat = packed.reshape(-1)
    out, off = [], 0
    for shp in shapes:
        size = math.prod(shp)
        out.append(flat[off:off + size].reshape(shp))
        off += size
    return out


def kernel(x, c, w_ada, b_ada, g_pre_mix, g_post_mix, w_in, attn_sinks, lam_re, lam_im, log_step, ssm_b_re, ssm_b_im, ssm_c_re, ssm_c_im, ssm_d, w_glu, g_attn_out, g_ssm_out, w_out, g_pre_ffn, g_post_ffn, w_up, conv_w, conv_b, w_down, loss_target, m_w_ada, m_b_ada, m_g_pre_mix, m_g_post_mix, m_w_in, m_attn_sinks, m_lam_re, m_lam_im, m_log_step, m_ssm_b_re, m_ssm_b_im, m_ssm_c_re, m_ssm_c_im, m_ssm_d, m_w_glu, m_g_attn_out, m_g_ssm_out, m_w_out, m_g_pre_ffn, m_g_post_ffn, m_w_up, m_conv_w, m_conv_b, m_w_down, v_w_ada, v_b_ada, v_g_pre_mix, v_g_post_mix, v_w_in, v_attn_sinks, v_lam_re, v_lam_im, v_log_step, v_ssm_b_re, v_ssm_b_im, v_ssm_c_re, v_ssm_c_im, v_ssm_d, v_w_glu, v_g_attn_out, v_g_ssm_out, v_w_out, v_g_pre_ffn, v_g_post_ffn, v_w_up, v_conv_w, v_conv_b, v_w_down):
    given = dict(locals())
    wts = {n: given[n] for n in _WEIGHTS}
    mom = {n: given["m_" + n] for n in _WEIGHTS}
    var = {n: given["v_" + n] for n in _WEIGHTS}
    depth, d, ada_cols = w_ada.shape
    nchips = 4
    xi, yi, ci = lax.axis_index("x"), lax.axis_index("y"), lax.axis_index("c")
    chip = 2 * xi + yi
    cidx = jnp.reshape(ci, (1,)).astype(jnp.int32)

    cw_cols = conv_w.shape[2]
    vec = _pack([c, conv_w], 8)
    g1 = _allgather8(vec, "ag_cond").reshape(8, -1)
    c_all = g1[:, :d]
    cw_sh = g1[0::2, d:d + depth * 3 * cw_cols].reshape(nchips, depth, 3, cw_cols)
    conv_w_full = jnp.transpose(cw_sh, (1, 2, 0, 3)).reshape(depth, 3, nchips * cw_cols)

    ada_part, c_act = _ada_part(c_all, w_ada, "ada_part")
    g2 = _allgather8(ada_part.reshape(depth * 8, ada_cols), "ag_ada")
    ada = _ada_select(g2, b_ada, "ada_select")

    order = [(l, g) for l in range(depth) for g in _GROUPS]
    for table in (wts, mom, var):
        table["w_in"] = jnp.swapaxes(table["w_in"], 1, 2)
    members = {key: [wts[n][key[0]].astype(BF16) for n in _GROUPS[key[1]]] for key in order}
    flat = [s for key in order for s in members[key]]
    index, at = {}, 0
    for key in order:
        index[key] = list(range(at, at + len(members[key])))
        at += len(members[key])
    ag_sems, ag_srcs, ag_lands, ag_token = _ici_start("gather", flat, [index[key] for key in order], "ag_start", after=ada)
    ada = ada + ag_token[0:1, 0:1]

    def fetch(l, g, after):
        pos, ids = order.index((l, g)), index[(l, g)]
        srcs, lands = [ag_srcs[i] for i in ids], [ag_lands[i] for i in ids]
        srcs, lands = _ici_wait("gather", ag_sems[pos], srcs, lands, ag_token if after is None else after, f"ag_wait_l{l}_{g}")
        got = dict(zip(_GROUPS[g], _gather_finish(srcs, lands, f"ag_finish_l{l}_{g}")))
        if g == "ffn":
            return dict(w_up=got["w_up"], w_down=got["w_down"].reshape(-1, got["w_down"].shape[2]))
        w_in_t = got["w_in"].reshape(-1, d)
        split = w_in_t.shape[0] - nchips * got["w_glu"].shape[1]
        return dict(w_in=jnp.concatenate([w_in_t[split:], w_in_t[:split]], axis=0),
                    w_glu=got["w_glu"].reshape(-1, got["w_glu"].shape[2]), w_out=got["w_out"].reshape(-1, got["w_out"].shape[2]))

    wl = [_LayerWeights(functools.partial(fetch, l)) for l in range(depth)]
    wl[0].group("mix")
    ps = []
    for l in range(depth):
        small = {n: wts[n][l] for n in _SMALL if n != "b_ada"}
        small["conv_w"] = conv_w_full[l]
        ps.append(small)

    in_flight = {}

    def on_grads(l, g, gw):
        stacks = []
        for n in _GROUPS[g]:
            t = gw[n]
            if n == "w_in":
                uw = nchips * wts["w_glu"].shape[1]
                t = jnp.concatenate([t[uw:], t[:uw]], axis=0).reshape(nchips, -1, d)
            elif n != "w_up":
                t = t.reshape(nchips, t.shape[0] // nchips, t.shape[1])
            stacks.append(t)
        from_sibling = _exchange_halves(stacks, f"rs_sibling_l{l}_{g}")
        partials = [_add_half(s, r, cidx, f"rs_add_l{l}_{n}") for s, r, n in zip(stacks, from_sibling, _GROUPS[g])]
        sems, srcs, lands, token = _ici_start("scatter", partials, [list(range(len(partials)))], f"rs_start_l{l}_{g}")
        in_flight[(l, g)] = (sems[0], srcs, lands)
        return token

    loss_sum, grad_x, gss = _local_step(x[0], loss_target[0], ada, wl, ps, on_grads)
    loss = lax.psum(loss_sum, ("x", "y", "c"))

    reduced = {n: lax.empty(wts[n].shape, F32) for n in _BIG}
    for key in reversed(order):
        l, g = key
        sems, srcs, lands = in_flight[key]
        sent, landed = _ici_wait("scatter", sems, srcs, lands, grad_x, f"rs_wait_l{l}_{g}")
        for n, t, p in zip(_GROUPS[g], landed, sent):
            t = lax.dynamic_update_slice(t, lax.dynamic_slice_in_dim(p, chip, 1, axis=0), (chip, 0, 0))
            reduced[n] = _sum_slots_into(reduced[n], t.reshape(-1, t.shape[2]), l, cidx, nchips, f"rs_sum_l{l}_{n}")
    big_grads = dict(zip(_BIG, _share_halves([reduced[n] for n in _BIG], "rs_share")))

    small_parts = [jnp.stack([gss[l][n] for l in range(depth)]) for n in _SMALL]
    pack_small = _pack(small_parts, 8)
    pack_cw = _pack([jnp.stack([gss[l]["conv_w"] for l in range(depth)])], 8)
    rows_small = pack_small.shape[0]
    mine = jnp.concatenate([pack_small, pack_cw], axis=0)
    g3 = _allgather8(mine, "ag_small")
    total = _sum_slots(g3, 8, "sum_small")
    grads = dict(big_grads)
    for n, v in zip(_SMALL, _unpack(total[:rows_small], [wts[n].shape for n in _SMALL])):
        grads[n] = v
    conv_w_grad = _unpack(total[rows_small:], [(depth, 3, nchips * cw_cols)])[0]
    grads["conv_w"] = lax.dynamic_slice_in_dim(conv_w_grad, chip * cw_cols, cw_cols, axis=2)

    ada_rows = depth * 6 * d // _LANES
    d_ada_all = g3.reshape(8, -1, _LANES)[:, :ada_rows].reshape(8, depth, 6 * d)
    d_sel = lax.dynamic_slice_in_dim(jnp.transpose(d_ada_all, (1, 0, 2)), chip * ada_cols, ada_cols, axis=2)
    grads["w_ada"] = _wada_grad(jnp.transpose(c_act), d_sel, "w_ada_grad")

    delta, new_m, new_v = {}, {}, {}
    for n in _WEIGHTS:
        shp = wts[n].shape
        view = (lambda t: t) if len(shp) == 3 else (lambda t: t.reshape(1, -1, shp[-1]))
        outs = _adamw(view(wts[n]), view(grads[n]), view(mom[n]), view(var[n]), f"adamw_{n}")
        delta[n], new_m[n], new_v[n] = [t.reshape(shp) for t in outs]

    for table in (grads, delta, new_m, new_v):
        table["w_in"] = jnp.swapaxes(table["w_in"], 1, 2)
    return (loss, grad_x[None], *[grads[n] for n in _WEIGHTS], *[delta[n] for n in _WEIGHTS],
            *[new_m[n] for n in _WEIGHTS], *[new_v[n] for n in _WEIGHTS])
```

```python
import functools
import math

import jax
import jax.numpy as jnp
from jax import lax
from jax.experimental import pallas as pl
from jax.experimental.pallas import tpu as pltpu

F32 = jnp.float32
BF16 = jnp.bfloat16
EPS = 1e-6
NEG = -1e30
WINDOW = 128
HEAD_DIM = 64
KV_RATIO = 8
SSM_GROUP = 16
STATE = 64
PIECE = 128
PSTATES = 512
DEPTH = 2
ADAM_LR, ADAM_B1, ADAM_B2, ADAM_EPS, ADAM_WD, ADAM_STEP = 0.001, 0.9, 0.999, 1e-08, 0.01, 10
MIB = 1024 * 1024
_MATMUL_VMEM_BUDGET = 40 * MIB
MESH = pl.DeviceIdType.MESH


def _cparams(sem=None, vmem_mib=48):
    return pltpu.CompilerParams(dimension_semantics=sem, vmem_limit_bytes=vmem_mib * MIB)


def _gelu(x):
    c = math.sqrt(2.0 / math.pi)
    return 0.5 * x * (1.0 + jnp.tanh(c * (x + 0.044715 * (x * x * x))))


def _gelu_and_grad(x):
    c = math.sqrt(2.0 / math.pi)
    x2 = x * x
    t = jnp.tanh(c * (x + 0.044715 * (x2 * x)))
    half = 0.5 * (1.0 + t)
    return x * half, half + 0.5 * x * (1.0 - t * t) * c * (1.0 + 3.0 * 0.044715 * x2)


def _gelu_grad(x):
    return _gelu_and_grad(x)[1]


def _sigmoid(x):
    return 1.0 / (1.0 + jnp.exp(-x))


def _matmul(a, b, *, m, n, k, tm, tn, tk, out_dtype, name, ta=False, tb=False, a_stack=0, b_stack=0, o_stack=0):
    assert m % tm == 0 and n % tn == 0 and k % tk == 0, (name, m, n, k, tm, tn, tk)
    nk = k // tk

    if a_stack:
        assert not ta and (k // a_stack) % tk == 0
        per = (k // a_stack) // tk
        a_spec = pl.BlockSpec((None, tm, tk), lambda i, j, kk: (kk // per, i, kk % per))
    elif ta:
        a_spec = pl.BlockSpec((tk, tm), lambda i, j, kk: (kk, i))
    else:
        a_spec = pl.BlockSpec((tm, tk), lambda i, j, kk: (i, kk))
    if b_stack and tb:
        perb = (k // b_stack) // tk
        b_spec = pl.BlockSpec((None, tn, tk), lambda i, j, kk: (kk // perb, j, kk % perb))
    elif b_stack:
        perb = (n // b_stack) // tn
        b_spec = pl.BlockSpec((None, tk, tn), lambda i, j, kk: (j // perb, kk, j % perb))
    elif tb:
        b_spec = pl.BlockSpec((tn, tk), lambda i, j, kk: (j, kk))
    else:
        b_spec = pl.BlockSpec((tk, tn), lambda i, j, kk: (kk, j))
    if o_stack:
        pero = (n // o_stack) // tn
        o_spec = pl.BlockSpec((None, tm, tn), lambda i, j, kk: (j // pero, i, j % pero))
        o_shape = jax.ShapeDtypeStruct((o_stack, m, n // o_stack), out_dtype)
    else:
        o_spec = pl.BlockSpec((tm, tn), lambda i, j, kk: (i, j))
        o_shape = jax.ShapeDtypeStruct((m, n), out_dtype)
    dims = (((0 if ta else 1,), (1 if tb else 0,)), ((), ()))

    def body(a_ref, b_ref, o_ref, *acc):
        p = lax.dot_general(a_ref[...].astype(BF16), b_ref[...].astype(BF16), dims, preferred_element_type=F32)
        if nk == 1:
            o_ref[...] = p.astype(o_ref.dtype)
        else:
            acc_ref = acc[0]
            kk = pl.program_id(2)

            @pl.when(kk == 0)
            def _():
                acc_ref[...] = p

            @pl.when(kk > 0)
            def _():
                acc_ref[...] += p

            @pl.when(kk == nk - 1)
            def _():
                o_ref[...] = acc_ref[...].astype(o_ref.dtype)

    return pl.pallas_call(
        body, name=name, grid=(m // tm, n // tn, nk), in_specs=[a_spec, b_spec], out_specs=o_spec, out_shape=o_shape,
        scratch_shapes=[] if nk == 1 else [pltpu.VMEM((tm, tn), F32)],
        compiler_params=_cparams(("parallel", "parallel", "arbitrary"), 56),
    )(a, b)


def _row(d):
    return pl.BlockSpec((1, d), lambda i: (0, 0))


def _tok(tm, d):
    return pl.BlockSpec((tm, d), lambda i: (i, 0))


def _pre_fwd(x, g, sc, sh, name):
    s, d = x.shape
    tm = min(256, s)

    def body(x_ref, g_ref, sc_ref, sh_ref, h_ref):
        xv = x_ref[...]
        r = lax.rsqrt(jnp.mean(xv * xv, axis=-1, keepdims=True) + EPS)
        h_ref[...] = (((xv * r) * g_ref[...]) * (1.0 + sc_ref[...]) + sh_ref[...]).astype(BF16)

    return pl.pallas_call(
        body, name=name, grid=(s // tm,), in_specs=[_tok(tm, d), _row(d), _row(d), _row(d)], out_specs=_tok(tm, d),
        out_shape=jax.ShapeDtypeStruct((s, d), BF16), compiler_params=_cparams(("parallel",)),
    )(x, g, sc, sh)


def _post_fwd(x, o, g, gt, name):
    s, d = x.shape
    tm = min(256, s)

    def body(x_ref, o_ref, g_ref, gt_ref, y_ref):
        ov = o_ref[...].astype(F32)
        r = lax.rsqrt(jnp.mean(ov * ov, axis=-1, keepdims=True) + EPS)
        y_ref[...] = x_ref[...] + (1.0 + gt_ref[...]) * ((ov * r) * g_ref[...])

    return pl.pallas_call(
        body, name=name, grid=(s // tm,), in_specs=[_tok(tm, d), _tok(tm, d), _row(d), _row(d)], out_specs=_tok(tm, d),
        out_shape=jax.ShapeDtypeStruct((s, d), F32), compiler_params=_cparams(("parallel",)),
    )(x, o, g, gt)


def _post_bwd(dxo, o, g, gt, name):
    s, d = o.shape
    tm = min(256, s)

    def body(dx_ref, o_ref, g_ref, gt_ref, do_ref, dgt_ref, dg_ref):
        i = pl.program_id(0)
        dx = dx_ref[...]
        ov = o_ref[...].astype(F32)
        gv = g_ref[...]
        r = lax.rsqrt(jnp.mean(ov * ov, axis=-1, keepdims=True) + EPS)
        oh = ov * r
        dn = dx * (1.0 + gt_ref[...])
        e = dn * gv
        do_ref[...] = (r * (e - oh * jnp.mean(e * oh, axis=-1, keepdims=True))).astype(BF16)
        p_gt = jnp.sum(dx * (oh * gv), axis=0, keepdims=True)
        p_g = jnp.sum(dn * oh, axis=0, keepdims=True)

        @pl.when(i == 0)
        def _():
            dgt_ref[...] = p_gt
            dg_ref[...] = p_g

        @pl.when(i > 0)
        def _():
            dgt_ref[...] += p_gt
            dg_ref[...] += p_g

    row = jax.ShapeDtypeStruct((1, d), F32)
    return pl.pallas_call(
        body, name=name, grid=(s // tm,), in_specs=[_tok(tm, d), _tok(tm, d), _row(d), _row(d)],
        out_specs=(_tok(tm, d), _row(d), _row(d)), out_shape=(jax.ShapeDtypeStruct((s, d), BF16), row, row),
        compiler_params=_cparams(("arbitrary",)),
    )(dxo, o, g, gt)


def _pre_bwd(dres, dh, x, g, sc, name):
    s, d = x.shape
    tm = min(256, s)

    def body(dres_ref, dh_ref, x_ref, g_ref, sc_ref, dx_ref, dsh_ref, dsc_ref, dg_ref):
        i = pl.program_id(0)
        dh_v = dh_ref[...].astype(F32)
        xv = x_ref[...]
        gv = g_ref[...]
        one_sc = 1.0 + sc_ref[...]
        r = lax.rsqrt(jnp.mean(xv * xv, axis=-1, keepdims=True) + EPS)
        xh = xv * r
        e = dh_v * one_sc * gv
        dx_ref[...] = dres_ref[...] + r * (e - xh * jnp.mean(e * xh, axis=-1, keepdims=True))
        p_sh = jnp.sum(dh_v, axis=0, keepdims=True)
        p_sc = jnp.sum(dh_v * (xh * gv), axis=0, keepdims=True)
        p_g = jnp.sum(dh_v * one_sc * xh, axis=0, keepdims=True)

        @pl.when(i == 0)
        def _():
            dsh_ref[...] = p_sh
            dsc_ref[...] = p_sc
            dg_ref[...] = p_g

        @pl.when(i > 0)
        def _():
            dsh_ref[...] += p_sh
            dsc_ref[...] += p_sc
            dg_ref[...] += p_g

    row = jax.ShapeDtypeStruct((1, d), F32)
    return pl.pallas_call(
        body, name=name, grid=(s // tm,), in_specs=[_tok(tm, d), _tok(tm, d), _tok(tm, d), _row(d), _row(d)],
        out_specs=(_tok(tm, d), _row(d), _row(d), _row(d)), out_shape=(jax.ShapeDtypeStruct((s, d), F32), row, row, row),
        compiler_params=_cparams(("arbitrary",)),
    )(dres, dh, x, g, sc)


def _loss_head(y, tgt, name):
    s, d = y.shape
    tm = min(256, s)

    def body(y_ref, t_ref, dy_ref, l_ref):
        i = pl.program_id(0)
        err = y_ref[...] - t_ref[...]
        dy_ref[...] = err * (1.0 / d)
        part = jnp.zeros((1, 128), F32) + jnp.sum(err * err)

        @pl.when(i == 0)
        def _():
            l_ref[...] = part

        @pl.when(i > 0)
        def _():
            l_ref[...] += part

    return pl.pallas_call(
        body, name=name, grid=(s // tm,), in_specs=[_tok(tm, d), _tok(tm, d)],
        out_specs=(_tok(tm, d), pl.BlockSpec((1, 128), lambda i: (0, 0))),
        out_shape=(jax.ShapeDtypeStruct((s, d), F32), jax.ShapeDtypeStruct((1, 128), F32)),
        compiler_params=_cparams(("arbitrary",)),
    )(y, tgt)


def _gelu_fwd(y, name):
    s, u = y.shape
    tm = min(512, s)

    def body(y_ref, z_ref):
        z_ref[...] = _gelu(y_ref[...]).astype(BF16)

    return pl.pallas_call(
        body, name=name, grid=(s // tm,), in_specs=[_tok(tm, u)], out_specs=_tok(tm, u),
        out_shape=jax.ShapeDtypeStruct((s, u), BF16), compiler_params=_cparams(("parallel",)),
    )(y)


def _merge_fwd(attn, y, gl, ga, gs, name):
    s, aw = attn.shape
    uw = y.shape[1]
    tm = min(256, s)

    def body(a_ref, y_ref, gl_ref, ga_ref, gs_ref, m_ref):
        av = a_ref[...]
        ra = lax.rsqrt(jnp.mean(av * av, axis=-1, keepdims=True) + EPS)
        m_ref[:, :aw] = ((av * ra) * ga_ref[...]).astype(BF16)
        ssm = _gelu(y_ref[...]) * _sigmoid(gl_ref[...])
        rs = lax.rsqrt(jnp.mean(ssm * ssm, axis=-1, keepdims=True) + EPS)
        m_ref[:, aw:] = ((ssm * rs) * gs_ref[...]).astype(BF16)

    return pl.pallas_call(
        body, name=name, grid=(s // tm,), in_specs=[_tok(tm, aw), _tok(tm, uw), _tok(tm, uw), _row(aw), _row(uw)],
        out_specs=_tok(tm, aw + uw), out_shape=jax.ShapeDtypeStruct((s, aw + uw), BF16),
        compiler_params=_cparams(("parallel",)),
    )(attn, y, gl, ga, gs)


def _merge_bwd(dm, attn, y, gl, ga, gs, name):
    s, aw = attn.shape
    uw = y.shape[1]
    tm = min(256, s)

    def body(dm_ref, a_ref, y_ref, gl_ref, ga_ref, gs_ref, da_ref, dgl_ref, dz_ref, dga_ref, dgs_ref):
        i = pl.program_id(0)
        av = a_ref[...]
        dma = dm_ref[:, :aw].astype(F32)
        ra = lax.rsqrt(jnp.mean(av * av, axis=-1, keepdims=True) + EPS)
        ah = av * ra
        e = dma * ga_ref[...]
        da_ref[...] = (ra * (e - ah * jnp.mean(e * ah, axis=-1, keepdims=True))).astype(BF16)
        p_ga = jnp.sum(dma * ah, axis=0, keepdims=True)

        z = _gelu(y_ref[...])
        sig = _sigmoid(gl_ref[...])
        ssm = z * sig
        dms = dm_ref[:, aw:].astype(F32)
        rs = lax.rsqrt(jnp.mean(ssm * ssm, axis=-1, keepdims=True) + EPS)
        sh = ssm * rs
        e2 = dms * gs_ref[...]
        dssm = rs * (e2 - sh * jnp.mean(e2 * sh, axis=-1, keepdims=True))
        dz_ref[...] = dssm * sig
        dgl_ref[...] = (dssm * z * sig * (1.0 - sig)).astype(BF16)
        p_gs = jnp.sum(dms * sh, axis=0, keepdims=True)

        @pl.when(i == 0)
        def _():
            dga_ref[...] = p_ga
            dgs_ref[...] = p_gs

        @pl.when(i > 0)
        def _():
            dga_ref[...] += p_ga
            dgs_ref[...] += p_gs

    return pl.pallas_call(
        body, name=name, grid=(s // tm,),
        in_specs=[_tok(tm, aw + uw), _tok(tm, aw), _tok(tm, uw), _tok(tm, uw), _row(aw), _row(uw)],
        out_specs=(_tok(tm, aw), _tok(tm, uw), _tok(tm, uw), _row(aw), _row(uw)),
        out_shape=(jax.ShapeDtypeStruct((s, aw), BF16), jax.ShapeDtypeStruct((s, uw), BF16), jax.ShapeDtypeStruct((s, uw), F32),
                   jax.ShapeDtypeStruct((1, aw), F32), jax.ShapeDtypeStruct((1, uw), F32)),
        compiler_params=_cparams(("arbitrary",)),
    )(dm, attn, y, gl, ga, gs)


def _gelu_bwd(dzd, dz2, y, name):
    s, u = y.shape
    tm = min(512, s)

    def body(a_ref, b_ref, y_ref, o_ref):
        o_ref[...] = (a_ref[...] + b_ref[...]) * _gelu_grad(y_ref[...])

    return pl.pallas_call(
        body, name=name, grid=(s // tm,), in_specs=[_tok(tm, u), _tok(tm, u), _tok(tm, u)], out_specs=_tok(tm, u),
        out_shape=jax.ShapeDtypeStruct((s, u), F32), compiler_params=_cparams(("parallel",)),
    )(dzd, dz2, y)


def _attn_scores(qh, kb, sink, valid):
    s = lax.dot_general(qh, kb, (((1,), (1,)), ((), ())), preferred_element_type=F32) * (HEAD_DIM ** -0.5)
    s = jnp.where(valid, s, NEG)
    m = jnp.maximum(jnp.max(s, axis=-1, keepdims=True), sink)
    e = jnp.exp(s - m)
    esink = jnp.exp(sink - m)
    den = jnp.sum(e, axis=-1, keepdims=True) + esink
    return e / den, esink / den


def _attn_valid(i):
    qi = lax.broadcasted_iota(jnp.int32, (KV_RATIO * WINDOW, 2 * WINDOW), 0) % WINDOW
    kj = lax.broadcasted_iota(jnp.int32, (KV_RATIO * WINDOW, 2 * WINDOW), 1)
    return (kj > qi) & (kj <= qi + WINDOW) & ((kj >= WINDOW) | (i > 0))


def _stack_heads(ref, hk):
    return jnp.concatenate([ref[:, (hk * KV_RATIO + g) * HEAD_DIM:(hk * KV_RATIO + g + 1) * HEAD_DIM] for g in range(KV_RATIO)], axis=0)


def _stack_sinks(sink_ref, hk):
    return jnp.concatenate([jnp.full((WINDOW, 1), sink_ref[hk * KV_RATIO + g], F32) for g in range(KV_RATIO)], axis=0)


def _band(kvp, kvc, off):
    return jnp.concatenate([kvp[:, off:off + HEAD_DIM], kvc[:, off:off + HEAD_DIM]], axis=0)


def _attn_specs(aw, uw, kvw):
    qblk = uw // aw
    kvblk = (uw + aw) // (2 * kvw)
    assert uw % aw == 0 and (uw + aw) % (2 * kvw) == 0
    return [
        pl.BlockSpec(memory_space=pltpu.SMEM),
        pl.BlockSpec((WINDOW, aw), lambda i: (i, qblk)),
        pl.BlockSpec((WINDOW, 2 * kvw), lambda i: (i, kvblk)),
        pl.BlockSpec((WINDOW, 2 * kvw), lambda i: (jnp.maximum(i - 1, 0), kvblk)),
    ]


def _attn_fwd(proj, sinks, aw, uw, name):
    s = proj.shape[0]
    nq = aw // HEAD_DIM
    nkv = nq // KV_RATIO
    kvw = nkv * HEAD_DIM

    def body(sink_ref, q_ref, kvc_ref, kvp_ref, o_ref):
        valid = _attn_valid(pl.program_id(0))[:WINDOW]
        q = q_ref[...]
        kvc = kvc_ref[...]
        kvp = kvp_ref[...]
        for hk in range(nkv):
            kb = _band(kvp, kvc, hk * HEAD_DIM)
            vb = _band(kvp, kvc, kvw + hk * HEAD_DIM)
            for g in range(KV_RATIO):
                hq = hk * KV_RATIO + g
                p, _ = _attn_scores(q[:, hq * HEAD_DIM:(hq + 1) * HEAD_DIM], kb, sink_ref[hq], valid)
                o_ref[:, hq * HEAD_DIM:(hq + 1) * HEAD_DIM] = jnp.dot(p.astype(BF16), vb, preferred_element_type=F32)

    return pl.pallas_call(
        body, name=name, grid=(s // WINDOW,), in_specs=_attn_specs(aw, uw, kvw),
        out_specs=pl.BlockSpec((WINDOW, aw), lambda i: (i, 0)), out_shape=jax.ShapeDtypeStruct((s, aw), F32),
        compiler_params=_cparams(("parallel",)),
    )(sinks, proj, proj, proj)


def _attn_bwd(proj, sinks, attn, dattn, aw, uw, name):
    s = proj.shape[0]
    nq = aw // HEAD_DIM
    nkv = nq // KV_RATIO
    kvw = nkv * HEAD_DIM
    hd = HEAD_DIM

    def body(sink_ref, q_ref, kvc_ref, kvp_ref, o_ref, do_ref, dq_ref, dc_ref, dp_ref, ds_ref):
        i = pl.program_id(0)
        valid = _attn_valid(i)
        kvc = kvc_ref[...]
        kvp = kvp_ref[...]
        lane = lax.broadcasted_iota(jnp.int32, (1, nq), 1)
        dsink = jnp.zeros((1, nq), F32)
        for hk in range(nkv):
            kb = _band(kvp, kvc, hk * hd)
            vb = _band(kvp, kvc, kvw + hk * hd)
            qs = _stack_heads(q_ref, hk)
            dos = _stack_heads(do_ref, hk)
            p, psink = _attn_scores(qs, kb, _stack_sinks(sink_ref, hk), valid)
            delta = jnp.sum(dos.astype(F32) * _stack_heads(o_ref, hk), axis=-1, keepdims=True)
            dpv = lax.dot_general(dos, vb, (((1,), (1,)), ((), ())), preferred_element_type=F32)
            dsb = (p * (dpv - delta) * (hd ** -0.5)).astype(BF16)
            dqs = jnp.dot(dsb, kb, preferred_element_type=F32).astype(BF16)
            dkb = lax.dot_general(dsb, qs, (((0,), (0,)), ((), ())), preferred_element_type=F32)
            dvb = lax.dot_general(p.astype(BF16), dos, (((0,), (0,)), ((), ())), preferred_element_type=F32)
            sink_term = psink * delta
            for g in range(KV_RATIO):
                hq = hk * KV_RATIO + g
                dq_ref[:, hq * hd:(hq + 1) * hd] = dqs[g * WINDOW:(g + 1) * WINDOW]
                dsink = dsink + jnp.where(lane == hq, -jnp.sum(sink_term[g * WINDOW:(g + 1) * WINDOW]), 0.0)
            dp_ref[:, hk * hd:(hk + 1) * hd] = dkb[:WINDOW]
            dc_ref[:, hk * hd:(hk + 1) * hd] = dkb[WINDOW:]
            dp_ref[:, kvw + hk * hd:kvw + (hk + 1) * hd] = dvb[:WINDOW]
            dc_ref[:, kvw + hk * hd:kvw + (hk + 1) * hd] = dvb[WINDOW:]

        @pl.when(i == 0)
        def _():
            ds_ref[...] = dsink

        @pl.when(i > 0)
        def _():
            ds_ref[...] += dsink

    blk_a = pl.BlockSpec((WINDOW, aw), lambda i: (i, 0))
    blk_kv = pl.BlockSpec((WINDOW, 2 * kvw), lambda i: (i, 0))
    return pl.pallas_call(
        body, name=name, grid=(s // WINDOW,), in_specs=_attn_specs(aw, uw, kvw) + [blk_a, blk_a],
        out_specs=(blk_a, blk_kv, blk_kv, pl.BlockSpec((1, nq), lambda i: (0, 0))),
        out_shape=(jax.ShapeDtypeStruct((s, aw), BF16), jax.ShapeDtypeStruct((s, 2 * kvw), F32),
                   jax.ShapeDtypeStruct((s, 2 * kvw), F32), jax.ShapeDtypeStruct((1, nq), F32)),
        compiler_params=_cparams(("arbitrary",)),
    )(sinks, proj, proj, proj, attn, dattn)


def _assemble_dproj(du, dq, dkv_cur, dkv_prev, name):
    s, uw = du.shape
    aw = dq.shape[1]
    kv2 = dkv_cur.shape[1]
    nb = s // WINDOW

    def body(du_ref, dq_ref, dc_ref, dp_ref, o_ref):
        i = pl.program_id(0)
        o_ref[:, :uw] = du_ref[...].astype(BF16)
        o_ref[:, uw:uw + aw] = dq_ref[...]
        nxt = jnp.where(i < nb - 1, 1.0, 0.0)
        o_ref[:, uw + aw:] = (dc_ref[...] + nxt * dp_ref[...]).astype(BF16)

    return pl.pallas_call(
        body, name=name, grid=(nb,),
        in_specs=[_tok(WINDOW, uw), _tok(WINDOW, aw), _tok(WINDOW, kv2),
                  pl.BlockSpec((WINDOW, kv2), lambda i: (jnp.minimum(i + 1, nb - 1), 0))],
        out_specs=_tok(WINDOW, uw + aw + kv2), out_shape=jax.ShapeDtypeStruct((s, uw + aw + kv2), BF16),
        compiler_params=_cparams(("parallel",)),
    )(du, dq, dkv_cur, dkv_prev)


def _zoh(lr, li, ls, btr, bti):
    dt = jnp.exp(ls)
    mag = jnp.exp(lr * dt)
    ang = li * dt
    ar = mag * jnp.cos(ang)
    ai = mag * jnp.sin(ang)
    den = lr * lr + li * li
    fr = ((ar - 1.0) * lr + ai * li) / den
    fi = (ai * lr - (ar - 1.0) * li) / den
    return ar, ai, fr[None] * btr - fi[None] * bti, fr[None] * bti + fi[None] * btr


def _ssm_prep(lr, li, ls, btr, bti, name):
    def body(lr_ref, li_ref, ls_ref, btr_ref, bti_ref, ar_ref, ai_ref, bbr_ref, bbi_ref):
        ar, ai, bbr, bbi = _zoh(lr_ref[...], li_ref[...], ls_ref[...], btr_ref[...], bti_ref[...])
        ar_ref[...] = ar
        ai_ref[...] = ai
        bbr_ref[...] = bbr
        bbi_ref[...] = bbi

    s2 = jax.ShapeDtypeStruct(lr.shape, F32)
    s3 = jax.ShapeDtypeStruct(btr.shape, F32)
    return pl.pallas_call(body, name=name, out_shape=(s2, s2, s3, s3))(lr, li, ls, btr, bti)


def _ssm_prep_bwd(lr, li, ls, btr, bti, dar, dai, dbbr, dbbi, name):
    def body(lr_ref, li_ref, ls_ref, btr_ref, bti_ref, dar_ref, dai_ref, dbbr_ref, dbbi_ref, o1, o2, o3, o4, o5):
        _, vjp = jax.vjp(_zoh, lr_ref[...], li_ref[...], ls_ref[...], btr_ref[...], bti_ref[...])
        g = vjp((dar_ref[...], dai_ref[...], dbbr_ref[...], dbbi_ref[...]))
        for o, v in zip((o1, o2, o3, o4, o5), g):
            o[...] = v

    s2 = jax.ShapeDtypeStruct(lr.shape, F32)
    s3 = jax.ShapeDtypeStruct(btr.shape, F32)
    return pl.pallas_call(body, name=name, out_shape=(s2, s2, s2, s3, s3))(lr, li, ls, btr, bti, dar, dai, dbbr, dbbi)


def _state_tiles(ref):
    return [ref[:, cb * 128:(cb + 1) * 128] for cb in range(4)]


def _gather_rows(ref_re, ref_im, r, t):
    return jnp.concatenate([ref_re.at[cb][pl.ds(r, t, stride=8), :] for cb in range(4)]
                           + [ref_im.at[cb][pl.ds(r, t, stride=8), :] for cb in range(4)], axis=1)


def _scatter_rows(ref_re, ref_im, r, t, val):
    for cb in range(4):
        ref_re.at[cb][pl.ds(r, t, stride=8), :] = val[:, cb * 128:(cb + 1) * 128]
        ref_im.at[cb][pl.ds(r, t, stride=8), :] = val[:, PSTATES + cb * 128:PSTATES + (cb + 1) * 128]


def _ssm_fwd(proj, bp, cp, a_re, a_im, dvec, uw, name, t=128):
    s = proj.shape[0]
    npc = uw // PIECE
    assert npc == 8 and s % t == 0

    def body(u_ref, bp_ref, cp_ref, ar_ref, ai_ref, d_ref, y_ref, xr_ref, xi_ref, cr_ref, ci_ref):
        i = pl.program_id(0)

        @pl.when(i == 0)
        def _():
            cr_ref[...] = jnp.zeros_like(cr_ref)
            ci_ref[...] = jnp.zeros_like(ci_ref)

        for r in range(npc):
            bu = jnp.dot(u_ref[:, r * PIECE:(r + 1) * PIECE], bp_ref[r], preferred_element_type=F32)
            _scatter_rows(xr_ref, xi_ref, r, t, bu)
        ar = _state_tiles(ar_ref)
        ai = _state_tiles(ai_ref)

        def step(tt, carry):
            xr, xi = carry
            off = pl.multiple_of(tt * 8, 8)
            nr, ni = [], []
            for cb in range(4):
                vr = ar[cb] * xr[cb] - ai[cb] * xi[cb] + xr_ref[cb, pl.ds(off, 8), :]
                vi = ar[cb] * xi[cb] + ai[cb] * xr[cb] + xi_ref[cb, pl.ds(off, 8), :]
                xr_ref[cb, pl.ds(off, 8), :] = vr
                xi_ref[cb, pl.ds(off, 8), :] = vi
                nr.append(vr)
                ni.append(vi)
            return tuple(nr), tuple(ni)

        xr, xi = lax.fori_loop(0, t, step, (tuple(_state_tiles(cr_ref)), tuple(_state_tiles(ci_ref))), unroll=4)
        for cb in range(4):
            cr_ref[:, cb * 128:(cb + 1) * 128] = xr[cb]
            ci_ref[:, cb * 128:(cb + 1) * 128] = xi[cb]
        for r in range(npc):
            xs = _gather_rows(xr_ref, xi_ref, r, t).astype(BF16)
            y_ref[:, r * PIECE:(r + 1) * PIECE] = (
                jnp.dot(xs, cp_ref[r], preferred_element_type=F32)
                + d_ref[:, r * PIECE:(r + 1) * PIECE] * u_ref[:, r * PIECE:(r + 1) * PIECE].astype(F32))

    full3 = lambda shp: pl.BlockSpec(shp, lambda i: (0, 0, 0))
    full2 = lambda shp: pl.BlockSpec(shp, lambda i: (0, 0))
    xs_spec = pl.BlockSpec((4, t * 8, 128), lambda i: (0, i, 0))
    xs_shape = jax.ShapeDtypeStruct((4, s * 8, 128), F32)
    return pl.pallas_call(
        body, name=name, grid=(s // t,),
        in_specs=[pl.BlockSpec((t, uw), lambda i: (i, 0)), full3(bp.shape), full3(cp.shape), full2(a_re.shape), full2(a_im.shape),
                  full2(dvec.shape)],
        out_specs=(pl.BlockSpec((t, uw), lambda i: (i, 0)), xs_spec, xs_spec),
        out_shape=(jax.ShapeDtypeStruct((s, uw), F32), xs_shape, xs_shape),
        scratch_shapes=[pltpu.VMEM((8, PSTATES), F32), pltpu.VMEM((8, PSTATES), F32)],
        compiler_params=_cparams(("arbitrary",), 56),
    )(proj, bp, cp, a_re, a_im, dvec)


def _ssm_bwd(dy, proj, xs_re, xs_im, cpt, bpt, a_re, a_im, dvec, uw, name, t=128):
    s = proj.shape[0]
    npc = uw // PIECE
    nt = s // t
    assert npc == 8 and s % t == 0

    def body(dy_ref, u_ref, xr_ref, xi_ref, hr_ref, hi_ref, cpt_ref, bpt_ref, ar_ref, ai_ref, d_ref,
             du_ref, dbp_ref, dcp_ref, dar_ref, dai_ref, dd_ref, gr_ref, gi_ref, lr_ref, li_ref):
        i = pl.program_id(0)

        @pl.when(i == 0)
        def _():
            lr_ref[...] = jnp.zeros_like(lr_ref)
            li_ref[...] = jnp.zeros_like(li_ref)
            dbp_ref[...] = jnp.zeros_like(dbp_ref)
            dcp_ref[...] = jnp.zeros_like(dcp_ref)
            dar_ref[...] = jnp.zeros_like(dar_ref)
            dai_ref[...] = jnp.zeros_like(dai_ref)
            dd_ref[...] = jnp.zeros_like(dd_ref)

        dyb = dy_ref[...].astype(BF16)
        for r in range(npc):
            gx = jnp.dot(dyb[:, r * PIECE:(r + 1) * PIECE], cpt_ref[r], preferred_element_type=F32)
            _scatter_rows(gr_ref, gi_ref, r, t, gx)
        ar = _state_tiles(ar_ref)
        ai = _state_tiles(ai_ref)

        def adjoint(off, lam_r, lam_i, xpr, xpi, acc_r, acc_i):
            nr, ni, qr, qi = [], [], [], []
            for cb in range(4):
                vr = gr_ref[cb, pl.ds(off, 8), :] + ar[cb] * lam_r[cb] + ai[cb] * lam_i[cb]
                vi = gi_ref[cb, pl.ds(off, 8), :] + ar[cb] * lam_i[cb] - ai[cb] * lam_r[cb]
                gr_ref[cb, pl.ds(off, 8), :] = vr
                gi_ref[cb, pl.ds(off, 8), :] = vi
                nr.append(vr)
                ni.append(vi)
                qr.append(acc_r[cb] + vr * xpr[cb] + vi * xpi[cb])
                qi.append(acc_i[cb] + vi * xpr[cb] - vr * xpi[cb])
            return tuple(nr), tuple(ni), tuple(qr), tuple(qi)

        def step(j, carry):
            lam_r, lam_i, acc_r, acc_i = carry
            tt = t - 1 - j
            off = pl.multiple_of(tt * 8, 8)
            offp = pl.multiple_of(tt * 8 - 8, 8)
            xpr = [xr_ref[cb, pl.ds(offp, 8), :] for cb in range(4)]
            xpi = [xi_ref[cb, pl.ds(offp, 8), :] for cb in range(4)]
            return adjoint(off, lam_r, lam_i, xpr, xpi, acc_r, acc_i)

        zero4 = tuple(jnp.zeros((8, 128), F32) for _ in range(4))
        carry = lax.fori_loop(0, t - 1, step, (tuple(_state_tiles(lr_ref)), tuple(_state_tiles(li_ref)), zero4, zero4), unroll=4)
        has_prev = jnp.where(i < nt - 1, 1.0, 0.0)
        xpr = [hr_ref[cb] * has_prev for cb in range(4)]
        xpi = [hi_ref[cb] * has_prev for cb in range(4)]
        lam_r, lam_i, acc_r, acc_i = adjoint(0, carry[0], carry[1], xpr, xpi, carry[2], carry[3])
        for cb in range(4):
            lr_ref[:, cb * 128:(cb + 1) * 128] = lam_r[cb]
            li_ref[:, cb * 128:(cb + 1) * 128] = lam_i[cb]
            dar_ref[:, cb * 128:(cb + 1) * 128] += acc_r[cb]
            dai_ref[:, cb * 128:(cb + 1) * 128] += acc_i[cb]

        dyv = dy_ref[...]
        uv = u_ref[...]
        dd_ref[...] += jnp.sum(dyv * uv.astype(F32), axis=0, keepdims=True)
        for r in range(npc):
            lam = _gather_rows(gr_ref, gi_ref, r, t).astype(BF16)
            sl = slice(r * PIECE, (r + 1) * PIECE)
            du_ref[:, sl] = jnp.dot(lam, bpt_ref[r], preferred_element_type=F32) + d_ref[:, sl] * dyv[:, sl]
            dbp_ref[r] += lax.dot_general(uv[:, sl], lam, (((0,), (0,)), ((), ())), preferred_element_type=F32)
            xs = _gather_rows(xr_ref, xi_ref, r, t).astype(BF16)
            dcp_ref[r] += lax.dot_general(xs, dyb[:, sl], (((0,), (0,)), ((), ())), preferred_element_type=F32)

    rev = lambda i: (nt - 1 - i, 0)
    full3 = lambda shp: pl.BlockSpec(shp, lambda i: (0, 0, 0))
    full2 = lambda shp: pl.BlockSpec(shp, lambda i: (0, 0))
    xs_spec = pl.BlockSpec((4, t * 8, 128), lambda i: (0, nt - 1 - i, 0))
    halo_spec = pl.BlockSpec((4, 8, 128), lambda i: (0, jnp.maximum((nt - 1 - i) * t - 1, 0), 0))
    st = jax.ShapeDtypeStruct((8, PSTATES), F32)
    return pl.pallas_call(
        body, name=name, grid=(nt,),
        in_specs=[pl.BlockSpec((t, uw), rev), pl.BlockSpec((t, uw), rev), xs_spec, xs_spec, halo_spec, halo_spec,
                  full3(cpt.shape), full3(bpt.shape), full2(a_re.shape), full2(a_im.shape), full2(dvec.shape)],
        out_specs=(pl.BlockSpec((t, uw), rev), full3((npc, PIECE, 2 * PSTATES)), full3((npc, 2 * PSTATES, PIECE)),
                   full2((8, PSTATES)), full2((8, PSTATES)), full2((1, uw))),
        out_shape=(jax.ShapeDtypeStruct((s, uw), F32), jax.ShapeDtypeStruct((npc, PIECE, 2 * PSTATES), F32),
                   jax.ShapeDtypeStruct((npc, 2 * PSTATES, PIECE), F32), st, st, jax.ShapeDtypeStruct((1, uw), F32)),
        scratch_shapes=[pltpu.VMEM((4, t * 8, 128), F32), pltpu.VMEM((4, t * 8, 128), F32),
                        pltpu.VMEM((8, PSTATES), F32), pltpu.VMEM((8, PSTATES), F32)],
        compiler_params=_cparams(("arbitrary",), 56),
    )(dy, proj, xs_re, xs_im, xs_re, xs_im, cpt, bpt, a_re, a_im, dvec)


_RC = 16
_LC = 128


def _conv3_block(xv, halo, w, b, row):
    h6, h7 = halo[_RC - 2:_RC - 1], halo[_RC - 1:_RC]
    x1 = jnp.where(row == 0, h7, pltpu.roll(xv, 1, 0))
    x2 = jnp.where(row == 0, h6, jnp.where(row == 1, h7, pltpu.roll(xv, 2, 0)))
    return ((b + x2 * w[0:1]) + x1 * w[1:2]) + xv * w[2:3], (x2, x1, xv)


def _ffn_tiles(s, f):
    tm = min(256, s)
    tn = f // 4 if (f // 4) % _LC == 0 else f
    assert tm % _RC == 0 and tn % _LC == 0
    return tm, tn


def _conv_glu_fwd(up0, cw, cb, name):
    _, s, f = up0.shape
    tm, tn = _ffn_tiles(s, f)
    hb = tm // _RC

    def body(x_ref, h_ref, w_ref, b_ref, a_ref):
        first = jnp.where(pl.program_id(0) > 0, 1.0, 0.0)
        row = lax.broadcasted_iota(jnp.int32, (tm, tn), 0)
        ups = [_conv3_block(x_ref[p].astype(F32), h_ref[p].astype(F32) * first, w_ref[p], b_ref[p], row)[0] for p in range(2)]
        a_ref[...] = (_gelu(ups[1]) * ups[0]).astype(BF16)

    return pl.pallas_call(
        body, name=name, grid=(s // tm, f // tn),
        in_specs=[pl.BlockSpec((2, tm, tn), lambda i, j: (0, i, j)),
                  pl.BlockSpec((2, _RC, tn), lambda i, j: (0, jnp.maximum(i * hb - 1, 0), j)),
                  pl.BlockSpec((2, 3, tn), lambda i, j: (0, 0, j)), pl.BlockSpec((2, 1, tn), lambda i, j: (0, 0, j))],
        out_specs=pl.BlockSpec((tm, tn), lambda i, j: (i, j)), out_shape=jax.ShapeDtypeStruct((s, f), BF16),
        compiler_params=_cparams(("parallel", "parallel")),
    )(up0, up0, cw, cb)


def _ffn_bwd_gate(da, up0, cw, cb, name):
    _, s, f = up0.shape
    tm, tn = _ffn_tiles(s, f)
    hb = tm // _RC

    def body(da_ref, x_ref, h_ref, w_ref, b_ref, d_ref, dw_ref, db_ref):
        i = pl.program_id(1)
        first = jnp.where(i > 0, 1.0, 0.0)

        @pl.when(i == 0)
        def _():
            dw_ref[...] = jnp.zeros_like(dw_ref)
            db_ref[...] = jnp.zeros_like(db_ref)

        row = lax.broadcasted_iota(jnp.int32, (tm, tn), 0)
        ups, taps = [], []
        for p in range(2):
            up, tap = _conv3_block(x_ref[p].astype(F32), h_ref[p].astype(F32) * first, w_ref[p], b_ref[p], row)
            ups.append(up)
            taps.append(tap)
        dav = da_ref[...]
        gate, dgate = _gelu_and_grad(ups[1])
        douts = (dav * gate, dav * ups[0] * dgate)
        for p in range(2):
            d_ref[p] = douts[p].astype(BF16)
            db_ref[p] += jnp.sum(douts[p], axis=0, keepdims=True)
            for kk in range(3):
                dw_ref[p, kk:kk + 1, :] += jnp.sum(douts[p] * taps[p][kk], axis=0, keepdims=True)

    return pl.pallas_call(
        body, name=name, grid=(f // tn, s // tm),
        in_specs=[pl.BlockSpec((tm, tn), lambda j, i: (i, j)), pl.BlockSpec((2, tm, tn), lambda j, i: (0, i, j)),
                  pl.BlockSpec((2, _RC, tn), lambda j, i: (0, jnp.maximum(i * hb - 1, 0), j)),
                  pl.BlockSpec((2, 3, tn), lambda j, i: (0, 0, j)), pl.BlockSpec((2, 1, tn), lambda j, i: (0, 0, j))],
        out_specs=(pl.BlockSpec((2, tm, tn), lambda j, i: (0, i, j)), pl.BlockSpec((2, 3, tn), lambda j, i: (0, 0, j)),
                   pl.BlockSpec((2, 1, tn), lambda j, i: (0, 0, j))),
        out_shape=(jax.ShapeDtypeStruct((2, s, f), BF16), jax.ShapeDtypeStruct((2, 3, f), F32), jax.ShapeDtypeStruct((2, 1, f), F32)),
        compiler_params=_cparams(("parallel", "arbitrary")),
    )(da, up0, up0, cw, cb)


def _conv_bwd(dup, cw, name):
    _, s, f = dup.shape
    tm, tn = _ffn_tiles(s, f)
    hb = tm // _RC
    nb = s // tm

    def body(d_ref, h_ref, w_ref, o_ref):
        last = jnp.where(pl.program_id(0) < nb - 1, 1.0, 0.0)
        row = lax.broadcasted_iota(jnp.int32, (tm, tn), 0)
        for p in range(2):
            d = d_ref[p].astype(F32)
            h = h_ref[p].astype(F32) * last
            d1 = jnp.where(row == tm - 1, h[0:1], pltpu.roll(d, tm - 1, 0))
            d2 = jnp.where(row == tm - 1, h[1:2], jnp.where(row == tm - 2, h[0:1], pltpu.roll(d, tm - 2, 0)))
            w = w_ref[p]
            o_ref[p] = (d * w[2:3] + d1 * w[1:2] + d2 * w[0:1]).astype(BF16)

    return pl.pallas_call(
        body, name=name, grid=(nb, f // tn),
        in_specs=[pl.BlockSpec((2, tm, tn), lambda i, j: (0, i, j)),
                  pl.BlockSpec((2, _RC, tn), lambda i, j: (0, jnp.minimum((i + 1) * hb, s // _RC - 1), j)),
                  pl.BlockSpec((2, 3, tn), lambda i, j: (0, 0, j))],
        out_specs=pl.BlockSpec((2, tm, tn), lambda i, j: (0, i, j)), out_shape=jax.ShapeDtypeStruct((2, s, f), BF16),
        compiler_params=_cparams(("parallel", "parallel")),
    )(dup, dup, cw)


def _ada_part(c_all, w_ada, name):
    nb, d = c_all.shape
    depth, _, cols = w_ada.shape
    tn = 1024 if cols % 1024 == 0 else cols

    def body(c_ref, w_ref, o_ref, ca_ref):
        cv = c_ref[...]
        ca = cv * _sigmoid(cv)
        ca_ref[...] = ca
        o_ref[...] = jnp.dot(ca.astype(BF16), w_ref[...].astype(BF16), preferred_element_type=F32)

    return pl.pallas_call(
        body, name=name, grid=(depth, cols // tn),
        in_specs=[pl.BlockSpec((nb, d), lambda l, j: (0, 0)), pl.BlockSpec((None, d, tn), lambda l, j: (l, 0, j))],
        out_specs=(pl.BlockSpec((None, nb, tn), lambda l, j: (l, 0, j)), pl.BlockSpec((nb, d), lambda l, j: (0, 0))),
        out_shape=(jax.ShapeDtypeStruct((depth, nb, cols), F32), jax.ShapeDtypeStruct((nb, d), F32)),
        compiler_params=_cparams(("arbitrary", "arbitrary")),
    )(c_all, w_ada)


def _ada_select(gath, b_ada, name):
    depth, n6 = b_ada.shape
    cols = gath.shape[1]

    def body(g_ref, b_ref, o_ref):
        me = 4 * lax.axis_index("x") + 2 * lax.axis_index("y") + lax.axis_index("c")
        for l in range(depth):
            for j in range(n6 // cols):
                row = (2 * j) * (8 * depth) + l * 8 + me
                o_ref[l:l + 1, j * cols:(j + 1) * cols] = g_ref[pl.ds(row, 1), :] + b_ref[l:l + 1, j * cols:(j + 1) * cols]

    return pl.pallas_call(body, name=name, out_shape=jax.ShapeDtypeStruct((depth, n6), F32))(gath, b_ada)


def _wada_grad(ca_t, d_sel, name):
    d, nb = ca_t.shape
    depth, _, cols = d_sel.shape
    tm = min(256, d)

    def body(a_ref, g_ref, o_ref):
        acc = a_ref[:, 0:1] * g_ref[0:1, :]
        for b in range(1, nb):
            acc = acc + a_ref[:, b:b + 1] * g_ref[b:b + 1, :]
        o_ref[...] = acc

    return pl.pallas_call(
        body, name=name, grid=(depth, d // tm),
        in_specs=[pl.BlockSpec((tm, nb), lambda l, i: (i, 0)), pl.BlockSpec((None, nb, cols), lambda l, i: (l, 0, 0))],
        out_specs=pl.BlockSpec((None, tm, cols), lambda l, i: (l, i, 0)), out_shape=jax.ShapeDtypeStruct((depth, d, cols), F32),
        compiler_params=_cparams(("parallel", "parallel")),
    )(ca_t, d_sel)


def _block_rows(r, c):
    tr = r
    for cand in (2048, 1024, 512, 256, 128, 64, 32, 16, 8):
        if r % cand == 0 and cand * c * 4 <= MIB:
            tr = cand
            break
    else:
        for cand in (8, 16, 32):
            if r % cand == 0:
                tr = cand
                break
    return tr


def _adamw(w, g, m, v, name):
    nl, r, c = w.shape
    tr = _block_rows(r, c)
    c1 = 1.0 - ADAM_B1 ** ADAM_STEP
    c2 = 1.0 - ADAM_B2 ** ADAM_STEP

    def body(w_ref, g_ref, m_ref, v_ref, d_ref, nm_ref, nv_ref):
        gv = g_ref[...]
        nm = ADAM_B1 * m_ref[...] + (1.0 - ADAM_B1) * gv
        nv = ADAM_B2 * v_ref[...] + (1.0 - ADAM_B2) * (gv * gv)
        d_ref[...] = -ADAM_LR * ((nm / c1) / (jnp.sqrt(nv / c2) + ADAM_EPS) + ADAM_WD * w_ref[...])
        nm_ref[...] = nm
        nv_ref[...] = nv

    spec = pl.BlockSpec((None, tr, c), lambda l, i: (l, i, 0))
    shp = jax.ShapeDtypeStruct((nl, r, c), F32)
    return pl.pallas_call(
        body, name=name, grid=(nl, r // tr), in_specs=[spec] * 4, out_specs=(spec,) * 3, out_shape=(shp,) * 3,
        compiler_params=_cparams(("parallel", "parallel")),
    )(w, g, m, v)


def _sum_slots(x, nslots, name, out_dtype=F32):
    r = x.shape[0] // nslots
    c = x.shape[1]
    tr = _block_rows(r, c)
    nbk = r // tr

    def body(*refs):
        acc = refs[0][...].astype(F32)
        for k in range(1, nslots):
            acc = acc + refs[k][...].astype(F32)
        refs[nslots][...] = acc.astype(out_dtype)

    specs = [pl.BlockSpec((tr, c), functools.partial(lambda k, i: (k * nbk + i, 0), k)) for k in range(nslots)]
    return pl.pallas_call(
        body, name=name, grid=(nbk,), in_specs=specs, out_specs=pl.BlockSpec((tr, c), lambda i: (i, 0)),
        out_shape=jax.ShapeDtypeStruct((r, c), out_dtype), compiler_params=_cparams(("parallel",)),
    )(*([x] * nslots))


def _sum_slots_into(buf, x, layer, cidx, nslots, name):
    _, r2, c = buf.shape
    h = r2 // 2
    tr = _block_rows(h, c)
    nbk = h // tr

    def body(c_ref, b_ref, *refs):
        acc = refs[0][...].astype(F32)
        for k in range(1, nslots):
            acc = acc + refs[k][...].astype(F32)
        refs[nslots][...] = acc

    specs = [pl.BlockSpec((tr, c), functools.partial(lambda k, i, cr: (k * nbk + i, 0), k)) for k in range(nslots)]
    grid_spec = pltpu.PrefetchScalarGridSpec(
        num_scalar_prefetch=1, grid=(nbk,), in_specs=[_ANY] + specs,
        out_specs=pl.BlockSpec((None, tr, c), lambda i, cr: (layer, cr[0] * nbk + i, 0)))
    return pl.pallas_call(
        body, name=name, grid_spec=grid_spec, out_shape=jax.ShapeDtypeStruct(buf.shape, F32), input_output_aliases={1: 0},
        compiler_params=_cparams(("parallel",)),
    )(cidx, buf, *([x] * nslots))


def _share_halves(bufs, name):
    nw = len(bufs)

    def body(*refs):
        ins, outs = refs[:nw], refs[nw:2 * nw]
        send_sems, recv_sems = refs[2 * nw:]
        x, y, c = _mesh_pos()
        cps = []
        for w in range(nw):
            mine = _half_rows(bufs[w].shape[1], c, 8)
            cps.append(_remote(ins[w].at[:, mine, :], outs[w].at[:, mine, :], send_sems, recv_sems, w, (x, y, 1 - c)))
            cps[-1].start()
        for w in range(nw):
            other = outs[w].at[:, _half_rows(bufs[w].shape[1], 1 - c, 8), :]
            _remote(other, other, send_sems, recv_sems, w, (x, y, c)).wait_recv()
        for cp in cps:
            cp.wait_send()

    return pl.pallas_call(
        body, name=name, out_shape=[jax.ShapeDtypeStruct(t.shape, t.dtype) for t in bufs], in_specs=[_ANY] * nw, out_specs=[_ANY] * nw,
        input_output_aliases={w: w for w in range(nw)},
        scratch_shapes=[pltpu.SemaphoreType.DMA((nw,)), pltpu.SemaphoreType.DMA((nw,))],
    )(*bufs)


def _pick(dim, prefs):
    for p in prefs:
        if dim % p == 0:
            return p
    return dim


def _mm(a, b, m, n, k, name, out_dtype, **kw):
    tm = _pick(m, (1408, 1152, 1024, 512, 256, 128))
    tn = _pick(n, (1408, 1152, 1024, 512, 256, 128))
    kdiv = k // max(kw.get("a_stack", 0), kw.get("b_stack", 0) if kw.get("tb") else 0, 1)
    osize = jnp.dtype(out_dtype).itemsize
    tk = kdiv
    for cut in (1, 2, 4, 8, 16):
        tk = kdiv // cut
        vmem = 2 * 2 * tk * (tm + tn) + tm * tn * (2 * osize + 4 + (4 if tk < k else 0))
        if kdiv % cut == 0 and tk % 128 == 0 and vmem <= _MATMUL_VMEM_BUDGET:
            break
    return _matmul(a, b, m=m, n=n, k=k, tm=tm, tn=tn, tk=tk, out_dtype=out_dtype, name=name, **kw)


def _ssm_layout(p):
    g, st = p["lam_re"].shape
    npc = g * st // PSTATES
    lr = p["lam_re"].reshape(npc, PSTATES)
    li = p["lam_im"].reshape(npc, PSTATES)
    ls = jnp.broadcast_to(p["log_step"][:, None], (g, st)).reshape(npc, PSTATES)
    btr = jnp.transpose(p["ssm_b_re"], (2, 0, 1)).reshape(SSM_GROUP, npc, PSTATES)
    bti = jnp.transpose(p["ssm_b_im"], (2, 0, 1)).reshape(SSM_GROUP, npc, PSTATES)
    return lr, li, ls, btr, bti


def _ssm_pieces(bbr, bbi, c_re, c_im):
    npc = bbr.shape[1]
    gl = PSTATES // STATE
    eye = jnp.eye(gl, dtype=bool)

    def b_piece(bb):
        t = jnp.transpose(bb.reshape(SSM_GROUP, npc, gl, STATE), (1, 2, 0, 3))
        full = jnp.where(eye[None, :, None, :, None], t[:, :, :, None, :], 0.0)
        return full.reshape(npc, gl * SSM_GROUP, PSTATES)

    def c_piece(cc):
        t = jnp.transpose(cc.reshape(npc, gl, SSM_GROUP, STATE), (0, 1, 3, 2))
        full = jnp.where(eye[None, :, None, :, None], t[:, :, :, None, :], 0.0)
        return full.reshape(npc, PSTATES, gl * SSM_GROUP)

    bp = jnp.concatenate([b_piece(bbr), b_piece(bbi)], axis=2).astype(BF16)
    cp = jnp.concatenate([c_piece(c_re), c_piece(-c_im)], axis=1).astype(BF16)
    return bp, cp, jnp.swapaxes(bp, 1, 2), jnp.swapaxes(cp, 1, 2)


def _ssm_unpieces(dbp, dcp):
    npc = dbp.shape[0]
    gl = PSTATES // STATE
    idx = jnp.arange(gl)

    def b_diag(x):
        d = x.reshape(npc, gl, SSM_GROUP, gl, STATE)[:, idx, :, idx, :]
        return jnp.transpose(d, (2, 1, 0, 3)).reshape(SSM_GROUP, npc, PSTATES)

    def c_diag(x):
        d = x.reshape(npc, gl, STATE, gl, SSM_GROUP)[:, idx, :, idx, :]
        return jnp.transpose(d, (1, 0, 3, 2)).reshape(npc * gl, SSM_GROUP, STATE)

    return b_diag(dbp[:, :, :PSTATES]), b_diag(dbp[:, :, PSTATES:]), c_diag(dcp[:, :PSTATES, :]), -c_diag(dcp[:, PSTATES:, :])


class _LayerWeights:
    def __init__(self, fetch):
        self._fetch = fetch
        self._got = {}

    def group(self, g, after=None):
        if g not in self._got:
            self._got[g] = self._fetch(g, after)
        return self._got[g]


def _layer_fwd(l, x, ada6, weights, p):
    s, d = x.shape
    sh_m, sc_m, gt_m, sh_f, sc_f, gt_f = ada6
    w = dict(weights.group("mix", x))
    uw = w["w_glu"].shape[0]
    aw = w["w_out"].shape[0] - uw
    ncol = w["w_in"].shape[0]
    row = lambda v: v.reshape(1, -1)
    n = lambda t: f"l{l}_{t}"

    h = _pre_fwd(x, row(p["g_pre_mix"]), sc_m, sh_m, n("pre_mix"))
    proj = _mm(h, w["w_in"], s, ncol, d, n("proj"), BF16, tb=True)
    attn = _attn_fwd(proj, p["attn_sinks"], aw, uw, n("attn_fwd"))
    zin = _ssm_layout(p)
    a_re, a_im, bbr, bbi = _ssm_prep(*zin, n("ssm_prep"))
    bp, cp, bpt, cpt = _ssm_pieces(bbr, bbi, p["ssm_c_re"], p["ssm_c_im"])
    dvec = p["ssm_d"].reshape(1, uw)
    y, xs_re, xs_im = _ssm_fwd(proj, bp, cp, a_re, a_im, dvec, uw, n("ssm_fwd"), t=256 if s % 256 == 0 else 128)
    z = _gelu_fwd(y, n("gelu_fwd"))
    gl = _mm(z, w["w_glu"], s, uw, uw, n("glu"), F32)
    merged = _merge_fwd(attn, y, gl, row(p["g_attn_out"]), row(p["g_ssm_out"]), n("merge_fwd"))
    mix = _mm(merged, w["w_out"], s, d, aw + uw, n("out_proj"), BF16)
    x1 = _post_fwd(x, mix, row(p["g_post_mix"]), gt_m, n("post_mix"))

    w.update(weights.group("ffn", x1))
    f = w["w_down"].shape[0]
    h2 = _pre_fwd(x1, row(p["g_pre_ffn"]), sc_f, sh_f, n("pre_ffn"))
    up0 = _mm(h2, w["w_up"], s, 2 * f, d, n("up_proj"), BF16, b_stack=w["w_up"].shape[0], o_stack=2)
    cw2 = jnp.transpose(p["conv_w"].reshape(3, 2, f), (1, 0, 2))
    cb2 = p["conv_b"].reshape(2, 1, f)
    act = _conv_glu_fwd(up0, cw2, cb2, n("conv_glu"))
    ff = _mm(act, w["w_down"], s, d, f, n("down_proj"), BF16)
    x2 = _post_fwd(x1, ff, row(p["g_post_ffn"]), gt_f, n("post_ffn"))
    saved = dict(x=x, h=h, proj=proj, attn=attn, zin=zin, a_re=a_re, a_im=a_im, bpt=bpt, cpt=cpt, dvec=dvec, y=y, xs_re=xs_re,
                 xs_im=xs_im, z=z, gl=gl, merged=merged, mix=mix, x1=x1, h2=h2, up0=up0, cw2=cw2, cb2=cb2, act=act, ff=ff)
    return x2, saved


def _layer_bwd(l, dx2, ada6, weights, p, sv, on_grads):
    s, d = dx2.shape
    sh_m, sc_m, gt_m, sh_f, sc_f, gt_f = ada6
    w = {**weights.group("mix"), **weights.group("ffn")}
    uw = w["w_glu"].shape[0]
    aw = w["w_out"].shape[0] - uw
    ncol = w["w_in"].shape[0]
    f = w["w_down"].shape[0]
    nst = w["w_up"].shape[0]
    row = lambda v: v.reshape(1, -1)
    n = lambda t: f"l{l}_{t}"
    gw, gs = {}, {}

    dff, dgt_f, gs["g_post_ffn"] = _post_bwd(dx2, sv["ff"], row(p["g_post_ffn"]), gt_f, n("post_ffn_bwd"))
    gw["w_down"] = _mm(sv["act"], dff, f, d, s, n("down_dw"), BF16, ta=True)
    dact = _mm(dff, w["w_down"], s, f, d, n("down_dx"), F32, tb=True)
    dup, dcw2, dcb2 = _ffn_bwd_gate(dact, sv["up0"], sv["cw2"], sv["cb2"], n("ffn_gate_bwd"))
    gs["conv_w"] = jnp.transpose(dcw2, (1, 0, 2)).reshape(3, 2 * f)
    gs["conv_b"] = dcb2.reshape(2 * f)
    dup0 = _conv_bwd(dup, sv["cw2"], n("conv_bwd"))
    gw["w_up"] = _mm(sv["h2"], dup0, d, 2 * f, s, n("up_dw"), BF16, ta=True, b_stack=2, o_stack=nst)
    dh2 = _mm(dup0, w["w_up"], s, d, 2 * f, n("up_dx"), BF16, tb=True, a_stack=2, b_stack=nst)
    dx1, dsh_f, dsc_f, gs["g_pre_ffn"] = _pre_bwd(dx2, dh2, sv["x1"], row(p["g_pre_ffn"]), sc_f, n("pre_ffn_bwd"))
    token = on_grads(l, "ffn", {k: gw.pop(k) for k in ("w_up", "w_down")})
    if token is not None:
        gt_m = gt_m + token[0:1, 0:1]

    dmix, dgt_m, gs["g_post_mix"] = _post_bwd(dx1, sv["mix"], row(p["g_post_mix"]), gt_m, n("post_mix_bwd"))
    gw["w_out"] = _mm(sv["merged"], dmix, aw + uw, d, s, n("out_dw"), BF16, ta=True)
    dmerged = _mm(dmix, w["w_out"], s, aw + uw, d, n("out_dx"), BF16, tb=True)
    dattn, dgl, dzd, gs["g_attn_out"], gs["g_ssm_out"] = _merge_bwd(
        dmerged, sv["attn"], sv["y"], sv["gl"], row(p["g_attn_out"]), row(p["g_ssm_out"]), n("merge_bwd"))
    gw["w_glu"] = _mm(sv["z"], dgl, uw, uw, s, n("glu_dw"), BF16, ta=True)
    dz2 = _mm(dgl, w["w_glu"], s, uw, uw, n("glu_dx"), F32, tb=True)
    dy = _gelu_bwd(dzd, dz2, sv["y"], n("gelu_bwd"))
    du, dbp, dcp, dar, dai, dd = _ssm_bwd(dy, sv["proj"], sv["xs_re"], sv["xs_im"], sv["cpt"], sv["bpt"], sv["a_re"], sv["a_im"],
                                          sv["dvec"], uw, n("ssm_bwd"))
    dq, dkv_c, dkv_p, dsinks = _attn_bwd(sv["proj"], p["attn_sinks"], sv["attn"], dattn, aw, uw, n("attn_bwd"))
    dproj = _assemble_dproj(du, dq, dkv_c, dkv_p, n("dproj"))
    gw["w_in"] = _mm(dproj, sv["h"], ncol, d, s, n("in_dw"), BF16, ta=True)
    dh = _mm(dproj, w["w_in"], s, d, ncol, n("in_dx"), BF16)
    dx0, dsh_m, dsc_m, gs["g_pre_mix"] = _pre_bwd(dx1, dh, sv["x"], row(p["g_pre_mix"]), sc_m, n("pre_mix_bwd"))
    token = on_grads(l, "mix", {k: gw.pop(k) for k in ("w_in", "w_glu", "w_out")})

    dbbr, dbbi, dc_re, dc_im = _ssm_unpieces(dbp, dcp)
    dlr, dli, dls, dbtr, dbti = _ssm_prep_bwd(*sv["zin"], dar, dai, dbbr, dbbi, n("ssm_prep_bwd"))
    g, st = p["lam_re"].shape
    gs["lam_re"] = dlr.reshape(g, st)
    gs["lam_im"] = dli.reshape(g, st)
    gs["log_step"] = jnp.sum(dls.reshape(g, st), axis=1)
    gs["ssm_b_re"] = jnp.transpose(dbtr.reshape(SSM_GROUP, g, st), (1, 2, 0))
    gs["ssm_b_im"] = jnp.transpose(dbti.reshape(SSM_GROUP, g, st), (1, 2, 0))
    gs["ssm_c_re"] = dc_re
    gs["ssm_c_im"] = dc_im
    gs["ssm_d"] = dd.reshape(p["ssm_d"].shape)
    gs["attn_sinks"] = dsinks.reshape(-1)
    gs["b_ada"] = jnp.concatenate([dsh_m, dsc_m, dgt_m, dsh_f, dsc_f, dgt_f], axis=1).reshape(-1)
    for key in ("g_post_ffn", "g_pre_ffn", "g_post_mix", "g_attn_out", "g_ssm_out", "g_pre_mix"):
        gs[key] = gs[key].reshape(-1)
    return dx0, gs, token


def _local_step(x, tgt, ada, wl, pl_small, on_grads):
    d = x.shape[1]
    depth = len(wl)
    ada6 = [[ada[l:l + 1, k * d:(k + 1) * d] for k in range(6)] for l in range(depth)]
    saved = []
    h = x
    for l in range(depth):
        h, sv = _layer_fwd(l, h, ada6[l], wl[l], pl_small[l])
        saved.append(sv)
    dy, lsum = _loss_head(h, tgt, "loss_head")
    loss = 0.5 * lsum[0, 0] / d
    gss = [None] * depth
    dx = dy
    for l in reversed(range(depth)):
        dx, gss[l], token = _layer_bwd(l, dx, ada6[l], wl[l], pl_small[l], saved[l], on_grads)
        if token is not None and l > 0:
            ada6[l - 1][5] = ada6[l - 1][5] + token[0:1, 0:1]
    return loss, dx, gss


_ANY = pl.BlockSpec(memory_space=pl.ANY)


def _mesh_pos():
    return lax.axis_index("x"), lax.axis_index("y"), lax.axis_index("c")


def _other_chips(x, y):
    return [(1 - x, y), (x, 1 - y), (1 - x, 1 - y)]


def _remote(src, dst, send_sems, recv_sems, k, to):
    return pltpu.make_async_remote_copy(src_ref=src, dst_ref=dst, send_sem=send_sems.at[k], recv_sem=recv_sems.at[k],
                                        device_id=to, device_id_type=MESH)


def _allgather8(xs, name):
    m, n = xs.shape

    def body(x_ref, out_ref, send_sems, recv_sems):
        x, y, c = _mesh_pos()
        me, sibling = (x, y, c), (x, y, 1 - c)
        chips = _other_chips(x, y)

        def rows(px, py, pc):
            return out_ref.at[pl.ds((4 * px + 2 * py + pc) * m, m), :]

        def copy(k, block, to, src=None):
            return _remote(rows(*block) if src is None else src, rows(*block), send_sems, recv_sems, k, to)

        first = [copy(0, me, sibling, src=x_ref)]
        first += [copy(1 + j, me, (*chip, c), src=x_ref) for j, chip in enumerate(chips)]
        for cp in first:
            cp.start()
        passed = [copy(4 + j, (*chip, c), sibling) for j, chip in enumerate(chips)]
        for j, chip in enumerate(chips):
            copy(1 + j, (*chip, c), me).wait_recv()
            passed[j].start()
        copy(0, sibling, me).wait_recv()
        for j, chip in enumerate(chips):
            copy(4 + j, (*chip, 1 - c), me).wait_recv()
        for cp in first + passed:
            cp.wait_send()

    out = pl.pallas_call(
        body, name=name, out_shape=jax.ShapeDtypeStruct((8 * m, n), xs.dtype), in_specs=[_ANY], out_specs=_ANY,
        scratch_shapes=[pltpu.SemaphoreType.DMA((7,)), pltpu.SemaphoreType.DMA((7,))],
    )(xs)
    x, y, c = _mesh_pos()
    return lax.dynamic_update_slice(out, xs, ((4 * x + 2 * y + c) * m, 0))


def _half_rows(ref_rows, half, align):
    h = ref_rows // 2
    return pl.ds(pl.multiple_of(half * h, align), h)


_HBM = pl.BlockSpec(memory_space=pltpu.HBM)
_SEMS = pl.BlockSpec(memory_space=pltpu.SEMAPHORE)
_EFFECT = pltpu.SideEffectType.DATAFLOW_SIDE_EFFECTING
_TOKEN = jax.ShapeDtypeStruct((8, 128), F32)


def _in_hbm(arrays):
    return [pltpu.with_memory_space_constraint(a, pltpu.HBM) for a in arrays]


def _chip_copy(kind, srcs, lands, w, q, chip, mine, c, send, recv, k):
    if kind == "gather":
        rows = _half_rows(srcs[w].shape[0], c, 16)
        return _remote(srcs[w].at[rows, :], lands[w].at[mine, rows, :], send, recv, k, (*chip, c))
    return _remote(srcs[w].at[2 * chip[0] + chip[1]], lands[w].at[mine], send, recv, k, (*chip, c))


def _chip_landing(kind, srcs, lands, w, chip, c):
    if kind == "gather":
        return lands[w].at[2 * chip[0] + chip[1], _half_rows(srcs[w].shape[0], c, 16), :]
    return lands[w].at[2 * chip[0] + chip[1]]


def _ici_start(kind, srcs, groups, name, after=None):
    nw, ng = len(srcs), len(groups)
    lands = [lax.empty((4,) + s.shape if kind == "gather" else s.shape, s.dtype) for s in srcs]
    extra = [] if after is None else [after]

    def body(*refs):
        ins, lnd = refs[:nw], refs[nw:2 * nw]
        sems = refs[2 * nw + len(extra):2 * nw + len(extra) + 2 * ng]
        token = refs[-1]
        x, y, c = _mesh_pos()
        for g, members in enumerate(groups):
            for j, w in enumerate(members):
                for q, chip in enumerate(_other_chips(x, y)):
                    _chip_copy(kind, ins, lnd, w, q, chip, 2 * x + y, c, sems[2 * g], sems[2 * g + 1], 3 * j + q).start()
        token[...] = jnp.zeros_like(token)

    sem_shapes = [pltpu.SemaphoreType.DMA((3 * len(members),)) for members in groups for _ in range(2)]
    out = pl.pallas_call(
        body, name=name,
        out_shape=(*sem_shapes, *[pltpu.HBM(s.shape, s.dtype) for s in srcs], *[pltpu.HBM(t.shape, t.dtype) for t in lands], _TOKEN),
        in_specs=[_HBM] * (2 * nw) + [_ANY] * len(extra),
        out_specs=(*([_SEMS] * (2 * ng)), *([_HBM] * (2 * nw)), pl.BlockSpec(memory_space=pltpu.VMEM)),
        input_output_aliases={i: 2 * ng + i for i in range(2 * nw)},
        compiler_params=pltpu.CompilerParams(has_side_effects=_EFFECT),
    )(*_in_hbm(srcs), *_in_hbm(lands), *extra)
    sems = [(out[2 * g], out[2 * g + 1]) for g in range(ng)]
    return sems, list(out[2 * ng:2 * ng + nw]), list(out[2 * ng + nw:2 * ng + 2 * nw]), out[-1]


def _ici_wait(kind, sems, srcs, lands, after, name):
    nm = len(srcs)

    def body(*refs):
        ins, lnd = refs[:nm], refs[nm:2 * nm]
        send, recv = refs[2 * nm], refs[2 * nm + 1]
        x, y, c = _mesh_pos()
        for j in range(nm):
            for q, chip in enumerate(_other_chips(x, y)):
                _chip_copy(kind, ins, lnd, j, q, chip, 2 * x + y, c, send, recv, 3 * j + q).wait_send()
                landed = _chip_landing(kind, ins, lnd, j, chip, c)
                _remote(landed, landed, send, recv, 3 * j + q, (x, y, c)).wait_recv()

    out = pl.pallas_call(
        body, name=name, out_shape=(*[pltpu.HBM(s.shape, s.dtype) for s in srcs], *[pltpu.HBM(t.shape, t.dtype) for t in lands]),
        in_specs=[_HBM] * (2 * nm) + [_SEMS, _SEMS, _ANY], out_specs=tuple([_HBM] * (2 * nm)),
        input_output_aliases={i: i for i in range(2 * nm)},
        compiler_params=pltpu.CompilerParams(has_side_effects=_EFFECT),
    )(*srcs, *lands, sems[0], sems[1], after)
    return list(out[:nm]), list(out[nm:])


def _gather_finish(shards, lands, name):
    nw = len(shards)

    def body(*refs):
        ins, lnd, outs = refs[:nw], refs[nw:2 * nw], refs[2 * nw:3 * nw]
        send_sems, recv_sems = refs[3 * nw:]
        x, y, c = _mesh_pos()
        mine = 2 * x + y
        sibling = (x, y, 1 - c)
        chips = _other_chips(x, y)

        def blk(ref, chip_idx, half):
            return ref.at[chip_idx, _half_rows(ref.shape[1], half, 16), :]

        sends = []
        for w in range(nw):
            for q, chip in enumerate(chips):
                k = 2 * chip[0] + chip[1]
                sends.append(_remote(blk(lnd[w], k, c), blk(outs[w], k, c), send_sems, recv_sems, 4 * w + q, sibling))
            sends.append(_remote(ins[w], outs[w].at[mine], send_sems, recv_sems, 4 * w + 3, sibling))
        for cp in sends:
            cp.start()
        for w in range(nw):
            for q, chip in enumerate(chips):
                other = blk(outs[w], 2 * chip[0] + chip[1], 1 - c)
                _remote(other, other, send_sems, recv_sems, 4 * w + q, (x, y, c)).wait_recv()
            own = outs[w].at[mine]
            _remote(own, own, send_sems, recv_sems, 4 * w + 3, (x, y, c)).wait_recv()
        for cp in sends:
            cp.wait_send()

    return pl.pallas_call(
        body, name=name, out_shape=[jax.ShapeDtypeStruct(t.shape, t.dtype) for t in lands],
        in_specs=[_ANY] * (2 * nw), out_specs=[_ANY] * nw, input_output_aliases={nw + w: w for w in range(nw)},
        scratch_shapes=[pltpu.SemaphoreType.DMA((4 * nw,)), pltpu.SemaphoreType.DMA((4 * nw,))],
    )(*shards, *lands)


def _exchange_halves(gs, name):
    nw = len(gs)

    def body(*refs):
        ins, outs = refs[:nw], refs[nw:2 * nw]
        send_sems, recv_sems = refs[2 * nw:]
        x, y, c = _mesh_pos()
        cps = []
        for w in range(nw):
            src = ins[w].at[:, _half_rows(gs[w].shape[1], 1 - c, 16), :]
            cps.append(_remote(src, outs[w], send_sems, recv_sems, w, (x, y, 1 - c)))
            cps[-1].start()
        for cp in cps:
            cp.wait_recv()
        for cp in cps:
            cp.wait_send()

    return pl.pallas_call(
        body, name=name, out_shape=[jax.ShapeDtypeStruct((4, g.shape[1] // 2, g.shape[2]), g.dtype) for g in gs],
        in_specs=[_ANY] * nw, out_specs=[_ANY] * nw,
        scratch_shapes=[pltpu.SemaphoreType.DMA((nw,)), pltpu.SemaphoreType.DMA((nw,))],
    )(*gs)


def _add_half(g, recv, cidx, name):
    _, r, c = g.shape
    h = r // 2
    tr = _block_rows(h, c // 2)
    nbh = h // tr
    assert tr % 16 == 0

    def body(c_ref, g_ref, r_ref, o_ref):
        o_ref[...] = (g_ref[...].astype(F32) + r_ref[...].astype(F32)).astype(BF16)

    grid_spec = pltpu.PrefetchScalarGridSpec(
        num_scalar_prefetch=1, grid=(4, nbh),
        in_specs=[pl.BlockSpec((None, tr, c), lambda s, i, cr: (s, cr[0] * nbh + i, 0)),
                  pl.BlockSpec((None, tr, c), lambda s, i, cr: (s, i, 0))],
        out_specs=pl.BlockSpec((None, tr, c), lambda s, i, cr: (s, i, 0)))
    return pl.pallas_call(
        body, name=name, grid_spec=grid_spec, out_shape=jax.ShapeDtypeStruct((4, h, c), BF16),
        compiler_params=_cparams(("parallel", "parallel")),
    )(cidx, g, recv)


_BIG = ("w_in", "w_glu", "w_out", "w_up", "w_down")
_GROUPS = {"mix": ("w_in", "w_glu", "w_out"), "ffn": ("w_up", "w_down")}
_SMALL = ("b_ada", "g_pre_mix", "g_post_mix", "attn_sinks", "lam_re", "lam_im", "log_step", "ssm_b_re", "ssm_b_im", "ssm_c_re",
          "ssm_c_im", "ssm_d", "g_attn_out", "g_ssm_out", "g_pre_ffn", "g_post_ffn", "conv_b")
_WEIGHTS = ("w_ada", "b_ada", "g_pre_mix", "g_post_mix", "w_in", "attn_sinks", "lam_re", "lam_im", "log_step", "ssm_b_re", "ssm_b_im",
            "ssm_c_re", "ssm_c_im", "ssm_d", "w_glu", "g_attn_out", "g_ssm_out", "w_out", "g_pre_ffn", "g_post_ffn", "w_up", "conv_w",
            "conv_b", "w_down")
_LANES = 1024


def _pack(parts, rows_to):
    flat = jnp.concatenate([p.reshape(-1) for p in parts])
    per = _LANES * rows_to
    total = -(-flat.shape[0] // per) * per
    return jnp.pad(flat, (0, total - flat.shape[0])).reshape(total // _LANES, _LANES)


def _unpack(packed, shapes):
    flat = packed.reshape(-1)
    out, off = [], 0
    for shp in shapes:
        size = math.prod(shp)
        out.append(flat[off:off + size].reshape(shp))
        off += size
    return out


def kernel(x, c, w_ada, b_ada, g_pre_mix, g_post_mix, w_in, attn_sinks, lam_re, lam_im, log_step, ssm_b_re, ssm_b_im, ssm_c_re, ssm_c_im, ssm_d, w_glu, g_attn_out, g_ssm_out, w_out, g_pre_ffn, g_post_ffn, w_up, conv_w, conv_b, w_down, loss_target, m_w_ada, m_b_ada, m_g_pre_mix, m_g_post_mix, m_w_in, m_attn_sinks, m_lam_re, m_lam_im, m_log_step, m_ssm_b_re, m_ssm_b_im, m_ssm_c_re, m_ssm_c_im, m_ssm_d, m_w_glu, m_g_attn_out, m_g_ssm_out, m_w_out, m_g_pre_ffn, m_g_post_ffn, m_w_up, m_conv_w, m_conv_b, m_w_down, v_w_ada, v_b_ada, v_g_pre_mix, v_g_post_mix, v_w_in, v_attn_sinks, v_lam_re, v_lam_im, v_log_step, v_ssm_b_re, v_ssm_b_im, v_ssm_c_re, v_ssm_c_im, v_ssm_d, v_w_glu, v_g_attn_out, v_g_ssm_out, v_w_out, v_g_pre_ffn, v_g_post_ffn, v_w_up, v_conv_w, v_conv_b, v_w_down):
    given = dict(locals())
    wts = {n: given[n] for n in _WEIGHTS}
    mom = {n: given["m_" + n] for n in _WEIGHTS}
    var = {n: given["v_" + n] for n in _WEIGHTS}
    depth, d, ada_cols = w_ada.shape
    nchips = 4
    xi, yi, ci = lax.axis_index("x"), lax.axis_index("y"), lax.axis_index("c")
    chip = 2 * xi + yi
    cidx = jnp.reshape(ci, (1,)).astype(jnp.int32)

    cw_cols = conv_w.shape[2]
    vec = _pack([c, conv_w], 8)
    g1 = _allgather8(vec, "ag_cond").reshape(8, -1)
    c_all = g1[:, :d]
    cw_sh = g1[0::2, d:d + depth * 3 * cw_cols].reshape(nchips, depth, 3, cw_cols)
    conv_w_full = jnp.transpose(cw_sh, (1, 2, 0, 3)).reshape(depth, 3, nchips * cw_cols)

    ada_part, c_act = _ada_part(c_all, w_ada, "ada_part")
    g2 = _allgather8(ada_part.reshape(depth * 8, ada_cols), "ag_ada")
    ada = _ada_select(g2, b_ada, "ada_select")

    order = [(l, g) for l in range(depth) for g in _GROUPS]
    for table in (wts, mom, var):
        table["w_in"] = jnp.swapaxes(table["w_in"], 1, 2)
    members = {key: [wts[n][key[0]].astype(BF16) for n in _GROUPS[key[1]]] for key in order}
    flat = [s for key in order for s in members[key]]
    index, at = {}, 0
    for key in order:
        index[key] = list(range(at, at + len(members[key])))
        at += len(members[key])
    ag_sems, ag_srcs, ag_lands, ag_token = _ici_start("gather", flat, [index[key] for key in order], "ag_start", after=ada)
    ada = ada + ag_token[0:1, 0:1]

    def fetch(l, g, after):
        pos, ids = order.index((l, g)), index[(l, g)]
        srcs, lands = [ag_srcs[i] for i in ids], [ag_lands[i] for i in ids]
        srcs, lands = _ici_wait("gather", ag_sems[pos], srcs, lands, ag_token if after is None else after, f"ag_wait_l{l}_{g}")
        got = dict(zip(_GROUPS[g], _gather_finish(srcs, lands, f"ag_finish_l{l}_{g}")))
        if g == "ffn":
            return dict(w_up=got["w_up"], w_down=got["w_down"].reshape(-1, got["w_down"].shape[2]))
        w_in_t = got["w_in"].reshape(-1, d)
        split = w_in_t.shape[0] - nchips * got["w_glu"].shape[1]
        return dict(w_in=jnp.concatenate([w_in_t[split:], w_in_t[:split]], axis=0),
                    w_glu=got["w_glu"].reshape(-1, got["w_glu"].shape[2]), w_out=got["w_out"].reshape(-1, got["w_out"].shape[2]))

    wl = [_LayerWeights(functools.partial(fetch, l)) for l in range(depth)]
    wl[0].group("mix")
    ps = []
    for l in range(depth):
        small = {n: wts[n][l] for n in _SMALL if n != "b_ada"}
        small["conv_w"] = conv_w_full[l]
        ps.append(small)

    in_flight = {}

    def on_grads(l, g, gw):
        stacks = []
        for n in _GROUPS[g]:
            t = gw[n]
            if n == "w_in":
                uw = nchips * wts["w_glu"].shape[1]
                t = jnp.concatenate([t[uw:], t[:uw]], axis=0).reshape(nchips, -1, d)
            elif n != "w_up":
                t = t.reshape(nchips, t.shape[0] // nchips, t.shape[1])
            stacks.append(t)
        from_sibling = _exchange_halves(stacks, f"rs_sibling_l{l}_{g}")
        partials = [_add_half(s, r, cidx, f"rs_add_l{l}_{n}") for s, r, n in zip(stacks, from_sibling, _GROUPS[g])]
        sems, srcs, lands, token = _ici_start("scatter", partials, [list(range(len(partials)))], f"rs_start_l{l}_{g}")
        in_flight[(l, g)] = (sems[0], srcs, lands)
        return token

    loss_sum, grad_x, gss = _local_step(x[0], loss_target[0], ada, wl, ps, on_grads)
    loss = lax.psum(loss_sum, ("x", "y", "c"))

    reduced = {n: lax.empty(wts[n].shape, F32) for n in _BIG}
    for key in reversed(order):
        l, g = key
        sems, srcs, lands = in_flight[key]
        sent, landed = _ici_wait("scatter", sems, srcs, lands, grad_x, f"rs_wait_l{l}_{g}")
        for n, t, p in zip(_GROUPS[g], landed, sent):
            t = lax.dynamic_update_slice(t, lax.dynamic_slice_in_dim(p, chip, 1, axis=0), (chip, 0, 0))
            reduced[n] = _sum_slots_into(reduced[n], t.reshape(-1, t.shape[2]), l, cidx, nchips, f"rs_sum_l{l}_{n}")
    big_grads = dict(zip(_BIG, _share_halves([reduced[n] for n in _BIG], "rs_share")))

    small_parts = [jnp.stack([gss[l][n] for l in range(depth)]) for n in _SMALL]
    pack_small = _pack(small_parts, 8)
    pack_cw = _pack([jnp.stack([gss[l]["conv_w"] for l in range(depth)])], 8)
    rows_small = pack_small.shape[0]
    mine = jnp.concatenate([pack_small, pack_cw], axis=0).astype(BF16)
    g3 = _allgather8(mine, "ag_small")
    total = _sum_slots(g3, 8, "sum_small")
    grads = dict(big_grads)
    for n, v in zip(_SMALL, _unpack(total[:rows_small], [wts[n].shape for n in _SMALL])):
        grads[n] = v
    conv_w_grad = _unpack(total[rows_small:], [(depth, 3, nchips * cw_cols)])[0]
    grads["conv_w"] = lax.dynamic_slice_in_dim(conv_w_grad, chip * cw_cols, cw_cols, axis=2)

    ada_rows = depth * 6 * d // _LANES
    d_ada_all = g3.reshape(8, -1, _LANES)[:, :ada_rows].reshape(8, depth, 6 * d).astype(F32)
    d_sel = lax.dynamic_slice_in_dim(jnp.transpose(d_ada_all, (1, 0, 2)), chip * ada_cols, ada_cols, axis=2)
    grads["w_ada"] = _wada_grad(jnp.transpose(c_act), d_sel, "w_ada_grad")

    delta, new_m, new_v = {}, {}, {}
    for n in _WEIGHTS:
        shp = wts[n].shape
        view = (lambda t: t) if len(shp) == 3 else (lambda t: t.reshape(1, -1, shp[-1]))
        outs = _adamw(view(wts[n]), view(grads[n]), view(mom[n]), view(var[n]), f"adamw_{n}")
        delta[n], new_m[n], new_v[n] = [t.reshape(shp) for t in outs]

    for table in (grads, delta, new_m, new_v):
        table["w_in"] = jnp.swapaxes(table["w_in"], 1, 2)
    return (loss, grad_x[None], *[grads[n] for n in _WEIGHTS], *[delta[n] for n in _WEIGHTS],
            *[new_m[n] for n in _WEIGHTS], *[new_v[n] for n in _WEIGHTS])
```

```python
import functools
import math

import jax
import jax.numpy as jnp
from jax import lax
from jax.experimental import pallas as pl
from jax.experimental.pallas import tpu as pltpu

F32 = jnp.float32
BF16 = jnp.bfloat16
EPS = 1e-6
NEG = -1e30
WINDOW = 128
HEAD_DIM = 64
KV_RATIO = 8
SSM_GROUP = 16
STATE = 64
PIECE = 128
PSTATES = 512
DEPTH = 2
ADAM_LR, ADAM_B1, ADAM_B2, ADAM_EPS, ADAM_WD, ADAM_STEP = 0.001, 0.9, 0.999, 1e-08, 0.01, 10
MIB = 1024 * 1024
_MATMUL_VMEM_BUDGET = 40 * MIB
MESH = pl.DeviceIdType.MESH


def _cparams(sem=None, vmem_mib=48):
    return pltpu.CompilerParams(dimension_semantics=sem, vmem_limit_bytes=vmem_mib * MIB)


def _gelu(x):
    c = math.sqrt(2.0 / math.pi)
    return 0.5 * x * (1.0 + jnp.tanh(c * (x + 0.044715 * (x * x * x))))


def _gelu_and_grad(x):
    c = math.sqrt(2.0 / math.pi)
    x2 = x * x
    t = jnp.tanh(c * (x + 0.044715 * (x2 * x)))
    half = 0.5 * (1.0 + t)
    return x * half, half + 0.5 * x * (1.0 - t * t) * c * (1.0 + 3.0 * 0.044715 * x2)


def _gelu_grad(x):
    return _gelu_and_grad(x)[1]


def _sigmoid(x):
    return 1.0 / (1.0 + jnp.exp(-x))


def _matmul(a, b, *, m, n, k, tm, tn, tk, out_dtype, name, ta=False, tb=False, a_stack=0, b_stack=0, o_stack=0):
    assert m % tm == 0 and n % tn == 0 and k % tk == 0, (name, m, n, k, tm, tn, tk)
    nk = k // tk

    if a_stack:
        assert not ta and (k // a_stack) % tk == 0
        per = (k // a_stack) // tk
        a_spec = pl.BlockSpec((None, tm, tk), lambda i, j, kk: (kk // per, i, kk % per))
    elif ta:
        a_spec = pl.BlockSpec((tk, tm), lambda i, j, kk: (kk, i))
    else:
        a_spec = pl.BlockSpec((tm, tk), lambda i, j, kk: (i, kk))
    if b_stack and tb:
        perb = (k // b_stack) // tk
        b_spec = pl.BlockSpec((None, tn, tk), lambda i, j, kk: (kk // perb, j, kk % perb))
    elif b_stack:
        perb = (n // b_stack) // tn
        b_spec = pl.BlockSpec((None, tk, tn), lambda i, j, kk: (j // perb, kk, j % perb))
    elif tb:
        b_spec = pl.BlockSpec((tn, tk), lambda i, j, kk: (j, kk))
    else:
        b_spec = pl.BlockSpec((tk, tn), lambda i, j, kk: (kk, j))
    if o_stack:
        pero = (n // o_stack) // tn
        o_spec = pl.BlockSpec((None, tm, tn), lambda i, j, kk: (j // pero, i, j % pero))
        o_shape = jax.ShapeDtypeStruct((o_stack, m, n // o_stack), out_dtype)
    else:
        o_spec = pl.BlockSpec((tm, tn), lambda i, j, kk: (i, j))
        o_shape = jax.ShapeDtypeStruct((m, n), out_dtype)
    dims = (((0 if ta else 1,), (1 if tb else 0,)), ((), ()))

    def body(a_ref, b_ref, o_ref, *acc):
        p = lax.dot_general(a_ref[...].astype(BF16), b_ref[...].astype(BF16), dims, preferred_element_type=F32)
        if nk == 1:
            o_ref[...] = p.astype(o_ref.dtype)
        else:
            acc_ref = acc[0]
            kk = pl.program_id(2)

            @pl.when(kk == 0)
            def _():
                acc_ref[...] = p

            @pl.when(kk > 0)
            def _():
                acc_ref[...] += p

            @pl.when(kk == nk - 1)
            def _():
                o_ref[...] = acc_ref[...].astype(o_ref.dtype)

    return pl.pallas_call(
        body, name=name, grid=(m // tm, n // tn, nk), in_specs=[a_spec, b_spec], out_specs=o_spec, out_shape=o_shape,
        scratch_shapes=[] if nk == 1 else [pltpu.VMEM((tm, tn), F32)],
        compiler_params=_cparams(("parallel", "parallel", "arbitrary"), 56),
    )(a, b)


def _row(d):
    return pl.BlockSpec((1, d), lambda i: (0, 0))


def _tok(tm, d):
    return pl.BlockSpec((tm, d), lambda i: (i, 0))


def _pre_fwd(x, g, sc, sh, name):
    s, d = x.shape
    tm = min(512, s)

    def body(x_ref, g_ref, sc_ref, sh_ref, h_ref):
        xv = x_ref[...]
        r = lax.rsqrt(jnp.mean(xv * xv, axis=-1, keepdims=True) + EPS)
        h_ref[...] = (((xv * r) * g_ref[...]) * (1.0 + sc_ref[...]) + sh_ref[...]).astype(BF16)

    return pl.pallas_call(
        body, name=name, grid=(s // tm,), in_specs=[_tok(tm, d), _row(d), _row(d), _row(d)], out_specs=_tok(tm, d),
        out_shape=jax.ShapeDtypeStruct((s, d), BF16), compiler_params=_cparams(("parallel",)),
    )(x, g, sc, sh)


def _post_fwd(x, o, g, gt, name):
    s, d = x.shape
    tm = min(512, s)

    def body(x_ref, o_ref, g_ref, gt_ref, y_ref):
        ov = o_ref[...].astype(F32)
        r = lax.rsqrt(jnp.mean(ov * ov, axis=-1, keepdims=True) + EPS)
        y_ref[...] = x_ref[...] + (1.0 + gt_ref[...]) * ((ov * r) * g_ref[...])

    return pl.pallas_call(
        body, name=name, grid=(s // tm,), in_specs=[_tok(tm, d), _tok(tm, d), _row(d), _row(d)], out_specs=_tok(tm, d),
        out_shape=jax.ShapeDtypeStruct((s, d), F32), compiler_params=_cparams(("parallel",)),
    )(x, o, g, gt)


def _post_bwd(dxo, o, g, gt, name):
    s, d = o.shape
    tm = min(512, s)

    def body(dx_ref, o_ref, g_ref, gt_ref, do_ref, dgt_ref, dg_ref):
        i = pl.program_id(0)
        dx = dx_ref[...]
        ov = o_ref[...].astype(F32)
        gv = g_ref[...]
        r = lax.rsqrt(jnp.mean(ov * ov, axis=-1, keepdims=True) + EPS)
        oh = ov * r
        dn = dx * (1.0 + gt_ref[...])
        e = dn * gv
        do_ref[...] = (r * (e - oh * jnp.mean(e * oh, axis=-1, keepdims=True))).astype(BF16)
        p_gt = jnp.sum(dx * (oh * gv), axis=0, keepdims=True)
        p_g = jnp.sum(dn * oh, axis=0, keepdims=True)

        @pl.when(i == 0)
        def _():
            dgt_ref[...] = p_gt
            dg_ref[...] = p_g

        @pl.when(i > 0)
        def _():
            dgt_ref[...] += p_gt
            dg_ref[...] += p_g

    row = jax.ShapeDtypeStruct((1, d), F32)
    return pl.pallas_call(
        body, name=name, grid=(s // tm,), in_specs=[_tok(tm, d), _tok(tm, d), _row(d), _row(d)],
        out_specs=(_tok(tm, d), _row(d), _row(d)), out_shape=(jax.ShapeDtypeStruct((s, d), BF16), row, row),
        compiler_params=_cparams(("arbitrary",)),
    )(dxo, o, g, gt)


def _pre_bwd(dres, dh, x, g, sc, name):
    s, d = x.shape
    tm = min(256, s)

    def body(dres_ref, dh_ref, x_ref, g_ref, sc_ref, dx_ref, dsh_ref, dsc_ref, dg_ref):
        i = pl.program_id(0)
        dh_v = dh_ref[...].astype(F32)
        xv = x_ref[...]
        gv = g_ref[...]
        one_sc = 1.0 + sc_ref[...]
        r = lax.rsqrt(jnp.mean(xv * xv, axis=-1, keepdims=True) + EPS)
        xh = xv * r
        e = dh_v * one_sc * gv
        dx_ref[...] = dres_ref[...] + r * (e - xh * jnp.mean(e * xh, axis=-1, keepdims=True))
        p_sh = jnp.sum(dh_v, axis=0, keepdims=True)
        p_sc = jnp.sum(dh_v * (xh * gv), axis=0, keepdims=True)
        p_g = jnp.sum(dh_v * one_sc * xh, axis=0, keepdims=True)

        @pl.when(i == 0)
        def _():
            dsh_ref[...] = p_sh
            dsc_ref[...] = p_sc
            dg_ref[...] = p_g

        @pl.when(i > 0)
        def _():
            dsh_ref[...] += p_sh
            dsc_ref[...] += p_sc
            dg_ref[...] += p_g

    row = jax.ShapeDtypeStruct((1, d), F32)
    return pl.pallas_call(
        body, name=name, grid=(s // tm,), in_specs=[_tok(tm, d), _tok(tm, d), _tok(tm, d), _row(d), _row(d)],
        out_specs=(_tok(tm, d), _row(d), _row(d), _row(d)), out_shape=(jax.ShapeDtypeStruct((s, d), F32), row, row, row),
        compiler_params=_cparams(("arbitrary",)),
    )(dres, dh, x, g, sc)


def _loss_head(y, tgt, name):
    s, d = y.shape
    tm = min(256, s)

    def body(y_ref, t_ref, dy_ref, l_ref):
        i = pl.program_id(0)
        err = y_ref[...] - t_ref[...]
        dy_ref[...] = err * (1.0 / d)
        part = jnp.zeros((1, 128), F32) + jnp.sum(err * err)

        @pl.when(i == 0)
        def _():
            l_ref[...] = part

        @pl.when(i > 0)
        def _():
            l_ref[...] += part

    return pl.pallas_call(
        body, name=name, grid=(s // tm,), in_specs=[_tok(tm, d), _tok(tm, d)],
        out_specs=(_tok(tm, d), pl.BlockSpec((1, 128), lambda i: (0, 0))),
        out_shape=(jax.ShapeDtypeStruct((s, d), F32), jax.ShapeDtypeStruct((1, 128), F32)),
        compiler_params=_cparams(("arbitrary",)),
    )(y, tgt)


def _gelu_fwd(y, name):
    s, u = y.shape
    tm = min(512, s)

    def body(y_ref, z_ref):
        z_ref[...] = _gelu(y_ref[...]).astype(BF16)

    return pl.pallas_call(
        body, name=name, grid=(s // tm,), in_specs=[_tok(tm, u)], out_specs=_tok(tm, u),
        out_shape=jax.ShapeDtypeStruct((s, u), BF16), compiler_params=_cparams(("parallel",)),
    )(y)


def _merge_fwd(attn, y, gl, ga, gs, name):
    s, aw = attn.shape
    uw = y.shape[1]
    tm = min(256, s)

    def body(a_ref, y_ref, gl_ref, ga_ref, gs_ref, m_ref):
        av = a_ref[...]
        ra = lax.rsqrt(jnp.mean(av * av, axis=-1, keepdims=True) + EPS)
        m_ref[:, :aw] = ((av * ra) * ga_ref[...]).astype(BF16)
        ssm = _gelu(y_ref[...]) * _sigmoid(gl_ref[...])
        rs = lax.rsqrt(jnp.mean(ssm * ssm, axis=-1, keepdims=True) + EPS)
        m_ref[:, aw:] = ((ssm * rs) * gs_ref[...]).astype(BF16)

    return pl.pallas_call(
        body, name=name, grid=(s // tm,), in_specs=[_tok(tm, aw), _tok(tm, uw), _tok(tm, uw), _row(aw), _row(uw)],
        out_specs=_tok(tm, aw + uw), out_shape=jax.ShapeDtypeStruct((s, aw + uw), BF16),
        compiler_params=_cparams(("parallel",)),
    )(attn, y, gl, ga, gs)


def _merge_bwd(dm, attn, y, gl, ga, gs, name):
    s, aw = attn.shape
    uw = y.shape[1]
    tm = min(256, s)

    def body(dm_ref, a_ref, y_ref, gl_ref, ga_ref, gs_ref, da_ref, dgl_ref, dz_ref, dga_ref, dgs_ref):
        i = pl.program_id(0)
        av = a_ref[...]
        dma = dm_ref[:, :aw].astype(F32)
        ra = lax.rsqrt(jnp.mean(av * av, axis=-1, keepdims=True) + EPS)
        ah = av * ra
        e = dma * ga_ref[...]
        da_ref[...] = (ra * (e - ah * jnp.mean(e * ah, axis=-1, keepdims=True))).astype(BF16)
        p_ga = jnp.sum(dma * ah, axis=0, keepdims=True)

        z = _gelu(y_ref[...])
        sig = _sigmoid(gl_ref[...])
        ssm = z * sig
        dms = dm_ref[:, aw:].astype(F32)
        rs = lax.rsqrt(jnp.mean(ssm * ssm, axis=-1, keepdims=True) + EPS)
        sh = ssm * rs
        e2 = dms * gs_ref[...]
        dssm = rs * (e2 - sh * jnp.mean(e2 * sh, axis=-1, keepdims=True))
        dz_ref[...] = dssm * sig
        dgl_ref[...] = (dssm * z * sig * (1.0 - sig)).astype(BF16)
        p_gs = jnp.sum(dms * sh, axis=0, keepdims=True)

        @pl.when(i == 0)
        def _():
            dga_ref[...] = p_ga
            dgs_ref[...] = p_gs

        @pl.when(i > 0)
        def _():
            dga_ref[...] += p_ga
            dgs_ref[...] += p_gs

    return pl.pallas_call(
        body, name=name, grid=(s // tm,),
        in_specs=[_tok(tm, aw + uw), _tok(tm, aw), _tok(tm, uw), _tok(tm, uw), _row(aw), _row(uw)],
        out_specs=(_tok(tm, aw), _tok(tm, uw), _tok(tm, uw), _row(aw), _row(uw)),
        out_shape=(jax.ShapeDtypeStruct((s, aw), BF16), jax.ShapeDtypeStruct((s, uw), BF16), jax.ShapeDtypeStruct((s, uw), F32),
                   jax.ShapeDtypeStruct((1, aw), F32), jax.ShapeDtypeStruct((1, uw), F32)),
        compiler_params=_cparams(("arbitrary",)),
    )(dm, attn, y, gl, ga, gs)


def _gelu_bwd(dzd, dz2, y, name):
    s, u = y.shape
    tm = min(512, s)

    def body(a_ref, b_ref, y_ref, o_ref):
        o_ref[...] = (a_ref[...] + b_ref[...]) * _gelu_grad(y_ref[...])

    return pl.pallas_call(
        body, name=name, grid=(s // tm,), in_specs=[_tok(tm, u), _tok(tm, u), _tok(tm, u)], out_specs=_tok(tm, u),
        out_shape=jax.ShapeDtypeStruct((s, u), F32), compiler_params=_cparams(("parallel",)),
    )(dzd, dz2, y)


def _attn_scores(qh, kb, sink, valid):
    s = lax.dot_general(qh, kb, (((1,), (1,)), ((), ())), preferred_element_type=F32) * (HEAD_DIM ** -0.5)
    s = jnp.where(valid, s, NEG)
    m = jnp.maximum(jnp.max(s, axis=-1, keepdims=True), sink)
    e = jnp.exp(s - m)
    esink = jnp.exp(sink - m)
    den = jnp.sum(e, axis=-1, keepdims=True) + esink
    return e / den, esink / den


def _attn_valid(i):
    qi = lax.broadcasted_iota(jnp.int32, (KV_RATIO * WINDOW, 2 * WINDOW), 0) % WINDOW
    kj = lax.broadcasted_iota(jnp.int32, (KV_RATIO * WINDOW, 2 * WINDOW), 1)
    return (kj > qi) & (kj <= qi + WINDOW) & ((kj >= WINDOW) | (i > 0))


def _stack_heads(ref, hk):
    return jnp.concatenate([ref[:, (hk * KV_RATIO + g) * HEAD_DIM:(hk * KV_RATIO + g + 1) * HEAD_DIM] for g in range(KV_RATIO)], axis=0)


def _stack_sinks(sink_ref, hk):
    return jnp.concatenate([jnp.full((WINDOW, 1), sink_ref[hk * KV_RATIO + g], F32) for g in range(KV_RATIO)], axis=0)


def _band(kvp, kvc, off):
    return jnp.concatenate([kvp[:, off:off + HEAD_DIM], kvc[:, off:off + HEAD_DIM]], axis=0)


def _attn_specs(aw, uw, kvw):
    qblk = uw // aw
    kvblk = (uw + aw) // (2 * kvw)
    assert uw % aw == 0 and (uw + aw) % (2 * kvw) == 0
    return [
        pl.BlockSpec(memory_space=pltpu.SMEM),
        pl.BlockSpec((WINDOW, aw), lambda i: (i, qblk)),
        pl.BlockSpec((WINDOW, 2 * kvw), lambda i: (i, kvblk)),
        pl.BlockSpec((WINDOW, 2 * kvw), lambda i: (jnp.maximum(i - 1, 0), kvblk)),
    ]


def _attn_fwd(proj, sinks, aw, uw, name):
    s = proj.shape[0]
    nq = aw // HEAD_DIM
    nkv = nq // KV_RATIO
    kvw = nkv * HEAD_DIM

    def body(sink_ref, q_ref, kvc_ref, kvp_ref, o_ref):
        valid = _attn_valid(pl.program_id(0))[:WINDOW]
        q = q_ref[...]
        kvc = kvc_ref[...]
        kvp = kvp_ref[...]
        for hk in range(nkv):
            kb = _band(kvp, kvc, hk * HEAD_DIM)
            vb = _band(kvp, kvc, kvw + hk * HEAD_DIM)
            for g in range(KV_RATIO):
                hq = hk * KV_RATIO + g
                p, _ = _attn_scores(q[:, hq * HEAD_DIM:(hq + 1) * HEAD_DIM], kb, sink_ref[hq], valid)
                o_ref[:, hq * HEAD_DIM:(hq + 1) * HEAD_DIM] = jnp.dot(p.astype(BF16), vb, preferred_element_type=F32)

    return pl.pallas_call(
        body, name=name, grid=(s // WINDOW,), in_specs=_attn_specs(aw, uw, kvw),
        out_specs=pl.BlockSpec((WINDOW, aw), lambda i: (i, 0)), out_shape=jax.ShapeDtypeStruct((s, aw), F32),
        compiler_params=_cparams(("parallel",)),
    )(sinks, proj, proj, proj)


def _attn_bwd(proj, sinks, attn, dattn, aw, uw, name):
    s = proj.shape[0]
    nq = aw // HEAD_DIM
    nkv = nq // KV_RATIO
    kvw = nkv * HEAD_DIM
    hd = HEAD_DIM

    def body(sink_ref, q_ref, kvc_ref, kvp_ref, o_ref, do_ref, dq_ref, dc_ref, dp_ref, ds_ref):
        i = pl.program_id(0)
        valid = _attn_valid(i)
        kvc = kvc_ref[...]
        kvp = kvp_ref[...]
        lane = lax.broadcasted_iota(jnp.int32, (1, nq), 1)
        dsink = jnp.zeros((1, nq), F32)
        for hk in range(nkv):
            kb = _band(kvp, kvc, hk * hd)
            vb = _band(kvp, kvc, kvw + hk * hd)
            qs = _stack_heads(q_ref, hk)
            dos = _stack_heads(do_ref, hk)
            p, psink = _attn_scores(qs, kb, _stack_sinks(sink_ref, hk), valid)
            delta = jnp.sum(dos.astype(F32) * _stack_heads(o_ref, hk), axis=-1, keepdims=True)
            dpv = lax.dot_general(dos, vb, (((1,), (1,)), ((), ())), preferred_element_type=F32)
            dsb = (p * (dpv - delta) * (hd ** -0.5)).astype(BF16)
            dqs = jnp.dot(dsb, kb, preferred_element_type=F32).astype(BF16)
            dkb = lax.dot_general(dsb, qs, (((0,), (0,)), ((), ())), preferred_element_type=F32)
            dvb = lax.dot_general(p.astype(BF16), dos, (((0,), (0,)), ((), ())), preferred_element_type=F32)
            sink_term = psink * delta
            for g in range(KV_RATIO):
                hq = hk * KV_RATIO + g
                dq_ref[:, hq * hd:(hq + 1) * hd] = dqs[g * WINDOW:(g + 1) * WINDOW]
                dsink = dsink + jnp.where(lane == hq, -jnp.sum(sink_term[g * WINDOW:(g + 1) * WINDOW]), 0.0)
            dp_ref[:, hk * hd:(hk + 1) * hd] = dkb[:WINDOW]
            dc_ref[:, hk * hd:(hk + 1) * hd] = dkb[WINDOW:]
            dp_ref[:, kvw + hk * hd:kvw + (hk + 1) * hd] = dvb[:WINDOW]
            dc_ref[:, kvw + hk * hd:kvw + (hk + 1) * hd] = dvb[WINDOW:]

        @pl.when(i == 0)
        def _():
            ds_ref[...] = dsink

        @pl.when(i > 0)
        def _():
            ds_ref[...] += dsink

    blk_a = pl.BlockSpec((WINDOW, aw), lambda i: (i, 0))
    blk_kv = pl.BlockSpec((WINDOW, 2 * kvw), lambda i: (i, 0))
    return pl.pallas_call(
        body, name=name, grid=(s // WINDOW,), in_specs=_attn_specs(aw, uw, kvw) + [blk_a, blk_a],
        out_specs=(blk_a, blk_kv, blk_kv, pl.BlockSpec((1, nq), lambda i: (0, 0))),
        out_shape=(jax.ShapeDtypeStruct((s, aw), BF16), jax.ShapeDtypeStruct((s, 2 * kvw), F32),
                   jax.ShapeDtypeStruct((s, 2 * kvw), F32), jax.ShapeDtypeStruct((1, nq), F32)),
        compiler_params=_cparams(("arbitrary",)),
    )(sinks, proj, proj, proj, attn, dattn)


def _assemble_dproj(du, dq, dkv_cur, dkv_prev, name):
    s, uw = du.shape
    aw = dq.shape[1]
    kv2 = dkv_cur.shape[1]
    nb = s // WINDOW

    def body(du_ref, dq_ref, dc_ref, dp_ref, o_ref):
        i = pl.program_id(0)
        o_ref[:, :uw] = du_ref[...].astype(BF16)
        o_ref[:, uw:uw + aw] = dq_ref[...]
        nxt = jnp.where(i < nb - 1, 1.0, 0.0)
        o_ref[:, uw + aw:] = (dc_ref[...] + nxt * dp_ref[...]).astype(BF16)

    return pl.pallas_call(
        body, name=name, grid=(nb,),
        in_specs=[_tok(WINDOW, uw), _tok(WINDOW, aw), _tok(WINDOW, kv2),
                  pl.BlockSpec((WINDOW, kv2), lambda i: (jnp.minimum(i + 1, nb - 1), 0))],
        out_specs=_tok(WINDOW, uw + aw + kv2), out_shape=jax.ShapeDtypeStruct((s, uw + aw + kv2), BF16),
        compiler_params=_cparams(("parallel",)),
    )(du, dq, dkv_cur, dkv_prev)


def _zoh(lr, li, ls, btr, bti):
    dt = jnp.exp(ls)
    mag = jnp.exp(lr * dt)
    ang = li * dt
    ar = mag * jnp.cos(ang)
    ai = mag * jnp.sin(ang)
    den = lr * lr + li * li
    fr = ((ar - 1.0) * lr + ai * li) / den
    fi = (ai * lr - (ar - 1.0) * li) / den
    return ar, ai, fr[None] * btr - fi[None] * bti, fr[None] * bti + fi[None] * btr


def _ssm_prep(lr, li, ls, btr, bti, name):
    def body(lr_ref, li_ref, ls_ref, btr_ref, bti_ref, ar_ref, ai_ref, bbr_ref, bbi_ref):
        ar, ai, bbr, bbi = _zoh(lr_ref[...], li_ref[...], ls_ref[...], btr_ref[...], bti_ref[...])
        ar_ref[...] = ar
        ai_ref[...] = ai
        bbr_ref[...] = bbr
        bbi_ref[...] = bbi

    s2 = jax.ShapeDtypeStruct(lr.shape, F32)
    s3 = jax.ShapeDtypeStruct(btr.shape, F32)
    return pl.pallas_call(body, name=name, out_shape=(s2, s2, s3, s3))(lr, li, ls, btr, bti)


def _ssm_prep_bwd(lr, li, ls, btr, bti, dar, dai, dbbr, dbbi, name):
    def body(lr_ref, li_ref, ls_ref, btr_ref, bti_ref, dar_ref, dai_ref, dbbr_ref, dbbi_ref, o1, o2, o3, o4, o5):
        _, vjp = jax.vjp(_zoh, lr_ref[...], li_ref[...], ls_ref[...], btr_ref[...], bti_ref[...])
        g = vjp((dar_ref[...], dai_ref[...], dbbr_ref[...], dbbi_ref[...]))
        for o, v in zip((o1, o2, o3, o4, o5), g):
            o[...] = v

    s2 = jax.ShapeDtypeStruct(lr.shape, F32)
    s3 = jax.ShapeDtypeStruct(btr.shape, F32)
    return pl.pallas_call(body, name=name, out_shape=(s2, s2, s2, s3, s3))(lr, li, ls, btr, bti, dar, dai, dbbr, dbbi)


def _state_tiles(ref):
    return [ref[:, cb * 128:(cb + 1) * 128] for cb in range(4)]


def _gather_rows(ref_re, ref_im, r, t):
    return jnp.concatenate([ref_re.at[cb][pl.ds(r, t, stride=8), :] for cb in range(4)]
                           + [ref_im.at[cb][pl.ds(r, t, stride=8), :] for cb in range(4)], axis=1)


def _scatter_rows(ref_re, ref_im, r, t, val):
    for cb in range(4):
        ref_re.at[cb][pl.ds(r, t, stride=8), :] = val[:, cb * 128:(cb + 1) * 128]
        ref_im.at[cb][pl.ds(r, t, stride=8), :] = val[:, PSTATES + cb * 128:PSTATES + (cb + 1) * 128]


def _ssm_fwd(proj, bp, cp, a_re, a_im, dvec, uw, name, t=128):
    s = proj.shape[0]
    npc = uw // PIECE
    assert npc == 8 and s % t == 0

    def body(u_ref, bp_ref, cp_ref, ar_ref, ai_ref, d_ref, y_ref, xr_ref, xi_ref, cr_ref, ci_ref):
        i = pl.program_id(0)

        @pl.when(i == 0)
        def _():
            cr_ref[...] = jnp.zeros_like(cr_ref)
            ci_ref[...] = jnp.zeros_like(ci_ref)

        for r in range(npc):
            bu = jnp.dot(u_ref[:, r * PIECE:(r + 1) * PIECE], bp_ref[r], preferred_element_type=F32)
            _scatter_rows(xr_ref, xi_ref, r, t, bu)
        ar = _state_tiles(ar_ref)
        ai = _state_tiles(ai_ref)

        def step(tt, carry):
            xr, xi = carry
            off = pl.multiple_of(tt * 8, 8)
            nr, ni = [], []
            for cb in range(4):
                vr = ar[cb] * xr[cb] - ai[cb] * xi[cb] + xr_ref[cb, pl.ds(off, 8), :]
                vi = ar[cb] * xi[cb] + ai[cb] * xr[cb] + xi_ref[cb, pl.ds(off, 8), :]
                xr_ref[cb, pl.ds(off, 8), :] = vr
                xi_ref[cb, pl.ds(off, 8), :] = vi
                nr.append(vr)
                ni.append(vi)
            return tuple(nr), tuple(ni)

        xr, xi = lax.fori_loop(0, t, step, (tuple(_state_tiles(cr_ref)), tuple(_state_tiles(ci_ref))), unroll=4)
        for cb in range(4):
            cr_ref[:, cb * 128:(cb + 1) * 128] = xr[cb]
            ci_ref[:, cb * 128:(cb + 1) * 128] = xi[cb]
        for r in range(npc):
            xs = _gather_rows(xr_ref, xi_ref, r, t).astype(BF16)
            y_ref[:, r * PIECE:(r + 1) * PIECE] = (
                jnp.dot(xs, cp_ref[r], preferred_element_type=F32)
                + d_ref[:, r * PIECE:(r + 1) * PIECE] * u_ref[:, r * PIECE:(r + 1) * PIECE].astype(F32))

    full3 = lambda shp: pl.BlockSpec(shp, lambda i: (0, 0, 0))
    full2 = lambda shp: pl.BlockSpec(shp, lambda i: (0, 0))
    xs_spec = pl.BlockSpec((4, t * 8, 128), lambda i: (0, i, 0))
    xs_shape = jax.ShapeDtypeStruct((4, s * 8, 128), F32)
    return pl.pallas_call(
        body, name=name, grid=(s // t,),
        in_specs=[pl.BlockSpec((t, uw), lambda i: (i, 0)), full3(bp.shape), full3(cp.shape), full2(a_re.shape), full2(a_im.shape),
                  full2(dvec.shape)],
        out_specs=(pl.BlockSpec((t, uw), lambda i: (i, 0)), xs_spec, xs_spec),
        out_shape=(jax.ShapeDtypeStruct((s, uw), F32), xs_shape, xs_shape),
        scratch_shapes=[pltpu.VMEM((8, PSTATES), F32), pltpu.VMEM((8, PSTATES), F32)],
        compiler_params=_cparams(("arbitrary",), 56),
    )(proj, bp, cp, a_re, a_im, dvec)


def _ssm_bwd(dy, proj, xs_re, xs_im, cpt, bpt, a_re, a_im, dvec, uw, name, t=128):
    s = proj.shape[0]
    npc = uw // PIECE
    nt = s // t
    assert npc == 8 and s % t == 0

    def body(dy_ref, u_ref, xr_ref, xi_ref, hr_ref, hi_ref, cpt_ref, bpt_ref, ar_ref, ai_ref, d_ref,
             du_ref, dbp_ref, dcp_ref, dar_ref, dai_ref, dd_ref, gr_ref, gi_ref, lr_ref, li_ref):
        i = pl.program_id(0)

        @pl.when(i == 0)
        def _():
            lr_ref[...] = jnp.zeros_like(lr_ref)
            li_ref[...] = jnp.zeros_like(li_ref)
            dbp_ref[...] = jnp.zeros_like(dbp_ref)
            dcp_ref[...] = jnp.zeros_like(dcp_ref)
            dar_ref[...] = jnp.zeros_like(dar_ref)
            dai_ref[...] = jnp.zeros_like(dai_ref)
            dd_ref[...] = jnp.zeros_like(dd_ref)

        dyb = dy_ref[...].astype(BF16)
        for r in range(npc):
            gx = jnp.dot(dyb[:, r * PIECE:(r + 1) * PIECE], cpt_ref[r], preferred_element_type=F32)
            _scatter_rows(gr_ref, gi_ref, r, t, gx)
        ar = _state_tiles(ar_ref)
        ai = _state_tiles(ai_ref)

        def adjoint(off, lam_r, lam_i, xpr, xpi, acc_r, acc_i):
            nr, ni, qr, qi = [], [], [], []
            for cb in range(4):
                vr = gr_ref[cb, pl.ds(off, 8), :] + ar[cb] * lam_r[cb] + ai[cb] * lam_i[cb]
                vi = gi_ref[cb, pl.ds(off, 8), :] + ar[cb] * lam_i[cb] - ai[cb] * lam_r[cb]
                gr_ref[cb, pl.ds(off, 8), :] = vr
                gi_ref[cb, pl.ds(off, 8), :] = vi
                nr.append(vr)
                ni.append(vi)
                qr.append(acc_r[cb] + vr * xpr[cb] + vi * xpi[cb])
                qi.append(acc_i[cb] + vi * xpr[cb] - vr * xpi[cb])
            return tuple(nr), tuple(ni), tuple(qr), tuple(qi)

        def step(j, carry):
            lam_r, lam_i, acc_r, acc_i = carry
            tt = t - 1 - j
            off = pl.multiple_of(tt * 8, 8)
            offp = pl.multiple_of(tt * 8 - 8, 8)
            xpr = [xr_ref[cb, pl.ds(offp, 8), :] for cb in range(4)]
            xpi = [xi_ref[cb, pl.ds(offp, 8), :] for cb in range(4)]
            return adjoint(off, lam_r, lam_i, xpr, xpi, acc_r, acc_i)

        zero4 = tuple(jnp.zeros((8, 128), F32) for _ in range(4))
        carry = lax.fori_loop(0, t - 1, step, (tuple(_state_tiles(lr_ref)), tuple(_state_tiles(li_ref)), zero4, zero4), unroll=4)
        has_prev = jnp.where(i < nt - 1, 1.0, 0.0)
        xpr = [hr_ref[cb] * has_prev for cb in range(4)]
        xpi = [hi_ref[cb] * has_prev for cb in range(4)]
        lam_r, lam_i, acc_r, acc_i = adjoint(0, carry[0], carry[1], xpr, xpi, carry[2], carry[3])
        for cb in range(4):
            lr_ref[:, cb * 128:(cb + 1) * 128] = lam_r[cb]
            li_ref[:, cb * 128:(cb + 1) * 128] = lam_i[cb]
            dar_ref[:, cb * 128:(cb + 1) * 128] += acc_r[cb]
            dai_ref[:, cb * 128:(cb + 1) * 128] += acc_i[cb]

        dyv = dy_ref[...]
        uv = u_ref[...]
        dd_ref[...] += jnp.sum(dyv * uv.astype(F32), axis=0, keepdims=True)
        for r in range(npc):
            lam = _gather_rows(gr_ref, gi_ref, r, t).astype(BF16)
            sl = slice(r * PIECE, (r + 1) * PIECE)
            du_ref[:, sl] = jnp.dot(lam, bpt_ref[r], preferred_element_type=F32) + d_ref[:, sl] * dyv[:, sl]
            dbp_ref[r] += lax.dot_general(uv[:, sl], lam, (((0,), (0,)), ((), ())), preferred_element_type=F32)
            xs = _gather_rows(xr_ref, xi_ref, r, t).astype(BF16)
            dcp_ref[r] += lax.dot_general(xs, dyb[:, sl], (((0,), (0,)), ((), ())), preferred_element_type=F32)

    rev = lambda i: (nt - 1 - i, 0)
    full3 = lambda shp: pl.BlockSpec(shp, lambda i: (0, 0, 0))
    full2 = lambda shp: pl.BlockSpec(shp, lambda i: (0, 0))
    xs_spec = pl.BlockSpec((4, t * 8, 128), lambda i: (0, nt - 1 - i, 0))
    halo_spec = pl.BlockSpec((4, 8, 128), lambda i: (0, jnp.maximum((nt - 1 - i) * t - 1, 0), 0))
    st = jax.ShapeDtypeStruct((8, PSTATES), F32)
    return pl.pallas_call(
        body, name=name, grid=(nt,),
        in_specs=[pl.BlockSpec((t, uw), rev), pl.BlockSpec((t, uw), rev), xs_spec, xs_spec, halo_spec, halo_spec,
                  full3(cpt.shape), full3(bpt.shape), full2(a_re.shape), full2(a_im.shape), full2(dvec.shape)],
        out_specs=(pl.BlockSpec((t, uw), rev), full3((npc, PIECE, 2 * PSTATES)), full3((npc, 2 * PSTATES, PIECE)),
                   full2((8, PSTATES)), full2((8, PSTATES)), full2((1, uw))),
        out_shape=(jax.ShapeDtypeStruct((s, uw), F32), jax.ShapeDtypeStruct((npc, PIECE, 2 * PSTATES), F32),
                   jax.ShapeDtypeStruct((npc, 2 * PSTATES, PIECE), F32), st, st, jax.ShapeDtypeStruct((1, uw), F32)),
        scratch_shapes=[pltpu.VMEM((4, t * 8, 128), F32), pltpu.VMEM((4, t * 8, 128), F32),
                        pltpu.VMEM((8, PSTATES), F32), pltpu.VMEM((8, PSTATES), F32)],
        compiler_params=_cparams(("arbitrary",), 56),
    )(dy, proj, xs_re, xs_im, xs_re, xs_im, cpt, bpt, a_re, a_im, dvec)


_RC = 16
_LC = 128


def _conv3_block(xv, halo, w, b, row):
    h6, h7 = halo[_RC - 2:_RC - 1], halo[_RC - 1:_RC]
    x1 = jnp.where(row == 0, h7, pltpu.roll(xv, 1, 0))
    x2 = jnp.where(row == 0, h6, jnp.where(row == 1, h7, pltpu.roll(xv, 2, 0)))
    return ((b + x2 * w[0:1]) + x1 * w[1:2]) + xv * w[2:3], (x2, x1, xv)


def _ffn_tiles(s, f):
    tm = min(256, s)
    tn = f // 4 if (f // 4) % _LC == 0 else f
    assert tm % _RC == 0 and tn % _LC == 0
    return tm, tn


def _conv_glu_fwd(up0, cw, cb, name):
    _, s, f = up0.shape
    tm, tn = _ffn_tiles(s, f)
    hb = tm // _RC

    def body(x_ref, h_ref, w_ref, b_ref, a_ref):
        first = jnp.where(pl.program_id(0) > 0, 1.0, 0.0)
        row = lax.broadcasted_iota(jnp.int32, (tm, tn), 0)
        ups = [_conv3_block(x_ref[p].astype(F32), h_ref[p].astype(F32) * first, w_ref[p], b_ref[p], row)[0] for p in range(2)]
        a_ref[...] = (_gelu(ups[1]) * ups[0]).astype(BF16)

    return pl.pallas_call(
        body, name=name, grid=(s // tm, f // tn),
        in_specs=[pl.BlockSpec((2, tm, tn), lambda i, j: (0, i, j)),
                  pl.BlockSpec((2, _RC, tn), lambda i, j: (0, jnp.maximum(i * hb - 1, 0), j)),
                  pl.BlockSpec((2, 3, tn), lambda i, j: (0, 0, j)), pl.BlockSpec((2, 1, tn), lambda i, j: (0, 0, j))],
        out_specs=pl.BlockSpec((tm, tn), lambda i, j: (i, j)), out_shape=jax.ShapeDtypeStruct((s, f), BF16),
        compiler_params=_cparams(("parallel", "parallel")),
    )(up0, up0, cw, cb)


def _ffn_bwd_gate(da, up0, cw, cb, name):
    _, s, f = up0.shape
    tm, tn = _ffn_tiles(s, f)
    hb = tm // _RC

    def body(da_ref, x_ref, h_ref, w_ref, b_ref, d_ref, dw_ref, db_ref):
        i = pl.program_id(1)
        first = jnp.where(i > 0, 1.0, 0.0)

        @pl.when(i == 0)
        def _():
            dw_ref[...] = jnp.zeros_like(dw_ref)
            db_ref[...] = jnp.zeros_like(db_ref)

        row = lax.broadcasted_iota(jnp.int32, (tm, tn), 0)
        ups, taps = [], []
        for p in range(2):
            up, tap = _conv3_block(x_ref[p].astype(F32), h_ref[p].astype(F32) * first, w_ref[p], b_ref[p], row)
            ups.append(up)
            taps.append(tap)
        dav = da_ref[...]
        gate, dgate = _gelu_and_grad(ups[1])
        douts = (dav * gate, dav * ups[0] * dgate)
        for p in range(2):
            d_ref[p] = douts[p].astype(BF16)
            db_ref[p] += jnp.sum(douts[p], axis=0, keepdims=True)
            for kk in range(3):
                dw_ref[p, kk:kk + 1, :] += jnp.sum(douts[p] * taps[p][kk], axis=0, keepdims=True)

    return pl.pallas_call(
        body, name=name, grid=(f // tn, s // tm),
        in_specs=[pl.BlockSpec((tm, tn), lambda j, i: (i, j)), pl.BlockSpec((2, tm, tn), lambda j, i: (0, i, j)),
                  pl.BlockSpec((2, _RC, tn), lambda j, i: (0, jnp.maximum(i * hb - 1, 0), j)),
                  pl.BlockSpec((2, 3, tn), lambda j, i: (0, 0, j)), pl.BlockSpec((2, 1, tn), lambda j, i: (0, 0, j))],
        out_specs=(pl.BlockSpec((2, tm, tn), lambda j, i: (0, i, j)), pl.BlockSpec((2, 3, tn), lambda j, i: (0, 0, j)),
                   pl.BlockSpec((2, 1, tn), lambda j, i: (0, 0, j))),
        out_shape=(jax.ShapeDtypeStruct((2, s, f), BF16), jax.ShapeDtypeStruct((2, 3, f), F32), jax.ShapeDtypeStruct((2, 1, f), F32)),
        compiler_params=_cparams(("parallel", "arbitrary")),
    )(da, up0, up0, cw, cb)


def _conv_bwd(dup, cw, name):
    _, s, f = dup.shape
    tm, tn = _ffn_tiles(s, f)
    hb = tm // _RC
    nb = s // tm

    def body(d_ref, h_ref, w_ref, o_ref):
        last = jnp.where(pl.program_id(0) < nb - 1, 1.0, 0.0)
        row = lax.broadcasted_iota(jnp.int32, (tm, tn), 0)
        for p in range(2):
            d = d_ref[p].astype(F32)
            h = h_ref[p].astype(F32) * last
            d1 = jnp.where(row == tm - 1, h[0:1], pltpu.roll(d, tm - 1, 0))
            d2 = jnp.where(row == tm - 1, h[1:2], jnp.where(row == tm - 2, h[0:1], pltpu.roll(d, tm - 2, 0)))
            w = w_ref[p]
            o_ref[p] = (d * w[2:3] + d1 * w[1:2] + d2 * w[0:1]).astype(BF16)

    return pl.pallas_call(
        body, name=name, grid=(nb, f // tn),
        in_specs=[pl.BlockSpec((2, tm, tn), lambda i, j: (0, i, j)),
                  pl.BlockSpec((2, _RC, tn), lambda i, j: (0, jnp.minimum((i + 1) * hb, s // _RC - 1), j)),
                  pl.BlockSpec((2, 3, tn), lambda i, j: (0, 0, j))],
        out_specs=pl.BlockSpec((2, tm, tn), lambda i, j: (0, i, j)), out_shape=jax.ShapeDtypeStruct((2, s, f), BF16),
        compiler_params=_cparams(("parallel", "parallel")),
    )(dup, dup, cw)


def _ada_part(c_all, w_ada, name):
    nb, d = c_all.shape
    depth, _, cols = w_ada.shape
    tn = 1024 if cols % 1024 == 0 else cols

    def body(c_ref, w_ref, o_ref, ca_ref):
        cv = c_ref[...]
        ca = cv * _sigmoid(cv)
        ca_ref[...] = ca
        o_ref[...] = jnp.dot(ca.astype(BF16), w_ref[...].astype(BF16), preferred_element_type=F32)

    return pl.pallas_call(
        body, name=name, grid=(depth, cols // tn),
        in_specs=[pl.BlockSpec((nb, d), lambda l, j: (0, 0)), pl.BlockSpec((None, d, tn), lambda l, j: (l, 0, j))],
        out_specs=(pl.BlockSpec((None, nb, tn), lambda l, j: (l, 0, j)), pl.BlockSpec((nb, d), lambda l, j: (0, 0))),
        out_shape=(jax.ShapeDtypeStruct((depth, nb, cols), F32), jax.ShapeDtypeStruct((nb, d), F32)),
        compiler_params=_cparams(("arbitrary", "arbitrary")),
    )(c_all, w_ada)


def _ada_select(gath, b_ada, name):
    depth, n6 = b_ada.shape
    cols = gath.shape[1]

    def body(g_ref, b_ref, o_ref):
        me = 4 * lax.axis_index("x") + 2 * lax.axis_index("y") + lax.axis_index("c")
        for l in range(depth):
            for j in range(n6 // cols):
                row = (2 * j) * (8 * depth) + l * 8 + me
                o_ref[l:l + 1, j * cols:(j + 1) * cols] = g_ref[pl.ds(row, 1), :] + b_ref[l:l + 1, j * cols:(j + 1) * cols]

    return pl.pallas_call(body, name=name, out_shape=jax.ShapeDtypeStruct((depth, n6), F32))(gath, b_ada)


def _wada_grad(ca_t, d_sel, name):
    d, nb = ca_t.shape
    depth, _, cols = d_sel.shape
    tm = min(256, d)

    def body(a_ref, g_ref, o_ref):
        acc = a_ref[:, 0:1] * g_ref[0:1, :]
        for b in range(1, nb):
            acc = acc + a_ref[:, b:b + 1] * g_ref[b:b + 1, :]
        o_ref[...] = acc

    return pl.pallas_call(
        body, name=name, grid=(depth, d // tm),
        in_specs=[pl.BlockSpec((tm, nb), lambda l, i: (i, 0)), pl.BlockSpec((None, nb, cols), lambda l, i: (l, 0, 0))],
        out_specs=pl.BlockSpec((None, tm, cols), lambda l, i: (l, i, 0)), out_shape=jax.ShapeDtypeStruct((depth, d, cols), F32),
        compiler_params=_cparams(("parallel", "parallel")),
    )(ca_t, d_sel)


def _block_rows(r, c):
    tr = r
    for cand in (2048, 1024, 512, 256, 128, 64, 32, 16, 8):
        if r % cand == 0 and cand * c * 4 <= 2 * MIB:
            tr = cand
            break
    else:
        for cand in (8, 16, 32):
            if r % cand == 0:
                tr = cand
                break
    return tr


def _adamw(w, g, m, v, name):
    nl, r, c = w.shape
    tr = _block_rows(r, c)
    c1 = 1.0 - ADAM_B1 ** ADAM_STEP
    c2 = 1.0 - ADAM_B2 ** ADAM_STEP

    def body(w_ref, g_ref, m_ref, v_ref, d_ref, nm_ref, nv_ref):
        gv = g_ref[...]
        nm = ADAM_B1 * m_ref[...] + (1.0 - ADAM_B1) * gv
        nv = ADAM_B2 * v_ref[...] + (1.0 - ADAM_B2) * (gv * gv)
        d_ref[...] = -ADAM_LR * ((nm / c1) / (jnp.sqrt(nv / c2) + ADAM_EPS) + ADAM_WD * w_ref[...])
        nm_ref[...] = nm
        nv_ref[...] = nv

    spec = pl.BlockSpec((None, tr, c), lambda l, i: (l, i, 0))
    shp = jax.ShapeDtypeStruct((nl, r, c), F32)
    return pl.pallas_call(
        body, name=name, grid=(nl, r // tr), in_specs=[spec] * 4, out_specs=(spec,) * 3, out_shape=(shp,) * 3,
        compiler_params=_cparams(("parallel", "parallel")),
    )(w, g, m, v)


def _sum_slots(x, nslots, name, out_dtype=F32):
    r = x.shape[0] // nslots
    c = x.shape[1]
    tr = _block_rows(r, c)
    nbk = r // tr

    def body(*refs):
        acc = refs[0][...].astype(F32)
        for k in range(1, nslots):
            acc = acc + refs[k][...].astype(F32)
        refs[nslots][...] = acc.astype(out_dtype)

    specs = [pl.BlockSpec((tr, c), functools.partial(lambda k, i: (k * nbk + i, 0), k)) for k in range(nslots)]
    return pl.pallas_call(
        body, name=name, grid=(nbk,), in_specs=specs, out_specs=pl.BlockSpec((tr, c), lambda i: (i, 0)),
        out_shape=jax.ShapeDtypeStruct((r, c), out_dtype), compiler_params=_cparams(("parallel",)),
    )(*([x] * nslots))


def _sum_slots_into(buf, x, layer, cidx, nslots, name):
    _, r2, c = buf.shape
    h = r2 // 2
    tr = _block_rows(h, c)
    nbk = h // tr

    def body(c_ref, b_ref, *refs):
        acc = refs[0][...].astype(F32)
        for k in range(1, nslots):
            acc = acc + refs[k][...].astype(F32)
        refs[nslots][...] = acc

    specs = [pl.BlockSpec((tr, c), functools.partial(lambda k, i, cr: (k * nbk + i, 0), k)) for k in range(nslots)]
    grid_spec = pltpu.PrefetchScalarGridSpec(
        num_scalar_prefetch=1, grid=(nbk,), in_specs=[_ANY] + specs,
        out_specs=pl.BlockSpec((None, tr, c), lambda i, cr: (layer, cr[0] * nbk + i, 0)))
    return pl.pallas_call(
        body, name=name, grid_spec=grid_spec, out_shape=jax.ShapeDtypeStruct(buf.shape, F32), input_output_aliases={1: 0},
        compiler_params=_cparams(("parallel",)),
    )(cidx, buf, *([x] * nslots))


def _share_halves(bufs, name):
    nw = len(bufs)

    def body(*refs):
        ins, outs = refs[:nw], refs[nw:2 * nw]
        send_sems, recv_sems = refs[2 * nw:]
        x, y, c = _mesh_pos()
        cps = []
        for w in range(nw):
            mine = _half_rows(bufs[w].shape[1], c, 8)
            cps.append(_remote(ins[w].at[:, mine, :], outs[w].at[:, mine, :], send_sems, recv_sems, w, (x, y, 1 - c)))
            cps[-1].start()
        for w in range(nw):
            other = outs[w].at[:, _half_rows(bufs[w].shape[1], 1 - c, 8), :]
            _remote(other, other, send_sems, recv_sems, w, (x, y, c)).wait_recv()
        for cp in cps:
            cp.wait_send()

    return pl.pallas_call(
        body, name=name, out_shape=[jax.ShapeDtypeStruct(t.shape, t.dtype) for t in bufs], in_specs=[_ANY] * nw, out_specs=[_ANY] * nw,
        input_output_aliases={w: w for w in range(nw)},
        scratch_shapes=[pltpu.SemaphoreType.DMA((nw,)), pltpu.SemaphoreType.DMA((nw,))],
    )(*bufs)


def _pick(dim, prefs):
    for p in prefs:
        if dim % p == 0:
            return p
    return dim


def _mm(a, b, m, n, k, name, out_dtype, **kw):
    tm = _pick(m, (1408, 1152, 1024, 512, 256, 128))
    tn = _pick(n, (1408, 1152, 1024, 512, 256, 128))
    kdiv = k // max(kw.get("a_stack", 0), kw.get("b_stack", 0) if kw.get("tb") else 0, 1)
    osize = jnp.dtype(out_dtype).itemsize
    tk = kdiv
    for cut in (1, 2, 4, 8, 16):
        tk = kdiv // cut
        vmem = 2 * 2 * tk * (tm + tn) + tm * tn * (2 * osize + 4 + (4 if tk < k else 0))
        if kdiv % cut == 0 and tk % 128 == 0 and vmem <= _MATMUL_VMEM_BUDGET:
            break
    return _matmul(a, b, m=m, n=n, k=k, tm=tm, tn=tn, tk=tk, out_dtype=out_dtype, name=name, **kw)


def _ssm_layout(p):
    g, st = p["lam_re"].shape
    npc = g * st // PSTATES
    lr = p["lam_re"].reshape(npc, PSTATES)
    li = p["lam_im"].reshape(npc, PSTATES)
    ls = jnp.broadcast_to(p["log_step"][:, None], (g, st)).reshape(npc, PSTATES)
    btr = jnp.transpose(p["ssm_b_re"], (2, 0, 1)).reshape(SSM_GROUP, npc, PSTATES)
    bti = jnp.transpose(p["ssm_b_im"], (2, 0, 1)).reshape(SSM_GROUP, npc, PSTATES)
    return lr, li, ls, btr, bti


def _ssm_pieces(bbr, bbi, c_re, c_im):
    npc = bbr.shape[1]
    gl = PSTATES // STATE
    eye = jnp.eye(gl, dtype=bool)

    def b_piece(bb):
        t = jnp.transpose(bb.reshape(SSM_GROUP, npc, gl, STATE), (1, 2, 0, 3))
        full = jnp.where(eye[None, :, None, :, None], t[:, :, :, None, :], 0.0)
        return full.reshape(npc, gl * SSM_GROUP, PSTATES)

    def c_piece(cc):
        t = jnp.transpose(cc.reshape(npc, gl, SSM_GROUP, STATE), (0, 1, 3, 2))
        full = jnp.where(eye[None, :, None, :, None], t[:, :, :, None, :], 0.0)
        return full.reshape(npc, PSTATES, gl * SSM_GROUP)

    bp = jnp.concatenate([b_piece(bbr), b_piece(bbi)], axis=2).astype(BF16)
    cp = jnp.concatenate([c_piece(c_re), c_piece(-c_im)], axis=1).astype(BF16)
    return bp, cp, jnp.swapaxes(bp, 1, 2), jnp.swapaxes(cp, 1, 2)


def _ssm_unpieces(dbp, dcp):
    npc = dbp.shape[0]
    gl = PSTATES // STATE
    idx = jnp.arange(gl)

    def b_diag(x):
        d = x.reshape(npc, gl, SSM_GROUP, gl, STATE)[:, idx, :, idx, :]
        return jnp.transpose(d, (2, 1, 0, 3)).reshape(SSM_GROUP, npc, PSTATES)

    def c_diag(x):
        d = x.reshape(npc, gl, STATE, gl, SSM_GROUP)[:, idx, :, idx, :]
        return jnp.transpose(d, (1, 0, 3, 2)).reshape(npc * gl, SSM_GROUP, STATE)

    return b_diag(dbp[:, :, :PSTATES]), b_diag(dbp[:, :, PSTATES:]), c_diag(dcp[:, :PSTATES, :]), -c_diag(dcp[:, PSTATES:, :])


class _LayerWeights:
    def __init__(self, fetch):
        self._fetch = fetch
        self._got = {}

    def group(self, g, after=None):
        if g not in self._got:
            self._got[g] = self._fetch(g, after)
        return self._got[g]


def _layer_fwd(l, x, ada6, weights, p):
    s, d = x.shape
    sh_m, sc_m, gt_m, sh_f, sc_f, gt_f = ada6
    w = dict(weights.group("mix", x))
    uw = w["w_glu"].shape[0]
    aw = w["w_out"].shape[0] - uw
    ncol = w["w_in"].shape[0]
    row = lambda v: v.reshape(1, -1)
    n = lambda t: f"l{l}_{t}"

    h = _pre_fwd(x, row(p["g_pre_mix"]), sc_m, sh_m, n("pre_mix"))
    proj = _mm(h, w["w_in"], s, ncol, d, n("proj"), BF16, tb=True)
    attn = _attn_fwd(proj, p["attn_sinks"], aw, uw, n("attn_fwd"))
    zin = _ssm_layout(p)
    a_re, a_im, bbr, bbi = _ssm_prep(*zin, n("ssm_prep"))
    bp, cp, bpt, cpt = _ssm_pieces(bbr, bbi, p["ssm_c_re"], p["ssm_c_im"])
    dvec = p["ssm_d"].reshape(1, uw)
    y, xs_re, xs_im = _ssm_fwd(proj, bp, cp, a_re, a_im, dvec, uw, n("ssm_fwd"), t=256 if s % 256 == 0 else 128)
    z = _gelu_fwd(y, n("gelu_fwd"))
    gl = _mm(z, w["w_glu"], s, uw, uw, n("glu"), F32)
    merged = _merge_fwd(attn, y, gl, row(p["g_attn_out"]), row(p["g_ssm_out"]), n("merge_fwd"))
    mix = _mm(merged, w["w_out"], s, d, aw + uw, n("out_proj"), BF16)
    x1 = _post_fwd(x, mix, row(p["g_post_mix"]), gt_m, n("post_mix"))

    w.update(weights.group("ffn", x1))
    f = w["w_down"].shape[0]
    h2 = _pre_fwd(x1, row(p["g_pre_ffn"]), sc_f, sh_f, n("pre_ffn"))
    up0 = _mm(h2, w["w_up"], s, 2 * f, d, n("up_proj"), BF16, b_stack=w["w_up"].shape[0], o_stack=2)
    cw2 = jnp.transpose(p["conv_w"].reshape(3, 2, f), (1, 0, 2))
    cb2 = p["conv_b"].reshape(2, 1, f)
    act = _conv_glu_fwd(up0, cw2, cb2, n("conv_glu"))
    ff = _mm(act, w["w_down"], s, d, f, n("down_proj"), BF16)
    x2 = _post_fwd(x1, ff, row(p["g_post_ffn"]), gt_f, n("post_ffn"))
    saved = dict(x=x, h=h, proj=proj, attn=attn, zin=zin, a_re=a_re, a_im=a_im, bpt=bpt, cpt=cpt, dvec=dvec, y=y, xs_re=xs_re,
                 xs_im=xs_im, z=z, gl=gl, merged=merged, mix=mix, x1=x1, h2=h2, up0=up0, cw2=cw2, cb2=cb2, act=act, ff=ff)
    return x2, saved


def _layer_bwd(l, dx2, ada6, weights, p, sv, on_grads):
    s, d = dx2.shape
    sh_m, sc_m, gt_m, sh_f, sc_f, gt_f = ada6
    w = {**weights.group("mix"), **weights.group("ffn")}
    uw = w["w_glu"].shape[0]
    aw = w["w_out"].shape[0] - uw
    ncol = w["w_in"].shape[0]
    f = w["w_down"].shape[0]
    nst = w["w_up"].shape[0]
    row = lambda v: v.reshape(1, -1)
    n = lambda t: f"l{l}_{t}"
    gw, gs = {}, {}

    dff, dgt_f, gs["g_post_ffn"] = _post_bwd(dx2, sv["ff"], row(p["g_post_ffn"]), gt_f, n("post_ffn_bwd"))
    gw["w_down"] = _mm(sv["act"], dff, f, d, s, n("down_dw"), BF16, ta=True)
    dact = _mm(dff, w["w_down"], s, f, d, n("down_dx"), F32, tb=True)
    dup, dcw2, dcb2 = _ffn_bwd_gate(dact, sv["up0"], sv["cw2"], sv["cb2"], n("ffn_gate_bwd"))
    gs["conv_w"] = jnp.transpose(dcw2, (1, 0, 2)).reshape(3, 2 * f)
    gs["conv_b"] = dcb2.reshape(2 * f)
    dup0 = _conv_bwd(dup, sv["cw2"], n("conv_bwd"))
    gw["w_up"] = _mm(sv["h2"], dup0, d, 2 * f, s, n("up_dw"), BF16, ta=True, b_stack=2, o_stack=nst)
    dh2 = _mm(dup0, w["w_up"], s, d, 2 * f, n("up_dx"), BF16, tb=True, a_stack=2, b_stack=nst)
    dx1, dsh_f, dsc_f, gs["g_pre_ffn"] = _pre_bwd(dx2, dh2, sv["x1"], row(p["g_pre_ffn"]), sc_f, n("pre_ffn_bwd"))
    token = on_grads(l, "ffn", {k: gw.pop(k) for k in ("w_up", "w_down")})
    if token is not None:
        gt_m = gt_m + token[0:1, 0:1]

    dmix, dgt_m, gs["g_post_mix"] = _post_bwd(dx1, sv["mix"], row(p["g_post_mix"]), gt_m, n("post_mix_bwd"))
    gw["w_out"] = _mm(sv["merged"], dmix, aw + uw, d, s, n("out_dw"), BF16, ta=True)
    dmerged = _mm(dmix, w["w_out"], s, aw + uw, d, n("out_dx"), BF16, tb=True)
    dattn, dgl, dzd, gs["g_attn_out"], gs["g_ssm_out"] = _merge_bwd(
        dmerged, sv["attn"], sv["y"], sv["gl"], row(p["g_attn_out"]), row(p["g_ssm_out"]), n("merge_bwd"))
    gw["w_glu"] = _mm(sv["z"], dgl, uw, uw, s, n("glu_dw"), BF16, ta=True)
    dz2 = _mm(dgl, w["w_glu"], s, uw, uw, n("glu_dx"), F32, tb=True)
    dy = _gelu_bwd(dzd, dz2, sv["y"], n("gelu_bwd"))
    du, dbp, dcp, dar, dai, dd = _ssm_bwd(dy, sv["proj"], sv["xs_re"], sv["xs_im"], sv["cpt"], sv["bpt"], sv["a_re"], sv["a_im"],
                                          sv["dvec"], uw, n("ssm_bwd"))
    dq, dkv_c, dkv_p, dsinks = _attn_bwd(sv["proj"], p["attn_sinks"], sv["attn"], dattn, aw, uw, n("attn_bwd"))
    dproj = _assemble_dproj(du, dq, dkv_c, dkv_p, n("dproj"))
    gw["w_in"] = _mm(dproj, sv["h"], ncol, d, s, n("in_dw"), BF16, ta=True)
    dh = _mm(dproj, w["w_in"], s, d, ncol, n("in_dx"), BF16)
    dx0, dsh_m, dsc_m, gs["g_pre_mix"] = _pre_bwd(dx1, dh, sv["x"], row(p["g_pre_mix"]), sc_m, n("pre_mix_bwd"))
    token = on_grads(l, "mix", {k: gw.pop(k) for k in ("w_in", "w_glu", "w_out")})

    dbbr, dbbi, dc_re, dc_im = _ssm_unpieces(dbp, dcp)
    dlr, dli, dls, dbtr, dbti = _ssm_prep_bwd(*sv["zin"], dar, dai, dbbr, dbbi, n("ssm_prep_bwd"))
    g, st = p["lam_re"].shape
    gs["lam_re"] = dlr.reshape(g, st)
    gs["lam_im"] = dli.reshape(g, st)
    gs["log_step"] = jnp.sum(dls.reshape(g, st), axis=1)
    gs["ssm_b_re"] = jnp.transpose(dbtr.reshape(SSM_GROUP, g, st), (1, 2, 0))
    gs["ssm_b_im"] = jnp.transpose(dbti.reshape(SSM_GROUP, g, st), (1, 2, 0))
    gs["ssm_c_re"] = dc_re
    gs["ssm_c_im"] = dc_im
    gs["ssm_d"] = dd.reshape(p["ssm_d"].shape)
    gs["attn_sinks"] = dsinks.reshape(-1)
    gs["b_ada"] = jnp.concatenate([dsh_m, dsc_m, dgt_m, dsh_f, dsc_f, dgt_f], axis=1).reshape(-1)
    for key in ("g_post_ffn", "g_pre_ffn", "g_post_mix", "g_attn_out", "g_ssm_out", "g_pre_mix"):
        gs[key] = gs[key].reshape(-1)
    return dx0, gs, token


def _local_step(x, tgt, ada, wl, pl_small, on_grads):
    d = x.shape[1]
    depth = len(wl)
    ada6 = [[ada[l:l + 1, k * d:(k + 1) * d] for k in range(6)] for l in range(depth)]
    saved = []
    h = x
    for l in range(depth):
        h, sv = _layer_fwd(l, h, ada6[l], wl[l], pl_small[l])
        saved.append(sv)
    dy, lsum = _loss_head(h, tgt, "loss_head")
    loss = 0.5 * lsum[0, 0] / d
    gss = [None] * depth
    dx = dy
    for l in reversed(range(depth)):
        dx, gss[l], token = _layer_bwd(l, dx, ada6[l], wl[l], pl_small[l], saved[l], on_grads)
        if token is not None and l > 0:
            ada6[l - 1][5] = ada6[l - 1][5] + token[0:1, 0:1]
    return loss, dx, gss


_ANY = pl.BlockSpec(memory_space=pl.ANY)


def _mesh_pos():
    return lax.axis_index("x"), lax.axis_index("y"), lax.axis_index("c")


def _other_chips(x, y):
    return [(1 - x, y), (x, 1 - y), (1 - x, 1 - y)]


def _remote(src, dst, send_sems, recv_sems, k, to):
    return pltpu.make_async_remote_copy(src_ref=src, dst_ref=dst, send_sem=send_sems.at[k], recv_sem=recv_sems.at[k],
                                        device_id=to, device_id_type=MESH)


def _allgather8(xs, name):
    m, n = xs.shape

    def body(x_ref, out_ref, send_sems, recv_sems):
        x, y, c = _mesh_pos()
        me, sibling = (x, y, c), (x, y, 1 - c)
        chips = _other_chips(x, y)

        def rows(px, py, pc):
            return out_ref.at[pl.ds((4 * px + 2 * py + pc) * m, m), :]

        def copy(k, block, to, src=None):
            return _remote(rows(*block) if src is None else src, rows(*block), send_sems, recv_sems, k, to)

        first = [copy(0, me, sibling, src=x_ref)]
        first += [copy(1 + j, me, (*chip, c), src=x_ref) for j, chip in enumerate(chips)]
        for cp in first:
            cp.start()
        passed = [copy(4 + j, (*chip, c), sibling) for j, chip in enumerate(chips)]
        for j, chip in enumerate(chips):
            copy(1 + j, (*chip, c), me).wait_recv()
            passed[j].start()
        copy(0, sibling, me).wait_recv()
        for j, chip in enumerate(chips):
            copy(4 + j, (*chip, 1 - c), me).wait_recv()
        for cp in first + passed:
            cp.wait_send()

    out = pl.pallas_call(
        body, name=name, out_shape=jax.ShapeDtypeStruct((8 * m, n), xs.dtype), in_specs=[_ANY], out_specs=_ANY,
        scratch_shapes=[pltpu.SemaphoreType.DMA((7,)), pltpu.SemaphoreType.DMA((7,))],
    )(xs)
    x, y, c = _mesh_pos()
    return lax.dynamic_update_slice(out, xs, ((4 * x + 2 * y + c) * m, 0))


def _half_rows(ref_rows, half, align):
    h = ref_rows // 2
    return pl.ds(pl.multiple_of(half * h, align), h)


_HBM = pl.BlockSpec(memory_space=pltpu.HBM)
_SEMS = pl.BlockSpec(memory_space=pltpu.SEMAPHORE)
_EFFECT = pltpu.SideEffectType.DATAFLOW_SIDE_EFFECTING
_TOKEN = jax.ShapeDtypeStruct((8, 128), F32)


def _in_hbm(arrays):
    return [pltpu.with_memory_space_constraint(a, pltpu.HBM) for a in arrays]


def _chip_copy(kind, srcs, lands, w, q, chip, mine, c, send, recv, k):
    if kind == "gather":
        rows = _half_rows(srcs[w].shape[0], c, 16)
        return _remote(srcs[w].at[rows, :], lands[w].at[mine, rows, :], send, recv, k, (*chip, c))
    return _remote(srcs[w].at[2 * chip[0] + chip[1]], lands[w].at[mine], send, recv, k, (*chip, c))


def _chip_landing(kind, srcs, lands, w, chip, c):
    if kind == "gather":
        return lands[w].at[2 * chip[0] + chip[1], _half_rows(srcs[w].shape[0], c, 16), :]
    return lands[w].at[2 * chip[0] + chip[1]]


def _ici_start(kind, srcs, groups, name, after=None):
    nw, ng = len(srcs), len(groups)
    lands = [lax.empty((4,) + s.shape if kind == "gather" else s.shape, s.dtype) for s in srcs]
    extra = [] if after is None else [after]

    def body(*refs):
        ins, lnd = refs[:nw], refs[nw:2 * nw]
        sems = refs[2 * nw + len(extra):2 * nw + len(extra) + 2 * ng]
        token = refs[-1]
        x, y, c = _mesh_pos()
        for g, members in enumerate(groups):
            for j, w in enumerate(members):
                for q, chip in enumerate(_other_chips(x, y)):
                    _chip_copy(kind, ins, lnd, w, q, chip, 2 * x + y, c, sems[2 * g], sems[2 * g + 1], 3 * j + q).start()
        token[...] = jnp.zeros_like(token)

    sem_shapes = [pltpu.SemaphoreType.DMA((3 * len(members),)) for members in groups for _ in range(2)]
    out = pl.pallas_call(
        body, name=name,
        out_shape=(*sem_shapes, *[pltpu.HBM(s.shape, s.dtype) for s in srcs], *[pltpu.HBM(t.shape, t.dtype) for t in lands], _TOKEN),
        in_specs=[_HBM] * (2 * nw) + [_ANY] * len(extra),
        out_specs=(*([_SEMS] * (2 * ng)), *([_HBM] * (2 * nw)), pl.BlockSpec(memory_space=pltpu.VMEM)),
        input_output_aliases={i: 2 * ng + i for i in range(2 * nw)},
        compiler_params=pltpu.CompilerParams(has_side_effects=_EFFECT),
    )(*_in_hbm(srcs), *_in_hbm(lands), *extra)
    sems = [(out[2 * g], out[2 * g + 1]) for g in range(ng)]
    return sems, list(out[2 * ng:2 * ng + nw]), list(out[2 * ng + nw:2 * ng + 2 * nw]), out[-1]


def _ici_wait(kind, sems, srcs, lands, after, name):
    nm = len(srcs)

    def body(*refs):
        ins, lnd = refs[:nm], refs[nm:2 * nm]
        send, recv = refs[2 * nm], refs[2 * nm + 1]
        x, y, c = _mesh_pos()
        for j in range(nm):
            for q, chip in enumerate(_other_chips(x, y)):
                _chip_copy(kind, ins, lnd, j, q, chip, 2 * x + y, c, send, recv, 3 * j + q).wait_send()
                landed = _chip_landing(kind, ins, lnd, j, chip, c)
                _remote(landed, landed, send, recv, 3 * j + q, (x, y, c)).wait_recv()

    out = pl.pallas_call(
        body, name=name, out_shape=(*[pltpu.HBM(s.shape, s.dtype) for s in srcs], *[pltpu.HBM(t.shape, t.dtype) for t in lands]),
        in_specs=[_HBM] * (2 * nm) + [_SEMS, _SEMS, _ANY], out_specs=tuple([_HBM] * (2 * nm)),
        input_output_aliases={i: i for i in range(2 * nm)},
        compiler_params=pltpu.CompilerParams(has_side_effects=_EFFECT),
    )(*srcs, *lands, sems[0], sems[1], after)
    return list(out[:nm]), list(out[nm:])


def _gather_finish(shards, lands, name):
    nw = len(shards)

    def body(*refs):
        ins, lnd, outs = refs[:nw], refs[nw:2 * nw], refs[2 * nw:3 * nw]
        send_sems, recv_sems = refs[3 * nw:]
        x, y, c = _mesh_pos()
        mine = 2 * x + y
        sibling = (x, y, 1 - c)
        chips = _other_chips(x, y)

        def blk(ref, chip_idx, half):
            return ref.at[chip_idx, _half_rows(ref.shape[1], half, 16), :]

        sends = []
        for w in range(nw):
            for q, chip in enumerate(chips):
                k = 2 * chip[0] + chip[1]
                sends.append(_remote(blk(lnd[w], k, c), blk(outs[w], k, c), send_sems, recv_sems, 4 * w + q, sibling))
            sends.append(_remote(ins[w], outs[w].at[mine], send_sems, recv_sems, 4 * w + 3, sibling))
        for cp in sends:
            cp.start()
        for w in range(nw):
            for q, chip in enumerate(chips):
                other = blk(outs[w], 2 * chip[0] + chip[1], 1 - c)
                _remote(other, other, send_sems, recv_sems, 4 * w + q, (x, y, c)).wait_recv()
            own = outs[w].at[mine]
            _remote(own, own, send_sems, recv_sems, 4 * w + 3, (x, y, c)).wait_recv()
        for cp in sends:
            cp.wait_send()

    return pl.pallas_call(
        body, name=name, out_shape=[jax.ShapeDtypeStruct(t.shape, t.dtype) for t in lands],
        in_specs=[_ANY] * (2 * nw), out_specs=[_ANY] * nw, input_output_aliases={nw + w: w for w in range(nw)},
        scratch_shapes=[pltpu.SemaphoreType.DMA((4 * nw,)), pltpu.SemaphoreType.DMA((4 * nw,))],
    )(*shards, *lands)


def _exchange_halves(gs, name):
    nw = len(gs)

    def body(*refs):
        ins, outs = refs[:nw], refs[nw:2 * nw]
        send_sems, recv_sems = refs[2 * nw:]
        x, y, c = _mesh_pos()
        cps = []
        for w in range(nw):
            src = ins[w].at[:, _half_rows(gs[w].shape[1], 1 - c, 16), :]
            cps.append(_remote(src, outs[w], send_sems, recv_sems, w, (x, y, 1 - c)))
            cps[-1].start()
        for cp in cps:
            cp.wait_recv()
        for cp in cps:
            cp.wait_send()

    return pl.pallas_call(
        body, name=name, out_shape=[jax.ShapeDtypeStruct((4, g.shape[1] // 2, g.shape[2]), g.dtype) for g in gs],
        in_specs=[_ANY] * nw, out_specs=[_ANY] * nw,
        scratch_shapes=[pltpu.SemaphoreType.DMA((nw,)), pltpu.SemaphoreType.DMA((nw,))],
    )(*gs)


def _add_half(g, recv, cidx, name):
    _, r, c = g.shape
    h = r // 2
    tr = _block_rows(h, c // 2)
    nbh = h // tr
    assert tr % 16 == 0

    def body(c_ref, g_ref, r_ref, o_ref):
        o_ref[...] = (g_ref[...].astype(F32) + r_ref[...].astype(F32)).astype(BF16)

    grid_spec = pltpu.PrefetchScalarGridSpec(
        num_scalar_prefetch=1, grid=(4, nbh),
        in_specs=[pl.BlockSpec((None, tr, c), lambda s, i, cr: (s, cr[0] * nbh + i, 0)),
                  pl.BlockSpec((None, tr, c), lambda s, i, cr: (s, i, 0))],
        out_specs=pl.BlockSpec((None, tr, c), lambda s, i, cr: (s, i, 0)))
    return pl.pallas_call(
        body, name=name, grid_spec=grid_spec, out_shape=jax.ShapeDtypeStruct((4, h, c), BF16),
        compiler_params=_cparams(("parallel", "parallel")),
    )(cidx, g, recv)


_BIG = ("w_in", "w_glu", "w_out", "w_up", "w_down")
_GROUPS = {"mix": ("w_in", "w_glu", "w_out"), "ffn": ("w_up", "w_down")}
_SMALL = ("b_ada", "g_pre_mix", "g_post_mix", "attn_sinks", "lam_re", "lam_im", "log_step", "ssm_b_re", "ssm_b_im", "ssm_c_re",
          "ssm_c_im", "ssm_d", "g_attn_out", "g_ssm_out", "g_pre_ffn", "g_post_ffn", "conv_b")
_WEIGHTS = ("w_ada", "b_ada", "g_pre_mix", "g_post_mix", "w_in", "attn_sinks", "lam_re", "lam_im", "log_step", "ssm_b_re", "ssm_b_im",
            "ssm_c_re", "ssm_c_im", "ssm_d", "w_glu", "g_attn_out", "g_ssm_out", "w_out", "g_pre_ffn", "g_post_ffn", "w_up", "conv_w",
            "conv_b", "w_down")
_LANES = 1024


def _pack(parts, rows_to):
    flat = jnp.concatenate([p.reshape(-1) for p in parts])
    per = _LANES * rows_to
    total = -(-flat.shape[0] // per) * per
    return jnp.pad(flat, (0, total - flat.shape[0])).reshape(total // _LANES, _LANES)


def _unpack(packed, shapes):
    flat = packed.reshape(-1)
    out, off = [], 0
    for shp in shapes:
        size = math.prod(shp)
        out.append(flat[off:off + size].reshape(shp))
        off += size
    return out


def kernel(x, c, w_ada, b_ada, g_pre_mix, g_post_mix, w_in, attn_sinks, lam_re, lam_im, log_step, ssm_b_re, ssm_b_im, ssm_c_re, ssm_c_im, ssm_d, w_glu, g_attn_out, g_ssm_out, w_out, g_pre_ffn, g_post_ffn, w_up, conv_w, conv_b, w_down, loss_target, m_w_ada, m_b_ada, m_g_pre_mix, m_g_post_mix, m_w_in, m_attn_sinks, m_lam_re, m_lam_im, m_log_step, m_ssm_b_re, m_ssm_b_im, m_ssm_c_re, m_ssm_c_im, m_ssm_d, m_w_glu, m_g_attn_out, m_g_ssm_out, m_w_out, m_g_pre_ffn, m_g_post_ffn, m_w_up, m_conv_w, m_conv_b, m_w_down, v_w_ada, v_b_ada, v_g_pre_mix, v_g_post_mix, v_w_in, v_attn_sinks, v_lam_re, v_lam_im, v_log_step, v_ssm_b_re, v_ssm_b_im, v_ssm_c_re, v_ssm_c_im, v_ssm_d, v_w_glu, v_g_attn_out, v_g_ssm_out, v_w_out, v_g_pre_ffn, v_g_post_ffn, v_w_up, v_conv_w, v_conv_b, v_w_down):
    given = dict(locals())
    wts = {n: given[n] for n in _WEIGHTS}
    mom = {n: given["m_" + n] for n in _WEIGHTS}
    var = {n: given["v_" + n] for n in _WEIGHTS}
    depth, d, ada_cols = w_ada.shape
    nchips = 4
    xi, yi, ci = lax.axis_index("x"), lax.axis_index("y"), lax.axis_index("c")
    chip = 2 * xi + yi
    cidx = jnp.reshape(ci, (1,)).astype(jnp.int32)

    cw_cols = conv_w.shape[2]
    vec = _pack([c, conv_w], 8)
    g1 = _allgather8(vec, "ag_cond").reshape(8, -1)
    c_all = g1[:, :d]
    cw_sh = g1[0::2, d:d + depth * 3 * cw_cols].reshape(nchips, depth, 3, cw_cols)
    conv_w_full = jnp.transpose(cw_sh, (1, 2, 0, 3)).reshape(depth, 3, nchips * cw_cols)

    ada_part, c_act = _ada_part(c_all, w_ada, "ada_part")
    g2 = _allgather8(ada_part.reshape(depth * 8, ada_cols), "ag_ada")
    ada = _ada_select(g2, b_ada, "ada_select")

    order = [(l, g) for l in range(depth) for g in _GROUPS]
    for table in (wts, mom, var):
        table["w_in"] = jnp.swapaxes(table["w_in"], 1, 2)
    members = {key: [wts[n][key[0]].astype(BF16) for n in _GROUPS[key[1]]] for key in order}
    flat = [s for key in order for s in members[key]]
    index, at = {}, 0
    for key in order:
        index[key] = list(range(at, at + len(members[key])))
        at += len(members[key])
    ag_sems, ag_srcs, ag_lands, ag_token = _ici_start("gather", flat, [index[key] for key in order], "ag_start", after=ada)
    ada = ada + ag_token[0:1, 0:1]

    def fetch(l, g, after):
        pos, ids = order.index((l, g)), index[(l, g)]
        srcs, lands = [ag_srcs[i] for i in ids], [ag_lands[i] for i in ids]
        srcs, lands = _ici_wait("gather", ag_sems[pos], srcs, lands, ag_token if after is None else after, f"ag_wait_l{l}_{g}")
        got = dict(zip(_GROUPS[g], _gather_finish(srcs, lands, f"ag_finish_l{l}_{g}")))
        if g == "ffn":
            return dict(w_up=got["w_up"], w_down=got["w_down"].reshape(-1, got["w_down"].shape[2]))
        w_in_t = got["w_in"].reshape(-1, d)
        split = w_in_t.shape[0] - nchips * got["w_glu"].shape[1]
        return dict(w_in=jnp.concatenate([w_in_t[split:], w_in_t[:split]], axis=0),
                    w_glu=got["w_glu"].reshape(-1, got["w_glu"].shape[2]), w_out=got["w_out"].reshape(-1, got["w_out"].shape[2]))

    wl = [_LayerWeights(functools.partial(fetch, l)) for l in range(depth)]
    wl[0].group("mix")
    ps = []
    for l in range(depth):
        small = {n: wts[n][l] for n in _SMALL if n != "b_ada"}
        small["conv_w"] = conv_w_full[l]
        ps.append(small)

    in_flight = {}

    def on_grads(l, g, gw):
        stacks = []
        for n in _GROUPS[g]:
            t = gw[n]
            if n == "w_in":
                uw = nchips * wts["w_glu"].shape[1]
                t = jnp.concatenate([t[uw:], t[:uw]], axis=0).reshape(nchips, -1, d)
            elif n != "w_up":
                t = t.reshape(nchips, t.shape[0] // nchips, t.shape[1])
            stacks.append(t)
        from_sibling = _exchange_halves(stacks, f"rs_sibling_l{l}_{g}")
        partials = [_add_half(s, r, cidx, f"rs_add_l{l}_{n}") for s, r, n in zip(stacks, from_sibling, _GROUPS[g])]
        sems, srcs, lands, token = _ici_start("scatter", partials, [list(range(len(partials)))], f"rs_start_l{l}_{g}")
        in_flight[(l, g)] = (sems[0], srcs, lands)
        return token

    loss_sum, grad_x, gss = _local_step(x[0], loss_target[0], ada, wl, ps, on_grads)
    loss = lax.psum(loss_sum, ("x", "y", "c"))

    reduced = {n: lax.empty(wts[n].shape, F32) for n in _BIG}
    for key in reversed(order):
        l, g = key
        sems, srcs, lands = in_flight[key]
        sent, landed = _ici_wait("scatter", sems, srcs, lands, grad_x, f"rs_wait_l{l}_{g}")
        for n, t, p in zip(_GROUPS[g], landed, sent):
            t = lax.dynamic_update_slice(t, lax.dynamic_slice_in_dim(p, chip, 1, axis=0), (chip, 0, 0))
            reduced[n] = _sum_slots_into(reduced[n], t.reshape(-1, t.shape[2]), l, cidx, nchips, f"rs_sum_l{l}_{n}")
    big_grads = dict(zip(_BIG, _share_halves([reduced[n] for n in _BIG], "rs_share")))

    small_parts = [jnp.stack([gss[l][n] for l in range(depth)]) for n in _SMALL]
    pack_small = _pack(small_parts, 8)
    pack_cw = _pack([jnp.stack([gss[l]["conv_w"] for l in range(depth)])], 8)
    rows_small = pack_small.shape[0]
    mine = jnp.concatenate([pack_small, pack_cw], axis=0).astype(BF16)
    g3 = _allgather8(mine, "ag_small")
    total = _sum_slots(g3, 8, "sum_small")
    grads = dict(big_grads)
    for n, v in zip(_SMALL, _unpack(total[:rows_small], [wts[n].shape for n in _SMALL])):
        grads[n] = v
    conv_w_grad = _unpack(total[rows_small:], [(depth, 3, nchips * cw_cols)])[0]
    grads["conv_w"] = lax.dynamic_slice_in_dim(conv_w_grad, chip * cw_cols, cw_cols, axis=2)

    ada_rows = depth * 6 * d // _LANES
    d_ada_all = g3.reshape(8, -1, _LANES)[:, :ada_rows].reshape(8, depth, 6 * d).astype(F32)
    d_sel = lax.dynamic_slice_in_dim(jnp.transpose(d_ada_all, (1, 0, 2)), chip * ada_cols, ada_cols, axis=2)
    grads["w_ada"] = _wada_grad(jnp.transpose(c_act), d_sel, "w_ada_grad")

    delta, new_m, new_v = {}, {}, {}
    for n in _WEIGHTS:
        shp = wts[n].shape
        view = (lambda t: t) if len(shp) == 3 else (lambda t: t.reshape(1, -1, shp[-1]))
        outs = _adamw(view(wts[n]), view(grads[n]), view(mom[n]), view(var[n]), f"adamw_{n}")
        delta[n], new_m[n], new_v[n] = [t.reshape(shp) for t in outs]

    for table in (grads, delta, new_m, new_v):
        table["w_in"] = jnp.swapaxes(table["w_in"], 1, 2)
    return (loss, grad_x[None], *[grads[n] for n in _WEIGHTS], *[delta[n] for n in _WEIGHTS],
            *[new_m[n] for n in _WEIGHTS], *[new_v[n] for n in _WEIGHTS])
```

```python
import functools
import math

import jax
import jax.numpy as jnp
from jax import lax
from jax.experimental import pallas as pl
from jax.experimental.pallas import tpu as pltpu

F32 = jnp.float32
BF16 = jnp.bfloat16
EPS = 1e-6
NEG = -1e30
WINDOW = 128
HEAD_DIM = 64
KV_RATIO = 8
SSM_GROUP = 16
STATE = 64
PIECE = 128
PSTATES = 512
DEPTH = 2
ADAM_LR, ADAM_B1, ADAM_B2, ADAM_EPS, ADAM_WD, ADAM_STEP = 0.001, 0.9, 0.999, 1e-08, 0.01, 10
MIB = 1024 * 1024
_MATMUL_VMEM_BUDGET = 40 * MIB
MESH = pl.DeviceIdType.MESH


def _cparams(sem=None, vmem_mib=48):
    return pltpu.CompilerParams(dimension_semantics=sem, vmem_limit_bytes=vmem_mib * MIB)


def _gelu(x):
    c = math.sqrt(2.0 / math.pi)
    return 0.5 * x * (1.0 + jnp.tanh(c * (x + 0.044715 * (x * x * x))))


def _gelu_and_grad(x):
    c = math.sqrt(2.0 / math.pi)
    x2 = x * x
    t = jnp.tanh(c * (x + 0.044715 * (x2 * x)))
    half = 0.5 * (1.0 + t)
    return x * half, half + 0.5 * x * (1.0 - t * t) * c * (1.0 + 3.0 * 0.044715 * x2)


def _gelu_grad(x):
    return _gelu_and_grad(x)[1]


def _sigmoid(x):
    return 1.0 / (1.0 + jnp.exp(-x))


def _matmul(a, b, *, m, n, k, tm, tn, tk, out_dtype, name, ta=False, tb=False, a_stack=0, b_stack=0, o_stack=0):
    assert m % tm == 0 and n % tn == 0 and k % tk == 0, (name, m, n, k, tm, tn, tk)
    nk = k // tk

    if a_stack:
        assert not ta and (k // a_stack) % tk == 0
        per = (k // a_stack) // tk
        a_spec = pl.BlockSpec((None, tm, tk), lambda i, j, kk: (kk // per, i, kk % per))
    elif ta:
        a_spec = pl.BlockSpec((tk, tm), lambda i, j, kk: (kk, i))
    else:
        a_spec = pl.BlockSpec((tm, tk), lambda i, j, kk: (i, kk))
    if b_stack and tb:
        perb = (k // b_stack) // tk
        b_spec = pl.BlockSpec((None, tn, tk), lambda i, j, kk: (kk // perb, j, kk % perb))
    elif b_stack:
        perb = (n // b_stack) // tn
        b_spec = pl.BlockSpec((None, tk, tn), lambda i, j, kk: (j // perb, kk, j % perb))
    elif tb:
        b_spec = pl.BlockSpec((tn, tk), lambda i, j, kk: (j, kk))
    else:
        b_spec = pl.BlockSpec((tk, tn), lambda i, j, kk: (kk, j))
    if o_stack:
        pero = (n // o_stack) // tn
        o_spec = pl.BlockSpec((None, tm, tn), lambda i, j, kk: (j // pero, i, j % pero))
        o_shape = jax.ShapeDtypeStruct((o_stack, m, n // o_stack), out_dtype)
    else:
        o_spec = pl.BlockSpec((tm, tn), lambda i, j, kk: (i, j))
        o_shape = jax.ShapeDtypeStruct((m, n), out_dtype)
    dims = (((0 if ta else 1,), (1 if tb else 0,)), ((), ()))

    def body(a_ref, b_ref, o_ref, *acc):
        p = lax.dot_general(a_ref[...].astype(BF16), b_ref[...].astype(BF16), dims, preferred_element_type=F32)
        if nk == 1:
            o_ref[...] = p.astype(o_ref.dtype)
        else:
            acc_ref = acc[0]
            kk = pl.program_id(2)

            @pl.when(kk == 0)
            def _():
                acc_ref[...] = p

            @pl.when(kk > 0)
            def _():
                acc_ref[...] += p

            @pl.when(kk == nk - 1)
            def _():
                o_ref[...] = acc_ref[...].astype(o_ref.dtype)

    return pl.pallas_call(
        body, name=name, grid=(m // tm, n // tn, nk), in_specs=[a_spec, b_spec], out_specs=o_spec, out_shape=o_shape,
        scratch_shapes=[] if nk == 1 else [pltpu.VMEM((tm, tn), F32)],
        compiler_params=_cparams(("parallel", "parallel", "arbitrary"), 56),
    )(a, b)


def _row(d):
    return pl.BlockSpec((1, d), lambda i: (0, 0))


def _tok(tm, d):
    return pl.BlockSpec((tm, d), lambda i: (i, 0))


def _pre_fwd(x, g, sc, sh, name):
    s, d = x.shape
    tm = min(512, s)

    def body(x_ref, g_ref, sc_ref, sh_ref, h_ref):
        xv = x_ref[...]
        r = lax.rsqrt(jnp.mean(xv * xv, axis=-1, keepdims=True) + EPS)
        h_ref[...] = (((xv * r) * g_ref[...]) * (1.0 + sc_ref[...]) + sh_ref[...]).astype(BF16)

    return pl.pallas_call(
        body, name=name, grid=(s // tm,), in_specs=[_tok(tm, d), _row(d), _row(d), _row(d)], out_specs=_tok(tm, d),
        out_shape=jax.ShapeDtypeStruct((s, d), BF16), compiler_params=_cparams(("parallel",)),
    )(x, g, sc, sh)


def _post_fwd(x, o, g, gt, name):
    s, d = x.shape
    tm = min(512, s)

    def body(x_ref, o_ref, g_ref, gt_ref, y_ref):
        ov = o_ref[...].astype(F32)
        r = lax.rsqrt(jnp.mean(ov * ov, axis=-1, keepdims=True) + EPS)
        y_ref[...] = x_ref[...] + (1.0 + gt_ref[...]) * ((ov * r) * g_ref[...])

    return pl.pallas_call(
        body, name=name, grid=(s // tm,), in_specs=[_tok(tm, d), _tok(tm, d), _row(d), _row(d)], out_specs=_tok(tm, d),
        out_shape=jax.ShapeDtypeStruct((s, d), F32), compiler_params=_cparams(("parallel",)),
    )(x, o, g, gt)


def _post_bwd(dxo, o, g, gt, name):
    s, d = o.shape
    tm = min(512, s)

    def body(dx_ref, o_ref, g_ref, gt_ref, do_ref, dgt_ref, dg_ref):
        i = pl.program_id(0)
        dx = dx_ref[...]
        ov = o_ref[...].astype(F32)
        gv = g_ref[...]
        r = lax.rsqrt(jnp.mean(ov * ov, axis=-1, keepdims=True) + EPS)
        oh = ov * r
        dn = dx * (1.0 + gt_ref[...])
        e = dn * gv
        do_ref[...] = (r * (e - oh * jnp.mean(e * oh, axis=-1, keepdims=True))).astype(BF16)
        p_gt = jnp.sum(dx * (oh * gv), axis=0, keepdims=True)
        p_g = jnp.sum(dn * oh, axis=0, keepdims=True)

        @pl.when(i == 0)
        def _():
            dgt_ref[...] = p_gt
            dg_ref[...] = p_g

        @pl.when(i > 0)
        def _():
            dgt_ref[...] += p_gt
            dg_ref[...] += p_g

    row = jax.ShapeDtypeStruct((1, d), F32)
    return pl.pallas_call(
        body, name=name, grid=(s // tm,), in_specs=[_tok(tm, d), _tok(tm, d), _row(d), _row(d)],
        out_specs=(_tok(tm, d), _row(d), _row(d)), out_shape=(jax.ShapeDtypeStruct((s, d), BF16), row, row),
        compiler_params=_cparams(("arbitrary",)),
    )(dxo, o, g, gt)


def _pre_bwd(dres, dh, x, g, sc, name):
    s, d = x.shape
    tm = min(256, s)

    def body(dres_ref, dh_ref, x_ref, g_ref, sc_ref, dx_ref, dsh_ref, dsc_ref, dg_ref):
        i = pl.program_id(0)
        dh_v = dh_ref[...].astype(F32)
        xv = x_ref[...]
        gv = g_ref[...]
        one_sc = 1.0 + sc_ref[...]
        r = lax.rsqrt(jnp.mean(xv * xv, axis=-1, keepdims=True) + EPS)
        xh = xv * r
        e = dh_v * one_sc * gv
        dx_ref[...] = dres_ref[...] + r * (e - xh * jnp.mean(e * xh, axis=-1, keepdims=True))
        p_sh = jnp.sum(dh_v, axis=0, keepdims=True)
        p_sc = jnp.sum(dh_v * (xh * gv), axis=0, keepdims=True)
        p_g = jnp.sum(dh_v * one_sc * xh, axis=0, keepdims=True)

        @pl.when(i == 0)
        def _():
            dsh_ref[...] = p_sh
            dsc_ref[...] = p_sc
            dg_ref[...] = p_g

        @pl.when(i > 0)
        def _():
            dsh_ref[...] += p_sh
            dsc_ref[...] += p_sc
            dg_ref[...] += p_g

    row = jax.ShapeDtypeStruct((1, d), F32)
    return pl.pallas_call(
        body, name=name, grid=(s // tm,), in_specs=[_tok(tm, d), _tok(tm, d), _tok(tm, d), _row(d), _row(d)],
        out_specs=(_tok(tm, d), _row(d), _row(d), _row(d)), out_shape=(jax.ShapeDtypeStruct((s, d), F32), row, row, row),
        compiler_params=_cparams(("arbitrary",)),
    )(dres, dh, x, g, sc)


def _loss_head(y, tgt, name):
    s, d = y.shape
    tm = min(256, s)

    def body(y_ref, t_ref, dy_ref, l_ref):
        i = pl.program_id(0)
        err = y_ref[...] - t_ref[...]
        dy_ref[...] = err * (1.0 / d)
        part = jnp.zeros((1, 128), F32) + jnp.sum(err * err)

        @pl.when(i == 0)
        def _():
            l_ref[...] = part

        @pl.when(i > 0)
        def _():
            l_ref[...] += part

    return pl.pallas_call(
        body, name=name, grid=(s // tm,), in_specs=[_tok(tm, d), _tok(tm, d)],
        out_specs=(_tok(tm, d), pl.BlockSpec((1, 128), lambda i: (0, 0))),
        out_shape=(jax.ShapeDtypeStruct((s, d), F32), jax.ShapeDtypeStruct((1, 128), F32)),
        compiler_params=_cparams(("arbitrary",)),
    )(y, tgt)


def _gelu_fwd(y, name):
    s, u = y.shape
    tm = min(512, s)

    def body(y_ref, z_ref):
        z_ref[...] = _gelu(y_ref[...]).astype(BF16)

    return pl.pallas_call(
        body, name=name, grid=(s // tm,), in_specs=[_tok(tm, u)], out_specs=_tok(tm, u),
        out_shape=jax.ShapeDtypeStruct((s, u), BF16), compiler_params=_cparams(("parallel",)),
    )(y)


def _merge_fwd(attn, y, gl, ga, gs, name):
    s, aw = attn.shape
    uw = y.shape[1]
    tm = min(256, s)

    def body(a_ref, y_ref, gl_ref, ga_ref, gs_ref, m_ref):
        av = a_ref[...].astype(F32)
        ra = lax.rsqrt(jnp.mean(av * av, axis=-1, keepdims=True) + EPS)
        m_ref[:, :aw] = ((av * ra) * ga_ref[...]).astype(BF16)
        ssm = _gelu(y_ref[...]) * _sigmoid(gl_ref[...].astype(F32))
        rs = lax.rsqrt(jnp.mean(ssm * ssm, axis=-1, keepdims=True) + EPS)
        m_ref[:, aw:] = ((ssm * rs) * gs_ref[...]).astype(BF16)

    return pl.pallas_call(
        body, name=name, grid=(s // tm,), in_specs=[_tok(tm, aw), _tok(tm, uw), _tok(tm, uw), _row(aw), _row(uw)],
        out_specs=_tok(tm, aw + uw), out_shape=jax.ShapeDtypeStruct((s, aw + uw), BF16),
        compiler_params=_cparams(("parallel",)),
    )(attn, y, gl, ga, gs)


def _merge_bwd(dm, attn, y, gl, ga, gs, name):
    s, aw = attn.shape
    uw = y.shape[1]
    tm = min(256, s)

    def body(dm_ref, a_ref, y_ref, gl_ref, ga_ref, gs_ref, da_ref, dgl_ref, dz_ref, dga_ref, dgs_ref):
        i = pl.program_id(0)
        av = a_ref[...].astype(F32)
        dma = dm_ref[:, :aw].astype(F32)
        ra = lax.rsqrt(jnp.mean(av * av, axis=-1, keepdims=True) + EPS)
        ah = av * ra
        e = dma * ga_ref[...]
        da_ref[...] = (ra * (e - ah * jnp.mean(e * ah, axis=-1, keepdims=True))).astype(BF16)
        p_ga = jnp.sum(dma * ah, axis=0, keepdims=True)

        z = _gelu(y_ref[...])
        sig = _sigmoid(gl_ref[...].astype(F32))
        ssm = z * sig
        dms = dm_ref[:, aw:].astype(F32)
        rs = lax.rsqrt(jnp.mean(ssm * ssm, axis=-1, keepdims=True) + EPS)
        sh = ssm * rs
        e2 = dms * gs_ref[...]
        dssm = rs * (e2 - sh * jnp.mean(e2 * sh, axis=-1, keepdims=True))
        dz_ref[...] = dssm * sig
        dgl_ref[...] = (dssm * z * sig * (1.0 - sig)).astype(BF16)
        p_gs = jnp.sum(dms * sh, axis=0, keepdims=True)

        @pl.when(i == 0)
        def _():
            dga_ref[...] = p_ga
            dgs_ref[...] = p_gs

        @pl.when(i > 0)
        def _():
            dga_ref[...] += p_ga
            dgs_ref[...] += p_gs

    return pl.pallas_call(
        body, name=name, grid=(s // tm,),
        in_specs=[_tok(tm, aw + uw), _tok(tm, aw), _tok(tm, uw), _tok(tm, uw), _row(aw), _row(uw)],
        out_specs=(_tok(tm, aw), _tok(tm, uw), _tok(tm, uw), _row(aw), _row(uw)),
        out_shape=(jax.ShapeDtypeStruct((s, aw), BF16), jax.ShapeDtypeStruct((s, uw), BF16), jax.ShapeDtypeStruct((s, uw), F32),
                   jax.ShapeDtypeStruct((1, aw), F32), jax.ShapeDtypeStruct((1, uw), F32)),
        compiler_params=_cparams(("arbitrary",)),
    )(dm, attn, y, gl, ga, gs)


def _gelu_bwd(dzd, dz2, y, name):
    s, u = y.shape
    tm = min(512, s)

    def body(a_ref, b_ref, y_ref, o_ref):
        o_ref[...] = (a_ref[...] + b_ref[...]) * _gelu_grad(y_ref[...])

    return pl.pallas_call(
        body, name=name, grid=(s // tm,), in_specs=[_tok(tm, u), _tok(tm, u), _tok(tm, u)], out_specs=_tok(tm, u),
        out_shape=jax.ShapeDtypeStruct((s, u), F32), compiler_params=_cparams(("parallel",)),
    )(dzd, dz2, y)


def _attn_scores(qh, kb, sink, valid):
    s = lax.dot_general(qh, kb, (((1,), (1,)), ((), ())), preferred_element_type=F32) * (HEAD_DIM ** -0.5)
    s = jnp.where(valid, s, NEG)
    m = jnp.maximum(jnp.max(s, axis=-1, keepdims=True), sink)
    e = jnp.exp(s - m)
    esink = jnp.exp(sink - m)
    den = jnp.sum(e, axis=-1, keepdims=True) + esink
    return e / den, esink / den


def _attn_valid(i):
    qi = lax.broadcasted_iota(jnp.int32, (KV_RATIO * WINDOW, 2 * WINDOW), 0) % WINDOW
    kj = lax.broadcasted_iota(jnp.int32, (KV_RATIO * WINDOW, 2 * WINDOW), 1)
    return (kj > qi) & (kj <= qi + WINDOW) & ((kj >= WINDOW) | (i > 0))


def _stack_heads(ref, hk):
    return jnp.concatenate([ref[:, (hk * KV_RATIO + g) * HEAD_DIM:(hk * KV_RATIO + g + 1) * HEAD_DIM] for g in range(KV_RATIO)], axis=0)


def _stack_sinks(sink_ref, hk):
    return jnp.concatenate([jnp.full((WINDOW, 1), sink_ref[hk * KV_RATIO + g], F32) for g in range(KV_RATIO)], axis=0)


def _band(kvp, kvc, off):
    return jnp.concatenate([kvp[:, off:off + HEAD_DIM], kvc[:, off:off + HEAD_DIM]], axis=0)


def _attn_specs(aw, uw, kvw):
    qblk = uw // aw
    kvblk = (uw + aw) // (2 * kvw)
    assert uw % aw == 0 and (uw + aw) % (2 * kvw) == 0
    return [
        pl.BlockSpec(memory_space=pltpu.SMEM),
        pl.BlockSpec((WINDOW, aw), lambda i: (i, qblk)),
        pl.BlockSpec((WINDOW, 2 * kvw), lambda i: (i, kvblk)),
        pl.BlockSpec((WINDOW, 2 * kvw), lambda i: (jnp.maximum(i - 1, 0), kvblk)),
    ]


def _attn_fwd(proj, sinks, aw, uw, name):
    s = proj.shape[0]
    nq = aw // HEAD_DIM
    nkv = nq // KV_RATIO
    kvw = nkv * HEAD_DIM

    def body(sink_ref, q_ref, kvc_ref, kvp_ref, o_ref):
        valid = _attn_valid(pl.program_id(0))[:WINDOW]
        q = q_ref[...]
        kvc = kvc_ref[...]
        kvp = kvp_ref[...]
        for hk in range(nkv):
            kb = _band(kvp, kvc, hk * HEAD_DIM)
            vb = _band(kvp, kvc, kvw + hk * HEAD_DIM)
            for g in range(KV_RATIO):
                hq = hk * KV_RATIO + g
                p, _ = _attn_scores(q[:, hq * HEAD_DIM:(hq + 1) * HEAD_DIM], kb, sink_ref[hq], valid)
                o_ref[:, hq * HEAD_DIM:(hq + 1) * HEAD_DIM] = jnp.dot(p.astype(BF16), vb, preferred_element_type=F32).astype(BF16)

    return pl.pallas_call(
        body, name=name, grid=(s // WINDOW,), in_specs=_attn_specs(aw, uw, kvw),
        out_specs=pl.BlockSpec((WINDOW, aw), lambda i: (i, 0)), out_shape=jax.ShapeDtypeStruct((s, aw), BF16),
        compiler_params=_cparams(("parallel",)),
    )(sinks, proj, proj, proj)


def _attn_bwd(proj, sinks, attn, dattn, aw, uw, name):
    s = proj.shape[0]
    nq = aw // HEAD_DIM
    nkv = nq // KV_RATIO
    kvw = nkv * HEAD_DIM
    hd = HEAD_DIM

    def body(sink_ref, q_ref, kvc_ref, kvp_ref, o_ref, do_ref, dq_ref, dc_ref, dp_ref, ds_ref):
        i = pl.program_id(0)
        valid = _attn_valid(i)
        kvc = kvc_ref[...]
        kvp = kvp_ref[...]
        lane = lax.broadcasted_iota(jnp.int32, (1, nq), 1)
        dsink = jnp.zeros((1, nq), F32)
        for hk in range(nkv):
            kb = _band(kvp, kvc, hk * hd)
            vb = _band(kvp, kvc, kvw + hk * hd)
            qs = _stack_heads(q_ref, hk)
            dos = _stack_heads(do_ref, hk)
            p, psink = _attn_scores(qs, kb, _stack_sinks(sink_ref, hk), valid)
            delta = jnp.sum(dos.astype(F32) * _stack_heads(o_ref, hk), axis=-1, keepdims=True)
            dpv = lax.dot_general(dos, vb, (((1,), (1,)), ((), ())), preferred_element_type=F32)
            dsb = (p * (dpv - delta) * (hd ** -0.5)).astype(BF16)
            dqs = jnp.dot(dsb, kb, preferred_element_type=F32).astype(BF16)
            dkb = lax.dot_general(dsb, qs, (((0,), (0,)), ((), ())), preferred_element_type=F32)
            dvb = lax.dot_general(p.astype(BF16), dos, (((0,), (0,)), ((), ())), preferred_element_type=F32)
            sink_term = psink * delta
            for g in range(KV_RATIO):
                hq = hk * KV_RATIO + g
                dq_ref[:, hq * hd:(hq + 1) * hd] = dqs[g * WINDOW:(g + 1) * WINDOW]
                dsink = dsink + jnp.where(lane == hq, -jnp.sum(sink_term[g * WINDOW:(g + 1) * WINDOW]), 0.0)
            dp_ref[:, hk * hd:(hk + 1) * hd] = dkb[:WINDOW]
            dc_ref[:, hk * hd:(hk + 1) * hd] = dkb[WINDOW:]
            dp_ref[:, kvw + hk * hd:kvw + (hk + 1) * hd] = dvb[:WINDOW]
            dc_ref[:, kvw + hk * hd:kvw + (hk + 1) * hd] = dvb[WINDOW:]

        @pl.when(i == 0)
        def _():
            ds_ref[...] = dsink

        @pl.when(i > 0)
        def _():
            ds_ref[...] += dsink

    blk_a = pl.BlockSpec((WINDOW, aw), lambda i: (i, 0))
    blk_kv = pl.BlockSpec((WINDOW, 2 * kvw), lambda i: (i, 0))
    return pl.pallas_call(
        body, name=name, grid=(s // WINDOW,), in_specs=_attn_specs(aw, uw, kvw) + [blk_a, blk_a],
        out_specs=(blk_a, blk_kv, blk_kv, pl.BlockSpec((1, nq), lambda i: (0, 0))),
        out_shape=(jax.ShapeDtypeStruct((s, aw), BF16), jax.ShapeDtypeStruct((s, 2 * kvw), F32),
                   jax.ShapeDtypeStruct((s, 2 * kvw), F32), jax.ShapeDtypeStruct((1, nq), F32)),
        compiler_params=_cparams(("arbitrary",)),
    )(sinks, proj, proj, proj, attn, dattn)


def _assemble_dproj(du, dq, dkv_cur, dkv_prev, name):
    s, uw = du.shape
    aw = dq.shape[1]
    kv2 = dkv_cur.shape[1]
    nb = s // WINDOW

    def body(du_ref, dq_ref, dc_ref, dp_ref, o_ref):
        i = pl.program_id(0)
        o_ref[:, :uw] = du_ref[...].astype(BF16)
        o_ref[:, uw:uw + aw] = dq_ref[...]
        nxt = jnp.where(i < nb - 1, 1.0, 0.0)
        o_ref[:, uw + aw:] = (dc_ref[...] + nxt * dp_ref[...]).astype(BF16)

    return pl.pallas_call(
        body, name=name, grid=(nb,),
        in_specs=[_tok(WINDOW, uw), _tok(WINDOW, aw), _tok(WINDOW, kv2),
                  pl.BlockSpec((WINDOW, kv2), lambda i: (jnp.minimum(i + 1, nb - 1), 0))],
        out_specs=_tok(WINDOW, uw + aw + kv2), out_shape=jax.ShapeDtypeStruct((s, uw + aw + kv2), BF16),
        compiler_params=_cparams(("parallel",)),
    )(du, dq, dkv_cur, dkv_prev)


def _zoh(lr, li, ls, btr, bti):
    dt = jnp.exp(ls)
    mag = jnp.exp(lr * dt)
    ang = li * dt
    ar = mag * jnp.cos(ang)
    ai = mag * jnp.sin(ang)
    den = lr * lr + li * li
    fr = ((ar - 1.0) * lr + ai * li) / den
    fi = (ai * lr - (ar - 1.0) * li) / den
    return ar, ai, fr[None] * btr - fi[None] * bti, fr[None] * bti + fi[None] * btr


def _ssm_prep(lr, li, ls, btr, bti, name):
    def body(lr_ref, li_ref, ls_ref, btr_ref, bti_ref, ar_ref, ai_ref, bbr_ref, bbi_ref):
        ar, ai, bbr, bbi = _zoh(lr_ref[...], li_ref[...], ls_ref[...], btr_ref[...], bti_ref[...])
        ar_ref[...] = ar
        ai_ref[...] = ai
        bbr_ref[...] = bbr
        bbi_ref[...] = bbi

    s2 = jax.ShapeDtypeStruct(lr.shape, F32)
    s3 = jax.ShapeDtypeStruct(btr.shape, F32)
    return pl.pallas_call(body, name=name, out_shape=(s2, s2, s3, s3))(lr, li, ls, btr, bti)


def _ssm_prep_bwd(lr, li, ls, btr, bti, dar, dai, dbbr, dbbi, name):
    def body(lr_ref, li_ref, ls_ref, btr_ref, bti_ref, dar_ref, dai_ref, dbbr_ref, dbbi_ref, o1, o2, o3, o4, o5):
        _, vjp = jax.vjp(_zoh, lr_ref[...], li_ref[...], ls_ref[...], btr_ref[...], bti_ref[...])
        g = vjp((dar_ref[...], dai_ref[...], dbbr_ref[...], dbbi_ref[...]))
        for o, v in zip((o1, o2, o3, o4, o5), g):
            o[...] = v

    s2 = jax.ShapeDtypeStruct(lr.shape, F32)
    s3 = jax.ShapeDtypeStruct(btr.shape, F32)
    return pl.pallas_call(body, name=name, out_shape=(s2, s2, s2, s3, s3))(lr, li, ls, btr, bti, dar, dai, dbbr, dbbi)


def _state_tiles(ref):
    return [ref[:, cb * 128:(cb + 1) * 128] for cb in range(4)]


def _gather_rows(ref_re, ref_im, r, t):
    return jnp.concatenate([ref_re.at[cb][pl.ds(r, t, stride=8), :] for cb in range(4)]
                           + [ref_im.at[cb][pl.ds(r, t, stride=8), :] for cb in range(4)], axis=1)


def _scatter_rows(ref_re, ref_im, r, t, val):
    for cb in range(4):
        ref_re.at[cb][pl.ds(r, t, stride=8), :] = val[:, cb * 128:(cb + 1) * 128]
        ref_im.at[cb][pl.ds(r, t, stride=8), :] = val[:, PSTATES + cb * 128:PSTATES + (cb + 1) * 128]


def _ssm_fwd(proj, bp, cp, a_re, a_im, dvec, uw, name, t=128):
    s = proj.shape[0]
    npc = uw // PIECE
    assert npc == 8 and s % t == 0

    def body(u_ref, bp_ref, cp_ref, ar_ref, ai_ref, d_ref, y_ref, xr_ref, xi_ref, cr_ref, ci_ref):
        i = pl.program_id(0)

        @pl.when(i == 0)
        def _():
            cr_ref[...] = jnp.zeros_like(cr_ref)
            ci_ref[...] = jnp.zeros_like(ci_ref)

        for r in range(npc):
            bu = jnp.dot(u_ref[:, r * PIECE:(r + 1) * PIECE], bp_ref[r], preferred_element_type=F32)
            _scatter_rows(xr_ref, xi_ref, r, t, bu)
        ar = _state_tiles(ar_ref)
        ai = _state_tiles(ai_ref)

        def step(tt, carry):
            xr, xi = carry
            off = pl.multiple_of(tt * 8, 8)
            nr, ni = [], []
            for cb in range(4):
                vr = ar[cb] * xr[cb] - ai[cb] * xi[cb] + xr_ref[cb, pl.ds(off, 8), :]
                vi = ar[cb] * xi[cb] + ai[cb] * xr[cb] + xi_ref[cb, pl.ds(off, 8), :]
                xr_ref[cb, pl.ds(off, 8), :] = vr
                xi_ref[cb, pl.ds(off, 8), :] = vi
                nr.append(vr)
                ni.append(vi)
            return tuple(nr), tuple(ni)

        xr, xi = lax.fori_loop(0, t, step, (tuple(_state_tiles(cr_ref)), tuple(_state_tiles(ci_ref))), unroll=4)
        for cb in range(4):
            cr_ref[:, cb * 128:(cb + 1) * 128] = xr[cb]
            ci_ref[:, cb * 128:(cb + 1) * 128] = xi[cb]
        for r in range(npc):
            xs = _gather_rows(xr_ref, xi_ref, r, t).astype(BF16)
            y_ref[:, r * PIECE:(r + 1) * PIECE] = (
                jnp.dot(xs, cp_ref[r], preferred_element_type=F32)
                + d_ref[:, r * PIECE:(r + 1) * PIECE] * u_ref[:, r * PIECE:(r + 1) * PIECE].astype(F32))

    full3 = lambda shp: pl.BlockSpec(shp, lambda i: (0, 0, 0))
    full2 = lambda shp: pl.BlockSpec(shp, lambda i: (0, 0))
    xs_spec = pl.BlockSpec((4, t * 8, 128), lambda i: (0, i, 0))
    xs_shape = jax.ShapeDtypeStruct((4, s * 8, 128), F32)
    return pl.pallas_call(
        body, name=name, grid=(s // t,),
        in_specs=[pl.BlockSpec((t, uw), lambda i: (i, 0)), full3(bp.shape), full3(cp.shape), full2(a_re.shape), full2(a_im.shape),
                  full2(dvec.shape)],
        out_specs=(pl.BlockSpec((t, uw), lambda i: (i, 0)), xs_spec, xs_spec),
        out_shape=(jax.ShapeDtypeStruct((s, uw), F32), xs_shape, xs_shape),
        scratch_shapes=[pltpu.VMEM((8, PSTATES), F32), pltpu.VMEM((8, PSTATES), F32)],
        compiler_params=_cparams(("arbitrary",), 56),
    )(proj, bp, cp, a_re, a_im, dvec)


def _ssm_bwd(dy, proj, xs_re, xs_im, cpt, bpt, a_re, a_im, dvec, uw, name, t=128):
    s = proj.shape[0]
    npc = uw // PIECE
    nt = s // t
    assert npc == 8 and s % t == 0

    def body(dy_ref, u_ref, xr_ref, xi_ref, hr_ref, hi_ref, cpt_ref, bpt_ref, ar_ref, ai_ref, d_ref,
             du_ref, dbp_ref, dcp_ref, dar_ref, dai_ref, dd_ref, gr_ref, gi_ref, lr_ref, li_ref):
        i = pl.program_id(0)

        @pl.when(i == 0)
        def _():
            lr_ref[...] = jnp.zeros_like(lr_ref)
            li_ref[...] = jnp.zeros_like(li_ref)
            dbp_ref[...] = jnp.zeros_like(dbp_ref)
            dcp_ref[...] = jnp.zeros_like(dcp_ref)
            dar_ref[...] = jnp.zeros_like(dar_ref)
            dai_ref[...] = jnp.zeros_like(dai_ref)
            dd_ref[...] = jnp.zeros_like(dd_ref)

        dyb = dy_ref[...].astype(BF16)
        for r in range(npc):
            gx = jnp.dot(dyb[:, r * PIECE:(r + 1) * PIECE], cpt_ref[r], preferred_element_type=F32)
            _scatter_rows(gr_ref, gi_ref, r, t, gx)
        ar = _state_tiles(ar_ref)
        ai = _state_tiles(ai_ref)

        def adjoint(off, lam_r, lam_i, xpr, xpi, acc_r, acc_i):
            nr, ni, qr, qi = [], [], [], []
            for cb in range(4):
                vr = gr_ref[cb, pl.ds(off, 8), :] + ar[cb] * lam_r[cb] + ai[cb] * lam_i[cb]
                vi = gi_ref[cb, pl.ds(off, 8), :] + ar[cb] * lam_i[cb] - ai[cb] * lam_r[cb]
                gr_ref[cb, pl.ds(off, 8), :] = vr
                gi_ref[cb, pl.ds(off, 8), :] = vi
                nr.append(vr)
                ni.append(vi)
                qr.append(acc_r[cb] + vr * xpr[cb] + vi * xpi[cb])
                qi.append(acc_i[cb] + vi * xpr[cb] - vr * xpi[cb])
            return tuple(nr), tuple(ni), tuple(qr), tuple(qi)

        def step(j, carry):
            lam_r, lam_i, acc_r, acc_i = carry
            tt = t - 1 - j
            off = pl.multiple_of(tt * 8, 8)
            offp = pl.multiple_of(tt * 8 - 8, 8)
            xpr = [xr_ref[cb, pl.ds(offp, 8), :] for cb in range(4)]
            xpi = [xi_ref[cb, pl.ds(offp, 8), :] for cb in range(4)]
            return adjoint(off, lam_r, lam_i, xpr, xpi, acc_r, acc_i)

        zero4 = tuple(jnp.zeros((8, 128), F32) for _ in range(4))
        carry = lax.fori_loop(0, t - 1, step, (tuple(_state_tiles(lr_ref)), tuple(_state_tiles(li_ref)), zero4, zero4), unroll=4)
        has_prev = jnp.where(i < nt - 1, 1.0, 0.0)
        xpr = [hr_ref[cb] * has_prev for cb in range(4)]
        xpi = [hi_ref[cb] * has_prev for cb in range(4)]
        lam_r, lam_i, acc_r, acc_i = adjoint(0, carry[0], carry[1], xpr, xpi, carry[2], carry[3])
        for cb in range(4):
            lr_ref[:, cb * 128:(cb + 1) * 128] = lam_r[cb]
            li_ref[:, cb * 128:(cb + 1) * 128] = lam_i[cb]
            dar_ref[:, cb * 128:(cb + 1) * 128] += acc_r[cb]
            dai_ref[:, cb * 128:(cb + 1) * 128] += acc_i[cb]

        dyv = dy_ref[...]
        uv = u_ref[...]
        dd_ref[...] += jnp.sum(dyv * uv.astype(F32), axis=0, keepdims=True)
        for r in range(npc):
            lam = _gather_rows(gr_ref, gi_ref, r, t).astype(BF16)
            sl = slice(r * PIECE, (r + 1) * PIECE)
            du_ref[:, sl] = jnp.dot(lam, bpt_ref[r], preferred_element_type=F32) + d_ref[:, sl] * dyv[:, sl]
            dbp_ref[r] += lax.dot_general(uv[:, sl], lam, (((0,), (0,)), ((), ())), preferred_element_type=F32)
            xs = _gather_rows(xr_ref, xi_ref, r, t).astype(BF16)
            dcp_ref[r] += lax.dot_general(xs, dyb[:, sl], (((0,), (0,)), ((), ())), preferred_element_type=F32)

    rev = lambda i: (nt - 1 - i, 0)
    full3 = lambda shp: pl.BlockSpec(shp, lambda i: (0, 0, 0))
    full2 = lambda shp: pl.BlockSpec(shp, lambda i: (0, 0))
    xs_spec = pl.BlockSpec((4, t * 8, 128), lambda i: (0, nt - 1 - i, 0))
    halo_spec = pl.BlockSpec((4, 8, 128), lambda i: (0, jnp.maximum((nt - 1 - i) * t - 1, 0), 0))
    st = jax.ShapeDtypeStruct((8, PSTATES), F32)
    return pl.pallas_call(
        body, name=name, grid=(nt,),
        in_specs=[pl.BlockSpec((t, uw), rev), pl.BlockSpec((t, uw), rev), xs_spec, xs_spec, halo_spec, halo_spec,
                  full3(cpt.shape), full3(bpt.shape), full2(a_re.shape), full2(a_im.shape), full2(dvec.shape)],
        out_specs=(pl.BlockSpec((t, uw), rev), full3((npc, PIECE, 2 * PSTATES)), full3((npc, 2 * PSTATES, PIECE)),
                   full2((8, PSTATES)), full2((8, PSTATES)), full2((1, uw))),
        out_shape=(jax.ShapeDtypeStruct((s, uw), F32), jax.ShapeDtypeStruct((npc, PIECE, 2 * PSTATES), F32),
                   jax.ShapeDtypeStruct((npc, 2 * PSTATES, PIECE), F32), st, st, jax.ShapeDtypeStruct((1, uw), F32)),
        scratch_shapes=[pltpu.VMEM((4, t * 8, 128), F32), pltpu.VMEM((4, t * 8, 128), F32),
                        pltpu.VMEM((8, PSTATES), F32), pltpu.VMEM((8, PSTATES), F32)],
        compiler_params=_cparams(("arbitrary",), 56),
    )(dy, proj, xs_re, xs_im, xs_re, xs_im, cpt, bpt, a_re, a_im, dvec)


_RC = 16
_LC = 128


def _conv3_block(xv, halo, w, b, row):
    h6, h7 = halo[_RC - 2:_RC - 1], halo[_RC - 1:_RC]
    x1 = jnp.where(row == 0, h7, pltpu.roll(xv, 1, 0))
    x2 = jnp.where(row == 0, h6, jnp.where(row == 1, h7, pltpu.roll(xv, 2, 0)))
    return ((b + x2 * w[0:1]) + x1 * w[1:2]) + xv * w[2:3], (x2, x1, xv)


def _ffn_tiles(s, f):
    tm = min(256, s)
    tn = f // 4 if (f // 4) % _LC == 0 else f
    assert tm % _RC == 0 and tn % _LC == 0
    return tm, tn


def _conv_glu_fwd(up0, cw, cb, name):
    _, s, f = up0.shape
    tm, tn = _ffn_tiles(s, f)
    hb = tm // _RC

    def body(x_ref, h_ref, w_ref, b_ref, a_ref):
        first = jnp.where(pl.program_id(0) > 0, 1.0, 0.0)
        row = lax.broadcasted_iota(jnp.int32, (tm, tn), 0)
        ups = [_conv3_block(x_ref[p].astype(F32), h_ref[p].astype(F32) * first, w_ref[p], b_ref[p], row)[0] for p in range(2)]
        a_ref[...] = (_gelu(ups[1]) * ups[0]).astype(BF16)

    return pl.pallas_call(
        body, name=name, grid=(s // tm, f // tn),
        in_specs=[pl.BlockSpec((2, tm, tn), lambda i, j: (0, i, j)),
                  pl.BlockSpec((2, _RC, tn), lambda i, j: (0, jnp.maximum(i * hb - 1, 0), j)),
                  pl.BlockSpec((2, 3, tn), lambda i, j: (0, 0, j)), pl.BlockSpec((2, 1, tn), lambda i, j: (0, 0, j))],
        out_specs=pl.BlockSpec((tm, tn), lambda i, j: (i, j)), out_shape=jax.ShapeDtypeStruct((s, f), BF16),
        compiler_params=_cparams(("parallel", "parallel")),
    )(up0, up0, cw, cb)


def _ffn_bwd_gate(da, up0, cw, cb, name):
    _, s, f = up0.shape
    tm, tn = _ffn_tiles(s, f)
    hb = tm // _RC

    def body(da_ref, x_ref, h_ref, w_ref, b_ref, d_ref, dw_ref, db_ref):
        i = pl.program_id(1)
        first = jnp.where(i > 0, 1.0, 0.0)

        @pl.when(i == 0)
        def _():
            dw_ref[...] = jnp.zeros_like(dw_ref)
            db_ref[...] = jnp.zeros_like(db_ref)

        row = lax.broadcasted_iota(jnp.int32, (tm, tn), 0)
        ups, taps = [], []
        for p in range(2):
            up, tap = _conv3_block(x_ref[p].astype(F32), h_ref[p].astype(F32) * first, w_ref[p], b_ref[p], row)
            ups.append(up)
            taps.append(tap)
        dav = da_ref[...]
        gate, dgate = _gelu_and_grad(ups[1])
        douts = (dav * gate, dav * ups[0] * dgate)
        for p in range(2):
            d_ref[p] = douts[p].astype(BF16)
            db_ref[p] += jnp.sum(douts[p], axis=0, keepdims=True)
            for kk in range(3):
                dw_ref[p, kk:kk + 1, :] += jnp.sum(douts[p] * taps[p][kk], axis=0, keepdims=True)

    return pl.pallas_call(
        body, name=name, grid=(f // tn, s // tm),
        in_specs=[pl.BlockSpec((tm, tn), lambda j, i: (i, j)), pl.BlockSpec((2, tm, tn), lambda j, i: (0, i, j)),
                  pl.BlockSpec((2, _RC, tn), lambda j, i: (0, jnp.maximum(i * hb - 1, 0), j)),
                  pl.BlockSpec((2, 3, tn), lambda j, i: (0, 0, j)), pl.BlockSpec((2, 1, tn), lambda j, i: (0, 0, j))],
        out_specs=(pl.BlockSpec((2, tm, tn), lambda j, i: (0, i, j)), pl.BlockSpec((2, 3, tn), lambda j, i: (0, 0, j)),
                   pl.BlockSpec((2, 1, tn), lambda j, i: (0, 0, j))),
        out_shape=(jax.ShapeDtypeStruct((2, s, f), BF16), jax.ShapeDtypeStruct((2, 3, f), F32), jax.ShapeDtypeStruct((2, 1, f), F32)),
        compiler_params=_cparams(("parallel", "arbitrary")),
    )(da, up0, up0, cw, cb)


def _conv_bwd(dup, cw, name):
    _, s, f = dup.shape
    tm, tn = _ffn_tiles(s, f)
    hb = tm // _RC
    nb = s // tm

    def body(d_ref, h_ref, w_ref, o_ref):
        last = jnp.where(pl.program_id(0) < nb - 1, 1.0, 0.0)
        row = lax.broadcasted_iota(jnp.int32, (tm, tn), 0)
        for p in range(2):
            d = d_ref[p].astype(F32)
            h = h_ref[p].astype(F32) * last
            d1 = jnp.where(row == tm - 1, h[0:1], pltpu.roll(d, tm - 1, 0))
            d2 = jnp.where(row == tm - 1, h[1:2], jnp.where(row == tm - 2, h[0:1], pltpu.roll(d, tm - 2, 0)))
            w = w_ref[p]
            o_ref[p] = (d * w[2:3] + d1 * w[1:2] + d2 * w[0:1]).astype(BF16)

    return pl.pallas_call(
        body, name=name, grid=(nb, f // tn),
        in_specs=[pl.BlockSpec((2, tm, tn), lambda i, j: (0, i, j)),
                  pl.BlockSpec((2, _RC, tn), lambda i, j: (0, jnp.minimum((i + 1) * hb, s // _RC - 1), j)),
                  pl.BlockSpec((2, 3, tn), lambda i, j: (0, 0, j))],
        out_specs=pl.BlockSpec((2, tm, tn), lambda i, j: (0, i, j)), out_shape=jax.ShapeDtypeStruct((2, s, f), BF16),
        compiler_params=_cparams(("parallel", "parallel")),
    )(dup, dup, cw)


def _ada_part(c_all, w_ada, name):
    nb, d = c_all.shape
    depth, _, cols = w_ada.shape
    tn = 1024 if cols % 1024 == 0 else cols

    def body(c_ref, w_ref, o_ref, ca_ref):
        cv = c_ref[...]
        ca = cv * _sigmoid(cv)
        ca_ref[...] = ca
        o_ref[...] = jnp.dot(ca.astype(BF16), w_ref[...].astype(BF16), preferred_element_type=F32)

    return pl.pallas_call(
        body, name=name, grid=(depth, cols // tn),
        in_specs=[pl.BlockSpec((nb, d), lambda l, j: (0, 0)), pl.BlockSpec((None, d, tn), lambda l, j: (l, 0, j))],
        out_specs=(pl.BlockSpec((None, nb, tn), lambda l, j: (l, 0, j)), pl.BlockSpec((nb, d), lambda l, j: (0, 0))),
        out_shape=(jax.ShapeDtypeStruct((depth, nb, cols), F32), jax.ShapeDtypeStruct((nb, d), F32)),
        compiler_params=_cparams(("arbitrary", "arbitrary")),
    )(c_all, w_ada)


def _ada_select(gath, b_ada, name):
    depth, n6 = b_ada.shape
    cols = gath.shape[1]

    def body(g_ref, b_ref, o_ref):
        me = 4 * lax.axis_index("x") + 2 * lax.axis_index("y") + lax.axis_index("c")
        for l in range(depth):
            for j in range(n6 // cols):
                row = (2 * j) * (8 * depth) + l * 8 + me
                o_ref[l:l + 1, j * cols:(j + 1) * cols] = g_ref[pl.ds(row, 1), :] + b_ref[l:l + 1, j * cols:(j + 1) * cols]

    return pl.pallas_call(body, name=name, out_shape=jax.ShapeDtypeStruct((depth, n6), F32))(gath, b_ada)


def _wada_grad(ca_t, d_sel, name):
    d, nb = ca_t.shape
    depth, _, cols = d_sel.shape
    tm = min(256, d)

    def body(a_ref, g_ref, o_ref):
        acc = a_ref[:, 0:1] * g_ref[0:1, :]
        for b in range(1, nb):
            acc = acc + a_ref[:, b:b + 1] * g_ref[b:b + 1, :]
        o_ref[...] = acc

    return pl.pallas_call(
        body, name=name, grid=(depth, d // tm),
        in_specs=[pl.BlockSpec((tm, nb), lambda l, i: (i, 0)), pl.BlockSpec((None, nb, cols), lambda l, i: (l, 0, 0))],
        out_specs=pl.BlockSpec((None, tm, cols), lambda l, i: (l, i, 0)), out_shape=jax.ShapeDtypeStruct((depth, d, cols), F32),
        compiler_params=_cparams(("parallel", "parallel")),
    )(ca_t, d_sel)


def _block_rows(r, c):
    tr = r
    for cand in (2048, 1024, 512, 256, 128, 64, 32, 16, 8):
        if r % cand == 0 and cand * c * 4 <= 2 * MIB:
            tr = cand
            break
    else:
        for cand in (8, 16, 32):
            if r % cand == 0:
                tr = cand
                break
    return tr


def _adamw(w, g, m, v, name):
    nl, r, c = w.shape
    tr = _block_rows(r, c)
    c1 = 1.0 - ADAM_B1 ** ADAM_STEP
    c2 = 1.0 - ADAM_B2 ** ADAM_STEP

    def body(w_ref, g_ref, m_ref, v_ref, d_ref, nm_ref, nv_ref):
        gv = g_ref[...]
        nm = ADAM_B1 * m_ref[...] + (1.0 - ADAM_B1) * gv
        nv = ADAM_B2 * v_ref[...] + (1.0 - ADAM_B2) * (gv * gv)
        d_ref[...] = -ADAM_LR * ((nm / c1) / (jnp.sqrt(nv / c2) + ADAM_EPS) + ADAM_WD * w_ref[...])
        nm_ref[...] = nm
        nv_ref[...] = nv

    spec = pl.BlockSpec((None, tr, c), lambda l, i: (l, i, 0))
    shp = jax.ShapeDtypeStruct((nl, r, c), F32)
    return pl.pallas_call(
        body, name=name, grid=(nl, r // tr), in_specs=[spec] * 4, out_specs=(spec,) * 3, out_shape=(shp,) * 3,
        compiler_params=_cparams(("parallel", "parallel")),
    )(w, g, m, v)


def _sum_slots(x, nslots, name, out_dtype=F32):
    r = x.shape[0] // nslots
    c = x.shape[1]
    tr = _block_rows(r, c)
    nbk = r // tr

    def body(*refs):
        acc = refs[0][...].astype(F32)
        for k in range(1, nslots):
            acc = acc + refs[k][...].astype(F32)
        refs[nslots][...] = acc.astype(out_dtype)

    specs = [pl.BlockSpec((tr, c), functools.partial(lambda k, i: (k * nbk + i, 0), k)) for k in range(nslots)]
    return pl.pallas_call(
        body, name=name, grid=(nbk,), in_specs=specs, out_specs=pl.BlockSpec((tr, c), lambda i: (i, 0)),
        out_shape=jax.ShapeDtypeStruct((r, c), out_dtype), compiler_params=_cparams(("parallel",)),
    )(*([x] * nslots))


def _sum_slots_into(buf, x, layer, cidx, nslots, name):
    _, r2, c = buf.shape
    h = r2 // 2
    tr = _block_rows(h, c)
    nbk = h // tr

    def body(c_ref, b_ref, *refs):
        acc = refs[0][...].astype(F32)
        for k in range(1, nslots):
            acc = acc + refs[k][...].astype(F32)
        refs[nslots][...] = acc

    specs = [pl.BlockSpec((tr, c), functools.partial(lambda k, i, cr: (k * nbk + i, 0), k)) for k in range(nslots)]
    grid_spec = pltpu.PrefetchScalarGridSpec(
        num_scalar_prefetch=1, grid=(nbk,), in_specs=[_ANY] + specs,
        out_specs=pl.BlockSpec((None, tr, c), lambda i, cr: (layer, cr[0] * nbk + i, 0)))
    return pl.pallas_call(
        body, name=name, grid_spec=grid_spec, out_shape=jax.ShapeDtypeStruct(buf.shape, F32), input_output_aliases={1: 0},
        compiler_params=_cparams(("parallel",)),
    )(cidx, buf, *([x] * nslots))


def _share_halves(bufs, name):
    nw = len(bufs)

    def body(*refs):
        ins, outs = refs[:nw], refs[nw:2 * nw]
        send_sems, recv_sems = refs[2 * nw:]
        x, y, c = _mesh_pos()
        cps = []
        for w in range(nw):
            mine = _half_rows(bufs[w].shape[1], c, 8)
            cps.append(_remote(ins[w].at[:, mine, :], outs[w].at[:, mine, :], send_sems, recv_sems, w, (x, y, 1 - c)))
            cps[-1].start()
        for w in range(nw):
            other = outs[w].at[:, _half_rows(bufs[w].shape[1], 1 - c, 8), :]
            _remote(other, other, send_sems, recv_sems, w, (x, y, c)).wait_recv()
        for cp in cps:
            cp.wait_send()

    return pl.pallas_call(
        body, name=name, out_shape=[jax.ShapeDtypeStruct(t.shape, t.dtype) for t in bufs], in_specs=[_ANY] * nw, out_specs=[_ANY] * nw,
        input_output_aliases={w: w for w in range(nw)},
        scratch_shapes=[pltpu.SemaphoreType.DMA((nw,)), pltpu.SemaphoreType.DMA((nw,))],
    )(*bufs)


def _pick(dim, prefs):
    for p in prefs:
        if dim % p == 0:
            return p
    return dim


def _mm(a, b, m, n, k, name, out_dtype, **kw):
    tm = _pick(m, (1408, 1152, 1024, 512, 256, 128))
    tn = _pick(n, (1408, 1152, 1024, 512, 256, 128))
    kdiv = k // max(kw.get("a_stack", 0), kw.get("b_stack", 0) if kw.get("tb") else 0, 1)
    osize = jnp.dtype(out_dtype).itemsize
    tk = kdiv
    for cut in (1, 2, 4, 8, 16):
        tk = kdiv // cut
        vmem = 2 * 2 * tk * (tm + tn) + tm * tn * (2 * osize + 4 + (4 if tk < k else 0))
        if kdiv % cut == 0 and tk % 128 == 0 and vmem <= _MATMUL_VMEM_BUDGET:
            break
    return _matmul(a, b, m=m, n=n, k=k, tm=tm, tn=tn, tk=tk, out_dtype=out_dtype, name=name, **kw)


def _ssm_layout(p):
    g, st = p["lam_re"].shape
    npc = g * st // PSTATES
    lr = p["lam_re"].reshape(npc, PSTATES)
    li = p["lam_im"].reshape(npc, PSTATES)
    ls = jnp.broadcast_to(p["log_step"][:, None], (g, st)).reshape(npc, PSTATES)
    btr = jnp.transpose(p["ssm_b_re"], (2, 0, 1)).reshape(SSM_GROUP, npc, PSTATES)
    bti = jnp.transpose(p["ssm_b_im"], (2, 0, 1)).reshape(SSM_GROUP, npc, PSTATES)
    return lr, li, ls, btr, bti


def _ssm_pieces(bbr, bbi, c_re, c_im):
    npc = bbr.shape[1]
    gl = PSTATES // STATE
    eye = jnp.eye(gl, dtype=bool)

    def b_piece(bb):
        t = jnp.transpose(bb.reshape(SSM_GROUP, npc, gl, STATE), (1, 2, 0, 3))
        full = jnp.where(eye[None, :, None, :, None], t[:, :, :, None, :], 0.0)
        return full.reshape(npc, gl * SSM_GROUP, PSTATES)

    def c_piece(cc):
        t = jnp.transpose(cc.reshape(npc, gl, SSM_GROUP, STATE), (0, 1, 3, 2))
        full = jnp.where(eye[None, :, None, :, None], t[:, :, :, None, :], 0.0)
        return full.reshape(npc, PSTATES, gl * SSM_GROUP)

    bp = jnp.concatenate([b_piece(bbr), b_piece(bbi)], axis=2).astype(BF16)
    cp = jnp.concatenate([c_piece(c_re), c_piece(-c_im)], axis=1).astype(BF16)
    return bp, cp, jnp.swapaxes(bp, 1, 2), jnp.swapaxes(cp, 1, 2)


def _ssm_unpieces(dbp, dcp):
    npc = dbp.shape[0]
    gl = PSTATES // STATE
    idx = jnp.arange(gl)

    def b_diag(x):
        d = x.reshape(npc, gl, SSM_GROUP, gl, STATE)[:, idx, :, idx, :]
        return jnp.transpose(d, (2, 1, 0, 3)).reshape(SSM_GROUP, npc, PSTATES)

    def c_diag(x):
        d = x.reshape(npc, gl, STATE, gl, SSM_GROUP)[:, idx, :, idx, :]
        return jnp.transpose(d, (1, 0, 3, 2)).reshape(npc * gl, SSM_GROUP, STATE)

    return b_diag(dbp[:, :, :PSTATES]), b_diag(dbp[:, :, PSTATES:]), c_diag(dcp[:, :PSTATES, :]), -c_diag(dcp[:, PSTATES:, :])


class _LayerWeights:
    def __init__(self, fetch):
        self._fetch = fetch
        self._got = {}

    def group(self, g, after=None):
        if g not in self._got:
            self._got[g] = self._fetch(g, after)
        return self._got[g]


def _layer_fwd(l, x, ada6, weights, p):
    s, d = x.shape
    sh_m, sc_m, gt_m, sh_f, sc_f, gt_f = ada6
    w = dict(weights.group("mix", x))
    uw = w["w_glu"].shape[0]
    aw = w["w_out"].shape[0] - uw
    ncol = w["w_in"].shape[0]
    row = lambda v: v.reshape(1, -1)
    n = lambda t: f"l{l}_{t}"

    h = _pre_fwd(x, row(p["g_pre_mix"]), sc_m, sh_m, n("pre_mix"))
    proj = _mm(h, w["w_in"], s, ncol, d, n("proj"), BF16, tb=True)
    attn = _attn_fwd(proj, p["attn_sinks"], aw, uw, n("attn_fwd"))
    zin = _ssm_layout(p)
    a_re, a_im, bbr, bbi = _ssm_prep(*zin, n("ssm_prep"))
    bp, cp, bpt, cpt = _ssm_pieces(bbr, bbi, p["ssm_c_re"], p["ssm_c_im"])
    dvec = p["ssm_d"].reshape(1, uw)
    y, xs_re, xs_im = _ssm_fwd(proj, bp, cp, a_re, a_im, dvec, uw, n("ssm_fwd"), t=256 if s % 256 == 0 else 128)
    z = _gelu_fwd(y, n("gelu_fwd"))
    gl = _mm(z, w["w_glu"], s, uw, uw, n("glu"), BF16)
    merged = _merge_fwd(attn, y, gl, row(p["g_attn_out"]), row(p["g_ssm_out"]), n("merge_fwd"))
    mix = _mm(merged, w["w_out"], s, d, aw + uw, n("out_proj"), BF16)
    x1 = _post_fwd(x, mix, row(p["g_post_mix"]), gt_m, n("post_mix"))

    w.update(weights.group("ffn", x1))
    f = w["w_down"].shape[0]
    h2 = _pre_fwd(x1, row(p["g_pre_ffn"]), sc_f, sh_f, n("pre_ffn"))
    up0 = _mm(h2, w["w_up"], s, 2 * f, d, n("up_proj"), BF16, b_stack=w["w_up"].shape[0], o_stack=2)
    cw2 = jnp.transpose(p["conv_w"].reshape(3, 2, f), (1, 0, 2))
    cb2 = p["conv_b"].reshape(2, 1, f)
    act = _conv_glu_fwd(up0, cw2, cb2, n("conv_glu"))
    ff = _mm(act, w["w_down"], s, d, f, n("down_proj"), BF16)
    x2 = _post_fwd(x1, ff, row(p["g_post_ffn"]), gt_f, n("post_ffn"))
    saved = dict(x=x, h=h, proj=proj, attn=attn, zin=zin, a_re=a_re, a_im=a_im, bpt=bpt, cpt=cpt, dvec=dvec, y=y, xs_re=xs_re,
                 xs_im=xs_im, z=z, gl=gl, merged=merged, mix=mix, x1=x1, h2=h2, up0=up0, cw2=cw2, cb2=cb2, act=act, ff=ff)
    return x2, saved


def _layer_bwd(l, dx2, ada6, weights, p, sv, on_grads):
    s, d = dx2.shape
    sh_m, sc_m, gt_m, sh_f, sc_f, gt_f = ada6
    w = {**weights.group("mix"), **weights.group("ffn")}
    uw = w["w_glu"].shape[0]
    aw = w["w_out"].shape[0] - uw
    ncol = w["w_in"].shape[0]
    f = w["w_down"].shape[0]
    nst = w["w_up"].shape[0]
    row = lambda v: v.reshape(1, -1)
    n = lambda t: f"l{l}_{t}"
    gw, gs = {}, {}

    dff, dgt_f, gs["g_post_ffn"] = _post_bwd(dx2, sv["ff"], row(p["g_post_ffn"]), gt_f, n("post_ffn_bwd"))
    gw["w_down"] = _mm(sv["act"], dff, f, d, s, n("down_dw"), BF16, ta=True)
    dact = _mm(dff, w["w_down"], s, f, d, n("down_dx"), F32, tb=True)
    dup, dcw2, dcb2 = _ffn_bwd_gate(dact, sv["up0"], sv["cw2"], sv["cb2"], n("ffn_gate_bwd"))
    gs["conv_w"] = jnp.transpose(dcw2, (1, 0, 2)).reshape(3, 2 * f)
    gs["conv_b"] = dcb2.reshape(2 * f)
    dup0 = _conv_bwd(dup, sv["cw2"], n("conv_bwd"))
    gw["w_up"] = _mm(sv["h2"], dup0, d, 2 * f, s, n("up_dw"), BF16, ta=True, b_stack=2, o_stack=nst)
    dh2 = _mm(dup0, w["w_up"], s, d, 2 * f, n("up_dx"), BF16, tb=True, a_stack=2, b_stack=nst)
    dx1, dsh_f, dsc_f, gs["g_pre_ffn"] = _pre_bwd(dx2, dh2, sv["x1"], row(p["g_pre_ffn"]), sc_f, n("pre_ffn_bwd"))
    token = on_grads(l, "ffn", {k: gw.pop(k) for k in ("w_up", "w_down")})
    if token is not None:
        gt_m = gt_m + token[0:1, 0:1]

    dmix, dgt_m, gs["g_post_mix"] = _post_bwd(dx1, sv["mix"], row(p["g_post_mix"]), gt_m, n("post_mix_bwd"))
    gw["w_out"] = _mm(sv["merged"], dmix, aw + uw, d, s, n("out_dw"), BF16, ta=True)
    dmerged = _mm(dmix, w["w_out"], s, aw + uw, d, n("out_dx"), BF16, tb=True)
    dattn, dgl, dzd, gs["g_attn_out"], gs["g_ssm_out"] = _merge_bwd(
        dmerged, sv["attn"], sv["y"], sv["gl"], row(p["g_attn_out"]), row(p["g_ssm_out"]), n("merge_bwd"))
    gw["w_glu"] = _mm(sv["z"], dgl, uw, uw, s, n("glu_dw"), BF16, ta=True)
    dz2 = _mm(dgl, w["w_glu"], s, uw, uw, n("glu_dx"), F32, tb=True)
    dy = _gelu_bwd(dzd, dz2, sv["y"], n("gelu_bwd"))
    du, dbp, dcp, dar, dai, dd = _ssm_bwd(dy, sv["proj"], sv["xs_re"], sv["xs_im"], sv["cpt"], sv["bpt"], sv["a_re"], sv["a_im"],
                                          sv["dvec"], uw, n("ssm_bwd"))
    dq, dkv_c, dkv_p, dsinks = _attn_bwd(sv["proj"], p["attn_sinks"], sv["attn"], dattn, aw, uw, n("attn_bwd"))
    dproj = _assemble_dproj(du, dq, dkv_c, dkv_p, n("dproj"))
    gw["w_in"] = _mm(dproj, sv["h"], ncol, d, s, n("in_dw"), BF16, ta=True)
    dh = _mm(dproj, w["w_in"], s, d, ncol, n("in_dx"), BF16)
    dx0, dsh_m, dsc_m, gs["g_pre_mix"] = _pre_bwd(dx1, dh, sv["x"], row(p["g_pre_mix"]), sc_m, n("pre_mix_bwd"))
    token = on_grads(l, "mix", {k: gw.pop(k) for k in ("w_in", "w_glu", "w_out")})

    dbbr, dbbi, dc_re, dc_im = _ssm_unpieces(dbp, dcp)
    dlr, dli, dls, dbtr, dbti = _ssm_prep_bwd(*sv["zin"], dar, dai, dbbr, dbbi, n("ssm_prep_bwd"))
    g, st = p["lam_re"].shape
    gs["lam_re"] = dlr.reshape(g, st)
    gs["lam_im"] = dli.reshape(g, st)
    gs["log_step"] = jnp.sum(dls.reshape(g, st), axis=1)
    gs["ssm_b_re"] = jnp.transpose(dbtr.reshape(SSM_GROUP, g, st), (1, 2, 0))
    gs["ssm_b_im"] = jnp.transpose(dbti.reshape(SSM_GROUP, g, st), (1, 2, 0))
    gs["ssm_c_re"] = dc_re
    gs["ssm_c_im"] = dc_im
    gs["ssm_d"] = dd.reshape(p["ssm_d"].shape)
    gs["attn_sinks"] = dsinks.reshape(-1)
    gs["b_ada"] = jnp.concatenate([dsh_m, dsc_m, dgt_m, dsh_f, dsc_f, dgt_f], axis=1).reshape(-1)
    for key in ("g_post_ffn", "g_pre_ffn", "g_post_mix", "g_attn_out", "g_ssm_out", "g_pre_mix"):
        gs[key] = gs[key].reshape(-1)
    return dx0, gs, token


def _local_step(x, tgt, ada, wl, pl_small, on_grads):
    d = x.shape[1]
    depth = len(wl)
    ada6 = [[ada[l:l + 1, k * d:(k + 1) * d] for k in range(6)] for l in range(depth)]
    saved = []
    h = x
    for l in range(depth):
        h, sv = _layer_fwd(l, h, ada6[l], wl[l], pl_small[l])
        saved.append(sv)
    dy, lsum = _loss_head(h, tgt, "loss_head")
    loss = 0.5 * lsum[0, 0] / d
    gss = [None] * depth
    dx = dy
    for l in reversed(range(depth)):
        dx, gss[l], token = _layer_bwd(l, dx, ada6[l], wl[l], pl_small[l], saved[l], on_grads)
        if token is not None and l > 0:
            ada6[l - 1][5] = ada6[l - 1][5] + token[0:1, 0:1]
    return loss, dx, gss


_ANY = pl.BlockSpec(memory_space=pl.ANY)


def _mesh_pos():
    return lax.axis_index("x"), lax.axis_index("y"), lax.axis_index("c")


def _other_chips(x, y):
    return [(1 - x, y), (x, 1 - y), (1 - x, 1 - y)]


def _remote(src, dst, send_sems, recv_sems, k, to):
    return pltpu.make_async_remote_copy(src_ref=src, dst_ref=dst, send_sem=send_sems.at[k], recv_sem=recv_sems.at[k],
                                        device_id=to, device_id_type=MESH)


def _allgather8(xs, name):
    m, n = xs.shape

    def body(x_ref, out_ref, send_sems, recv_sems):
        x, y, c = _mesh_pos()
        me, sibling = (x, y, c), (x, y, 1 - c)
        chips = _other_chips(x, y)

        def rows(px, py, pc):
            return out_ref.at[pl.ds((4 * px + 2 * py + pc) * m, m), :]

        def copy(k, block, to, src=None):
            return _remote(rows(*block) if src is None else src, rows(*block), send_sems, recv_sems, k, to)

        first = [copy(0, me, sibling, src=x_ref)]
        first += [copy(1 + j, me, (*chip, c), src=x_ref) for j, chip in enumerate(chips)]
        for cp in first:
            cp.start()
        passed = [copy(4 + j, (*chip, c), sibling) for j, chip in enumerate(chips)]
        for j, chip in enumerate(chips):
            copy(1 + j, (*chip, c), me).wait_recv()
            passed[j].start()
        copy(0, sibling, me).wait_recv()
        for j, chip in enumerate(chips):
            copy(4 + j, (*chip, 1 - c), me).wait_recv()
        for cp in first + passed:
            cp.wait_send()

    out = pl.pallas_call(
        body, name=name, out_shape=jax.ShapeDtypeStruct((8 * m, n), xs.dtype), in_specs=[_ANY], out_specs=_ANY,
        scratch_shapes=[pltpu.SemaphoreType.DMA((7,)), pltpu.SemaphoreType.DMA((7,))],
    )(xs)
    x, y, c = _mesh_pos()
    return lax.dynamic_update_slice(out, xs, ((4 * x + 2 * y + c) * m, 0))


def _half_rows(ref_rows, half, align):
    h = ref_rows // 2
    return pl.ds(pl.multiple_of(half * h, align), h)


_HBM = pl.BlockSpec(memory_space=pltpu.HBM)
_SEMS = pl.BlockSpec(memory_space=pltpu.SEMAPHORE)
_EFFECT = pltpu.SideEffectType.DATAFLOW_SIDE_EFFECTING
_TOKEN = jax.ShapeDtypeStruct((8, 128), F32)


def _in_hbm(arrays):
    return [pltpu.with_memory_space_constraint(a, pltpu.HBM) for a in arrays]


def _chip_copy(kind, srcs, lands, w, q, chip, mine, c, send, recv, k):
    if kind == "gather":
        rows = _half_rows(srcs[w].shape[0], c, 16)
        return _remote(srcs[w].at[rows, :], lands[w].at[mine, rows, :], send, recv, k, (*chip, c))
    return _remote(srcs[w].at[2 * chip[0] + chip[1]], lands[w].at[mine], send, recv, k, (*chip, c))


def _chip_landing(kind, srcs, lands, w, chip, c):
    if kind == "gather":
        return lands[w].at[2 * chip[0] + chip[1], _half_rows(srcs[w].shape[0], c, 16), :]
    return lands[w].at[2 * chip[0] + chip[1]]


def _ici_start(kind, srcs, groups, name, after=None):
    nw, ng = len(srcs), len(groups)
    lands = [lax.empty((4,) + s.shape if kind == "gather" else s.shape, s.dtype) for s in srcs]
    extra = [] if after is None else [after]

    def body(*refs):
        ins, lnd = refs[:nw], refs[nw:2 * nw]
        sems = refs[2 * nw + len(extra):2 * nw + len(extra) + 2 * ng]
        token = refs[-1]
        x, y, c = _mesh_pos()
        for g, members in enumerate(groups):
            for j, w in enumerate(members):
                for q, chip in enumerate(_other_chips(x, y)):
                    _chip_copy(kind, ins, lnd, w, q, chip, 2 * x + y, c, sems[2 * g], sems[2 * g + 1], 3 * j + q).start()
        token[...] = jnp.zeros_like(token)

    sem_shapes = [pltpu.SemaphoreType.DMA((3 * len(members),)) for members in groups for _ in range(2)]
    out = pl.pallas_call(
        body, name=name,
        out_shape=(*sem_shapes, *[pltpu.HBM(s.shape, s.dtype) for s in srcs], *[pltpu.HBM(t.shape, t.dtype) for t in lands], _TOKEN),
        in_specs=[_HBM] * (2 * nw) + [_ANY] * len(extra),
        out_specs=(*([_SEMS] * (2 * ng)), *([_HBM] * (2 * nw)), pl.BlockSpec(memory_space=pltpu.VMEM)),
        input_output_aliases={i: 2 * ng + i for i in range(2 * nw)},
        compiler_params=pltpu.CompilerParams(has_side_effects=_EFFECT),
    )(*_in_hbm(srcs), *_in_hbm(lands), *extra)
    sems = [(out[2 * g], out[2 * g + 1]) for g in range(ng)]
    return sems, list(out[2 * ng:2 * ng + nw]), list(out[2 * ng + nw:2 * ng + 2 * nw]), out[-1]


def _ici_wait(kind, sems, srcs, lands, after, name):
    nm = len(srcs)

    def body(*refs):
        ins, lnd = refs[:nm], refs[nm:2 * nm]
        send, recv = refs[2 * nm], refs[2 * nm + 1]
        x, y, c = _mesh_pos()
        for j in range(nm):
            for q, chip in enumerate(_other_chips(x, y)):
                _chip_copy(kind, ins, lnd, j, q, chip, 2 * x + y, c, send, recv, 3 * j + q).wait_send()
                landed = _chip_landing(kind, ins, lnd, j, chip, c)
                _remote(landed, landed, send, recv, 3 * j + q, (x, y, c)).wait_recv()

    out = pl.pallas_call(
        body, name=name, out_shape=(*[pltpu.HBM(s.shape, s.dtype) for s in srcs], *[pltpu.HBM(t.shape, t.dtype) for t in lands]),
        in_specs=[_HBM] * (2 * nm) + [_SEMS, _SEMS, _ANY], out_specs=tuple([_HBM] * (2 * nm)),
        input_output_aliases={i: i for i in range(2 * nm)},
        compiler_params=pltpu.CompilerParams(has_side_effects=_EFFECT),
    )(*srcs, *lands, sems[0], sems[1], after)
    return list(out[:nm]), list(out[nm:])


def _gather_finish(shards, lands, name):
    nw = len(shards)

    def body(*refs):
        ins, lnd, outs = refs[:nw], refs[nw:2 * nw], refs[2 * nw:3 * nw]
        send_sems, recv_sems = refs[3 * nw:]
        x, y, c = _mesh_pos()
        mine = 2 * x + y
        sibling = (x, y, 1 - c)
        chips = _other_chips(x, y)

        def blk(ref, chip_idx, half):
            return ref.at[chip_idx, _half_rows(ref.shape[1], half, 16), :]

        sends = []
        for w in range(nw):
            for q, chip in enumerate(chips):
                k = 2 * chip[0] + chip[1]
                sends.append(_remote(blk(lnd[w], k, c), blk(outs[w], k, c), send_sems, recv_sems, 4 * w + q, sibling))
            sends.append(_remote(ins[w], outs[w].at[mine], send_sems, recv_sems, 4 * w + 3, sibling))
        for cp in sends:
            cp.start()
        for w in range(nw):
            for q, chip in enumerate(chips):
                other = blk(outs[w], 2 * chip[0] + chip[1], 1 - c)
                _remote(other, other, send_sems, recv_sems, 4 * w + q, (x, y, c)).wait_recv()
            own = outs[w].at[mine]
            _remote(own, own, send_sems, recv_sems, 4 * w + 3, (x, y, c)).wait_recv()
        for cp in sends:
            cp.wait_send()

    return pl.pallas_call(
        body, name=name, out_shape=[jax.ShapeDtypeStruct(t.shape, t.dtype) for t in lands],
        in_specs=[_ANY] * (2 * nw), out_specs=[_ANY] * nw, input_output_aliases={nw + w: w for w in range(nw)},
        scratch_shapes=[pltpu.SemaphoreType.DMA((4 * nw,)), pltpu.SemaphoreType.DMA((4 * nw,))],
    )(*shards, *lands)


def _exchange_halves(gs, name):
    nw = len(gs)

    def body(*refs):
        ins, outs = refs[:nw], refs[nw:2 * nw]
        send_sems, recv_sems = refs[2 * nw:]
        x, y, c = _mesh_pos()
        cps = []
        for w in range(nw):
            src = ins[w].at[:, _half_rows(gs[w].shape[1], 1 - c, 16), :]
            cps.append(_remote(src, outs[w], send_sems, recv_sems, w, (x, y, 1 - c)))
            cps[-1].start()
        for cp in cps:
            cp.wait_recv()
        for cp in cps:
            cp.wait_send()

    return pl.pallas_call(
        body, name=name, out_shape=[jax.ShapeDtypeStruct((4, g.shape[1] // 2, g.shape[2]), g.dtype) for g in gs],
        in_specs=[_ANY] * nw, out_specs=[_ANY] * nw,
        scratch_shapes=[pltpu.SemaphoreType.DMA((nw,)), pltpu.SemaphoreType.DMA((nw,))],
    )(*gs)


def _add_half(g, recv, cidx, name):
    _, r, c = g.shape
    h = r // 2
    tr = _block_rows(h, c // 2)
    nbh = h // tr
    assert tr % 16 == 0

    def body(c_ref, g_ref, r_ref, o_ref):
        o_ref[...] = (g_ref[...].astype(F32) + r_ref[...].astype(F32)).astype(BF16)

    grid_spec = pltpu.PrefetchScalarGridSpec(
        num_scalar_prefetch=1, grid=(4, nbh),
        in_specs=[pl.BlockSpec((None, tr, c), lambda s, i, cr: (s, cr[0] * nbh + i, 0)),
                  pl.BlockSpec((None, tr, c), lambda s, i, cr: (s, i, 0))],
        out_specs=pl.BlockSpec((None, tr, c), lambda s, i, cr: (s, i, 0)))
    return pl.pallas_call(
        body, name=name, grid_spec=grid_spec, out_shape=jax.ShapeDtypeStruct((4, h, c), BF16),
        compiler_params=_cparams(("parallel", "parallel")),
    )(cidx, g, recv)


_BIG = ("w_in", "w_glu", "w_out", "w_up", "w_down")
_GROUPS = {"mix": ("w_in", "w_glu", "w_out"), "ffn": ("w_up", "w_down")}
_SMALL = ("b_ada", "g_pre_mix", "g_post_mix", "attn_sinks", "lam_re", "lam_im", "log_step", "ssm_b_re", "ssm_b_im", "ssm_c_re",
          "ssm_c_im", "ssm_d", "g_attn_out", "g_ssm_out", "g_pre_ffn", "g_post_ffn", "conv_b")
_WEIGHTS = ("w_ada", "b_ada", "g_pre_mix", "g_post_mix", "w_in", "attn_sinks", "lam_re", "lam_im", "log_step", "ssm_b_re", "ssm_b_im",
            "ssm_c_re", "ssm_c_im", "ssm_d", "w_glu", "g_attn_out", "g_ssm_out", "w_out", "g_pre_ffn", "g_post_ffn", "w_up", "conv_w",
            "conv_b", "w_down")
_LANES = 1024


def _pack(parts, rows_to):
    flat = jnp.concatenate([p.reshape(-1) for p in parts])
    per = _LANES * rows_to
    total = -(-flat.shape[0] // per) * per
    return jnp.pad(flat, (0, total - flat.shape[0])).reshape(total // _LANES, _LANES)


def _unpack(packed, shapes):
    flat = packed.reshape(-1)
    out, off = [], 0
    for shp in shapes:
        size = math.prod(shp)
        out.append(flat[off:off + size].reshape(shp))
        off += size
    return out


def kernel(x, c, w_ada, b_ada, g_pre_mix, g_post_mix, w_in, attn_sinks, lam_re, lam_im, log_step, ssm_b_re, ssm_b_im, ssm_c_re, ssm_c_im, ssm_d, w_glu, g_attn_out, g_ssm_out, w_out, g_pre_ffn, g_post_ffn, w_up, conv_w, conv_b, w_down, loss_target, m_w_ada, m_b_ada, m_g_pre_mix, m_g_post_mix, m_w_in, m_attn_sinks, m_lam_re, m_lam_im, m_log_step, m_ssm_b_re, m_ssm_b_im, m_ssm_c_re, m_ssm_c_im, m_ssm_d, m_w_glu, m_g_attn_out, m_g_ssm_out, m_w_out, m_g_pre_ffn, m_g_post_ffn, m_w_up, m_conv_w, m_conv_b, m_w_down, v_w_ada, v_b_ada, v_g_pre_mix, v_g_post_mix, v_w_in, v_attn_sinks, v_lam_re, v_lam_im, v_log_step, v_ssm_b_re, v_ssm_b_im, v_ssm_c_re, v_ssm_c_im, v_ssm_d, v_w_glu, v_g_attn_out, v_g_ssm_out, v_w_out, v_g_pre_ffn, v_g_post_ffn, v_w_up, v_conv_w, v_conv_b, v_w_down):
    given = dict(locals())
    wts = {n: given[n] for n in _WEIGHTS}
    mom = {n: given["m_" + n] for n in _WEIGHTS}
    var = {n: given["v_" + n] for n in _WEIGHTS}
    depth, d, ada_cols = w_ada.shape
    nchips = 4
    xi, yi, ci = lax.axis_index("x"), lax.axis_index("y"), lax.axis_index("c")
    chip = 2 * xi + yi
    cidx = jnp.reshape(ci, (1,)).astype(jnp.int32)

    cw_cols = conv_w.shape[2]
    vec = _pack([c, conv_w], 8)
    g1 = _allgather8(vec, "ag_cond").reshape(8, -1)
    c_all = g1[:, :d]
    cw_sh = g1[0::2, d:d + depth * 3 * cw_cols].reshape(nchips, depth, 3, cw_cols)
    conv_w_full = jnp.transpose(cw_sh, (1, 2, 0, 3)).reshape(depth, 3, nchips * cw_cols)

    ada_part, c_act = _ada_part(c_all, w_ada, "ada_part")
    g2 = _allgather8(ada_part.reshape(depth * 8, ada_cols), "ag_ada")
    ada = _ada_select(g2, b_ada, "ada_select")

    order = [(l, g) for l in range(depth) for g in _GROUPS]
    for table in (wts, mom, var):
        table["w_in"] = jnp.swapaxes(table["w_in"], 1, 2)
    members = {key: [wts[n][key[0]].astype(BF16) for n in _GROUPS[key[1]]] for key in order}
    flat = [s for key in order for s in members[key]]
    index, at = {}, 0
    for key in order:
        index[key] = list(range(at, at + len(members[key])))
        at += len(members[key])
    ag_sems, ag_srcs, ag_lands, ag_token = _ici_start("gather", flat, [index[key] for key in order], "ag_start", after=ada)
    ada = ada + ag_token[0:1, 0:1]

    def fetch(l, g, after):
        pos, ids = order.index((l, g)), index[(l, g)]
        srcs, lands = [ag_srcs[i] for i in ids], [ag_lands[i] for i in ids]
        srcs, lands = _ici_wait("gather", ag_sems[pos], srcs, lands, ag_token if after is None else after, f"ag_wait_l{l}_{g}")
        got = dict(zip(_GROUPS[g], _gather_finish(srcs, lands, f"ag_finish_l{l}_{g}")))
        if g == "ffn":
            return dict(w_up=got["w_up"], w_down=got["w_down"].reshape(-1, got["w_down"].shape[2]))
        w_in_t = got["w_in"].reshape(-1, d)
        split = w_in_t.shape[0] - nchips * got["w_glu"].shape[1]
        return dict(w_in=jnp.concatenate([w_in_t[split:], w_in_t[:split]], axis=0),
                    w_glu=got["w_glu"].reshape(-1, got["w_glu"].shape[2]), w_out=got["w_out"].reshape(-1, got["w_out"].shape[2]))

    wl = [_LayerWeights(functools.partial(fetch, l)) for l in range(depth)]
    wl[0].group("mix")
    ps = []
    for l in range(depth):
        small = {n: wts[n][l] for n in _SMALL if n != "b_ada"}
        small["conv_w"] = conv_w_full[l]
        ps.append(small)

    in_flight = {}

    def on_grads(l, g, gw):
        stacks = []
        for n in _GROUPS[g]:
            t = gw[n]
            if n == "w_in":
                uw = nchips * wts["w_glu"].shape[1]
                t = jnp.concatenate([t[uw:], t[:uw]], axis=0).reshape(nchips, -1, d)
            elif n != "w_up":
                t = t.reshape(nchips, t.shape[0] // nchips, t.shape[1])
            stacks.append(t)
        from_sibling = _exchange_halves(stacks, f"rs_sibling_l{l}_{g}")
        partials = [_add_half(s, r, cidx, f"rs_add_l{l}_{n}") for s, r, n in zip(stacks, from_sibling, _GROUPS[g])]
        sems, srcs, lands, token = _ici_start("scatter", partials, [list(range(len(partials)))], f"rs_start_l{l}_{g}")
        in_flight[(l, g)] = (sems[0], srcs, lands)
        return token

    loss_sum, grad_x, gss = _local_step(x[0], loss_target[0], ada, wl, ps, on_grads)
    loss = lax.psum(loss_sum, ("x", "y", "c"))

    reduced = {n: lax.empty(wts[n].shape, F32) for n in _BIG}
    for key in reversed(order):
        l, g = key
        sems, srcs, lands = in_flight[key]
        sent, landed = _ici_wait("scatter", sems, srcs, lands, grad_x, f"rs_wait_l{l}_{g}")
        for n, t, p in zip(_GROUPS[g], landed, sent):
            t = lax.dynamic_update_slice(t, lax.dynamic_slice_in_dim(p, chip, 1, axis=0), (chip, 0, 0))
            reduced[n] = _sum_slots_into(reduced[n], t.reshape(-1, t.shape[2]), l, cidx, nchips, f"rs_sum_l{l}_{n}")
    big_grads = dict(zip(_BIG, _share_halves([reduced[n] for n in _BIG], "rs_share")))

    small_parts = [jnp.stack([gss[l][n] for l in range(depth)]) for n in _SMALL]
    pack_small = _pack(small_parts, 8)
    pack_cw = _pack([jnp.stack([gss[l]["conv_w"] for l in range(depth)])], 8)
    rows_small = pack_small.shape[0]
    mine = jnp.concatenate([pack_small, pack_cw], axis=0).astype(BF16)
    g3 = _allgather8(mine, "ag_small")
    total = _sum_slots(g3, 8, "sum_small")
    grads = dict(big_grads)
    for n, v in zip(_SMALL, _unpack(total[:rows_small], [wts[n].shape for n in _SMALL])):
        grads[n] = v
    conv_w_grad = _unpack(total[rows_small:], [(depth, 3, nchips * cw_cols)])[0]
    grads["conv_w"] = lax.dynamic_slice_in_dim(conv_w_grad, chip * cw_cols, cw_cols, axis=2)

    ada_rows = depth * 6 * d // _LANES
    d_ada_all = g3.reshape(8, -1, _LANES)[:, :ada_rows].reshape(8, depth, 6 * d).astype(F32)
    d_sel = lax.dynamic_slice_in_dim(jnp.transpose(d_ada_all, (1, 0, 2)), chip * ada_cols, ada_cols, axis=2)
    grads["w_ada"] = _wada_grad(jnp.transpose(c_act), d_sel, "w_ada_grad")

    delta, new_m, new_v = {}, {}, {}
    for n in _WEIGHTS:
        shp = wts[n].shape
        view = (lambda t: t) if len(shp) == 3 else (lambda t: t.reshape(1, -1, shp[-1]))
        outs = _adamw(view(wts[n]), view(grads[n]), view(mom[n]), view(var[n]), f"adamw_{n}")
        delta[n], new_m[n], new_v[n] = [t.reshape(shp) for t in outs]

    for table in (grads, delta, new_m, new_v):
        table["w_in"] = jnp.swapaxes(table["w_in"], 1, 2)
    return (loss, grad_x[None], *[grads[n] for n in _WEIGHTS], *[delta[n] for n in _WEIGHTS],
            *[new_m[n] for n in _WEIGHTS], *[new_v[n] for n in _WEIGHTS])
```
